```python
import functools
import jax, jax.numpy as jnp
from jax import lax
import numpy as np

D_MODEL = 1024
BATCH = 16
SEQ = 2048
DEPTH = 2
DEC_BATCH = 128
DEC_SEQ = 1
PAST_LEN = 8192
PAGE_SIZE = 128

MIX_W = D_MODEL
MLA_V = 128
MLA_W = D_MODEL // 2
MLA_HEADS = MLA_W // MLA_V
MLA_NOPE = 128
MLA_ROPE = 64
MLA_QK = MLA_NOPE + MLA_ROPE
Q_LORA = (3 * D_MODEL) // 8
KV_LORA = D_MODEL // 4
ROPE_THETA = 10000.0
ATTN_SCALE = MLA_QK ** -0.5
Q_BLOCK = 128
RWKV_W = D_MODEL // 4
RWKV_HEAD = 64
RWKV_HEADS = RWKV_W // RWKV_HEAD
DECAY_LORA = 64
ICLR_LORA = 64
SHIFT_W = 3 * RWKV_W + DECAY_LORA + ICLR_LORA
GN_EPS = 64e-5
GM_W = D_MODEL // 4
GM_GROUPS = 4
GM_GROUP_W = GM_W // GM_GROUPS
CHUNK = 128
IN_SPLITS = (Q_LORA, KV_LORA, MLA_ROPE, MLA_W, SHIFT_W, RWKV_W, GM_W, GM_W, GM_W)
IN_W = Q_LORA + KV_LORA + MLA_ROPE + MLA_W + SHIFT_W + RWKV_W + 3 * GM_W
RMS_EPS = 1e-6

kernel_name = "hybrid_mla_rwkv7_gmlp_step"

F32 = jnp.float32


def split_last(x, sizes):
    idx = [int(i) for i in np.cumsum(sizes)[:-1]]
    return jnp.split(x, idx, axis=-1)


def rms_norm(x, g):
    xf = x.astype(F32)
    y = xf * lax.rsqrt(jnp.mean(xf * xf, axis=-1, keepdims=True) + RMS_EPS)
    return (y * g.astype(F32)).astype(x.dtype)


def layer_norm(x, g, b, eps=1e-5):
    xf = x.astype(F32)
    mu = jnp.mean(xf, axis=-1, keepdims=True)
    var = jnp.mean(jnp.square(xf - mu), axis=-1, keepdims=True)
    y = (xf - mu) * lax.rsqrt(var + eps)
    return (y * g.astype(F32) + b.astype(F32)).astype(x.dtype)


def rope(x, pos):
    half = MLA_ROPE // 2
    inv = ROPE_THETA ** (-jnp.arange(half, dtype=F32) / half)
    ang = pos.astype(F32)[:, None] * inv[None, :]
    cos = jnp.cos(ang)[None, :, None, :]
    sin = jnp.sin(ang)[None, :, None, :]
    xf = x.astype(F32)
    x1, x2 = xf[..., :half], xf[..., half:]
    return jnp.concatenate([x1 * cos - x2 * sin, x1 * sin + x2 * cos], axis=-1).astype(x.dtype)


def mla_prompt_attention(q_nope, q_pe, ckv, kpe, w_uk, w_uv):
    B, S = q_nope.shape[:2]
    k_nope = jnp.einsum('bsr,rhd->bshd', ckv, w_uk)
    v = jnp.einsum('bsr,rhd->bshd', ckv, w_uv)
    kpos = jnp.arange(S)

    def block(i):
        start = i * Q_BLOCK
        qn = lax.dynamic_slice_in_dim(q_nope, start, Q_BLOCK, axis=1)
        qp = lax.dynamic_slice_in_dim(q_pe, start, Q_BLOCK, axis=1)
        s = (jnp.einsum('bqhd,bkhd->bhqk', qn, k_nope)
             + jnp.einsum('bqhd,bkd->bhqk', qp, kpe)).astype(F32) * ATTN_SCALE
        qpos = start + jnp.arange(Q_BLOCK)
        s = jnp.where(qpos[:, None] >= kpos[None, :], s, -jnp.inf)
        p = jax.nn.softmax(s, axis=-1).astype(v.dtype)
        return jnp.einsum('bhqk,bkhd->bqhd', p, v)

    o = lax.map(block, jnp.arange(S // Q_BLOCK))
    return o.transpose(1, 0, 2, 3, 4).reshape(B, S, MLA_HEADS, MLA_V)


def mla_decode_attention(q_nope, q_pe, ckv, kpe, w_uk, w_uv, ckv_pool, kpe_pool, page_table):
    B, L = q_nope.shape[:2]
    ckv_past = ckv_pool[page_table].reshape(B, -1, KV_LORA)
    kpe_past = kpe_pool[page_table].reshape(B, -1, MLA_ROPE)
    P = ckv_past.shape[1]
    q_abs = jnp.einsum('blhd,rhd->blhr', q_nope, w_uk)
    s_past = (jnp.einsum('blhr,bpr->bhlp', q_abs, ckv_past)
              + jnp.einsum('blhd,bpd->bhlp', q_pe, kpe_past)).astype(F32) * ATTN_SCALE
    s_new = (jnp.einsum('blhr,bmr->bhlm', q_abs, ckv)
             + jnp.einsum('blhd,bmd->bhlm', q_pe, kpe)).astype(F32) * ATTN_SCALE
    causal = jnp.tril(jnp.ones((L, L), dtype=bool))
    s_new = jnp.where(causal, s_new, -jnp.inf)
    p = jax.nn.softmax(jnp.concatenate([s_past, s_new], axis=-1), axis=-1).astype(ckv.dtype)
    o_lat = (jnp.einsum('bhlp,bpr->blhr', p[..., :P], ckv_past)
             + jnp.einsum('bhlm,bmr->blhr', p[..., P:], ckv))
    return jnp.einsum('blhr,rhd->blhd', o_lat, w_uv)


def rwkv7_time_mix(p, shift0, wkv0, lp):
    B, L, _ = p.shape
    prev = jnp.concatenate([shift0[:, None].astype(p.dtype), p[:, :-1]], axis=1)
    xm = p + (prev - p) * lp['rw_mu']
    r, k, v, wd, ad = split_last(xm, (RWKV_W, RWKV_W, RWKV_W, DECAY_LORA, ICLR_LORA))
    w = -jax.nn.softplus(-(lp['rw_w0'] + jnp.tanh(wd) @ lp['rw_w_decay_up'])) - 0.5
    decay = jnp.exp(-jnp.exp(w.astype(F32)))
    a = jax.nn.sigmoid(lp['rw_a0'] + ad @ lp['rw_w_a_up'])

    def heads(t):
        return t.reshape(B, L, RWKV_HEADS, RWKV_HEAD).astype(F32)

    kk = heads(k * lp['rw_k_k'])
    kk = kk / jnp.maximum(jnp.sqrt(jnp.sum(kk * kk, axis=-1, keepdims=True)), 1e-12)
    k = k * (1 + (a - 1) * lp['rw_k_a'])
    rh, kh, vh, ah, dh = heads(r), heads(k), heads(v), heads(a), heads(decay)

    def step(S, inp):
        r_t, k_t, v_t, kk_t, b_t, d_t = inp
        sa = jnp.einsum('bhvk,bhk->bhv', S, kk_t)
        S = S * d_t[:, :, None, :] - sa[..., None] * b_t[:, :, None, :] + v_t[..., None] * k_t[:, :, None, :]
        return S, jnp.einsum('bhvk,bhk->bhv', S, r_t)

    xs = tuple(t.transpose(1, 0, 2, 3) for t in (rh, kh, vh, kk, kk * ah, dh))
    S_fin, o = lax.scan(step, wkv0.astype(F32), xs)
    o = o.transpose(1, 0, 2, 3)
    mu = jnp.mean(o, axis=-1, keepdims=True)
    var = jnp.mean(jnp.square(o - mu), axis=-1, keepdims=True)
    on = ((o - mu) * lax.rsqrt(var + GN_EPS)).reshape(B, L, RWKV_W)
    on = on * lp['rw_lnx_g'].astype(F32) + lp['rw_lnx_b'].astype(F32)
    bonus = jnp.sum(rh * kh * lp['rw_r_k'].astype(F32), axis=-1, keepdims=True) * vh
    out = (on + bonus.reshape(B, L, RWKV_W)).astype(p.dtype)
    return out, S_fin.astype(wkv0.dtype), p[:, -1]


def chunk_gmlp(u, v, lp):
    B, L, _ = v.shape
    vn = layer_norm(v, lp['gm_ln_g'], lp['gm_ln_b'])
    n_chunk = -(-L // CHUNK)
    vc = jnp.pad(vn, ((0, 0), (0, n_chunk * CHUNK - L), (0, 0)))
    vc = vc.reshape(B, n_chunk, CHUNK, GM_GROUPS, GM_GROUP_W)
    ws = lp['gm_w_s'] * jnp.tril(jnp.ones((CHUNK, CHUNK), vn.dtype))
    z = jnp.einsum('gts,bcsgd->bctgd', ws, vc) + lp['gm_b_s'].T[:, :, None]
    z = z.reshape(B, n_chunk * CHUNK, GM_W)[:, :L]
    return u * z, vn


def hybrid_layer(x, c, pos, attend, wkv0, shift0, lp):
    B, L, _ = x.shape
    mod = jax.nn.silu(c) @ lp['w_ada'] + lp['b_ada']
    shift, scale, gate = jnp.split(mod, 3, axis=-1)
    h = rms_norm(x, lp['norm_pre_g']) * (1 + scale[:, None]) + shift[:, None]
    proj = jnp.einsum('bld,de->ble', h, lp['w_in'])
    q_down, kv_down, k_rope, mla_gate, rw_in, rw_gate, gm_u, gm_v, gm_gate = split_last(proj, IN_SPLITS)
    q = jnp.einsum('blr,rhd->blhd', rms_norm(q_down, lp['q_norm_g']), lp['w_uq'])
    q_nope, q_pe = q[..., :MLA_NOPE], rope(q[..., MLA_NOPE:], pos)
    ckv = rms_norm(kv_down, lp['kv_norm_g'])
    kpe = rope(k_rope[:, :, None, :], pos)[:, :, 0]
    o_mla = attend(q_nope, q_pe, ckv, kpe, lp['w_uk'], lp['w_uv']).reshape(B, L, MLA_W)
    o_rw, wkv_new, shift_new = rwkv7_time_mix(rw_in, shift0, wkv0, lp)
    o_gm, vn = chunk_gmlp(gm_u, gm_v, lp)
    mixed = jnp.concatenate([jax.nn.silu(mla_gate) * o_mla,
                             jax.nn.silu(rw_gate) * o_rw,
                             jax.nn.silu(gm_gate) * o_gm], axis=-1)
    out = jnp.einsum('ble,ed->bld', mixed, lp['w_out'])
    y = x + gate[:, None] * rms_norm(out, lp['norm_post_g'])
    return y, ckv, kpe, wkv_new, shift_new, vn


def setup_inputs(seed: int = 0) -> dict:
    key = jax.random.key(seed)
    ks = iter(jax.random.split(key, 48))

    def nrm(shape, scale):
        return jax.random.normal(next(ks), shape, F32) * scale

    def gain(shape):
        return 1.0 + nrm(shape, 0.05)

    n_pages = PAST_LEN // PAGE_SIZE
    n_used = DEC_BATCH * n_pages
    n_pool = (n_used * 5) // 4
    page_table = jax.random.permutation(next(ks), n_pool)[:n_used].reshape(DEC_BATCH, n_pages).astype(jnp.int32)
    return {
        'x_prompt': nrm((BATCH, SEQ, D_MODEL), 1.0),
        'x_sample': nrm((DEC_BATCH, DEC_SEQ, D_MODEL), 1.0),
        'c_prompt': nrm((BATCH, D_MODEL), 1.0),
        'c_sample': nrm((DEC_BATCH, D_MODEL), 1.0),
        'cache_ckv': nrm((DEPTH, n_pool, PAGE_SIZE, KV_LORA), 1.0),
        'cache_kpe': nrm((DEPTH, n_pool, PAGE_SIZE, MLA_ROPE), 1.0),
        'page_table': page_table,
        'state_wkv': nrm((DEPTH, DEC_BATCH, RWKV_HEADS, RWKV_HEAD, RWKV_HEAD), 0.5),
        'state_shift': nrm((DEPTH, DEC_BATCH, SHIFT_W), 1.0),
        'w_ada': nrm((DEPTH, D_MODEL, 3 * D_MODEL), D_MODEL ** -0.5),
        'b_ada': nrm((DEPTH, 3 * D_MODEL), 0.01),
        'norm_pre_g': gain((DEPTH, D_MODEL)),
        'norm_post_g': gain((DEPTH, D_MODEL)),
        'w_in': nrm((DEPTH, D_MODEL, IN_W), D_MODEL ** -0.5),
        'q_norm_g': gain((DEPTH, Q_LORA)),
        'kv_norm_g': gain((DEPTH, KV_LORA)),
        'w_uq': nrm((DEPTH, Q_LORA, MLA_HEADS, MLA_QK), Q_LORA ** -0.5),
        'w_uk': nrm((DEPTH, KV_LORA, MLA_HEADS, MLA_NOPE), KV_LORA ** -0.5),
        'w_uv': nrm((DEPTH, KV_LORA, MLA_HEADS, MLA_V), KV_LORA ** -0.5),
        'rw_mu': jax.random.uniform(next(ks), (DEPTH, SHIFT_W), F32),
        'rw_w0': jax.random.uniform(next(ks), (DEPTH, RWKV_W), F32, -6.0, 0.0),
        'rw_w_decay_up': nrm((DEPTH, DECAY_LORA, RWKV_W), 0.1 * DECAY_LORA ** -0.5),
        'rw_a0': nrm((DEPTH, RWKV_W), 0.1),
        'rw_w_a_up': nrm((DEPTH, ICLR_LORA, RWKV_W), 0.5 * ICLR_LORA ** -0.5),
        'rw_k_k': 0.85 + nrm((DEPTH, RWKV_W), 0.05),
        'rw_k_a': gain((DEPTH, RWKV_W)),
        'rw_r_k': nrm((DEPTH, RWKV_HEADS, RWKV_HEAD), 0.1),
        'rw_lnx_g': gain((DEPTH, RWKV_W)),
        'rw_lnx_b': nrm((DEPTH, RWKV_W), 0.01),
        'gm_ln_g': gain((DEPTH, GM_W)),
        'gm_ln_b': nrm((DEPTH, GM_W), 0.01),
        'gm_w_s': nrm((DEPTH, GM_GROUPS, CHUNK, CHUNK), CHUNK ** -0.5),
        'gm_b_s': gain((DEPTH, GM_GROUPS, CHUNK)),
        'w_out': nrm((DEPTH, MIX_W, D_MODEL), MIX_W ** -0.5),
    }


def reference(x_prompt, x_sample, c_prompt, c_sample, cache_ckv, cache_kpe, page_table,
              state_wkv, state_shift, w_ada, b_ada, norm_pre_g, norm_post_g, w_in,
              q_norm_g, kv_norm_g, w_uq, w_uk, w_uv, rw_mu, rw_w0, rw_w_decay_up, rw_a0,
              rw_w_a_up, rw_k_k, rw_k_a, rw_r_k, rw_lnx_g, rw_lnx_b, gm_ln_g, gm_ln_b,
              gm_w_s, gm_b_s, w_out):
    B, S, _ = x_prompt.shape
    Ls = x_sample.shape[1]
    past_len = page_table.shape[1] * cache_ckv.shape[2]
    pos_p = jnp.arange(S)
    pos_s = past_len + jnp.arange(Ls)
    last_start = ((S - 1) // CHUNK) * CHUNK

    y_p, y_s = x_prompt, x_sample
    ckv_p, kpe_p, ckv_s, kpe_s = [], [], [], []
    wkv_p, wkv_s, sh_p, sh_s, vc_p, vc_s = [], [], [], [], [], []
    for l in range(DEPTH):
        lp = {
            'w_ada': w_ada[l], 'b_ada': b_ada[l], 'norm_pre_g': norm_pre_g[l],
            'norm_post_g': norm_post_g[l], 'w_in': w_in[l], 'q_norm_g': q_norm_g[l],
            'kv_norm_g': kv_norm_g[l], 'w_uq': w_uq[l], 'w_uk': w_uk[l], 'w_uv': w_uv[l],
            'rw_mu': rw_mu[l], 'rw_w0': rw_w0[l], 'rw_w_decay_up': rw_w_decay_up[l],
            'rw_a0': rw_a0[l], 'rw_w_a_up': rw_w_a_up[l], 'rw_k_k': rw_k_k[l],
            'rw_k_a': rw_k_a[l], 'rw_r_k': rw_r_k[l], 'rw_lnx_g': rw_lnx_g[l],
            'rw_lnx_b': rw_lnx_b[l], 'gm_ln_g': gm_ln_g[l], 'gm_ln_b': gm_ln_b[l],
            'gm_w_s': gm_w_s[l], 'gm_b_s': gm_b_s[l], 'w_out': w_out[l],
        }
        wkv0 = jnp.zeros((B, RWKV_HEADS, RWKV_HEAD, RWKV_HEAD), F32)
        sh0 = jnp.zeros((B, SHIFT_W), x_prompt.dtype)
        y_p, a1, a2, a3, a4, a5 = hybrid_layer(y_p, c_prompt, pos_p, mla_prompt_attention, wkv0, sh0, lp)
        ckv_p.append(a1); kpe_p.append(a2); wkv_p.append(a3); sh_p.append(a4)
        vc_p.append(a5[:, last_start:])
        attend_s = functools.partial(mla_decode_attention, ckv_pool=cache_ckv[l],
                                     kpe_pool=cache_kpe[l], page_table=page_table)
        y_s, b1, b2, b3, b4, b5 = hybrid_layer(y_s, c_sample, pos_s, attend_s, state_wkv[l], state_shift[l], lp)
        ckv_s.append(b1); kpe_s.append(b2); wkv_s.append(b3); sh_s.append(b4); vc_s.append(b5)

    return (y_p, y_s,
            jnp.stack(ckv_p), jnp.stack(kpe_p), jnp.stack(ckv_s), jnp.stack(kpe_s),
            jnp.stack(wkv_p), jnp.stack(wkv_s), jnp.stack(sh_p), jnp.stack(sh_s),
            jnp.stack(vc_p), jnp.stack(vc_s))
```

```python
import functools

import numpy as np
import jax
import jax.numpy as jnp
from jax import lax
from jax.experimental import pallas as pl
from jax.experimental.pallas import tpu as pltpu

F32 = jnp.float32
BF16 = jnp.bfloat16

D_MODEL = 1024
MLA_V = 128
MLA_W = D_MODEL // 2
MLA_HEADS = MLA_W // MLA_V
MLA_NOPE = 128
MLA_ROPE = 64
MLA_QK = MLA_NOPE + MLA_ROPE
Q_LORA = (3 * D_MODEL) // 8
KV_LORA = D_MODEL // 4
ROPE_THETA = 10000.0
ATTN_SCALE = MLA_QK ** -0.5
RWKV_W = D_MODEL // 4
RWKV_HEAD = 64
RWKV_HEADS = RWKV_W // RWKV_HEAD
DECAY_LORA = 64
ICLR_LORA = 64
SHIFT_W = 3 * RWKV_W + DECAY_LORA + ICLR_LORA
GN_EPS = 64e-5
GM_W = D_MODEL // 4
GM_GROUPS = 4
GM_GROUP_W = GM_W // GM_GROUPS
CHUNK = 128
RMS_EPS = 1e-6
LN_EPS = 1e-5

LANES = 128
SUBLANES = 8
VMEM_LIMIT_BYTES = 56 * 1024 * 1024

QHEAD_W = 2 * LANES
OFF_Q = 0
OFF_KV = OFF_Q + Q_LORA
OFF_MG = OFF_KV + KV_LORA
OFF_RW = OFF_MG + MLA_W
OFF_RG = OFF_RW + SHIFT_W
OFF_GU = OFF_RG + RWKV_W
OFF_GV = OFF_GU + GM_W
OFF_GG = OFF_GV + GM_W
OFF_KR = OFF_GG + GM_W
IN_W_P = OFF_KR + LANES

NEG_BIG = -1e30


def _cparams(*sem):
    return pltpu.CompilerParams(dimension_semantics=sem, vmem_limit_bytes=VMEM_LIMIT_BYTES)


def _silu(x):
    return x * jax.nn.sigmoid(x)


def _rms(x, g, eps=RMS_EPS):
    return x * lax.rsqrt(jnp.mean(x * x, axis=-1, keepdims=True) + eps) * g


def _rows2d(ref):
    m = ref[...]
    return m.reshape(m.shape[-2], m.shape[-1])


def _group_index(idx, group):
    shift = group.bit_length() - 1
    assert 1 << shift == group
    return lax.shift_right_logical(idx, shift)


def _head_ones(width, head):
    r = _group_index(lax.broadcasted_iota(jnp.int32, (width, width), 0), head)
    c = _group_index(lax.broadcasted_iota(jnp.int32, (width, width), 1), head)
    return jnp.where(r == c, 1.0, 0.0).astype(BF16)


def _head_sum(x, ones_bd):
    hi = x.astype(BF16)
    lo = (x - hi.astype(F32)).astype(BF16)
    return (jnp.dot(hi, ones_bd, preferred_element_type=F32)
            + jnp.dot(lo, ones_bd, preferred_element_type=F32))


def _rope_pair(blk, cs, sn):
    return blk * cs + pltpu.roll(blk, MLA_ROPE, 1) * sn


def _ada_kernel(c_ref, w_ref, b_ref, o_ref):
    c = _silu(c_ref[...]).astype(BF16)
    o_ref[0] = jnp.dot(c, w_ref[0], preferred_element_type=F32) + b_ref[0]


def _ada_mod(c_all, w_ada, b_ada):
    depth = w_ada.shape[0]
    n_rows = c_all.shape[0]
    n_tiles = (3 * D_MODEL) // D_MODEL
    return pl.pallas_call(
        _ada_kernel,
        grid=(depth, n_tiles),
        in_specs=[
            pl.BlockSpec((n_rows, D_MODEL), lambda l, j: (0, 0)),
            pl.BlockSpec((1, D_MODEL, D_MODEL), lambda l, j: (l, 0, j)),
            pl.BlockSpec((1, 1, D_MODEL), lambda l, j: (l, 0, j)),
        ],
        out_specs=pl.BlockSpec((1, n_rows, D_MODEL), lambda l, j: (l, 0, j)),
        out_shape=jax.ShapeDtypeStruct((depth, n_rows, 3 * D_MODEL), F32),
        compiler_params=_cparams("arbitrary", "arbitrary"),
        name="ada_mod",
    )(c_all, w_ada.astype(BF16), b_ada.reshape(depth, 1, 3 * D_MODEL))


def _proj_common(x_ref, mod_ref, gpre_ref, win_ref):
    mod = _rows2d(mod_ref)
    shift = mod[:, 0:D_MODEL]
    scale = mod[:, D_MODEL:2 * D_MODEL]
    h = _rms(x_ref[...], gpre_ref[...]) * (1.0 + scale) + shift
    hb = h.astype(BF16)

    def seg(off, width):
        return jnp.dot(hb, win_ref[:, off:off + width], preferred_element_type=F32)

    return seg


def _q_heads(seg, gq_ref, wuq_ref, cs, sn):
    qn = _rms(seg(OFF_Q, Q_LORA), gq_ref[...]).astype(BF16)
    q_all = jnp.dot(qn, wuq_ref[...], preferred_element_type=F32)
    heads = []
    for h in range(MLA_HEADS):
        nope = q_all[:, h * QHEAD_W:h * QHEAD_W + LANES] * ATTN_SCALE
        pe = _rope_pair(q_all[:, h * QHEAD_W + LANES:(h + 1) * QHEAD_W], cs, sn) * ATTN_SCALE
        heads.append((nope, pe))
    return heads


def _gm_norm(seg, lng_ref, lnb_ref):
    v = seg(OFF_GV, GM_W)
    mu = jnp.mean(v, axis=-1, keepdims=True)
    c = v - mu
    var = jnp.mean(c * c, axis=-1, keepdims=True)
    return c * lax.rsqrt(var + LN_EPS) * lng_ref[...] + lnb_ref[...]


def _proj_prompt_kernel(x_ref, mod_ref, gpre_ref, win_ref, gq_ref, wuq_ref, gkv_ref, wuk_ref, wuv_ref,
                        cs_ref, sn_ref, lng_ref, lnb_ref, ws_ref, bs_ref,
                        q_ref, k_ref, v_ref, mg_ref, ckv_ref, kpe_ref, rw_ref, rg_ref, gm_ref, vn_ref,
                        *, tiles_per_seq):
    seg = _proj_common(x_ref, mod_ref, gpre_ref, win_ref)
    cs = cs_ref[...]
    sn = sn_ref[...]
    tm = x_ref.shape[0]

    for h, (nope, pe) in enumerate(_q_heads(seg, gq_ref, wuq_ref, cs, sn)):
        q_ref[:, h * QHEAD_W:h * QHEAD_W + LANES] = nope.astype(BF16)
        q_ref[:, h * QHEAD_W + LANES:(h + 1) * QHEAD_W] = pe.astype(BF16)
    ckv = _rms(seg(OFF_KV, KV_LORA), gkv_ref[...])
    ckv_ref[...] = ckv
    kpe = _rope_pair(seg(OFF_KR, LANES), cs, sn)
    kpe_ref[...] = kpe[:, :MLA_ROPE]
    ckv_b = ckv.astype(BF16)
    k_nope = jnp.dot(ckv_b, wuk_ref[...], preferred_element_type=F32)
    v_ref[...] = jnp.dot(ckv_b, wuv_ref[...], preferred_element_type=F32).astype(BF16)
    kpe_b = kpe.astype(BF16)
    for h in range(MLA_HEADS):
        k_ref[:, h * QHEAD_W:h * QHEAD_W + LANES] = k_nope[:, h * MLA_NOPE:(h + 1) * MLA_NOPE].astype(BF16)
        k_ref[:, h * QHEAD_W + LANES:(h + 1) * QHEAD_W] = kpe_b
    mg_ref[...] = _silu(seg(OFF_MG, MLA_W)).astype(BF16)

    rw_ref[...] = seg(OFF_RW, SHIFT_W)
    rg_ref[...] = _silu(seg(OFF_RG, RWKV_W))

    vn = _gm_norm(seg, lng_ref, lnb_ref)
    vn_b = vn.astype(BF16)
    gate_u = _silu(seg(OFF_GG, GM_W)) * seg(OFF_GU, GM_W)
    row = lax.broadcasted_iota(jnp.int32, (CHUNK, CHUNK), 0)
    col = lax.broadcasted_iota(jnp.int32, (CHUNK, CHUNK), 1)
    lane_group = _group_index(lax.broadcasted_iota(jnp.int32, (CHUNK, GM_W), 1), GM_GROUP_W)
    w_tril = [jnp.where(row >= col, ws_ref[g], 0.0).astype(BF16) for g in range(GM_GROUPS)]
    for c in range(tm // CHUNK):
        vc = vn_b[c * CHUNK:(c + 1) * CHUNK]
        z = bs_ref[...]
        for g in range(GM_GROUPS):
            zg = jnp.dot(w_tril[g], vc, preferred_element_type=F32)
            z = z + jnp.where(lane_group == g, zg, 0.0)
        gm_ref[c * CHUNK:(c + 1) * CHUNK, :] = (gate_u[c * CHUNK:(c + 1) * CHUNK] * z).astype(BF16)

    @pl.when(pl.program_id(0) % tiles_per_seq == tiles_per_seq - 1)
    def _():
        vn_ref[0] = vn[tm - CHUNK:, :]


def _proj_prompt(x2d, mod_b, lw, cs, sn, batch, seq, tm):
    rows = x2d.shape[0]
    tiles_per_seq = seq // tm
    const = lambda i: (0, 0)
    row_blk = lambda i: (i, 0)
    outs = [
        (QHEAD_W * MLA_HEADS, BF16), (QHEAD_W * MLA_HEADS, BF16), (MLA_W, BF16), (MLA_W, BF16),
        (KV_LORA, F32), (MLA_ROPE, F32), (SHIFT_W, F32), (RWKV_W, F32), (GM_W, BF16),
    ]
    out_shape = [jax.ShapeDtypeStruct((rows, w), dt) for w, dt in outs]
    out_specs = [pl.BlockSpec((tm, w), row_blk) for w, _ in outs]
    out_shape.append(jax.ShapeDtypeStruct((batch, CHUNK, GM_W), F32))
    out_specs.append(pl.BlockSpec((1, CHUNK, GM_W), lambda i: (i // tiles_per_seq, 0, 0)))
    return pl.pallas_call(
        functools.partial(_proj_prompt_kernel, tiles_per_seq=tiles_per_seq),
        grid=(rows // tm,),
        in_specs=[
            pl.BlockSpec((tm, D_MODEL), row_blk),
            pl.BlockSpec((1, 1, 3 * D_MODEL), lambda i: (i // tiles_per_seq, 0, 0)),
            pl.BlockSpec((1, D_MODEL), const),
            pl.BlockSpec((D_MODEL, IN_W_P), const),
            pl.BlockSpec((1, Q_LORA), const),
            pl.BlockSpec((Q_LORA, QHEAD_W * MLA_HEADS), const),
            pl.BlockSpec((1, KV_LORA), const),
            pl.BlockSpec((KV_LORA, MLA_W), const),
            pl.BlockSpec((KV_LORA, MLA_W), const),
            pl.BlockSpec((tm, LANES), lambda i: (i % tiles_per_seq, 0)),
            pl.BlockSpec((tm, LANES), lambda i: (i % tiles_per_seq, 0)),
            pl.BlockSpec((1, GM_W), const),
            pl.BlockSpec((1, GM_W), const),
            pl.BlockSpec((GM_GROUPS, CHUNK, CHUNK), lambda i: (0, 0, 0)),
            pl.BlockSpec((CHUNK, GM_W), const),
        ],
        out_specs=out_specs,
        out_shape=out_shape,
        compiler_params=_cparams("arbitrary"),
        name="proj_prompt",
    )(x2d, mod_b, lw["g_pre"], lw["w_in"], lw["g_q"], lw["w_uq"], lw["g_kv"], lw["w_uk"], lw["w_uv"],
      cs, sn, lw["ln_g"], lw["ln_b"], lw["gm_ws"], lw["gm_bias"])


def _proj_sample_kernel(x_ref, mod_ref, gpre_ref, win_ref, gq_ref, wuq_ref, gkv_ref, wukt_ref,
                        cs_ref, sn_ref, lng_ref, lnb_ref, coef_ref, bias_ref,
                        qa_ref, qp_ref, mg_ref, ckv_ref, kpe_ref, rw_ref, rg_ref, gm_ref, vn_ref):
    seg = _proj_common(x_ref, mod_ref, gpre_ref, win_ref)
    cs = cs_ref[...]
    sn = sn_ref[...]
    for h, (nope, pe) in enumerate(_q_heads(seg, gq_ref, wuq_ref, cs, sn)):
        qa_ref[:, h * KV_LORA:(h + 1) * KV_LORA] = jnp.dot(nope.astype(BF16), wukt_ref[h],
                                                           preferred_element_type=F32)
        qp_ref[:, h * LANES:(h + 1) * LANES] = pe
    ckv_ref[...] = _rms(seg(OFF_KV, KV_LORA), gkv_ref[...])
    kpe_ref[...] = _rope_pair(seg(OFF_KR, LANES), cs, sn)
    mg_ref[...] = _silu(seg(OFF_MG, MLA_W)).astype(BF16)
    rw_ref[...] = seg(OFF_RW, SHIFT_W)
    rg_ref[...] = _silu(seg(OFF_RG, RWKV_W))
    vn = _gm_norm(seg, lng_ref, lnb_ref)
    vn_ref[...] = vn
    z = vn * coef_ref[...] + bias_ref[...]
    gm_ref[...] = (_silu(seg(OFF_GG, GM_W)) * seg(OFF_GU, GM_W) * z).astype(BF16)


def _proj_sample(x2d, mod_rows, lw, cs, sn):
    rows = x2d.shape[0]
    const = lambda i: (0, 0)
    outs = [
        (KV_LORA * MLA_HEADS, F32), (LANES * MLA_HEADS, F32), (MLA_W, BF16), (KV_LORA, F32), (LANES, F32),
        (SHIFT_W, F32), (RWKV_W, F32), (GM_W, BF16), (GM_W, F32),
    ]
    return pl.pallas_call(
        _proj_sample_kernel,
        grid=(1,),
        in_specs=[
            pl.BlockSpec((rows, D_MODEL), const),
            pl.BlockSpec((rows, 3 * D_MODEL), const),
            pl.BlockSpec((1, D_MODEL), const),
            pl.BlockSpec((D_MODEL, IN_W_P), const),
            pl.BlockSpec((1, Q_LORA), const),
            pl.BlockSpec((Q_LORA, QHEAD_W * MLA_HEADS), const),
            pl.BlockSpec((1, KV_LORA), const),
            pl.BlockSpec((MLA_HEADS, MLA_NOPE, KV_LORA), lambda i: (0, 0, 0)),
            pl.BlockSpec((rows, LANES), const),
            pl.BlockSpec((rows, LANES), const),
            pl.BlockSpec((1, GM_W), const),
            pl.BlockSpec((1, GM_W), const),
            pl.BlockSpec((1, GM_W), const),
            pl.BlockSpec((1, GM_W), const),
        ],
        out_specs=[pl.BlockSpec((rows, w), const) for w, _ in outs],
        out_shape=[jax.ShapeDtypeStruct((rows, w), dt) for w, dt in outs],
        compiler_params=_cparams("arbitrary"),
        name="proj_sample",
    )(x2d, mod_rows, lw["g_pre"], lw["w_in"], lw["g_q"], lw["w_uq"], lw["g_kv"], lw["w_ukt"],
      cs, sn, lw["ln_g"], lw["ln_b"], lw["gm_coef0"], lw["gm_bias0"])


def _flash_kernel(q_ref, k_ref, v_ref, g_ref, o_ref, *, tq):
    seq = q_ref.shape[1]
    row = lax.broadcasted_iota(jnp.int32, (tq, tq), 0)
    col = lax.broadcasted_iota(jnp.int32, (tq, tq), 1)
    for qi in range(seq // tq):
        rows = slice(qi * tq, (qi + 1) * tq)
        q = q_ref[0, rows, :]
        m = jnp.full((tq, 1), NEG_BIG, F32)
        l = jnp.zeros((tq, 1), F32)
        acc = jnp.zeros((tq, MLA_V), F32)
        for ki in range(qi + 1):
            cols = slice(ki * tq, (ki + 1) * tq)
            s = lax.dot_general(q, k_ref[0, cols, :], (((1,), (1,)), ((), ())),
                                preferred_element_type=F32)
            if ki == qi:
                s = jnp.where(row >= col, s, NEG_BIG)
            m_new = jnp.maximum(m, jnp.max(s, axis=-1, keepdims=True))
            alpha = jnp.exp(m - m_new)
            p = jnp.exp(s - m_new)
            l = alpha * l + jnp.sum(p, axis=-1, keepdims=True)
            acc = alpha * acc + jnp.dot(p.astype(BF16), v_ref[0, cols, :], preferred_element_type=F32)
            m = m_new
        o_ref[0, rows, :] = (acc / l * g_ref[0, rows, :].astype(F32)).astype(BF16)


def _flash(q, k, v, g, batch, seq, tq):
    q3 = q.reshape(batch, seq, QHEAD_W * MLA_HEADS)
    k3 = k.reshape(batch, seq, QHEAD_W * MLA_HEADS)
    v3 = v.reshape(batch, seq, MLA_W)
    g3 = g.reshape(batch, seq, MLA_W)
    head_blk = lambda b, h: (b, 0, h)
    out = pl.pallas_call(
        functools.partial(_flash_kernel, tq=tq),
        grid=(batch, MLA_HEADS),
        in_specs=[
            pl.BlockSpec((1, seq, QHEAD_W), head_blk),
            pl.BlockSpec((1, seq, QHEAD_W), head_blk),
            pl.BlockSpec((1, seq, MLA_V), head_blk),
            pl.BlockSpec((1, seq, MLA_V), head_blk),
        ],
        out_specs=pl.BlockSpec((1, seq, MLA_V), head_blk),
        out_shape=jax.ShapeDtypeStruct((batch, seq, MLA_W), BF16),
        compiler_params=_cparams("arbitrary", "arbitrary"),
        name="flash_prompt",
    )(q3, k3, v3, g3)
    return out.reshape(batch * seq, MLA_W)


def _paged_kernel(pt_ref, qa_ref, qp_ref, cn_ref, kn_ref, *rest, pages_per_step):
    del pt_ref
    ckv_pages = rest[:pages_per_step]
    kpe_pages = rest[pages_per_step:2 * pages_per_step]
    o_ref, m_ref, l_ref, acc_ref = rest[2 * pages_per_step:]
    j = pl.program_id(1)

    @pl.when(j == 0)
    def _():
        m_ref[...] = jnp.full(m_ref.shape, NEG_BIG, F32)
        l_ref[...] = jnp.zeros(l_ref.shape, F32)
        acc_ref[...] = jnp.zeros(acc_ref.shape, F32)

    qa = qa_ref[0]
    qp = qp_ref[0][:, :MLA_ROPE]
    ckv = jnp.concatenate([r[0, 0] for r in ckv_pages], axis=0).astype(BF16)
    kpe = jnp.concatenate([r[0, 0] for r in kpe_pages], axis=0).astype(BF16)
    nt = (((1,), (1,)), ((), ()))
    s = (lax.dot_general(qa.astype(BF16), ckv, nt, preferred_element_type=F32)
         + lax.dot_general(qp.astype(BF16), kpe, nt, preferred_element_type=F32))
    m_old = m_ref[...]
    m_new = jnp.maximum(m_old, jnp.max(s, axis=-1, keepdims=True))
    alpha = jnp.exp(m_old - m_new)
    p = jnp.exp(s - m_new)
    l_ref[...] = alpha * l_ref[...] + jnp.sum(p, axis=-1, keepdims=True)
    acc_ref[...] = alpha * acc_ref[...] + jnp.dot(p.astype(BF16), ckv, preferred_element_type=F32)
    m_ref[...] = m_new

    @pl.when(j == pl.num_programs(1) - 1)
    def _():
        cn = cn_ref[0]
        kn = kn_ref[0][:, :MLA_ROPE]
        s_new = (jnp.sum(qa * cn, axis=-1, keepdims=True)
                 + jnp.sum(qp * kn, axis=-1, keepdims=True))
        m_old2 = m_ref[...]
        m_fin = jnp.maximum(m_old2, s_new)
        a2 = jnp.exp(m_old2 - m_fin)
        p_new = jnp.exp(s_new - m_fin)
        l_fin = a2 * l_ref[...] + p_new
        o_ref[0] = (a2 * acc_ref[...] + p_new * cn) / l_fin


def _paged_attention(qa, qp, ckv_new, kpe_new, cache_ckv, cache_kpe, page_table, layer, pages_per_step):
    dec_batch, n_pages = page_table.shape
    page = cache_ckv.shape[2]
    head_pad = ((0, 0), (0, SUBLANES - MLA_HEADS), (0, 0))
    qa3 = jnp.pad(qa.reshape(dec_batch, MLA_HEADS, KV_LORA), head_pad)
    qp3 = jnp.pad(qp.reshape(dec_batch, MLA_HEADS, LANES), head_pad)
    cn3 = ckv_new.reshape(dec_batch, 1, KV_LORA)
    kn3 = kpe_new.reshape(dec_batch, 1, LANES)
    row_blk = lambda b, j, pt: (b, 0, 0)

    def page_map(i):
        return lambda b, j, pt: (layer, pt[b, j * pages_per_step + i], 0, 0)

    in_specs = [
        pl.BlockSpec((1, SUBLANES, KV_LORA), row_blk),
        pl.BlockSpec((1, SUBLANES, LANES), row_blk),
        pl.BlockSpec((1, 1, KV_LORA), row_blk),
        pl.BlockSpec((1, 1, LANES), row_blk),
    ]
    in_specs += [pl.BlockSpec((1, 1, page, KV_LORA), page_map(i)) for i in range(pages_per_step)]
    in_specs += [pl.BlockSpec((1, 1, page, MLA_ROPE), page_map(i)) for i in range(pages_per_step)]
    out = pl.pallas_call(
        functools.partial(_paged_kernel, pages_per_step=pages_per_step),
        grid_spec=pltpu.PrefetchScalarGridSpec(
            num_scalar_prefetch=1,
            grid=(dec_batch, n_pages // pages_per_step),
            in_specs=in_specs,
            out_specs=pl.BlockSpec((1, SUBLANES, KV_LORA), row_blk),
            scratch_shapes=[
                pltpu.VMEM((SUBLANES, 1), F32),
                pltpu.VMEM((SUBLANES, 1), F32),
                pltpu.VMEM((SUBLANES, KV_LORA), F32),
            ],
        ),
        out_shape=jax.ShapeDtypeStruct((dec_batch, SUBLANES, KV_LORA), F32),
        compiler_params=_cparams("arbitrary", "arbitrary"),
        name="paged_attention",
    )(page_table, qa3, qp3, cn3, kn3, *([cache_ckv] * pages_per_step), *([cache_kpe] * pages_per_step))
    return out[:, :MLA_HEADS].reshape(dec_batch, MLA_HEADS * KV_LORA)


def _rwkv_prep_body(p, prev, mu_ref, w0_ref, wd_ref, a0_ref, wa_ref, kk_ref, ka_ref, rk_ref, outs):
    r_o, k_o, v_o, kk_o, b_o, d_o, bonus_o = outs
    xm = p + (prev - p) * mu_ref[...]
    r = xm[:, 0:RWKV_W]
    k = xm[:, RWKV_W:2 * RWKV_W]
    v = xm[:, 2 * RWKV_W:3 * RWKV_W]
    tail = xm[:, 3 * RWKV_W:]
    lane = lax.broadcasted_iota(jnp.int32, tail.shape, 1)
    lora_in = jnp.where(lane < DECAY_LORA, jnp.tanh(tail), tail).astype(BF16)
    dw = jnp.dot(lora_in, wd_ref[...], preferred_element_type=F32)
    da = jnp.dot(lora_in, wa_ref[...], preferred_element_type=F32)
    z = -(w0_ref[...] + dw)
    softplus = jnp.maximum(z, 0.0) + jnp.log(1.0 + jnp.exp(-jnp.abs(z)))
    w = -softplus - 0.5
    d_o[...] = jnp.exp(-jnp.exp(w))
    a = jax.nn.sigmoid(a0_ref[...] + da)
    ones_bd = _head_ones(RWKV_W, RWKV_HEAD)
    kk = k * kk_ref[...]
    kk = kk / jnp.maximum(jnp.sqrt(_head_sum(kk * kk, ones_bd)), 1e-12)
    k_mod = k * (1.0 + (a - 1.0) * ka_ref[...])
    r_o[...] = r
    k_o[...] = k_mod
    v_o[...] = v
    kk_o[...] = kk
    b_o[...] = kk * a
    bonus_o[...] = _head_sum(r * k_mod * rk_ref[...], ones_bd) * v


def _rwkv_prep_prompt_kernel(p_ref, pprev_ref, mu_ref, w0_ref, wd_ref, a0_ref, wa_ref, kk_ref, ka_ref, rk_ref,
                             *outs, tiles_per_seq):
    p = p_ref[...]
    first = pl.program_id(0) % tiles_per_seq == 0
    prev_row = jnp.where(first, 0.0, pprev_ref[SUBLANES - 1:SUBLANES, :])
    row = lax.broadcasted_iota(jnp.int32, p.shape, 0)
    prev = jnp.where(row == 0, prev_row, pltpu.roll(p, 1, 0))
    _rwkv_prep_body(p, prev, mu_ref, w0_ref, wd_ref, a0_ref, wa_ref, kk_ref, ka_ref, rk_ref, outs)


def _rwkv_prep_sample_kernel(p_ref, prev_ref, mu_ref, w0_ref, wd_ref, a0_ref, wa_ref, kk_ref, ka_ref, rk_ref,
                             *outs):
    _rwkv_prep_body(p_ref[...], prev_ref[...], mu_ref, w0_ref, wd_ref, a0_ref, wa_ref, kk_ref, ka_ref,
                    rk_ref, outs)


def _rwkv_prep(rw_in, lw, *, seq=None, tm=None, shift_rows=None):
    rows = rw_in.shape[0]
    const = lambda i: (0, 0)
    row_blk = lambda i: (i, 0)
    param_specs = [
        pl.BlockSpec((1, SHIFT_W), const),
        pl.BlockSpec((1, RWKV_W), const),
        pl.BlockSpec((LANES, RWKV_W), const),
        pl.BlockSpec((1, RWKV_W), const),
        pl.BlockSpec((LANES, RWKV_W), const),
        pl.BlockSpec((1, RWKV_W), const),
        pl.BlockSpec((1, RWKV_W), const),
        pl.BlockSpec((1, RWKV_W), const),
    ]
    params = (lw["rw_mu"], lw["rw_w0"], lw["rw_wd"], lw["rw_a0"], lw["rw_wa"], lw["rw_k_k"], lw["rw_k_a"],
              lw["rw_r_k"])
    if shift_rows is None:
        tiles_per_seq = seq // tm
        blocks_per_tile = tm // SUBLANES
        kern = functools.partial(_rwkv_prep_prompt_kernel, tiles_per_seq=tiles_per_seq)
        second = rw_in
        second_spec = pl.BlockSpec((SUBLANES, SHIFT_W),
                                   lambda i: (jnp.maximum(i * blocks_per_tile - 1, 0), 0))
    else:
        tm = rows
        kern = _rwkv_prep_sample_kernel
        second = shift_rows
        second_spec = pl.BlockSpec((tm, SHIFT_W), row_blk)
    return pl.pallas_call(
        kern,
        grid=(rows // tm,),
        in_specs=[pl.BlockSpec((tm, SHIFT_W), row_blk), second_spec] + param_specs,
        out_specs=[pl.BlockSpec((tm, RWKV_W), row_blk)] * 7,
        out_shape=[jax.ShapeDtypeStruct((rows, RWKV_W), F32)] * 7,
        compiler_params=_cparams("arbitrary"),
        name="rwkv_prep",
    )(rw_in, second, *params)


V_HALF = RWKV_HEAD // 2


def _scan_kernel(x_ref, v_ref, s0_ref, o_ref, sfin_ref, s_ref):
    tb = pl.program_id(1)

    @pl.when(tb == 0)
    def _():
        s_ref[...] = s0_ref[...]

    def step(t, carry):
        kk = x_ref[t, 0]
        b = x_ref[t, 1]
        d = x_ref[t, 2]
        kx = x_ref[t, 3]
        r = x_ref[t, 4]
        for vp in range(V_HALF):
            sv = s_ref[vp]
            sa = jnp.sum(sv * kk, axis=0, keepdims=True)
            sn = sv * d - sa * b + v_ref[t, pl.ds(vp, 1), :] * kx
            s_ref[vp] = sn
            o_ref[t, pl.ds(vp, 1), :] = jnp.sum(sn * r, axis=0, keepdims=True)
        return carry

    lax.fori_loop(0, x_ref.shape[0], step, 0)

    @pl.when(tb == pl.num_programs(1) - 1)
    def _():
        sfin_ref[...] = s_ref[...]


def _scan(x5, v_t, s0, t_blk):
    steps, _, _, lanes = x5.shape
    lane_tiles = lanes // LANES
    return pl.pallas_call(
        _scan_kernel,
        grid=(lane_tiles, steps // t_blk),
        in_specs=[
            pl.BlockSpec((t_blk, 5, RWKV_HEAD, LANES), lambda j, t: (t, 0, 0, j)),
            pl.BlockSpec((t_blk, V_HALF, LANES), lambda j, t: (t, 0, j)),
            pl.BlockSpec((V_HALF, RWKV_HEAD, LANES), lambda j, t: (0, 0, j)),
        ],
        out_specs=[
            pl.BlockSpec((t_blk, V_HALF, LANES), lambda j, t: (t, 0, j)),
            pl.BlockSpec((V_HALF, RWKV_HEAD, LANES), lambda j, t: (0, 0, j)),
        ],
        out_shape=[
            jax.ShapeDtypeStruct((steps, V_HALF, lanes), F32),
            jax.ShapeDtypeStruct((V_HALF, RWKV_HEAD, lanes), F32),
        ],
        scratch_shapes=[pltpu.VMEM((V_HALF, RWKV_HEAD, LANES), F32)],
        compiler_params=_cparams("arbitrary", "arbitrary"),
        name="rwkv_scan",
    )(x5, v_t, s0)


def _rwkv_mix(prep, state, batch, seq, t_blk):
    r, k_mod, v, kk, b, d, _ = prep
    H, N = RWKV_HEADS, RWKV_HEAD
    x5 = jnp.stack([kk, b, d, k_mod, r]).reshape(5, batch, seq, H, N)
    x5 = jnp.transpose(x5, (2, 0, 4, 1, 3)).reshape(seq, 5, N, 1, batch * H)
    x5 = jnp.broadcast_to(x5, (seq, 5, N, 2, batch * H)).reshape(seq, 5, N, 2 * batch * H)
    v_t = jnp.transpose(v.reshape(batch, seq, H, 2, V_HALF), (1, 4, 3, 0, 2)).reshape(seq, V_HALF, 2 * batch * H)
    s0 = jnp.transpose(state.reshape(batch, H, 2, V_HALF, N), (3, 4, 2, 0, 1)).reshape(V_HALF, N, 2 * batch * H)
    o_t, s_fin = _scan(x5, v_t, s0, t_blk)
    o = jnp.transpose(o_t.reshape(seq, V_HALF, 2, batch, H), (3, 0, 4, 2, 1)).reshape(batch * seq, H * N)
    s_new = jnp.transpose(s_fin.reshape(V_HALF, N, 2, batch, H), (3, 4, 2, 0, 1)).reshape(batch, H, N, N)
    return o, s_new


def _out_kernel(om_ref, orw_ref, bonus_ref, rg_ref, gm_ref, x_ref, mod_ref, wout_ref, gpost_ref,
                lnxg_ref, lnxb_ref, *rest, sample):
    if sample:
        wuv_ref, mg_ref, y_ref = rest
        lat = om_ref[...]
        o_mla = jnp.concatenate(
            [jnp.dot(lat[:, h * KV_LORA:(h + 1) * KV_LORA].astype(BF16), wuv_ref[h],
                     preferred_element_type=F32) for h in range(MLA_HEADS)], axis=-1)
        o_mla = (o_mla * mg_ref[...].astype(F32)).astype(BF16)
    else:
        (y_ref,) = rest
        o_mla = om_ref[...]
    ones_bd = _head_ones(RWKV_W, RWKV_HEAD)
    o = orw_ref[...]
    c = o - _head_sum(o, ones_bd) * (1.0 / RWKV_HEAD)
    var = _head_sum(c * c, ones_bd) * (1.0 / RWKV_HEAD)
    on = c * lax.rsqrt(var + GN_EPS) * lnxg_ref[...] + lnxb_ref[...]
    o_rw = ((on + bonus_ref[...]) * rg_ref[...]).astype(BF16)
    mixed = jnp.concatenate([o_mla, o_rw, gm_ref[...]], axis=-1)
    out = jnp.dot(mixed, wout_ref[...], preferred_element_type=F32)
    gate = _rows2d(mod_ref)[:, 2 * D_MODEL:]
    y_ref[...] = x_ref[...] + gate * _rms(out, gpost_ref[...])


def _out_proj(o_mla, o_rw, bonus, rg, gm, x2d, mod, lw, *, tm, seq=None, mg=None):
    rows = x2d.shape[0]
    sample = mg is not None
    const = lambda i: (0, 0)
    row_blk = lambda i: (i, 0)
    if sample:
        mod_spec = pl.BlockSpec((tm, 3 * D_MODEL), row_blk)
    else:
        tiles_per_seq = seq // tm
        mod_spec = pl.BlockSpec((1, 1, 3 * D_MODEL), lambda i: (i // tiles_per_seq, 0, 0))
    in_specs = [
        pl.BlockSpec((tm, o_mla.shape[1]), row_blk),
        pl.BlockSpec((tm, RWKV_W), row_blk),
        pl.BlockSpec((tm, RWKV_W), row_blk),
        pl.BlockSpec((tm, RWKV_W), row_blk),
        pl.BlockSpec((tm, GM_W), row_blk),
        pl.BlockSpec((tm, D_MODEL), row_blk),
        mod_spec,
        pl.BlockSpec((D_MODEL, D_MODEL), const),
        pl.BlockSpec((1, D_MODEL), const),
        pl.BlockSpec((1, RWKV_W), const),
        pl.BlockSpec((1, RWKV_W), const),
    ]
    args = [o_mla, o_rw, bonus, rg, gm, x2d, mod, lw["w_out"], lw["g_post"], lw["lnx_g"], lw["lnx_b"]]
    if sample:
        in_specs += [pl.BlockSpec((MLA_HEADS, KV_LORA, MLA_V), lambda i: (0, 0, 0)),
                     pl.BlockSpec((tm, MLA_W), row_blk)]
        args += [lw["w_uv_h"], mg]
    return pl.pallas_call(
        functools.partial(_out_kernel, sample=sample),
        grid=(rows // tm,),
        in_specs=in_specs,
        out_specs=pl.BlockSpec((tm, D_MODEL), row_blk),
        out_shape=jax.ShapeDtypeStruct((rows, D_MODEL), F32),
        compiler_params=_cparams("arbitrary"),
        name="out_sample" if sample else "out_prompt",
    )(*args)


def _swap_halves(w):
    half = MLA_ROPE // 2
    return jnp.concatenate([w[..., half:], w[..., :half]], axis=-1)


def _rope_tables(pos):
    half = MLA_ROPE // 2
    inv = ROPE_THETA ** (-jnp.arange(half, dtype=F32) / half)
    ang = pos.astype(F32)[:, None] * inv[None, :]
    cos, sin = jnp.cos(ang), jnp.sin(ang)
    zeros = jnp.zeros((pos.shape[0], LANES - MLA_ROPE), F32)
    return (jnp.concatenate([cos, cos, zeros], axis=-1), jnp.concatenate([-sin, sin, zeros], axis=-1))


def _layer_weights(l, w_in, norm_pre_g, norm_post_g, q_norm_g, kv_norm_g, w_uq, w_uk, w_uv, rw_mu, rw_w0,
                   rw_w_decay_up, rw_a0, rw_w_a_up, rw_k_k, rw_k_a, rw_r_k, rw_lnx_g, rw_lnx_b, gm_ln_g,
                   gm_ln_b, gm_w_s, gm_b_s, w_out):
    w = w_in[l]
    split_lo, split_hi = Q_LORA + KV_LORA, Q_LORA + KV_LORA + MLA_ROPE
    kr = w[:, split_lo:split_hi]
    w_in_p = jnp.concatenate([w[:, :split_lo], w[:, split_hi:], kr, _swap_halves(kr)], axis=1).astype(BF16)
    uq = w_uq[l]
    pe = uq[..., MLA_NOPE:]
    w_uq_p = jnp.concatenate([uq, _swap_halves(pe)], axis=-1).reshape(Q_LORA, MLA_HEADS * QHEAD_W).astype(BF16)
    zeros_lora = jnp.zeros((LANES - DECAY_LORA, RWKV_W), F32)
    row = lambda a: a.reshape(1, -1)
    return {
        "w_in": w_in_p, "g_pre": row(norm_pre_g[l]), "g_post": row(norm_post_g[l]),
        "g_q": row(q_norm_g[l]), "g_kv": row(kv_norm_g[l]), "w_uq": w_uq_p,
        "w_uk": w_uk[l].reshape(KV_LORA, MLA_W).astype(BF16),
        "w_uv": w_uv[l].reshape(KV_LORA, MLA_W).astype(BF16),
        "w_ukt": jnp.transpose(w_uk[l], (1, 2, 0)).astype(BF16),
        "w_uv_h": jnp.transpose(w_uv[l], (1, 0, 2)).astype(BF16),
        "rw_mu": row(rw_mu[l]), "rw_w0": row(rw_w0[l]), "rw_a0": row(rw_a0[l]),
        "rw_wd": jnp.concatenate([rw_w_decay_up[l], zeros_lora], axis=0).astype(BF16),
        "rw_wa": jnp.concatenate([zeros_lora, rw_w_a_up[l]], axis=0).astype(BF16),
        "rw_k_k": row(rw_k_k[l]), "rw_k_a": row(rw_k_a[l]), "rw_r_k": row(rw_r_k[l]),
        "lnx_g": row(rw_lnx_g[l]), "lnx_b": row(rw_lnx_b[l]),
        "ln_g": row(gm_ln_g[l]), "ln_b": row(gm_ln_b[l]),
        "gm_ws": gm_w_s[l],
        "gm_bias": jnp.repeat(gm_b_s[l].T, GM_GROUP_W, axis=1),
        "gm_coef0": row(jnp.repeat(gm_w_s[l][:, 0, 0], GM_GROUP_W)),
        "gm_bias0": row(jnp.repeat(gm_b_s[l][:, 0], GM_GROUP_W)),
        "w_out": w_out[l].astype(BF16),
    }


def _pick(full, want):
    return want if full % want == 0 else full


def kernel(x_prompt, x_sample, c_prompt, c_sample, cache_ckv, cache_kpe, page_table, state_wkv, state_shift, w_ada, b_ada, norm_pre_g, norm_post_g, w_in, q_norm_g, kv_norm_g, w_uq, w_uk, w_uv, rw_mu, rw_w0, rw_w_decay_up, rw_a0, rw_w_a_up, rw_k_k, rw_k_a, rw_r_k, rw_lnx_g, rw_lnx_b, gm_ln_g, gm_ln_b, gm_w_s, gm_b_s, w_out):
    batch, seq, _ = x_prompt.shape
    dec_batch, dec_seq, _ = x_sample.shape
    depth = w_in.shape[0]
    n_pages = page_table.shape[1]
    past_len = n_pages * cache_ckv.shape[2]
    assert dec_seq == 1 and seq % CHUNK == 0

    tm_proj = _pick(seq, 256)
    tm_out = _pick(seq, 512)
    tm_prep = _pick(seq, 512)
    tq = _pick(seq, 512)
    t_blk = _pick(seq, 32)
    pages_per_step = _pick(n_pages, 16)

    mod = _ada_mod(jnp.concatenate([c_prompt, c_sample], axis=0), w_ada, b_ada)
    cs_p, sn_p = _rope_tables(jnp.arange(seq))
    cs_s, sn_s = _rope_tables(jnp.full((dec_batch,), past_len))

    y_p = x_prompt.reshape(batch * seq, D_MODEL)
    y_s = x_sample.reshape(dec_batch, D_MODEL)
    zero_state = jnp.zeros((batch, RWKV_HEADS, RWKV_HEAD, RWKV_HEAD), F32)
    outs = {k: [] for k in ("ckv_p", "kpe_p", "ckv_s", "kpe_s", "wkv_p", "wkv_s", "sh_p", "sh_s", "vc_p", "vc_s")}
    for l in range(depth):
        lw = _layer_weights(l, w_in, norm_pre_g, norm_post_g, q_norm_g, kv_norm_g, w_uq, w_uk, w_uv, rw_mu,
                            rw_w0, rw_w_decay_up, rw_a0, rw_w_a_up, rw_k_k, rw_k_a, rw_r_k, rw_lnx_g,
                            rw_lnx_b, gm_ln_g, gm_ln_b, gm_w_s, gm_b_s, w_out)
        mod_p = mod[l, :batch].reshape(batch, 1, 3 * D_MODEL)
        q, k, v, mg, ckv, kpe, rw_in, rg, gm, vn_last = _proj_prompt(y_p, mod_p, lw, cs_p, sn_p, batch, seq,
                                                                     tm_proj)
        o_mla = _flash(q, k, v, mg, batch, seq, tq)
        prep = _rwkv_prep(rw_in, lw, seq=seq, tm=tm_prep)
        o_rw, wkv_new = _rwkv_mix(prep, zero_state, batch, seq, t_blk)
        y_p = _out_proj(o_mla, o_rw, prep[6], rg, gm, y_p, mod_p, lw, tm=tm_out, seq=seq)
        outs["ckv_p"].append(ckv.reshape(batch, seq, KV_LORA))
        outs["kpe_p"].append(kpe.reshape(batch, seq, MLA_ROPE))
        outs["wkv_p"].append(wkv_new)
        outs["sh_p"].append(rw_in.reshape(batch, seq, SHIFT_W)[:, -1])
        outs["vc_p"].append(vn_last)

        mod_s = mod[l, batch:]
        qa, qp, mg_s, ckv_s, kpe_s, rw_s, rg_s, gm_s, vn_s = _proj_sample(y_s, mod_s, lw, cs_s, sn_s)
        o_lat = _paged_attention(qa, qp, ckv_s, kpe_s, cache_ckv, cache_kpe, page_table, l, pages_per_step)
        prep_s = _rwkv_prep(rw_s, lw, shift_rows=state_shift[l])
        o_rw_s, wkv_s = _rwkv_mix(prep_s, state_wkv[l], dec_batch, 1, 1)
        y_s = _out_proj(o_lat, o_rw_s, prep_s[6], rg_s, gm_s, y_s, mod_s, lw, tm=dec_batch, mg=mg_s)
        outs["ckv_s"].append(ckv_s.reshape(dec_batch, 1, KV_LORA))
        outs["kpe_s"].append(kpe_s[:, :MLA_ROPE].reshape(dec_batch, 1, MLA_ROPE))
        outs["wkv_s"].append(wkv_s)
        outs["sh_s"].append(rw_s)
        outs["vc_s"].append(vn_s.reshape(dec_batch, 1, GM_W))

    st = lambda name: jnp.stack(outs[name])
    return (y_p.reshape(batch, seq, D_MODEL), y_s.reshape(dec_batch, 1, D_MODEL),
            st("ckv_p"), st("kpe_p"), st("ckv_s"), st("kpe_s"), st("wkv_p"), st("wkv_s"),
            st("sh_p"), st("sh_s"), st("vc_p"), st("vc_s"))
```

```python
import functools

import numpy as np
import jax
import jax.numpy as jnp
from jax import lax
from jax.experimental import pallas as pl
from jax.experimental.pallas import tpu as pltpu

F32 = jnp.float32
BF16 = jnp.bfloat16

D_MODEL = 1024
MLA_V = 128
MLA_W = D_MODEL // 2
MLA_HEADS = MLA_W // MLA_V
MLA_NOPE = 128
MLA_ROPE = 64
MLA_QK = MLA_NOPE + MLA_ROPE
Q_LORA = (3 * D_MODEL) // 8
KV_LORA = D_MODEL // 4
ROPE_THETA = 10000.0
ATTN_SCALE = MLA_QK ** -0.5
RWKV_W = D_MODEL // 4
RWKV_HEAD = 64
RWKV_HEADS = RWKV_W // RWKV_HEAD
DECAY_LORA = 64
ICLR_LORA = 64
SHIFT_W = 3 * RWKV_W + DECAY_LORA + ICLR_LORA
GN_EPS = 64e-5
GM_W = D_MODEL // 4
GM_GROUPS = 4
GM_GROUP_W = GM_W // GM_GROUPS
CHUNK = 128
RMS_EPS = 1e-6
LN_EPS = 1e-5

LANES = 128
SUBLANES = 8
VMEM_LIMIT_BYTES = 56 * 1024 * 1024

QHEAD_W = 2 * LANES
OFF_Q = 0
OFF_KV = OFF_Q + Q_LORA
OFF_MG = OFF_KV + KV_LORA
OFF_RW = OFF_MG + MLA_W
OFF_RG = OFF_RW + SHIFT_W
OFF_GU = OFF_RG + RWKV_W
OFF_GV = OFF_GU + GM_W
OFF_GG = OFF_GV + GM_W
OFF_KR = OFF_GG + GM_W
IN_W_P = OFF_KR + LANES

NEG_BIG = -1e30


def _cparams(*sem):
    return pltpu.CompilerParams(dimension_semantics=sem, vmem_limit_bytes=VMEM_LIMIT_BYTES)


def _silu(x):
    return x * jax.nn.sigmoid(x)


def _rms(x, g, eps=RMS_EPS):
    return x * lax.rsqrt(jnp.mean(x * x, axis=-1, keepdims=True) + eps) * g


def _rows2d(ref):
    m = ref[...]
    return m.reshape(m.shape[-2], m.shape[-1])


def _group_index(idx, group):
    shift = group.bit_length() - 1
    assert 1 << shift == group
    return lax.shift_right_logical(idx, shift)


def _head_ones(width, head):
    r = _group_index(lax.broadcasted_iota(jnp.int32, (width, width), 0), head)
    c = _group_index(lax.broadcasted_iota(jnp.int32, (width, width), 1), head)
    return jnp.where(r == c, 1.0, 0.0).astype(BF16)


def _head_sum(x, ones_bd):
    hi = x.astype(BF16)
    lo = (x - hi.astype(F32)).astype(BF16)
    return (jnp.dot(hi, ones_bd, preferred_element_type=F32)
            + jnp.dot(lo, ones_bd, preferred_element_type=F32))


def _rope_pair(blk, cs, sn):
    return blk * cs + pltpu.roll(blk, MLA_ROPE, 1) * sn


def _ada_kernel(c_ref, w_ref, b_ref, o_ref):
    c = _silu(c_ref[...]).astype(BF16)
    o_ref[0] = jnp.dot(c, w_ref[0], preferred_element_type=F32) + b_ref[0]


def _ada_mod(c_all, w_ada, b_ada):
    depth = w_ada.shape[0]
    n_rows = c_all.shape[0]
    n_tiles = (3 * D_MODEL) // D_MODEL
    return pl.pallas_call(
        _ada_kernel,
        grid=(depth, n_tiles),
        in_specs=[
            pl.BlockSpec((n_rows, D_MODEL), lambda l, j: (0, 0)),
            pl.BlockSpec((1, D_MODEL, D_MODEL), lambda l, j: (l, 0, j)),
            pl.BlockSpec((1, 1, D_MODEL), lambda l, j: (l, 0, j)),
        ],
        out_specs=pl.BlockSpec((1, n_rows, D_MODEL), lambda l, j: (l, 0, j)),
        out_shape=jax.ShapeDtypeStruct((depth, n_rows, 3 * D_MODEL), F32),
        compiler_params=_cparams("arbitrary", "arbitrary"),
        name="ada_mod",
    )(c_all, w_ada.astype(BF16), b_ada.reshape(depth, 1, 3 * D_MODEL))


def _proj_common(x_ref, mod_ref, gpre_ref, win_ref):
    mod = _rows2d(mod_ref)
    shift = mod[:, 0:D_MODEL]
    scale = mod[:, D_MODEL:2 * D_MODEL]
    h = _rms(x_ref[...], gpre_ref[...]) * (1.0 + scale) + shift
    hb = h.astype(BF16)

    def seg(off, width):
        return jnp.dot(hb, win_ref[:, off:off + width], preferred_element_type=F32)

    return seg


def _q_heads(seg, gq_ref, wuq_ref, cs, sn):
    qn = _rms(seg(OFF_Q, Q_LORA), gq_ref[...]).astype(BF16)
    q_all = jnp.dot(qn, wuq_ref[...], preferred_element_type=F32)
    heads = []
    for h in range(MLA_HEADS):
        nope = q_all[:, h * QHEAD_W:h * QHEAD_W + LANES] * ATTN_SCALE
        pe = _rope_pair(q_all[:, h * QHEAD_W + LANES:(h + 1) * QHEAD_W], cs, sn) * ATTN_SCALE
        heads.append((nope, pe))
    return heads


def _gm_norm(seg, lng_ref, lnb_ref):
    v = seg(OFF_GV, GM_W)
    mu = jnp.mean(v, axis=-1, keepdims=True)
    c = v - mu
    var = jnp.mean(c * c, axis=-1, keepdims=True)
    return c * lax.rsqrt(var + LN_EPS) * lng_ref[...] + lnb_ref[...]


def _proj_prompt_kernel(x_ref, mod_ref, gpre_ref, win_ref, gq_ref, wuq_ref, gkv_ref, wuk_ref, wuv_ref,
                        cs_ref, sn_ref, lng_ref, lnb_ref, ws_ref, bs_ref,
                        q_ref, k_ref, v_ref, mg_ref, ckv_ref, kpe_ref, rw_ref, rg_ref, gm_ref, vn_ref,
                        *, tiles_per_seq):
    seg = _proj_common(x_ref, mod_ref, gpre_ref, win_ref)
    cs = cs_ref[...]
    sn = sn_ref[...]
    tm = x_ref.shape[0]

    for h, (nope, pe) in enumerate(_q_heads(seg, gq_ref, wuq_ref, cs, sn)):
        q_ref[:, h * QHEAD_W:h * QHEAD_W + LANES] = nope.astype(BF16)
        q_ref[:, h * QHEAD_W + LANES:(h + 1) * QHEAD_W] = pe.astype(BF16)
    ckv = _rms(seg(OFF_KV, KV_LORA), gkv_ref[...])
    ckv_ref[...] = ckv
    kpe = _rope_pair(seg(OFF_KR, LANES), cs, sn)
    kpe_ref[...] = kpe[:, :MLA_ROPE]
    ckv_b = ckv.astype(BF16)
    k_nope = jnp.dot(ckv_b, wuk_ref[...], preferred_element_type=F32)
    v_ref[...] = jnp.dot(ckv_b, wuv_ref[...], preferred_element_type=F32).astype(BF16)
    kpe_b = kpe.astype(BF16)
    for h in range(MLA_HEADS):
        k_ref[:, h * QHEAD_W:h * QHEAD_W + LANES] = k_nope[:, h * MLA_NOPE:(h + 1) * MLA_NOPE].astype(BF16)
        k_ref[:, h * QHEAD_W + LANES:(h + 1) * QHEAD_W] = kpe_b
    mg_ref[...] = _silu(seg(OFF_MG, MLA_W)).astype(BF16)

    rw_ref[...] = seg(OFF_RW, SHIFT_W)
    rg_ref[...] = _silu(seg(OFF_RG, RWKV_W))

    vn = _gm_norm(seg, lng_ref, lnb_ref)
    vn_b = vn.astype(BF16)
    gate_u = _silu(seg(OFF_GG, GM_W)) * seg(OFF_GU, GM_W)
    row = lax.broadcasted_iota(jnp.int32, (CHUNK, CHUNK), 0)
    col = lax.broadcasted_iota(jnp.int32, (CHUNK, CHUNK), 1)
    lane_group = _group_index(lax.broadcasted_iota(jnp.int32, (CHUNK, GM_W), 1), GM_GROUP_W)
    w_tril = [jnp.where(row >= col, ws_ref[g], 0.0).astype(BF16) for g in range(GM_GROUPS)]
    for c in range(tm // CHUNK):
        vc = vn_b[c * CHUNK:(c + 1) * CHUNK]
        z = bs_ref[...]
        for g in range(GM_GROUPS):
            zg = jnp.dot(w_tril[g], vc, preferred_element_type=F32)
            z = z + jnp.where(lane_group == g, zg, 0.0)
        gm_ref[c * CHUNK:(c + 1) * CHUNK, :] = (gate_u[c * CHUNK:(c + 1) * CHUNK] * z).astype(BF16)

    @pl.when(pl.program_id(0) % tiles_per_seq == tiles_per_seq - 1)
    def _():
        vn_ref[0] = vn[tm - CHUNK:, :]


def _proj_prompt(x2d, mod_b, lw, cs, sn, batch, seq, tm):
    rows = x2d.shape[0]
    tiles_per_seq = seq // tm
    const = lambda i: (0, 0)
    row_blk = lambda i: (i, 0)
    outs = [
        (QHEAD_W * MLA_HEADS, BF16), (QHEAD_W * MLA_HEADS, BF16), (MLA_W, BF16), (MLA_W, BF16),
        (KV_LORA, F32), (MLA_ROPE, F32), (SHIFT_W, F32), (RWKV_W, F32), (GM_W, BF16),
    ]
    out_shape = [jax.ShapeDtypeStruct((rows, w), dt) for w, dt in outs]
    out_specs = [pl.BlockSpec((tm, w), row_blk) for w, _ in outs]
    out_shape.append(jax.ShapeDtypeStruct((batch, CHUNK, GM_W), F32))
    out_specs.append(pl.BlockSpec((1, CHUNK, GM_W), lambda i: (i // tiles_per_seq, 0, 0)))
    return pl.pallas_call(
        functools.partial(_proj_prompt_kernel, tiles_per_seq=tiles_per_seq),
        grid=(rows // tm,),
        in_specs=[
            pl.BlockSpec((tm, D_MODEL), row_blk),
            pl.BlockSpec((1, 1, 3 * D_MODEL), lambda i: (i // tiles_per_seq, 0, 0)),
            pl.BlockSpec((1, D_MODEL), const),
            pl.BlockSpec((D_MODEL, IN_W_P), const),
            pl.BlockSpec((1, Q_LORA), const),
            pl.BlockSpec((Q_LORA, QHEAD_W * MLA_HEADS), const),
            pl.BlockSpec((1, KV_LORA), const),
            pl.BlockSpec((KV_LORA, MLA_W), const),
            pl.BlockSpec((KV_LORA, MLA_W), const),
            pl.BlockSpec((tm, LANES), lambda i: (i % tiles_per_seq, 0)),
            pl.BlockSpec((tm, LANES), lambda i: (i % tiles_per_seq, 0)),
            pl.BlockSpec((1, GM_W), const),
            pl.BlockSpec((1, GM_W), const),
            pl.BlockSpec((GM_GROUPS, CHUNK, CHUNK), lambda i: (0, 0, 0)),
            pl.BlockSpec((CHUNK, GM_W), const),
        ],
        out_specs=out_specs,
        out_shape=out_shape,
        compiler_params=_cparams("arbitrary"),
        name="proj_prompt",
    )(x2d, mod_b, lw["g_pre"], lw["w_in"], lw["g_q"], lw["w_uq"], lw["g_kv"], lw["w_uk"], lw["w_uv"],
      cs, sn, lw["ln_g"], lw["ln_b"], lw["gm_ws"], lw["gm_bias"])


def _proj_sample_kernel(x_ref, mod_ref, gpre_ref, win_ref, gq_ref, wuq_ref, gkv_ref, wukt_ref,
                        cs_ref, sn_ref, lng_ref, lnb_ref, coef_ref, bias_ref,
                        qa_ref, qp_ref, mg_ref, ckv_ref, kpe_ref, rw_ref, rg_ref, gm_ref, vn_ref):
    seg = _proj_common(x_ref, mod_ref, gpre_ref, win_ref)
    cs = cs_ref[...]
    sn = sn_ref[...]
    for h, (nope, pe) in enumerate(_q_heads(seg, gq_ref, wuq_ref, cs, sn)):
        qa_ref[:, h * KV_LORA:(h + 1) * KV_LORA] = jnp.dot(nope.astype(BF16), wukt_ref[h],
                                                           preferred_element_type=F32)
        qp_ref[:, h * LANES:(h + 1) * LANES] = pe
    ckv_ref[...] = _rms(seg(OFF_KV, KV_LORA), gkv_ref[...])
    kpe_ref[...] = _rope_pair(seg(OFF_KR, LANES), cs, sn)
    mg_ref[...] = _silu(seg(OFF_MG, MLA_W)).astype(BF16)
    rw_ref[...] = seg(OFF_RW, SHIFT_W)
    rg_ref[...] = _silu(seg(OFF_RG, RWKV_W))
    vn = _gm_norm(seg, lng_ref, lnb_ref)
    vn_ref[...] = vn
    z = vn * coef_ref[...] + bias_ref[...]
    gm_ref[...] = (_silu(seg(OFF_GG, GM_W)) * seg(OFF_GU, GM_W) * z).astype(BF16)


def _proj_sample(x2d, mod_rows, lw, cs, sn):
    rows = x2d.shape[0]
    const = lambda i: (0, 0)
    outs = [
        (KV_LORA * MLA_HEADS, F32), (LANES * MLA_HEADS, F32), (MLA_W, BF16), (KV_LORA, F32), (LANES, F32),
        (SHIFT_W, F32), (RWKV_W, F32), (GM_W, BF16), (GM_W, F32),
    ]
    return pl.pallas_call(
        _proj_sample_kernel,
        grid=(1,),
        in_specs=[
            pl.BlockSpec((rows, D_MODEL), const),
            pl.BlockSpec((rows, 3 * D_MODEL), const),
            pl.BlockSpec((1, D_MODEL), const),
            pl.BlockSpec((D_MODEL, IN_W_P), const),
            pl.BlockSpec((1, Q_LORA), const),
            pl.BlockSpec((Q_LORA, QHEAD_W * MLA_HEADS), const),
            pl.BlockSpec((1, KV_LORA), const),
            pl.BlockSpec((MLA_HEADS, MLA_NOPE, KV_LORA), lambda i: (0, 0, 0)),
            pl.BlockSpec((rows, LANES), const),
            pl.BlockSpec((rows, LANES), const),
            pl.BlockSpec((1, GM_W), const),
            pl.BlockSpec((1, GM_W), const),
            pl.BlockSpec((1, GM_W), const),
            pl.BlockSpec((1, GM_W), const),
        ],
        out_specs=[pl.BlockSpec((rows, w), const) for w, _ in outs],
        out_shape=[jax.ShapeDtypeStruct((rows, w), dt) for w, dt in outs],
        compiler_params=_cparams("arbitrary"),
        name="proj_sample",
    )(x2d, mod_rows, lw["g_pre"], lw["w_in"], lw["g_q"], lw["w_uq"], lw["g_kv"], lw["w_ukt"],
      cs, sn, lw["ln_g"], lw["ln_b"], lw["gm_coef0"], lw["gm_bias0"])


def _flash_kernel(q_ref, k_ref, v_ref, g_ref, o_ref, *, tq):
    seq = q_ref.shape[1]
    row = lax.broadcasted_iota(jnp.int32, (tq, tq), 0)
    col = lax.broadcasted_iota(jnp.int32, (tq, tq), 1)
    for qi in range(seq // tq):
        rows = slice(qi * tq, (qi + 1) * tq)
        q = q_ref[0, rows, :]
        m = jnp.full((tq, 1), NEG_BIG, F32)
        l = jnp.zeros((tq, 1), F32)
        acc = jnp.zeros((tq, MLA_V), F32)
        for ki in range(qi + 1):
            cols = slice(ki * tq, (ki + 1) * tq)
            s = lax.dot_general(q, k_ref[0, cols, :], (((1,), (1,)), ((), ())),
                                preferred_element_type=F32)
            if ki == qi:
                s = jnp.where(row >= col, s, NEG_BIG)
            m_new = jnp.maximum(m, jnp.max(s, axis=-1, keepdims=True))
            alpha = jnp.exp(m - m_new)
            p = jnp.exp(s - m_new)
            l = alpha * l + jnp.sum(p, axis=-1, keepdims=True)
            acc = alpha * acc + jnp.dot(p.astype(BF16), v_ref[0, cols, :], preferred_element_type=F32)
            m = m_new
        o_ref[0, rows, :] = (acc / l * g_ref[0, rows, :].astype(F32)).astype(BF16)


def _flash(q, k, v, g, batch, seq, tq):
    q3 = q.reshape(batch, seq, QHEAD_W * MLA_HEADS)
    k3 = k.reshape(batch, seq, QHEAD_W * MLA_HEADS)
    v3 = v.reshape(batch, seq, MLA_W)
    g3 = g.reshape(batch, seq, MLA_W)
    head_blk = lambda b, h: (b, 0, h)
    out = pl.pallas_call(
        functools.partial(_flash_kernel, tq=tq),
        grid=(batch, MLA_HEADS),
        in_specs=[
            pl.BlockSpec((1, seq, QHEAD_W), head_blk),
            pl.BlockSpec((1, seq, QHEAD_W), head_blk),
            pl.BlockSpec((1, seq, MLA_V), head_blk),
            pl.BlockSpec((1, seq, MLA_V), head_blk),
        ],
        out_specs=pl.BlockSpec((1, seq, MLA_V), head_blk),
        out_shape=jax.ShapeDtypeStruct((batch, seq, MLA_W), BF16),
        compiler_params=_cparams("arbitrary", "arbitrary"),
        name="flash_prompt",
    )(q3, k3, v3, g3)
    return out.reshape(batch * seq, MLA_W)


def _paged_kernel(pt_ref, qa_ref, qp_ref, cn_ref, kn_ref, *rest, pages_per_step):
    del pt_ref
    ckv_pages = rest[:pages_per_step]
    kpe_pages = rest[pages_per_step:2 * pages_per_step]
    o_ref, m_ref, l_ref, acc_ref = rest[2 * pages_per_step:]
    j = pl.program_id(1)

    @pl.when(j == 0)
    def _():
        m_ref[...] = jnp.full(m_ref.shape, NEG_BIG, F32)
        l_ref[...] = jnp.zeros(l_ref.shape, F32)
        acc_ref[...] = jnp.zeros(acc_ref.shape, F32)

    qa = qa_ref[0]
    qp = qp_ref[0][:, :MLA_ROPE]
    ckv = jnp.concatenate([r[0, 0] for r in ckv_pages], axis=0).astype(BF16)
    kpe_t = jnp.concatenate([r[0, 0] for r in kpe_pages], axis=1).astype(BF16)
    s = (lax.dot_general(qa.astype(BF16), ckv, (((1,), (1,)), ((), ())), preferred_element_type=F32)
         + jnp.dot(qp.astype(BF16), kpe_t, preferred_element_type=F32))
    m_old = m_ref[...]
    m_new = jnp.maximum(m_old, jnp.max(s, axis=-1, keepdims=True))
    alpha = jnp.exp(m_old - m_new)
    p = jnp.exp(s - m_new)
    l_ref[...] = alpha * l_ref[...] + jnp.sum(p, axis=-1, keepdims=True)
    acc_ref[...] = alpha * acc_ref[...] + jnp.dot(p.astype(BF16), ckv, preferred_element_type=F32)
    m_ref[...] = m_new

    @pl.when(j == pl.num_programs(1) - 1)
    def _():
        cn = cn_ref[0]
        kn = kn_ref[0][:, :MLA_ROPE]
        s_new = (jnp.sum(qa * cn, axis=-1, keepdims=True)
                 + jnp.sum(qp * kn, axis=-1, keepdims=True))
        m_old2 = m_ref[...]
        m_fin = jnp.maximum(m_old2, s_new)
        a2 = jnp.exp(m_old2 - m_fin)
        p_new = jnp.exp(s_new - m_fin)
        l_fin = a2 * l_ref[...] + p_new
        o_ref[0] = (a2 * acc_ref[...] + p_new * cn) / l_fin


def _paged_attention(qa, qp, ckv_new, kpe_new, cache_ckv, cache_kpe_t, page_table, layer, pages_per_step):
    dec_batch, n_pages = page_table.shape
    page = cache_ckv.shape[2]
    head_pad = ((0, 0), (0, SUBLANES - MLA_HEADS), (0, 0))
    qa3 = jnp.pad(qa.reshape(dec_batch, MLA_HEADS, KV_LORA), head_pad)
    qp3 = jnp.pad(qp.reshape(dec_batch, MLA_HEADS, LANES), head_pad)
    cn3 = ckv_new.reshape(dec_batch, 1, KV_LORA)
    kn3 = kpe_new.reshape(dec_batch, 1, LANES)
    row_blk = lambda b, j, pt: (b, 0, 0)

    def page_map(i):
        return lambda b, j, pt: (layer, pt[b, j * pages_per_step + i], 0, 0)

    in_specs = [
        pl.BlockSpec((1, SUBLANES, KV_LORA), row_blk),
        pl.BlockSpec((1, SUBLANES, LANES), row_blk),
        pl.BlockSpec((1, 1, KV_LORA), row_blk),
        pl.BlockSpec((1, 1, LANES), row_blk),
    ]
    in_specs += [pl.BlockSpec((1, 1, page, KV_LORA), page_map(i)) for i in range(pages_per_step)]
    in_specs += [pl.BlockSpec((1, 1, MLA_ROPE, page), page_map(i)) for i in range(pages_per_step)]
    out = pl.pallas_call(
        functools.partial(_paged_kernel, pages_per_step=pages_per_step),
        grid_spec=pltpu.PrefetchScalarGridSpec(
            num_scalar_prefetch=1,
            grid=(dec_batch, n_pages // pages_per_step),
            in_specs=in_specs,
            out_specs=pl.BlockSpec((1, SUBLANES, KV_LORA), row_blk),
            scratch_shapes=[
                pltpu.VMEM((SUBLANES, 1), F32),
                pltpu.VMEM((SUBLANES, 1), F32),
                pltpu.VMEM((SUBLANES, KV_LORA), F32),
            ],
        ),
        out_shape=jax.ShapeDtypeStruct((dec_batch, SUBLANES, KV_LORA), F32),
        compiler_params=_cparams("arbitrary", "arbitrary"),
        name="paged_attention",
    )(page_table, qa3, qp3, cn3, kn3, *([cache_ckv] * pages_per_step), *([cache_kpe_t] * pages_per_step))
    return out[:, :MLA_HEADS].reshape(dec_batch, MLA_HEADS * KV_LORA)


def _rwkv_prep_values(p, prev, mu_ref, w0_ref, wd_ref, a0_ref, wa_ref, kk_ref, ka_ref, rk_ref):
    xm = p + (prev - p) * mu_ref[...]
    r = xm[:, 0:RWKV_W]
    k = xm[:, RWKV_W:2 * RWKV_W]
    v = xm[:, 2 * RWKV_W:3 * RWKV_W]
    tail = xm[:, 3 * RWKV_W:]
    lane = lax.broadcasted_iota(jnp.int32, tail.shape, 1)
    lora_in = jnp.where(lane < DECAY_LORA, jnp.tanh(tail), tail).astype(BF16)
    dw = jnp.dot(lora_in, wd_ref[...], preferred_element_type=F32)
    da = jnp.dot(lora_in, wa_ref[...], preferred_element_type=F32)
    z = -(w0_ref[...] + dw)
    softplus = jnp.maximum(z, 0.0) + jnp.log(1.0 + jnp.exp(-jnp.abs(z)))
    w = -softplus - 0.5
    log_decay = -jnp.exp(w)
    a = jax.nn.sigmoid(a0_ref[...] + da)
    ones_bd = _head_ones(RWKV_W, RWKV_HEAD)
    kk = k * kk_ref[...]
    kk = kk / jnp.maximum(jnp.sqrt(_head_sum(kk * kk, ones_bd)), 1e-12)
    k_mod = k * (1.0 + (a - 1.0) * ka_ref[...])
    bonus = _head_sum(r * k_mod * rk_ref[...], ones_bd) * v
    return r, k_mod, v, kk, kk * a, log_decay, bonus


def _rwkv_param_specs(index_map):
    widths = (SHIFT_W, RWKV_W, None, RWKV_W, None, RWKV_W, RWKV_W, RWKV_W)
    return [pl.BlockSpec((LANES, RWKV_W) if w is None else (1, w), index_map) for w in widths]


def _rwkv_params(lw):
    return (lw["rw_mu"], lw["rw_w0"], lw["rw_wd"], lw["rw_a0"], lw["rw_wa"], lw["rw_k_k"], lw["rw_k_a"],
            lw["rw_r_k"])


def _rwkv_prep_sample_kernel(p_ref, prev_ref, mu_ref, w0_ref, wd_ref, a0_ref, wa_ref, kk_ref, ka_ref, rk_ref,
                             r_o, k_o, v_o, kk_o, b_o, d_o, bonus_o):
    r, k_mod, v, kk, b, log_decay, bonus = _rwkv_prep_values(
        p_ref[...], prev_ref[...], mu_ref, w0_ref, wd_ref, a0_ref, wa_ref, kk_ref, ka_ref, rk_ref)
    r_o[...] = r
    k_o[...] = k_mod
    v_o[...] = v
    kk_o[...] = kk
    b_o[...] = b
    d_o[...] = jnp.exp(log_decay)
    bonus_o[...] = bonus


def _rwkv_prep_sample(rw_in, lw, shift_rows):
    rows = rw_in.shape[0]
    const = lambda i: (0, 0)
    return pl.pallas_call(
        _rwkv_prep_sample_kernel,
        grid=(1,),
        in_specs=[pl.BlockSpec((rows, SHIFT_W), const)] * 2 + _rwkv_param_specs(const),
        out_specs=[pl.BlockSpec((rows, RWKV_W), const)] * 7,
        out_shape=[jax.ShapeDtypeStruct((rows, RWKV_W), F32)] * 7,
        compiler_params=_cparams("arbitrary"),
        name="rwkv_prep_sample",
    )(rw_in, shift_rows, *_rwkv_params(lw))


RW_CHUNK = 64
RW_SUB = 16


def _split3(x):
    hi = x.astype(BF16)
    r1 = x - hi.astype(F32)
    mid = r1.astype(BF16)
    lo = (r1 - mid.astype(F32)).astype(BF16)
    return hi, mid, lo


def _block_diag(y, bd_mask):
    return jnp.where(bd_mask, jnp.concatenate([y] * RWKV_HEADS, axis=0), 0.0).astype(BF16)


def _mm(x, y_bd):
    return jnp.dot(x.astype(BF16), y_bd, preferred_element_type=F32)


def _rwkv_chunks(vals, n0, masks):
    bd_mask, tri_incl, strict, incl, same_sub, eye_tiled, ones_bd = masks
    C = RW_CHUNK
    seqs = range(len(vals))
    bd = lambda y: _block_diag(y, bd_mask)
    r, kx, v, kap, bb, lam = ([val[j] for val in vals] for j in range(6))
    cum = [sum(jnp.dot(tri_incl, part, preferred_element_type=F32) for part in _split3(lam[i])) for i in seqs]
    cum_last = [c[C - 1:C, :] for c in cum]
    e_neg = [jnp.exp(-c) for c in cum]
    ap = [jnp.concatenate([kap[i] * jnp.exp(cum[i] - lam[i]), r[i] * jnp.exp(cum[i])], axis=0).astype(BF16)
          for i in seqs]
    rhs = [jnp.concatenate([bd(bb[i] * e_neg[i]), bd(kx[i] * e_neg[i])], axis=0) for i in seqs]
    g = [lax.dot_general(ap[i], rhs[i], (((1,), (1,)), ((), ())), preferred_element_type=F32) for i in seqs]
    l_ab = [jnp.where(strict, gi[:C, :RWKV_W], 0.0) for gi in g]
    l_ak = [jnp.where(strict, gi[:C, RWKV_W:], 0.0) for gi in g]
    q_pb = [jnp.where(incl, gi[C:, :RWKV_W], 0.0) for gi in g]
    q_pk = [jnp.where(incl, gi[C:, RWKV_W:], 0.0) for gi in g]
    v_bd = [bd(vi) for vi in v]
    apn = [jnp.dot(ap[i], bd(n0[i]), preferred_element_type=F32) for i in seqs]
    z_w = [apn[i][:C] + _mm(l_ak[i], v_bd[i]) for i in seqs]

    l_d = [jnp.where(same_sub, li, 0.0) for li in l_ab]
    z_l = [l_ab[i] - l_d[i] for i in seqs]
    x = [(-li).astype(BF16) for li in l_d]
    power = l_d
    n_double = RW_SUB.bit_length() - 1
    for step in range(n_double):
        z_l = [z_l[i] + jnp.dot(x[i], bd(z_l[i]), preferred_element_type=F32) for i in seqs]
        z_w = [z_w[i] + jnp.dot(x[i], bd(z_w[i]), preferred_element_type=F32) for i in seqs]
        if step + 1 < n_double:
            power = [_mm(pw, bd(pw)) for pw in power]
            x = [pw.astype(BF16) for pw in power]
    n = z_l
    n_sq = [_mm(ni, bd(ni)) for ni in n]
    y = [z_w[i] + _mm(n_sq[i], bd(z_w[i])) for i in seqs]
    u = [_mm(n[i], bd(y[i])) - y[i] for i in seqs]

    o = [apn[i][C:] + _mm(q_pb[i], bd(u[i])) + _mm(q_pk[i], v_bd[i]) for i in seqs]

    e_end = [jnp.exp(cum_last[i] - cum[i]) for i in seqs]
    lhs_t = [jnp.concatenate([bb[i] * e_end[i], kx[i] * e_end[i]], axis=0).T.astype(BF16) for i in seqs]
    full = [jnp.dot(lhs_t[i], jnp.concatenate([u[i], v[i]], axis=0).astype(BF16), preferred_element_type=F32)
            for i in seqs]
    heads = [slice(h * RWKV_HEAD, (h + 1) * RWKV_HEAD) for h in range(RWKV_HEADS)]
    delta = [sum(jnp.where(bd_mask[hs], f[hs], 0.0) for hs in heads) for f in full]
    g_wide = [sum(jnp.dot(part, ones_bd, preferred_element_type=F32)
                  for part in _split3(jnp.where(eye_tiled, jnp.exp(cl), 0.0))) for cl in cum_last]
    return o, [n0[i] * g_wide[i] + delta[i] for i in seqs]


def _rwkv_prompt_kernel(p_ref, rg_ref, mu_ref, w0_ref, wd_ref, a0_ref, wa_ref, kk_ref, ka_ref, rk_ref,
                        lnxg_ref, lnxb_ref, o_ref, sfin_ref, state_ref, last_ref):
    nb, C, _ = p_ref.shape
    c_idx = pl.program_id(1)

    @pl.when(c_idx == 0)
    def _():
        state_ref[...] = jnp.zeros(state_ref.shape, F32)
        last_ref[...] = jnp.zeros(last_ref.shape, F32)

    p = p_ref[...].reshape(nb * C, SHIFT_W)
    row = lax.broadcasted_iota(jnp.int32, (nb * C, 1), 0)
    prev = pltpu.roll(p, 1, 0)
    for i in range(nb):
        prev = jnp.where(row == i * C, last_ref[i, 0:1, :], prev)
    vals = _rwkv_prep_values(p, prev, mu_ref, w0_ref, wd_ref, a0_ref, wa_ref, kk_ref, ka_ref, rk_ref)
    bonus = vals[6]

    lane = lax.broadcasted_iota(jnp.int32, (C, RWKV_W), 1)
    t_idx = lax.broadcasted_iota(jnp.int32, (C, RWKV_W), 0)
    s_idx = lane & (RWKV_HEAD - 1)
    row_big = lax.broadcasted_iota(jnp.int32, (RWKV_W, RWKV_W), 0)
    lane_big = lax.broadcasted_iota(jnp.int32, (RWKV_W, RWKV_W), 1)
    bd_mask = _group_index(row_big, RWKV_HEAD) == _group_index(lane_big, RWKV_HEAD)
    ones_bd = jnp.where(bd_mask, 1.0, 0.0).astype(BF16)
    tri_r = lax.broadcasted_iota(jnp.int32, (C, C), 0)
    tri_c = lax.broadcasted_iota(jnp.int32, (C, C), 1)
    masks = (bd_mask, jnp.where(tri_r >= tri_c, 1.0, 0.0).astype(BF16), t_idx > s_idx, t_idx >= s_idx,
             _group_index(t_idx, RW_SUB) == _group_index(s_idx, RW_SUB), t_idx == s_idx, ones_bd)

    per_seq = [tuple(x[i * C:(i + 1) * C] for x in vals[:6]) for i in range(nb)]
    outs, n_new = _rwkv_chunks(per_seq, [state_ref[i] for i in range(nb)], masks)
    for i in range(nb):
        state_ref[i] = n_new[i]
        last_ref[i, 0:1, :] = p[(i + 1) * C - 1:(i + 1) * C, :]
    o = jnp.concatenate(outs, axis=0)

    c = o - _head_sum(o, ones_bd) * (1.0 / RWKV_HEAD)
    var = _head_sum(c * c, ones_bd) * (1.0 / RWKV_HEAD)
    on = c * lax.rsqrt(var + GN_EPS) * lnxg_ref[...] + lnxb_ref[...]
    o_ref[...] = ((on + bonus) * rg_ref[...].reshape(nb * C, RWKV_W)).astype(BF16).reshape(nb, C, RWKV_W)

    @pl.when(c_idx == pl.num_programs(1) - 1)
    def _():
        sfin_ref[...] = state_ref[...]


def _rwkv_prompt(rw_in, rg, lw, batch, seq, nb):
    const = lambda b, c: (0, 0)
    blk = lambda b, c: (b, c, 0)
    o, s_fin = pl.pallas_call(
        _rwkv_prompt_kernel,
        grid=(batch // nb, seq // RW_CHUNK),
        in_specs=[pl.BlockSpec((nb, RW_CHUNK, SHIFT_W), blk), pl.BlockSpec((nb, RW_CHUNK, RWKV_W), blk)]
        + _rwkv_param_specs(const) + [pl.BlockSpec((1, RWKV_W), const)] * 2,
        out_specs=[pl.BlockSpec((nb, RW_CHUNK, RWKV_W), blk),
                   pl.BlockSpec((nb, RWKV_HEAD, RWKV_W), lambda b, c: (b, 0, 0))],
        out_shape=[jax.ShapeDtypeStruct((batch, seq, RWKV_W), BF16),
                   jax.ShapeDtypeStruct((batch, RWKV_HEAD, RWKV_W), F32)],
        scratch_shapes=[pltpu.VMEM((nb, RWKV_HEAD, RWKV_W), F32), pltpu.VMEM((nb, SUBLANES, SHIFT_W), F32)],
        compiler_params=_cparams("arbitrary", "arbitrary"),
        name="rwkv_prompt",
    )(rw_in.reshape(batch, seq, SHIFT_W), rg.reshape(batch, seq, RWKV_W), *_rwkv_params(lw),
      lw["lnx_g"], lw["lnx_b"])
    wkv = jnp.transpose(s_fin.reshape(batch, RWKV_HEAD, RWKV_HEADS, RWKV_HEAD), (0, 2, 3, 1))
    return o.reshape(batch * seq, RWKV_W), wkv


V_HALF = RWKV_HEAD // 2


def _scan_kernel(x_ref, v_ref, s0_ref, o_ref, sfin_ref, s_ref):
    tb = pl.program_id(1)

    @pl.when(tb == 0)
    def _():
        s_ref[...] = s0_ref[...]

    def step(t, carry):
        kk = x_ref[t, 0]
        b = x_ref[t, 1]
        d = x_ref[t, 2]
        kx = x_ref[t, 3]
        r = x_ref[t, 4]
        for vp in range(V_HALF):
            sv = s_ref[vp]
            sa = jnp.sum(sv * kk, axis=0, keepdims=True)
            sn = sv * d - sa * b + v_ref[t, pl.ds(vp, 1), :] * kx
            s_ref[vp] = sn
            o_ref[t, pl.ds(vp, 1), :] = jnp.sum(sn * r, axis=0, keepdims=True)
        return carry

    lax.fori_loop(0, x_ref.shape[0], step, 0)

    @pl.when(tb == pl.num_programs(1) - 1)
    def _():
        sfin_ref[...] = s_ref[...]


def _scan(x5, v_t, s0, t_blk):
    steps, _, _, lanes = x5.shape
    lane_tiles = lanes // LANES
    return pl.pallas_call(
        _scan_kernel,
        grid=(lane_tiles, steps // t_blk),
        in_specs=[
            pl.BlockSpec((t_blk, 5, RWKV_HEAD, LANES), lambda j, t: (t, 0, 0, j)),
            pl.BlockSpec((t_blk, V_HALF, LANES), lambda j, t: (t, 0, j)),
            pl.BlockSpec((V_HALF, RWKV_HEAD, LANES), lambda j, t: (0, 0, j)),
        ],
        out_specs=[
            pl.BlockSpec((t_blk, V_HALF, LANES), lambda j, t: (t, 0, j)),
            pl.BlockSpec((V_HALF, RWKV_HEAD, LANES), lambda j, t: (0, 0, j)),
        ],
        out_shape=[
            jax.ShapeDtypeStruct((steps, V_HALF, lanes), F32),
            jax.ShapeDtypeStruct((V_HALF, RWKV_HEAD, lanes), F32),
        ],
        scratch_shapes=[pltpu.VMEM((V_HALF, RWKV_HEAD, LANES), F32)],
        compiler_params=_cparams("arbitrary", "arbitrary"),
        name="rwkv_scan",
    )(x5, v_t, s0)


def _rwkv_mix(prep, state, batch, seq, t_blk):
    r, k_mod, v, kk, b, d, _ = prep
    H, N = RWKV_HEADS, RWKV_HEAD
    x5 = jnp.stack([kk, b, d, k_mod, r]).reshape(5, batch, seq, H, N)
    x5 = jnp.transpose(x5, (2, 0, 4, 1, 3)).reshape(seq, 5, N, 1, batch * H)
    x5 = jnp.broadcast_to(x5, (seq, 5, N, 2, batch * H)).reshape(seq, 5, N, 2 * batch * H)
    v_t = jnp.transpose(v.reshape(batch, seq, H, 2, V_HALF), (1, 4, 3, 0, 2)).reshape(seq, V_HALF, 2 * batch * H)
    s0 = jnp.transpose(state.reshape(batch, H, 2, V_HALF, N), (3, 4, 2, 0, 1)).reshape(V_HALF, N, 2 * batch * H)
    o_t, s_fin = _scan(x5, v_t, s0, t_blk)
    o = jnp.transpose(o_t.reshape(seq, V_HALF, 2, batch, H), (3, 0, 4, 2, 1)).reshape(batch * seq, H * N)
    s_new = jnp.transpose(s_fin.reshape(V_HALF, N, 2, batch, H), (3, 4, 2, 0, 1)).reshape(batch, H, N, N)
    return o, s_new


def _mix_out(mixed, x_ref, mod_ref, wout_ref, gpost_ref, y_ref):
    out = jnp.dot(mixed, wout_ref[...], preferred_element_type=F32)
    gate = _rows2d(mod_ref)[:, 2 * D_MODEL:]
    y_ref[...] = x_ref[...] + gate * _rms(out, gpost_ref[...])


def _out_prompt_kernel(om_ref, orw_ref, gm_ref, x_ref, mod_ref, wout_ref, gpost_ref, y_ref):
    mixed = jnp.concatenate([om_ref[...], orw_ref[...], gm_ref[...]], axis=-1)
    _mix_out(mixed, x_ref, mod_ref, wout_ref, gpost_ref, y_ref)


def _out_sample_kernel(lat_ref, mg_ref, orw_ref, bonus_ref, rg_ref, gm_ref, x_ref, mod_ref, wout_ref, gpost_ref,
                       lnxg_ref, lnxb_ref, wuv_ref, y_ref):
    lat = lat_ref[...]
    o_mla = jnp.concatenate(
        [jnp.dot(lat[:, h * KV_LORA:(h + 1) * KV_LORA].astype(BF16), wuv_ref[h],
                 preferred_element_type=F32) for h in range(MLA_HEADS)], axis=-1)
    o_mla = (o_mla * mg_ref[...].astype(F32)).astype(BF16)
    ones_bd = _head_ones(RWKV_W, RWKV_HEAD)
    o = orw_ref[...]
    c = o - _head_sum(o, ones_bd) * (1.0 / RWKV_HEAD)
    var = _head_sum(c * c, ones_bd) * (1.0 / RWKV_HEAD)
    on = c * lax.rsqrt(var + GN_EPS) * lnxg_ref[...] + lnxb_ref[...]
    o_rw = ((on + bonus_ref[...]) * rg_ref[...]).astype(BF16)
    mixed = jnp.concatenate([o_mla, o_rw, gm_ref[...]], axis=-1)
    _mix_out(mixed, x_ref, mod_ref, wout_ref, gpost_ref, y_ref)


def _out_prompt(o_mla, o_rw, gm, x2d, mod_b, lw, seq, tm):
    rows = x2d.shape[0]
    tiles_per_seq = seq // tm
    const = lambda i: (0, 0)
    row_blk = lambda i: (i, 0)
    return pl.pallas_call(
        _out_prompt_kernel,
        grid=(rows // tm,),
        in_specs=[
            pl.BlockSpec((tm, MLA_W), row_blk),
            pl.BlockSpec((tm, RWKV_W), row_blk),
            pl.BlockSpec((tm, GM_W), row_blk),
            pl.BlockSpec((tm, D_MODEL), row_blk),
            pl.BlockSpec((1, 1, 3 * D_MODEL), lambda i: (i // tiles_per_seq, 0, 0)),
            pl.BlockSpec((D_MODEL, D_MODEL), const),
            pl.BlockSpec((1, D_MODEL), const),
        ],
        out_specs=pl.BlockSpec((tm, D_MODEL), row_blk),
        out_shape=jax.ShapeDtypeStruct((rows, D_MODEL), F32),
        compiler_params=_cparams("arbitrary"),
        name="out_prompt",
    )(o_mla, o_rw, gm, x2d, mod_b, lw["w_out"], lw["g_post"])


def _out_sample(o_lat, mg, o_rw, bonus, rg, gm, x2d, mod_rows, lw):
    rows = x2d.shape[0]
    const = lambda i: (0, 0)
    full = lambda w: pl.BlockSpec((rows, w), const)
    return pl.pallas_call(
        _out_sample_kernel,
        grid=(1,),
        in_specs=[
            full(MLA_HEADS * KV_LORA), full(MLA_W), full(RWKV_W), full(RWKV_W), full(RWKV_W), full(GM_W),
            full(D_MODEL), full(3 * D_MODEL),
            pl.BlockSpec((D_MODEL, D_MODEL), const),
            pl.BlockSpec((1, D_MODEL), const),
            pl.BlockSpec((1, RWKV_W), const),
            pl.BlockSpec((1, RWKV_W), const),
            pl.BlockSpec((MLA_HEADS, KV_LORA, MLA_V), lambda i: (0, 0, 0)),
        ],
        out_specs=full(D_MODEL),
        out_shape=jax.ShapeDtypeStruct((rows, D_MODEL), F32),
        compiler_params=_cparams("arbitrary"),
        name="out_sample",
    )(o_lat, mg, o_rw, bonus, rg, gm, x2d, mod_rows, lw["w_out"], lw["g_post"], lw["lnx_g"], lw["lnx_b"],
      lw["w_uv_h"])


def _swap_halves(w):
    half = MLA_ROPE // 2
    return jnp.concatenate([w[..., half:], w[..., :half]], axis=-1)


def _rope_tables(pos):
    half = MLA_ROPE // 2
    inv = ROPE_THETA ** (-jnp.arange(half, dtype=F32) / half)
    ang = pos.astype(F32)[:, None] * inv[None, :]
    cos, sin = jnp.cos(ang), jnp.sin(ang)
    zeros = jnp.zeros((pos.shape[0], LANES - MLA_ROPE), F32)
    return (jnp.concatenate([cos, cos, zeros], axis=-1), jnp.concatenate([-sin, sin, zeros], axis=-1))


def _layer_weights(l, w_in, norm_pre_g, norm_post_g, q_norm_g, kv_norm_g, w_uq, w_uk, w_uv, rw_mu, rw_w0,
                   rw_w_decay_up, rw_a0, rw_w_a_up, rw_k_k, rw_k_a, rw_r_k, rw_lnx_g, rw_lnx_b, gm_ln_g,
                   gm_ln_b, gm_w_s, gm_b_s, w_out):
    w = w_in[l]
    split_lo, split_hi = Q_LORA + KV_LORA, Q_LORA + KV_LORA + MLA_ROPE
    kr = w[:, split_lo:split_hi]
    w_in_p = jnp.concatenate([w[:, :split_lo], w[:, split_hi:], kr, _swap_halves(kr)], axis=1).astype(BF16)
    uq = w_uq[l]
    pe = uq[..., MLA_NOPE:]
    w_uq_p = jnp.concatenate([uq, _swap_halves(pe)], axis=-1).reshape(Q_LORA, MLA_HEADS * QHEAD_W).astype(BF16)
    zeros_lora = jnp.zeros((LANES - DECAY_LORA, RWKV_W), F32)
    row = lambda a: a.reshape(1, -1)
    return {
        "w_in": w_in_p, "g_pre": row(norm_pre_g[l]), "g_post": row(norm_post_g[l]),
        "g_q": row(q_norm_g[l]), "g_kv": row(kv_norm_g[l]), "w_uq": w_uq_p,
        "w_uk": w_uk[l].reshape(KV_LORA, MLA_W).astype(BF16),
        "w_uv": w_uv[l].reshape(KV_LORA, MLA_W).astype(BF16),
        "w_ukt": jnp.transpose(w_uk[l], (1, 2, 0)).astype(BF16),
        "w_uv_h": jnp.transpose(w_uv[l], (1, 0, 2)).astype(BF16),
        "rw_mu": row(rw_mu[l]), "rw_w0": row(rw_w0[l]), "rw_a0": row(rw_a0[l]),
        "rw_wd": jnp.concatenate([rw_w_decay_up[l], zeros_lora], axis=0).astype(BF16),
        "rw_wa": jnp.concatenate([zeros_lora, rw_w_a_up[l]], axis=0).astype(BF16),
        "rw_k_k": row(rw_k_k[l]), "rw_k_a": row(rw_k_a[l]), "rw_r_k": row(rw_r_k[l]),
        "lnx_g": row(rw_lnx_g[l]), "lnx_b": row(rw_lnx_b[l]),
        "ln_g": row(gm_ln_g[l]), "ln_b": row(gm_ln_b[l]),
        "gm_ws": gm_w_s[l],
        "gm_bias": jnp.repeat(gm_b_s[l].T, GM_GROUP_W, axis=1),
        "gm_coef0": row(jnp.repeat(gm_w_s[l][:, 0, 0], GM_GROUP_W)),
        "gm_bias0": row(jnp.repeat(gm_b_s[l][:, 0], GM_GROUP_W)),
        "w_out": w_out[l].astype(BF16),
    }


def _pick(full, want):
    return want if full % want == 0 else full


def kernel(x_prompt, x_sample, c_prompt, c_sample, cache_ckv, cache_kpe, page_table, state_wkv, state_shift, w_ada, b_ada, norm_pre_g, norm_post_g, w_in, q_norm_g, kv_norm_g, w_uq, w_uk, w_uv, rw_mu, rw_w0, rw_w_decay_up, rw_a0, rw_w_a_up, rw_k_k, rw_k_a, rw_r_k, rw_lnx_g, rw_lnx_b, gm_ln_g, gm_ln_b, gm_w_s, gm_b_s, w_out):
    batch, seq, _ = x_prompt.shape
    dec_batch, dec_seq, _ = x_sample.shape
    depth = w_in.shape[0]
    n_pages = page_table.shape[1]
    past_len = n_pages * cache_ckv.shape[2]
    assert dec_seq == 1 and seq % CHUNK == 0

    assert seq % RW_CHUNK == 0
    tm_proj = _pick(seq, 256)
    tm_out = _pick(seq, 512)
    tq = _pick(seq, 512)
    rw_nb = _pick(batch, 8)
    pages_per_step = _pick(n_pages, 64)

    mod = _ada_mod(jnp.concatenate([c_prompt, c_sample], axis=0), w_ada, b_ada)
    cs_p, sn_p = _rope_tables(jnp.arange(seq))
    cs_s, sn_s = _rope_tables(jnp.full((dec_batch,), past_len))
    cache_kpe_t = jnp.swapaxes(cache_kpe, 2, 3)

    y_p = x_prompt.reshape(batch * seq, D_MODEL)
    y_s = x_sample.reshape(dec_batch, D_MODEL)
    outs = {k: [] for k in ("ckv_p", "kpe_p", "ckv_s", "kpe_s", "wkv_p", "wkv_s", "sh_p", "sh_s", "vc_p", "vc_s")}
    for l in range(depth):
        lw = _layer_weights(l, w_in, norm_pre_g, norm_post_g, q_norm_g, kv_norm_g, w_uq, w_uk, w_uv, rw_mu,
                            rw_w0, rw_w_decay_up, rw_a0, rw_w_a_up, rw_k_k, rw_k_a, rw_r_k, rw_lnx_g,
                            rw_lnx_b, gm_ln_g, gm_ln_b, gm_w_s, gm_b_s, w_out)
        mod_p = mod[l, :batch].reshape(batch, 1, 3 * D_MODEL)
        q, k, v, mg, ckv, kpe, rw_in, rg, gm, vn_last = _proj_prompt(y_p, mod_p, lw, cs_p, sn_p, batch, seq,
                                                                     tm_proj)
        o_mla = _flash(q, k, v, mg, batch, seq, tq)
        o_rw, wkv_new = _rwkv_prompt(rw_in, rg, lw, batch, seq, rw_nb)
        y_p = _out_prompt(o_mla, o_rw, gm, y_p, mod_p, lw, seq, tm_out)
        outs["ckv_p"].append(ckv.reshape(batch, seq, KV_LORA))
        outs["kpe_p"].append(kpe.reshape(batch, seq, MLA_ROPE))
        outs["wkv_p"].append(wkv_new)
        outs["sh_p"].append(rw_in.reshape(batch, seq, SHIFT_W)[:, -1])
        outs["vc_p"].append(vn_last)

        mod_s = mod[l, batch:]
        qa, qp, mg_s, ckv_s, kpe_s, rw_s, rg_s, gm_s, vn_s = _proj_sample(y_s, mod_s, lw, cs_s, sn_s)
        o_lat = _paged_attention(qa, qp, ckv_s, kpe_s, cache_ckv, cache_kpe_t, page_table, l, pages_per_step)
        prep_s = _rwkv_prep_sample(rw_s, lw, state_shift[l])
        o_rw_s, wkv_s = _rwkv_mix(prep_s, state_wkv[l], dec_batch, 1, 1)
        y_s = _out_sample(o_lat, mg_s, o_rw_s, prep_s[6], rg_s, gm_s, y_s, mod_s, lw)
        outs["ckv_s"].append(ckv_s.reshape(dec_batch, 1, KV_LORA))
        outs["kpe_s"].append(kpe_s[:, :MLA_ROPE].reshape(dec_batch, 1, MLA_ROPE))
        outs["wkv_s"].append(wkv_s)
        outs["sh_s"].append(rw_s)
        outs["vc_s"].append(vn_s.reshape(dec_batch, 1, GM_W))

    st = lambda name: jnp.stack(outs[name])
    return (y_p.reshape(batch, seq, D_MODEL), y_s.reshape(dec_batch, 1, D_MODEL),
            st("ckv_p"), st("kpe_p"), st("ckv_s"), st("kpe_s"), st("wkv_p"), st("wkv_s"),
            st("sh_p"), st("sh_s"), st("vc_p"), st("vc_s"))
```

```python
import functools

import numpy as np
import jax
import jax.numpy as jnp
from jax import lax
from jax.experimental import pallas as pl
from jax.experimental.pallas import tpu as pltpu

F32 = jnp.float32
BF16 = jnp.bfloat16

D_MODEL = 1024
MLA_V = 128
MLA_W = D_MODEL // 2
MLA_HEADS = MLA_W // MLA_V
MLA_NOPE = 128
MLA_ROPE = 64
MLA_QK = MLA_NOPE + MLA_ROPE
Q_LORA = (3 * D_MODEL) // 8
KV_LORA = D_MODEL // 4
ROPE_THETA = 10000.0
ATTN_SCALE = MLA_QK ** -0.5
RWKV_W = D_MODEL // 4
RWKV_HEAD = 64
RWKV_HEADS = RWKV_W // RWKV_HEAD
DECAY_LORA = 64
ICLR_LORA = 64
SHIFT_W = 3 * RWKV_W + DECAY_LORA + ICLR_LORA
GN_EPS = 64e-5
GM_W = D_MODEL // 4
GM_GROUPS = 4
GM_GROUP_W = GM_W // GM_GROUPS
CHUNK = 128
RMS_EPS = 1e-6
LN_EPS = 1e-5

LANES = 128
SUBLANES = 8
VMEM_LIMIT_BYTES = 56 * 1024 * 1024

QHEAD_W = 2 * LANES
OFF_Q = 0
OFF_KV = OFF_Q + Q_LORA
OFF_MG = OFF_KV + KV_LORA
OFF_RW = OFF_MG + MLA_W
OFF_RG = OFF_RW + SHIFT_W
OFF_GU = OFF_RG + RWKV_W
OFF_GV = OFF_GU + GM_W
OFF_GG = OFF_GV + GM_W
OFF_KR = OFF_GG + GM_W
IN_W_P = OFF_KR + LANES

NEG_BIG = -1e30


def _cparams(*sem):
    return pltpu.CompilerParams(dimension_semantics=sem, vmem_limit_bytes=VMEM_LIMIT_BYTES)


def _silu(x):
    return x * jax.nn.sigmoid(x)


def _rms(x, g, eps=RMS_EPS):
    return x * lax.rsqrt(jnp.mean(x * x, axis=-1, keepdims=True) + eps) * g


def _rows2d(ref):
    m = ref[...]
    return m.reshape(m.shape[-2], m.shape[-1])


def _group_index(idx, group):
    shift = group.bit_length() - 1
    assert 1 << shift == group
    return lax.shift_right_logical(idx, shift)


def _head_ones(width, head):
    r = _group_index(lax.broadcasted_iota(jnp.int32, (width, width), 0), head)
    c = _group_index(lax.broadcasted_iota(jnp.int32, (width, width), 1), head)
    return jnp.where(r == c, 1.0, 0.0).astype(BF16)


def _head_sum(x, ones_bd):
    hi = x.astype(BF16)
    lo = (x - hi.astype(F32)).astype(BF16)
    return (jnp.dot(hi, ones_bd, preferred_element_type=F32)
            + jnp.dot(lo, ones_bd, preferred_element_type=F32))


def _rope_pair(blk, cs, sn):
    return blk * cs + pltpu.roll(blk, MLA_ROPE, 1) * sn


def _ada_kernel(c_ref, w_ref, b_ref, o_ref):
    c = _silu(c_ref[...]).astype(BF16)
    o_ref[0] = jnp.dot(c, w_ref[0], preferred_element_type=F32) + b_ref[0]


def _ada_mod(c_all, w_ada, b_ada):
    depth = w_ada.shape[0]
    n_rows = c_all.shape[0]
    n_tiles = (3 * D_MODEL) // D_MODEL
    return pl.pallas_call(
        _ada_kernel,
        grid=(depth, n_tiles),
        in_specs=[
            pl.BlockSpec((n_rows, D_MODEL), lambda l, j: (0, 0)),
            pl.BlockSpec((1, D_MODEL, D_MODEL), lambda l, j: (l, 0, j)),
            pl.BlockSpec((1, 1, D_MODEL), lambda l, j: (l, 0, j)),
        ],
        out_specs=pl.BlockSpec((1, n_rows, D_MODEL), lambda l, j: (l, 0, j)),
        out_shape=jax.ShapeDtypeStruct((depth, n_rows, 3 * D_MODEL), F32),
        compiler_params=_cparams("arbitrary", "arbitrary"),
        name="ada_mod",
    )(c_all, w_ada.astype(BF16), b_ada.reshape(depth, 1, 3 * D_MODEL))


def _proj_common(x_ref, mod_ref, gpre_ref, win_ref):
    mod = _rows2d(mod_ref)
    shift = mod[:, 0:D_MODEL]
    scale = mod[:, D_MODEL:2 * D_MODEL]
    h = _rms(x_ref[...], gpre_ref[...]) * (1.0 + scale) + shift
    hb = h.astype(BF16)

    def seg(off, width):
        return jnp.dot(hb, win_ref[:, off:off + width], preferred_element_type=F32)

    return seg


def _q_heads(seg, gq_ref, wuq_ref, cs, sn):
    qn = _rms(seg(OFF_Q, Q_LORA), gq_ref[...]).astype(BF16)
    q_all = jnp.dot(qn, wuq_ref[...], preferred_element_type=F32)
    heads = []
    for h in range(MLA_HEADS):
        nope = q_all[:, h * QHEAD_W:h * QHEAD_W + LANES] * ATTN_SCALE
        pe = _rope_pair(q_all[:, h * QHEAD_W + LANES:(h + 1) * QHEAD_W], cs, sn) * ATTN_SCALE
        heads.append((nope, pe))
    return heads


def _gm_norm(seg, lng_ref, lnb_ref):
    v = seg(OFF_GV, GM_W)
    mu = jnp.mean(v, axis=-1, keepdims=True)
    c = v - mu
    var = jnp.mean(c * c, axis=-1, keepdims=True)
    return c * lax.rsqrt(var + LN_EPS) * lng_ref[...] + lnb_ref[...]


def _proj_prompt_kernel(x_ref, mod_ref, gpre_ref, win_ref, gq_ref, wuq_ref, gkv_ref, wuk_ref, wuv_ref,
                        cs_ref, sn_ref, lng_ref, lnb_ref, ws_ref, bs_ref,
                        q_ref, k_ref, v_ref, mg_ref, ckv_ref, kpe_ref, rw_ref, rg_ref, gm_ref, vn_ref,
                        *, tiles_per_seq):
    seg = _proj_common(x_ref, mod_ref, gpre_ref, win_ref)
    cs = cs_ref[...]
    sn = sn_ref[...]
    tm = x_ref.shape[0]

    for h, (nope, pe) in enumerate(_q_heads(seg, gq_ref, wuq_ref, cs, sn)):
        q_ref[:, h * QHEAD_W:h * QHEAD_W + LANES] = nope.astype(BF16)
        q_ref[:, h * QHEAD_W + LANES:(h + 1) * QHEAD_W] = pe.astype(BF16)
    ckv = _rms(seg(OFF_KV, KV_LORA), gkv_ref[...])
    ckv_ref[...] = ckv
    kpe = _rope_pair(seg(OFF_KR, LANES), cs, sn)
    kpe_ref[...] = kpe[:, :MLA_ROPE]
    ckv_b = ckv.astype(BF16)
    k_nope = jnp.dot(ckv_b, wuk_ref[...], preferred_element_type=F32)
    v_ref[...] = jnp.dot(ckv_b, wuv_ref[...], preferred_element_type=F32).astype(BF16)
    kpe_b = kpe.astype(BF16)
    for h in range(MLA_HEADS):
        k_ref[:, h * QHEAD_W:h * QHEAD_W + LANES] = k_nope[:, h * MLA_NOPE:(h + 1) * MLA_NOPE].astype(BF16)
        k_ref[:, h * QHEAD_W + LANES:(h + 1) * QHEAD_W] = kpe_b
    mg_ref[...] = _silu(seg(OFF_MG, MLA_W)).astype(BF16)

    rw_ref[...] = seg(OFF_RW, SHIFT_W)
    rg_ref[...] = _silu(seg(OFF_RG, RWKV_W))

    vn = _gm_norm(seg, lng_ref, lnb_ref)
    vn_b = vn.astype(BF16)
    gate_u = _silu(seg(OFF_GG, GM_W)) * seg(OFF_GU, GM_W)
    row = lax.broadcasted_iota(jnp.int32, (CHUNK, CHUNK), 0)
    col = lax.broadcasted_iota(jnp.int32, (CHUNK, CHUNK), 1)
    lane_group = _group_index(lax.broadcasted_iota(jnp.int32, (CHUNK, GM_W), 1), GM_GROUP_W)
    w_tril = [jnp.where(row >= col, ws_ref[g], 0.0).astype(BF16) for g in range(GM_GROUPS)]
    for c in range(tm // CHUNK):
        vc = vn_b[c * CHUNK:(c + 1) * CHUNK]
        z = bs_ref[...]
        for g in range(GM_GROUPS):
            zg = jnp.dot(w_tril[g], vc, preferred_element_type=F32)
            z = z + jnp.where(lane_group == g, zg, 0.0)
        gm_ref[c * CHUNK:(c + 1) * CHUNK, :] = (gate_u[c * CHUNK:(c + 1) * CHUNK] * z).astype(BF16)

    @pl.when(pl.program_id(0) % tiles_per_seq == tiles_per_seq - 1)
    def _():
        vn_ref[0] = vn[tm - CHUNK:, :]


def _proj_prompt(x2d, mod_b, lw, cs, sn, batch, seq, tm):
    rows = x2d.shape[0]
    tiles_per_seq = seq // tm
    const = lambda i: (0, 0)
    row_blk = lambda i: (i, 0)
    outs = [
        (QHEAD_W * MLA_HEADS, BF16), (QHEAD_W * MLA_HEADS, BF16), (MLA_W, BF16), (MLA_W, BF16),
        (KV_LORA, F32), (MLA_ROPE, F32), (SHIFT_W, F32), (RWKV_W, F32), (GM_W, BF16),
    ]
    out_shape = [jax.ShapeDtypeStruct((rows, w), dt) for w, dt in outs]
    out_specs = [pl.BlockSpec((tm, w), row_blk) for w, _ in outs]
    out_shape.append(jax.ShapeDtypeStruct((batch, CHUNK, GM_W), F32))
    out_specs.append(pl.BlockSpec((1, CHUNK, GM_W), lambda i: (i // tiles_per_seq, 0, 0)))
    return pl.pallas_call(
        functools.partial(_proj_prompt_kernel, tiles_per_seq=tiles_per_seq),
        grid=(rows // tm,),
        in_specs=[
            pl.BlockSpec((tm, D_MODEL), row_blk),
            pl.BlockSpec((1, 1, 3 * D_MODEL), lambda i: (i // tiles_per_seq, 0, 0)),
            pl.BlockSpec((1, D_MODEL), const),
            pl.BlockSpec((D_MODEL, IN_W_P), const),
            pl.BlockSpec((1, Q_LORA), const),
            pl.BlockSpec((Q_LORA, QHEAD_W * MLA_HEADS), const),
            pl.BlockSpec((1, KV_LORA), const),
            pl.BlockSpec((KV_LORA, MLA_W), const),
            pl.BlockSpec((KV_LORA, MLA_W), const),
            pl.BlockSpec((tm, LANES), lambda i: (i % tiles_per_seq, 0)),
            pl.BlockSpec((tm, LANES), lambda i: (i % tiles_per_seq, 0)),
            pl.BlockSpec((1, GM_W), const),
            pl.BlockSpec((1, GM_W), const),
            pl.BlockSpec((GM_GROUPS, CHUNK, CHUNK), lambda i: (0, 0, 0)),
            pl.BlockSpec((CHUNK, GM_W), const),
        ],
        out_specs=out_specs,
        out_shape=out_shape,
        compiler_params=_cparams("arbitrary"),
        name="proj_prompt",
    )(x2d, mod_b, lw["g_pre"], lw["w_in"], lw["g_q"], lw["w_uq"], lw["g_kv"], lw["w_uk"], lw["w_uv"],
      cs, sn, lw["ln_g"], lw["ln_b"], lw["gm_ws"], lw["gm_bias"])


def _proj_sample_kernel(x_ref, mod_ref, gpre_ref, win_ref, gq_ref, wuq_ref, gkv_ref, wukt_ref,
                        cs_ref, sn_ref, lng_ref, lnb_ref, coef_ref, bias_ref,
                        qa_ref, qp_ref, mg_ref, ckv_ref, kpe_ref, rw_ref, rg_ref, gm_ref, vn_ref):
    seg = _proj_common(x_ref, mod_ref, gpre_ref, win_ref)
    cs = cs_ref[...]
    sn = sn_ref[...]
    for h, (nope, pe) in enumerate(_q_heads(seg, gq_ref, wuq_ref, cs, sn)):
        qa_ref[:, h * KV_LORA:(h + 1) * KV_LORA] = jnp.dot(nope.astype(BF16), wukt_ref[h],
                                                           preferred_element_type=F32)
        qp_ref[:, h * LANES:(h + 1) * LANES] = pe
    ckv_ref[...] = _rms(seg(OFF_KV, KV_LORA), gkv_ref[...])
    kpe_ref[...] = _rope_pair(seg(OFF_KR, LANES), cs, sn)
    mg_ref[...] = _silu(seg(OFF_MG, MLA_W)).astype(BF16)
    rw_ref[...] = seg(OFF_RW, SHIFT_W)
    rg_ref[...] = _silu(seg(OFF_RG, RWKV_W))
    vn = _gm_norm(seg, lng_ref, lnb_ref)
    vn_ref[...] = vn
    z = vn * coef_ref[...] + bias_ref[...]
    gm_ref[...] = (_silu(seg(OFF_GG, GM_W)) * seg(OFF_GU, GM_W) * z).astype(BF16)


def _proj_sample(x2d, mod_rows, lw, cs, sn):
    rows = x2d.shape[0]
    const = lambda i: (0, 0)
    outs = [
        (KV_LORA * MLA_HEADS, F32), (LANES * MLA_HEADS, F32), (MLA_W, BF16), (KV_LORA, F32), (LANES, F32),
        (SHIFT_W, F32), (RWKV_W, F32), (GM_W, BF16), (GM_W, F32),
    ]
    return pl.pallas_call(
        _proj_sample_kernel,
        grid=(1,),
        in_specs=[
            pl.BlockSpec((rows, D_MODEL), const),
            pl.BlockSpec((rows, 3 * D_MODEL), const),
            pl.BlockSpec((1, D_MODEL), const),
            pl.BlockSpec((D_MODEL, IN_W_P), const),
            pl.BlockSpec((1, Q_LORA), const),
            pl.BlockSpec((Q_LORA, QHEAD_W * MLA_HEADS), const),
            pl.BlockSpec((1, KV_LORA), const),
            pl.BlockSpec((MLA_HEADS, MLA_NOPE, KV_LORA), lambda i: (0, 0, 0)),
            pl.BlockSpec((rows, LANES), const),
            pl.BlockSpec((rows, LANES), const),
            pl.BlockSpec((1, GM_W), const),
            pl.BlockSpec((1, GM_W), const),
            pl.BlockSpec((1, GM_W), const),
            pl.BlockSpec((1, GM_W), const),
        ],
        out_specs=[pl.BlockSpec((rows, w), const) for w, _ in outs],
        out_shape=[jax.ShapeDtypeStruct((rows, w), dt) for w, dt in outs],
        compiler_params=_cparams("arbitrary"),
        name="proj_sample",
    )(x2d, mod_rows, lw["g_pre"], lw["w_in"], lw["g_q"], lw["w_uq"], lw["g_kv"], lw["w_ukt"],
      cs, sn, lw["ln_g"], lw["ln_b"], lw["gm_coef0"], lw["gm_bias0"])


def _flash_kernel(q_ref, k_ref, v_ref, g_ref, o_ref, *, tq):
    seq = q_ref.shape[1]
    row = lax.broadcasted_iota(jnp.int32, (tq, tq), 0)
    col = lax.broadcasted_iota(jnp.int32, (tq, tq), 1)
    for qi in range(seq // tq):
        rows = slice(qi * tq, (qi + 1) * tq)
        q = q_ref[0, rows, :]
        m = jnp.full((tq, 1), NEG_BIG, F32)
        l = jnp.zeros((tq, 1), F32)
        acc = jnp.zeros((tq, MLA_V), F32)
        for ki in range(qi + 1):
            cols = slice(ki * tq, (ki + 1) * tq)
            s = lax.dot_general(q, k_ref[0, cols, :], (((1,), (1,)), ((), ())),
                                preferred_element_type=F32)
            if ki == qi:
                s = jnp.where(row >= col, s, NEG_BIG)
            m_new = jnp.maximum(m, jnp.max(s, axis=-1, keepdims=True))
            alpha = jnp.exp(m - m_new)
            p = jnp.exp(s - m_new)
            l = alpha * l + jnp.sum(p, axis=-1, keepdims=True)
            acc = alpha * acc + jnp.dot(p.astype(BF16), v_ref[0, cols, :], preferred_element_type=F32)
            m = m_new
        o_ref[0, rows, :] = (acc / l * g_ref[0, rows, :].astype(F32)).astype(BF16)


def _flash(q, k, v, g, batch, seq, tq):
    q3 = q.reshape(batch, seq, QHEAD_W * MLA_HEADS)
    k3 = k.reshape(batch, seq, QHEAD_W * MLA_HEADS)
    v3 = v.reshape(batch, seq, MLA_W)
    g3 = g.reshape(batch, seq, MLA_W)
    head_blk = lambda b, h: (b, 0, h)
    out = pl.pallas_call(
        functools.partial(_flash_kernel, tq=tq),
        grid=(batch, MLA_HEADS),
        in_specs=[
            pl.BlockSpec((1, seq, QHEAD_W), head_blk),
            pl.BlockSpec((1, seq, QHEAD_W), head_blk),
            pl.BlockSpec((1, seq, MLA_V), head_blk),
            pl.BlockSpec((1, seq, MLA_V), head_blk),
        ],
        out_specs=pl.BlockSpec((1, seq, MLA_V), head_blk),
        out_shape=jax.ShapeDtypeStruct((batch, seq, MLA_W), BF16),
        compiler_params=_cparams("arbitrary", "arbitrary"),
        name="flash_prompt",
    )(q3, k3, v3, g3)
    return out.reshape(batch * seq, MLA_W)


N_SLOTS = 2


def _paged_kernel(pt_ref, qa_ref, qp_ref, cn_ref, kn_ref, ckv_hbm, kpe_hbm, o_ref, ckv_buf, kpe_buf, sem,
                  *, layer, n_pages, page, group_tokens):
    b = pl.program_id(0)
    n_seq = pl.num_programs(0)
    slot = lax.rem(b, N_SLOTS)

    def page_copies(seq_idx, sl):
        copies = []
        for i in range(n_pages):
            pg = pt_ref[seq_idx, i]
            copies.append(pltpu.make_async_copy(
                ckv_hbm.at[layer, pg], ckv_buf.at[sl, pl.ds(i * page, page), :], sem.at[0, sl]))
            copies.append(pltpu.make_async_copy(
                kpe_hbm.at[layer, pg], kpe_buf.at[sl, :, pl.ds(i * page, page)], sem.at[1, sl]))
        return copies

    @pl.when(b == 0)
    def _():
        for c in page_copies(0, 0):
            c.start()

    for c in page_copies(b, slot):
        c.wait()

    nxt = lax.rem(b + 1, n_seq)
    nxt_slot = lax.rem(b + 1, N_SLOTS)
    for c in page_copies(nxt, nxt_slot):
        c.start()

    qa = qa_ref[0]
    qp = qp_ref[0][:, :MLA_ROPE]
    qa_b = qa.astype(BF16)
    qp_b = qp.astype(BF16)
    groups = [slice(g * group_tokens, (g + 1) * group_tokens) for g in range(n_pages * page // group_tokens)]
    ckv = [ckv_buf[slot, gs, :].astype(BF16) for gs in groups]
    kpe_t = [kpe_buf[slot, :, gs].astype(BF16) for gs in groups]
    s = [lax.dot_general(qa_b, ck, (((1,), (1,)), ((), ())), preferred_element_type=F32)
         + jnp.dot(qp_b, kp, preferred_element_type=F32) for ck, kp in zip(ckv, kpe_t)]
    m_g = [jnp.max(sg, axis=-1, keepdims=True) for sg in s]
    p = [jnp.exp(sg - mg) for sg, mg in zip(s, m_g)]
    l_g = [jnp.sum(pg, axis=-1, keepdims=True) for pg in p]
    acc_g = [jnp.dot(pg.astype(BF16), ck, preferred_element_type=F32) for pg, ck in zip(p, ckv)]
    cn = cn_ref[0]
    kn = kn_ref[0][:, :MLA_ROPE]
    s_new = jnp.sum(qa * cn, axis=-1, keepdims=True) + jnp.sum(qp * kn, axis=-1, keepdims=True)
    m = functools.reduce(jnp.maximum, m_g, s_new)
    p_new = jnp.exp(s_new - m)
    w_g = [jnp.exp(mg - m) for mg in m_g]
    l = sum(wg * lg for wg, lg in zip(w_g, l_g)) + p_new
    o_ref[0] = (sum(wg * ag for wg, ag in zip(w_g, acc_g)) + p_new * cn) / l

    @pl.when(b == n_seq - 1)
    def _():
        for c in page_copies(nxt, nxt_slot):
            c.wait()


def _paged_attention(qa, qp, ckv_new, kpe_new, cache_ckv, cache_kpe_t, page_table, layer):
    dec_batch, n_pages = page_table.shape
    page = cache_ckv.shape[2]
    head_pad = ((0, 0), (0, SUBLANES - MLA_HEADS), (0, 0))
    qa3 = jnp.pad(qa.reshape(dec_batch, MLA_HEADS, KV_LORA), head_pad)
    qp3 = jnp.pad(qp.reshape(dec_batch, MLA_HEADS, LANES), head_pad)
    cn3 = ckv_new.reshape(dec_batch, 1, KV_LORA)
    kn3 = kpe_new.reshape(dec_batch, 1, LANES)
    row_blk = lambda b, pt: (b, 0, 0)
    out = pl.pallas_call(
        functools.partial(_paged_kernel, layer=layer, n_pages=n_pages, page=page,
                          group_tokens=_pick(n_pages * page, 2048)),
        grid_spec=pltpu.PrefetchScalarGridSpec(
            num_scalar_prefetch=1,
            grid=(dec_batch,),
            in_specs=[
                pl.BlockSpec((1, SUBLANES, KV_LORA), row_blk),
                pl.BlockSpec((1, SUBLANES, LANES), row_blk),
                pl.BlockSpec((1, 1, KV_LORA), row_blk),
                pl.BlockSpec((1, 1, LANES), row_blk),
                pl.BlockSpec(memory_space=pl.ANY),
                pl.BlockSpec(memory_space=pl.ANY),
            ],
            out_specs=pl.BlockSpec((1, SUBLANES, KV_LORA), row_blk),
            scratch_shapes=[
                pltpu.VMEM((N_SLOTS, n_pages * page, KV_LORA), F32),
                pltpu.VMEM((N_SLOTS, MLA_ROPE, n_pages * page), F32),
                pltpu.SemaphoreType.DMA((2, N_SLOTS)),
            ],
        ),
        out_shape=jax.ShapeDtypeStruct((dec_batch, SUBLANES, KV_LORA), F32),
        compiler_params=_cparams("arbitrary"),
        name="paged_attention",
    )(page_table, qa3, qp3, cn3, kn3, cache_ckv, cache_kpe_t)
    return out[:, :MLA_HEADS].reshape(dec_batch, MLA_HEADS * KV_LORA)


def _rwkv_prep_values(p, prev, mu_ref, w0_ref, wd_ref, a0_ref, wa_ref, kk_ref, ka_ref, rk_ref):
    xm = p + (prev - p) * mu_ref[...]
    r = xm[:, 0:RWKV_W]
    k = xm[:, RWKV_W:2 * RWKV_W]
    v = xm[:, 2 * RWKV_W:3 * RWKV_W]
    tail = xm[:, 3 * RWKV_W:]
    lane = lax.broadcasted_iota(jnp.int32, tail.shape, 1)
    lora_in = jnp.where(lane < DECAY_LORA, jnp.tanh(tail), tail).astype(BF16)
    dw = jnp.dot(lora_in, wd_ref[...], preferred_element_type=F32)
    da = jnp.dot(lora_in, wa_ref[...], preferred_element_type=F32)
    z = -(w0_ref[...] + dw)
    softplus = jnp.maximum(z, 0.0) + jnp.log(1.0 + jnp.exp(-jnp.abs(z)))
    w = -softplus - 0.5
    log_decay = -jnp.exp(w)
    a = jax.nn.sigmoid(a0_ref[...] + da)
    ones_bd = _head_ones(RWKV_W, RWKV_HEAD)
    kk = k * kk_ref[...]
    kk = kk / jnp.maximum(jnp.sqrt(_head_sum(kk * kk, ones_bd)), 1e-12)
    k_mod = k * (1.0 + (a - 1.0) * ka_ref[...])
    bonus = _head_sum(r * k_mod * rk_ref[...], ones_bd) * v
    return r, k_mod, v, kk, kk * a, log_decay, bonus


def _rwkv_param_specs(index_map):
    widths = (SHIFT_W, RWKV_W, None, RWKV_W, None, RWKV_W, RWKV_W, RWKV_W)
    return [pl.BlockSpec((LANES, RWKV_W) if w is None else (1, w), index_map) for w in widths]


def _rwkv_params(lw):
    return (lw["rw_mu"], lw["rw_w0"], lw["rw_wd"], lw["rw_a0"], lw["rw_wa"], lw["rw_k_k"], lw["rw_k_a"],
            lw["rw_r_k"])


def _rwkv_prep_sample_kernel(p_ref, prev_ref, mu_ref, w0_ref, wd_ref, a0_ref, wa_ref, kk_ref, ka_ref, rk_ref,
                             r_o, k_o, v_o, kk_o, b_o, d_o, bonus_o):
    r, k_mod, v, kk, b, log_decay, bonus = _rwkv_prep_values(
        p_ref[...], prev_ref[...], mu_ref, w0_ref, wd_ref, a0_ref, wa_ref, kk_ref, ka_ref, rk_ref)
    r_o[...] = r
    k_o[...] = k_mod
    v_o[...] = v
    kk_o[...] = kk
    b_o[...] = b
    d_o[...] = jnp.exp(log_decay)
    bonus_o[...] = bonus


def _rwkv_prep_sample(rw_in, lw, shift_rows):
    rows = rw_in.shape[0]
    const = lambda i: (0, 0)
    return pl.pallas_call(
        _rwkv_prep_sample_kernel,
        grid=(1,),
        in_specs=[pl.BlockSpec((rows, SHIFT_W), const)] * 2 + _rwkv_param_specs(const),
        out_specs=[pl.BlockSpec((rows, RWKV_W), const)] * 7,
        out_shape=[jax.ShapeDtypeStruct((rows, RWKV_W), F32)] * 7,
        compiler_params=_cparams("arbitrary"),
        name="rwkv_prep_sample",
    )(rw_in, shift_rows, *_rwkv_params(lw))


RW_CHUNK = 64
RW_SUB = 16


def _split3(x):
    hi = x.astype(BF16)
    r1 = x - hi.astype(F32)
    mid = r1.astype(BF16)
    lo = (r1 - mid.astype(F32)).astype(BF16)
    return hi, mid, lo


def _block_diag(y, bd_mask):
    return jnp.where(bd_mask, jnp.concatenate([y] * RWKV_HEADS, axis=0), 0.0).astype(BF16)


def _mm(x, y_bd):
    return jnp.dot(x.astype(BF16), y_bd, preferred_element_type=F32)


def _rwkv_chunks(vals, n0, masks):
    bd_mask, tri_incl, strict, incl, same_sub, eye_tiled, ones_bd = masks
    C = RW_CHUNK
    seqs = range(len(vals))
    bd = lambda y: _block_diag(y, bd_mask)
    r, kx, v, kap, bb, lam = ([val[j] for val in vals] for j in range(6))
    cum = [sum(jnp.dot(tri_incl, part, preferred_element_type=F32) for part in _split3(lam[i])) for i in seqs]
    cum_last = [c[C - 1:C, :] for c in cum]
    e_neg = [jnp.exp(-c) for c in cum]
    ap = [jnp.concatenate([kap[i] * jnp.exp(cum[i] - lam[i]), r[i] * jnp.exp(cum[i])], axis=0).astype(BF16)
          for i in seqs]
    rhs = [jnp.concatenate([bd(bb[i] * e_neg[i]), bd(kx[i] * e_neg[i])], axis=0) for i in seqs]
    g = [lax.dot_general(ap[i], rhs[i], (((1,), (1,)), ((), ())), preferred_element_type=F32) for i in seqs]
    l_ab = [jnp.where(strict, gi[:C, :RWKV_W], 0.0) for gi in g]
    l_ak = [jnp.where(strict, gi[:C, RWKV_W:], 0.0) for gi in g]
    q_pb = [jnp.where(incl, gi[C:, :RWKV_W], 0.0) for gi in g]
    q_pk = [jnp.where(incl, gi[C:, RWKV_W:], 0.0) for gi in g]
    v_bd = [bd(vi) for vi in v]
    apn = [jnp.dot(ap[i], bd(n0[i]), preferred_element_type=F32) for i in seqs]
    z_w = [apn[i][:C] + _mm(l_ak[i], v_bd[i]) for i in seqs]

    l_d = [jnp.where(same_sub, li, 0.0) for li in l_ab]
    z_l = [l_ab[i] - l_d[i] for i in seqs]
    x = [(-li).astype(BF16) for li in l_d]
    power = l_d
    n_double = RW_SUB.bit_length() - 1
    for step in range(n_double):
        xz = [jnp.dot(x[i], jnp.concatenate([bd(z_l[i]), bd(z_w[i])], axis=1), preferred_element_type=F32)
              for i in seqs]
        z_l = [z_l[i] + xz[i][:, :RWKV_W] for i in seqs]
        z_w = [z_w[i] + xz[i][:, RWKV_W:] for i in seqs]
        if step + 1 < n_double:
            power = [_mm(pw, bd(pw)) for pw in power]
            x = [pw.astype(BF16) for pw in power]
    n = z_l
    n_sq = [_mm(ni, bd(ni)) for ni in n]
    y = [z_w[i] + _mm(n_sq[i], bd(z_w[i])) for i in seqs]
    u = [_mm(n[i], bd(y[i])) - y[i] for i in seqs]

    o = [apn[i][C:] + _mm(q_pb[i], bd(u[i])) + _mm(q_pk[i], v_bd[i]) for i in seqs]

    e_end = [jnp.exp(cum_last[i] - cum[i]) for i in seqs]
    lhs_t = [jnp.concatenate([bb[i] * e_end[i], kx[i] * e_end[i]], axis=0).T.astype(BF16) for i in seqs]
    full = [jnp.dot(lhs_t[i], jnp.concatenate([u[i], v[i]], axis=0).astype(BF16), preferred_element_type=F32)
            for i in seqs]
    heads = [slice(h * RWKV_HEAD, (h + 1) * RWKV_HEAD) for h in range(RWKV_HEADS)]
    delta = [sum(jnp.where(bd_mask[hs], f[hs], 0.0) for hs in heads) for f in full]
    g_wide = [sum(jnp.dot(part, ones_bd, preferred_element_type=F32)
                  for part in _split3(jnp.where(eye_tiled, jnp.exp(cl), 0.0))) for cl in cum_last]
    return o, [n0[i] * g_wide[i] + delta[i] for i in seqs]


def _rwkv_prompt_kernel(p_ref, rg_ref, mu_ref, w0_ref, wd_ref, a0_ref, wa_ref, kk_ref, ka_ref, rk_ref,
                        lnxg_ref, lnxb_ref, o_ref, sfin_ref, state_ref, last_ref):
    nb, C, _ = p_ref.shape
    c_idx = pl.program_id(1)

    @pl.when(c_idx == 0)
    def _():
        state_ref[...] = jnp.zeros(state_ref.shape, F32)
        last_ref[...] = jnp.zeros(last_ref.shape, F32)

    p = p_ref[...].reshape(nb * C, SHIFT_W)
    row = lax.broadcasted_iota(jnp.int32, (nb * C, 1), 0)
    prev = pltpu.roll(p, 1, 0)
    for i in range(nb):
        prev = jnp.where(row == i * C, last_ref[i, 0:1, :], prev)
    vals = _rwkv_prep_values(p, prev, mu_ref, w0_ref, wd_ref, a0_ref, wa_ref, kk_ref, ka_ref, rk_ref)
    bonus = vals[6]

    lane = lax.broadcasted_iota(jnp.int32, (C, RWKV_W), 1)
    t_idx = lax.broadcasted_iota(jnp.int32, (C, RWKV_W), 0)
    s_idx = lane & (RWKV_HEAD - 1)
    row_big = lax.broadcasted_iota(jnp.int32, (RWKV_W, RWKV_W), 0)
    lane_big = lax.broadcasted_iota(jnp.int32, (RWKV_W, RWKV_W), 1)
    bd_mask = _group_index(row_big, RWKV_HEAD) == _group_index(lane_big, RWKV_HEAD)
    ones_bd = jnp.where(bd_mask, 1.0, 0.0).astype(BF16)
    tri_r = lax.broadcasted_iota(jnp.int32, (C, C), 0)
    tri_c = lax.broadcasted_iota(jnp.int32, (C, C), 1)
    masks = (bd_mask, jnp.where(tri_r >= tri_c, 1.0, 0.0).astype(BF16), t_idx > s_idx, t_idx >= s_idx,
             _group_index(t_idx, RW_SUB) == _group_index(s_idx, RW_SUB), t_idx == s_idx, ones_bd)

    per_seq = [tuple(x[i * C:(i + 1) * C] for x in vals[:6]) for i in range(nb)]
    outs, n_new = _rwkv_chunks(per_seq, [state_ref[i] for i in range(nb)], masks)
    for i in range(nb):
        state_ref[i] = n_new[i]
        last_ref[i, 0:1, :] = p[(i + 1) * C - 1:(i + 1) * C, :]
    o = jnp.concatenate(outs, axis=0)

    c = o - _head_sum(o, ones_bd) * (1.0 / RWKV_HEAD)
    var = _head_sum(c * c, ones_bd) * (1.0 / RWKV_HEAD)
    on = c * lax.rsqrt(var + GN_EPS) * lnxg_ref[...] + lnxb_ref[...]
    o_ref[...] = ((on + bonus) * rg_ref[...].reshape(nb * C, RWKV_W)).astype(BF16).reshape(nb, C, RWKV_W)

    @pl.when(c_idx == pl.num_programs(1) - 1)
    def _():
        sfin_ref[...] = state_ref[...]


def _rwkv_prompt(rw_in, rg, lw, batch, seq, nb):
    const = lambda b, c: (0, 0)
    blk = lambda b, c: (b, c, 0)
    o, s_fin = pl.pallas_call(
        _rwkv_prompt_kernel,
        grid=(batch // nb, seq // RW_CHUNK),
        in_specs=[pl.BlockSpec((nb, RW_CHUNK, SHIFT_W), blk), pl.BlockSpec((nb, RW_CHUNK, RWKV_W), blk)]
        + _rwkv_param_specs(const) + [pl.BlockSpec((1, RWKV_W), const)] * 2,
        out_specs=[pl.BlockSpec((nb, RW_CHUNK, RWKV_W), blk),
                   pl.BlockSpec((nb, RWKV_HEAD, RWKV_W), lambda b, c: (b, 0, 0))],
        out_shape=[jax.ShapeDtypeStruct((batch, seq, RWKV_W), BF16),
                   jax.ShapeDtypeStruct((batch, RWKV_HEAD, RWKV_W), F32)],
        scratch_shapes=[pltpu.VMEM((nb, RWKV_HEAD, RWKV_W), F32), pltpu.VMEM((nb, SUBLANES, SHIFT_W), F32)],
        compiler_params=_cparams("arbitrary", "arbitrary"),
        name="rwkv_prompt",
    )(rw_in.reshape(batch, seq, SHIFT_W), rg.reshape(batch, seq, RWKV_W), *_rwkv_params(lw),
      lw["lnx_g"], lw["lnx_b"])
    wkv = jnp.transpose(s_fin.reshape(batch, RWKV_HEAD, RWKV_HEADS, RWKV_HEAD), (0, 2, 3, 1))
    return o.reshape(batch * seq, RWKV_W), wkv


V_HALF = RWKV_HEAD // 2


def _scan_kernel(x_ref, v_ref, s0_ref, o_ref, sfin_ref, s_ref):
    tb = pl.program_id(1)

    @pl.when(tb == 0)
    def _():
        s_ref[...] = s0_ref[...]

    def step(t, carry):
        kk = x_ref[t, 0]
        b = x_ref[t, 1]
        d = x_ref[t, 2]
        kx = x_ref[t, 3]
        r = x_ref[t, 4]
        for vp in range(V_HALF):
            sv = s_ref[vp]
            sa = jnp.sum(sv * kk, axis=0, keepdims=True)
            sn = sv * d - sa * b + v_ref[t, pl.ds(vp, 1), :] * kx
            s_ref[vp] = sn
            o_ref[t, pl.ds(vp, 1), :] = jnp.sum(sn * r, axis=0, keepdims=True)
        return carry

    lax.fori_loop(0, x_ref.shape[0], step, 0)

    @pl.when(tb == pl.num_programs(1) - 1)
    def _():
        sfin_ref[...] = s_ref[...]


def _scan(x5, v_t, s0, t_blk):
    steps, _, _, lanes = x5.shape
    lane_tiles = lanes // LANES
    return pl.pallas_call(
        _scan_kernel,
        grid=(lane_tiles, steps // t_blk),
        in_specs=[
            pl.BlockSpec((t_blk, 5, RWKV_HEAD, LANES), lambda j, t: (t, 0, 0, j)),
            pl.BlockSpec((t_blk, V_HALF, LANES), lambda j, t: (t, 0, j)),
            pl.BlockSpec((V_HALF, RWKV_HEAD, LANES), lambda j, t: (0, 0, j)),
        ],
        out_specs=[
            pl.BlockSpec((t_blk, V_HALF, LANES), lambda j, t: (t, 0, j)),
            pl.BlockSpec((V_HALF, RWKV_HEAD, LANES), lambda j, t: (0, 0, j)),
        ],
        out_shape=[
            jax.ShapeDtypeStruct((steps, V_HALF, lanes), F32),
            jax.ShapeDtypeStruct((V_HALF, RWKV_HEAD, lanes), F32),
        ],
        scratch_shapes=[pltpu.VMEM((V_HALF, RWKV_HEAD, LANES), F32)],
        compiler_params=_cparams("arbitrary", "arbitrary"),
        name="rwkv_scan",
    )(x5, v_t, s0)


def _rwkv_mix(prep, state, batch, seq, t_blk):
    r, k_mod, v, kk, b, d, _ = prep
    H, N = RWKV_HEADS, RWKV_HEAD
    x5 = jnp.stack([kk, b, d, k_mod, r]).reshape(5, batch, seq, H, N)
    x5 = jnp.transpose(x5, (2, 0, 4, 1, 3)).reshape(seq, 5, N, 1, batch * H)
    x5 = jnp.broadcast_to(x5, (seq, 5, N, 2, batch * H)).reshape(seq, 5, N, 2 * batch * H)
    v_t = jnp.transpose(v.reshape(batch, seq, H, 2, V_HALF), (1, 4, 3, 0, 2)).reshape(seq, V_HALF, 2 * batch * H)
    s0 = jnp.transpose(state.reshape(batch, H, 2, V_HALF, N), (3, 4, 2, 0, 1)).reshape(V_HALF, N, 2 * batch * H)
    o_t, s_fin = _scan(x5, v_t, s0, t_blk)
    o = jnp.transpose(o_t.reshape(seq, V_HALF, 2, batch, H), (3, 0, 4, 2, 1)).reshape(batch * seq, H * N)
    s_new = jnp.transpose(s_fin.reshape(V_HALF, N, 2, batch, H), (3, 4, 2, 0, 1)).reshape(batch, H, N, N)
    return o, s_new


def _mix_out(mixed, x_ref, mod_ref, wout_ref, gpost_ref, y_ref):
    out = jnp.dot(mixed, wout_ref[...], preferred_element_type=F32)
    gate = _rows2d(mod_ref)[:, 2 * D_MODEL:]
    y_ref[...] = x_ref[...] + gate * _rms(out, gpost_ref[...])


def _out_prompt_kernel(om_ref, orw_ref, gm_ref, x_ref, mod_ref, wout_ref, gpost_ref, y_ref):
    mixed = jnp.concatenate([om_ref[...], orw_ref[...], gm_ref[...]], axis=-1)
    _mix_out(mixed, x_ref, mod_ref, wout_ref, gpost_ref, y_ref)


def _out_sample_kernel(lat_ref, mg_ref, orw_ref, bonus_ref, rg_ref, gm_ref, x_ref, mod_ref, wout_ref, gpost_ref,
                       lnxg_ref, lnxb_ref, wuv_ref, y_ref):
    lat = lat_ref[...]
    o_mla = jnp.concatenate(
        [jnp.dot(lat[:, h * KV_LORA:(h + 1) * KV_LORA].astype(BF16), wuv_ref[h],
                 preferred_element_type=F32) for h in range(MLA_HEADS)], axis=-1)
    o_mla = (o_mla * mg_ref[...].astype(F32)).astype(BF16)
    ones_bd = _head_ones(RWKV_W, RWKV_HEAD)
    o = orw_ref[...]
    c = o - _head_sum(o, ones_bd) * (1.0 / RWKV_HEAD)
    var = _head_sum(c * c, ones_bd) * (1.0 / RWKV_HEAD)
    on = c * lax.rsqrt(var + GN_EPS) * lnxg_ref[...] + lnxb_ref[...]
    o_rw = ((on + bonus_ref[...]) * rg_ref[...]).astype(BF16)
    mixed = jnp.concatenate([o_mla, o_rw, gm_ref[...]], axis=-1)
    _mix_out(mixed, x_ref, mod_ref, wout_ref, gpost_ref, y_ref)


def _out_prompt(o_mla, o_rw, gm, x2d, mod_b, lw, seq, tm):
    rows = x2d.shape[0]
    tiles_per_seq = seq // tm
    const = lambda i: (0, 0)
    row_blk = lambda i: (i, 0)
    return pl.pallas_call(
        _out_prompt_kernel,
        grid=(rows // tm,),
        in_specs=[
            pl.BlockSpec((tm, MLA_W), row_blk),
            pl.BlockSpec((tm, RWKV_W), row_blk),
            pl.BlockSpec((tm, GM_W), row_blk),
            pl.BlockSpec((tm, D_MODEL), row_blk),
            pl.BlockSpec((1, 1, 3 * D_MODEL), lambda i: (i // tiles_per_seq, 0, 0)),
            pl.BlockSpec((D_MODEL, D_MODEL), const),
            pl.BlockSpec((1, D_MODEL), const),
        ],
        out_specs=pl.BlockSpec((tm, D_MODEL), row_blk),
        out_shape=jax.ShapeDtypeStruct((rows, D_MODEL), F32),
        compiler_params=_cparams("arbitrary"),
        name="out_prompt",
    )(o_mla, o_rw, gm, x2d, mod_b, lw["w_out"], lw["g_post"])


def _out_sample(o_lat, mg, o_rw, bonus, rg, gm, x2d, mod_rows, lw):
    rows = x2d.shape[0]
    const = lambda i: (0, 0)
    full = lambda w: pl.BlockSpec((rows, w), const)
    return pl.pallas_call(
        _out_sample_kernel,
        grid=(1,),
        in_specs=[
            full(MLA_HEADS * KV_LORA), full(MLA_W), full(RWKV_W), full(RWKV_W), full(RWKV_W), full(GM_W),
            full(D_MODEL), full(3 * D_MODEL),
            pl.BlockSpec((D_MODEL, D_MODEL), const),
            pl.BlockSpec((1, D_MODEL), const),
            pl.BlockSpec((1, RWKV_W), const),
            pl.BlockSpec((1, RWKV_W), const),
            pl.BlockSpec((MLA_HEADS, KV_LORA, MLA_V), lambda i: (0, 0, 0)),
        ],
        out_specs=full(D_MODEL),
        out_shape=jax.ShapeDtypeStruct((rows, D_MODEL), F32),
        compiler_params=_cparams("arbitrary"),
        name="out_sample",
    )(o_lat, mg, o_rw, bonus, rg, gm, x2d, mod_rows, lw["w_out"], lw["g_post"], lw["lnx_g"], lw["lnx_b"],
      lw["w_uv_h"])


def _swap_halves(w):
    half = MLA_ROPE // 2
    return jnp.concatenate([w[..., half:], w[..., :half]], axis=-1)


def _rope_tables(pos):
    half = MLA_ROPE // 2
    inv = ROPE_THETA ** (-jnp.arange(half, dtype=F32) / half)
    ang = pos.astype(F32)[:, None] * inv[None, :]
    cos, sin = jnp.cos(ang), jnp.sin(ang)
    zeros = jnp.zeros((pos.shape[0], LANES - MLA_ROPE), F32)
    return (jnp.concatenate([cos, cos, zeros], axis=-1), jnp.concatenate([-sin, sin, zeros], axis=-1))


def _layer_weights(l, w_in, norm_pre_g, norm_post_g, q_norm_g, kv_norm_g, w_uq, w_uk, w_uv, rw_mu, rw_w0,
                   rw_w_decay_up, rw_a0, rw_w_a_up, rw_k_k, rw_k_a, rw_r_k, rw_lnx_g, rw_lnx_b, gm_ln_g,
                   gm_ln_b, gm_w_s, gm_b_s, w_out):
    w = w_in[l]
    split_lo, split_hi = Q_LORA + KV_LORA, Q_LORA + KV_LORA + MLA_ROPE
    kr = w[:, split_lo:split_hi]
    w_in_p = jnp.concatenate([w[:, :split_lo], w[:, split_hi:], kr, _swap_halves(kr)], axis=1).astype(BF16)
    uq = w_uq[l]
    pe = uq[..., MLA_NOPE:]
    w_uq_p = jnp.concatenate([uq, _swap_halves(pe)], axis=-1).reshape(Q_LORA, MLA_HEADS * QHEAD_W).astype(BF16)
    zeros_lora = jnp.zeros((LANES - DECAY_LORA, RWKV_W), F32)
    row = lambda a: a.reshape(1, -1)
    return {
        "w_in": w_in_p, "g_pre": row(norm_pre_g[l]), "g_post": row(norm_post_g[l]),
        "g_q": row(q_norm_g[l]), "g_kv": row(kv_norm_g[l]), "w_uq": w_uq_p,
        "w_uk": w_uk[l].reshape(KV_LORA, MLA_W).astype(BF16),
        "w_uv": w_uv[l].reshape(KV_LORA, MLA_W).astype(BF16),
        "w_ukt": jnp.transpose(w_uk[l], (1, 2, 0)).astype(BF16),
        "w_uv_h": jnp.transpose(w_uv[l], (1, 0, 2)).astype(BF16),
        "rw_mu": row(rw_mu[l]), "rw_w0": row(rw_w0[l]), "rw_a0": row(rw_a0[l]),
        "rw_wd": jnp.concatenate([rw_w_decay_up[l], zeros_lora], axis=0).astype(BF16),
        "rw_wa": jnp.concatenate([zeros_lora, rw_w_a_up[l]], axis=0).astype(BF16),
        "rw_k_k": row(rw_k_k[l]), "rw_k_a": row(rw_k_a[l]), "rw_r_k": row(rw_r_k[l]),
        "lnx_g": row(rw_lnx_g[l]), "lnx_b": row(rw_lnx_b[l]),
        "ln_g": row(gm_ln_g[l]), "ln_b": row(gm_ln_b[l]),
        "gm_ws": gm_w_s[l],
        "gm_bias": jnp.repeat(gm_b_s[l].T, GM_GROUP_W, axis=1),
        "gm_coef0": row(jnp.repeat(gm_w_s[l][:, 0, 0], GM_GROUP_W)),
        "gm_bias0": row(jnp.repeat(gm_b_s[l][:, 0], GM_GROUP_W)),
        "w_out": w_out[l].astype(BF16),
    }


def _pick(full, want):
    return want if full % want == 0 else full


def kernel(x_prompt, x_sample, c_prompt, c_sample, cache_ckv, cache_kpe, page_table, state_wkv, state_shift, w_ada, b_ada, norm_pre_g, norm_post_g, w_in, q_norm_g, kv_norm_g, w_uq, w_uk, w_uv, rw_mu, rw_w0, rw_w_decay_up, rw_a0, rw_w_a_up, rw_k_k, rw_k_a, rw_r_k, rw_lnx_g, rw_lnx_b, gm_ln_g, gm_ln_b, gm_w_s, gm_b_s, w_out):
    batch, seq, _ = x_prompt.shape
    dec_batch, dec_seq, _ = x_sample.shape
    depth = w_in.shape[0]
    n_pages = page_table.shape[1]
    past_len = n_pages * cache_ckv.shape[2]
    assert dec_seq == 1 and seq % CHUNK == 0

    assert seq % RW_CHUNK == 0
    tm_proj = _pick(seq, 512)
    tm_out = _pick(seq, 1024)
    tq = _pick(seq, 512)
    rw_nb = _pick(batch, 8)

    mod = _ada_mod(jnp.concatenate([c_prompt, c_sample], axis=0), w_ada, b_ada)
    cs_p, sn_p = _rope_tables(jnp.arange(seq))
    cs_s, sn_s = _rope_tables(jnp.full((dec_batch,), past_len))
    cache_kpe_t = jnp.swapaxes(cache_kpe, 2, 3)

    y_p = x_prompt.reshape(batch * seq, D_MODEL)
    y_s = x_sample.reshape(dec_batch, D_MODEL)
    outs = {k: [] for k in ("ckv_p", "kpe_p", "ckv_s", "kpe_s", "wkv_p", "wkv_s", "sh_p", "sh_s", "vc_p", "vc_s")}
    for l in range(depth):
        lw = _layer_weights(l, w_in, norm_pre_g, norm_post_g, q_norm_g, kv_norm_g, w_uq, w_uk, w_uv, rw_mu,
                            rw_w0, rw_w_decay_up, rw_a0, rw_w_a_up, rw_k_k, rw_k_a, rw_r_k, rw_lnx_g,
                            rw_lnx_b, gm_ln_g, gm_ln_b, gm_w_s, gm_b_s, w_out)
        mod_p = mod[l, :batch].reshape(batch, 1, 3 * D_MODEL)
        q, k, v, mg, ckv, kpe, rw_in, rg, gm, vn_last = _proj_prompt(y_p, mod_p, lw, cs_p, sn_p, batch, seq,
                                                                     tm_proj)
        o_mla = _flash(q, k, v, mg, batch, seq, tq)
        o_rw, wkv_new = _rwkv_prompt(rw_in, rg, lw, batch, seq, rw_nb)
        y_p = _out_prompt(o_mla, o_rw, gm, y_p, mod_p, lw, seq, tm_out)
        outs["ckv_p"].append(ckv.reshape(batch, seq, KV_LORA))
        outs["kpe_p"].append(kpe.reshape(batch, seq, MLA_ROPE))
        outs["wkv_p"].append(wkv_new)
        outs["sh_p"].append(rw_in.reshape(batch, seq, SHIFT_W)[:, -1])
        outs["vc_p"].append(vn_last)

        mod_s = mod[l, batch:]
        qa, qp, mg_s, ckv_s, kpe_s, rw_s, rg_s, gm_s, vn_s = _proj_sample(y_s, mod_s, lw, cs_s, sn_s)
        o_lat = _paged_attention(qa, qp, ckv_s, kpe_s, cache_ckv, cache_kpe_t, page_table, l)
        prep_s = _rwkv_prep_sample(rw_s, lw, state_shift[l])
        o_rw_s, wkv_s = _rwkv_mix(prep_s, state_wkv[l], dec_batch, 1, 1)
        y_s = _out_sample(o_lat, mg_s, o_rw_s, prep_s[6], rg_s, gm_s, y_s, mod_s, lw)
        outs["ckv_s"].append(ckv_s.reshape(dec_batch, 1, KV_LORA))
        outs["kpe_s"].append(kpe_s[:, :MLA_ROPE].reshape(dec_batch, 1, MLA_ROPE))
        outs["wkv_s"].append(wkv_s)
        outs["sh_s"].append(rw_s)
        outs["vc_s"].append(vn_s.reshape(dec_batch, 1, GM_W))

    st = lambda name: jnp.stack(outs[name])
    return (y_p.reshape(batch, seq, D_MODEL), y_s.reshape(dec_batch, 1, D_MODEL),
            st("ckv_p"), st("kpe_p"), st("ckv_s"), st("kpe_s"), st("wkv_p"), st("wkv_s"),
            st("sh_p"), st("sh_s"), st("vc_p"), st("vc_s"))
```

```python
import functools

import numpy as np
import jax
import jax.numpy as jnp
from jax import lax
from jax.experimental import pallas as pl
from jax.experimental.pallas import tpu as pltpu

F32 = jnp.float32
BF16 = jnp.bfloat16

D_MODEL = 1024
MLA_V = 128
MLA_W = D_MODEL // 2
MLA_HEADS = MLA_W // MLA_V
MLA_NOPE = 128
MLA_ROPE = 64
MLA_QK = MLA_NOPE + MLA_ROPE
Q_LORA = (3 * D_MODEL) // 8
KV_LORA = D_MODEL // 4
ROPE_THETA = 10000.0
ATTN_SCALE = MLA_QK ** -0.5
Q_SCALE = ATTN_SCALE * float(np.log2(np.e))
RWKV_W = D_MODEL // 4
RWKV_HEAD = 64
RWKV_HEADS = RWKV_W // RWKV_HEAD
DECAY_LORA = 64
ICLR_LORA = 64
SHIFT_W = 3 * RWKV_W + DECAY_LORA + ICLR_LORA
GN_EPS = 64e-5
GM_W = D_MODEL // 4
GM_GROUPS = 4
GM_GROUP_W = GM_W // GM_GROUPS
CHUNK = 128
RMS_EPS = 1e-6
LN_EPS = 1e-5

LANES = 128
SUBLANES = 8
VMEM_LIMIT_BYTES = 56 * 1024 * 1024

QHEAD_W = 2 * LANES
OFF_Q = 0
OFF_KV = OFF_Q + Q_LORA
OFF_MG = OFF_KV + KV_LORA
OFF_RW = OFF_MG + MLA_W
OFF_RG = OFF_RW + SHIFT_W
OFF_GU = OFF_RG + RWKV_W
OFF_GV = OFF_GU + GM_W
OFF_GG = OFF_GV + GM_W
OFF_KR = OFF_GG + GM_W
IN_W_P = OFF_KR + LANES

NEG_BIG = -1e30


def _cparams(*sem):
    return pltpu.CompilerParams(dimension_semantics=sem, vmem_limit_bytes=VMEM_LIMIT_BYTES)


def _silu(x):
    return x * jax.nn.sigmoid(x)


def _rms(x, g, eps=RMS_EPS):
    return x * lax.rsqrt(jnp.mean(x * x, axis=-1, keepdims=True) + eps) * g


def _rows2d(ref):
    m = ref[...]
    return m.reshape(m.shape[-2], m.shape[-1])


def _group_index(idx, group):
    shift = group.bit_length() - 1
    assert 1 << shift == group
    return lax.shift_right_logical(idx, shift)


def _head_ones(width, head):
    r = _group_index(lax.broadcasted_iota(jnp.int32, (width, width), 0), head)
    c = _group_index(lax.broadcasted_iota(jnp.int32, (width, width), 1), head)
    return jnp.where(r == c, 1.0, 0.0).astype(BF16)


def _head_sum(x, ones_bd):
    hi = x.astype(BF16)
    lo = (x - hi.astype(F32)).astype(BF16)
    return (jnp.dot(hi, ones_bd, preferred_element_type=F32)
            + jnp.dot(lo, ones_bd, preferred_element_type=F32))


def _rope_pair(blk, cs, sn):
    return blk * cs + pltpu.roll(blk, MLA_ROPE, 1) * sn


def _ada_kernel(c_ref, w_ref, b_ref, o_ref):
    c = _silu(c_ref[...]).astype(BF16)
    o_ref[0] = jnp.dot(c, w_ref[0], preferred_element_type=F32) + b_ref[0]


def _ada_mod(c_all, w_ada, b_ada):
    depth = w_ada.shape[0]
    n_rows = c_all.shape[0]
    n_tiles = (3 * D_MODEL) // D_MODEL
    return pl.pallas_call(
        _ada_kernel,
        grid=(depth, n_tiles),
        in_specs=[
            pl.BlockSpec((n_rows, D_MODEL), lambda l, j: (0, 0)),
            pl.BlockSpec((1, D_MODEL, D_MODEL), lambda l, j: (l, 0, j)),
            pl.BlockSpec((1, 1, D_MODEL), lambda l, j: (l, 0, j)),
        ],
        out_specs=pl.BlockSpec((1, n_rows, D_MODEL), lambda l, j: (l, 0, j)),
        out_shape=jax.ShapeDtypeStruct((depth, n_rows, 3 * D_MODEL), F32),
        compiler_params=_cparams("arbitrary", "arbitrary"),
        name="ada_mod",
    )(c_all, w_ada.astype(BF16), b_ada.reshape(depth, 1, 3 * D_MODEL))


def _proj_common(x_ref, mod_ref, gpre_ref, win_ref):
    mod = _rows2d(mod_ref)
    shift = mod[:, 0:D_MODEL]
    scale = mod[:, D_MODEL:2 * D_MODEL]
    h = _rms(x_ref[...], gpre_ref[...]) * (1.0 + scale) + shift
    hb = h.astype(BF16)

    def seg(off, width):
        return jnp.dot(hb, win_ref[:, off:off + width], preferred_element_type=F32)

    return seg


def _q_heads(seg, gq_ref, wuq_ref, cs, sn):
    qn = _rms(seg(OFF_Q, Q_LORA), gq_ref[...]).astype(BF16)
    q_all = jnp.dot(qn, wuq_ref[...], preferred_element_type=F32)
    heads = []
    for h in range(MLA_HEADS):
        nope = q_all[:, h * QHEAD_W:h * QHEAD_W + LANES] * Q_SCALE
        pe = _rope_pair(q_all[:, h * QHEAD_W + LANES:(h + 1) * QHEAD_W], cs, sn) * Q_SCALE
        heads.append((nope, pe))
    return heads


def _gm_norm(seg, lng_ref, lnb_ref):
    v = seg(OFF_GV, GM_W)
    mu = jnp.mean(v, axis=-1, keepdims=True)
    c = v - mu
    var = jnp.mean(c * c, axis=-1, keepdims=True)
    return c * lax.rsqrt(var + LN_EPS) * lng_ref[...] + lnb_ref[...]


def _proj_prompt_kernel(x_ref, mod_ref, gpre_ref, win_ref, gq_ref, wuq_ref, gkv_ref, wuk_ref, wuv_ref,
                        cs_ref, sn_ref, lng_ref, lnb_ref, ws_ref, bs_ref,
                        q_ref, k_ref, v_ref, mg_ref, ckv_ref, kpe_ref, rw_ref, rg_ref, gm_ref, vn_ref,
                        *, tiles_per_seq):
    seg = _proj_common(x_ref, mod_ref, gpre_ref, win_ref)
    cs = cs_ref[...]
    sn = sn_ref[...]
    tm = x_ref.shape[0]

    for h, (nope, pe) in enumerate(_q_heads(seg, gq_ref, wuq_ref, cs, sn)):
        q_ref[:, h * QHEAD_W:h * QHEAD_W + LANES] = nope.astype(BF16)
        q_ref[:, h * QHEAD_W + LANES:(h + 1) * QHEAD_W] = pe.astype(BF16)
    ckv = _rms(seg(OFF_KV, KV_LORA), gkv_ref[...])
    ckv_ref[...] = ckv
    kpe = _rope_pair(seg(OFF_KR, LANES), cs, sn)
    kpe_ref[...] = kpe[:, :MLA_ROPE]
    ckv_b = ckv.astype(BF16)
    k_nope = jnp.dot(ckv_b, wuk_ref[...], preferred_element_type=F32)
    v_ref[...] = jnp.dot(ckv_b, wuv_ref[...], preferred_element_type=F32).astype(BF16)
    kpe_b = kpe.astype(BF16)
    for h in range(MLA_HEADS):
        k_ref[:, h * QHEAD_W:h * QHEAD_W + LANES] = k_nope[:, h * MLA_NOPE:(h + 1) * MLA_NOPE].astype(BF16)
        k_ref[:, h * QHEAD_W + LANES:(h + 1) * QHEAD_W] = kpe_b
    mg_ref[...] = _silu(seg(OFF_MG, MLA_W)).astype(BF16)

    rw_ref[...] = seg(OFF_RW, SHIFT_W)
    rg_ref[...] = _silu(seg(OFF_RG, RWKV_W))

    vn = _gm_norm(seg, lng_ref, lnb_ref)
    vn_b = vn.astype(BF16)
    gate_u = _silu(seg(OFF_GG, GM_W)) * seg(OFF_GU, GM_W)
    row = lax.broadcasted_iota(jnp.int32, (CHUNK, CHUNK), 0)
    col = lax.broadcasted_iota(jnp.int32, (CHUNK, CHUNK), 1)
    lane_group = _group_index(lax.broadcasted_iota(jnp.int32, (CHUNK, GM_W), 1), GM_GROUP_W)
    w_tril = [jnp.where(row >= col, ws_ref[g], 0.0).astype(BF16) for g in range(GM_GROUPS)]
    for c in range(tm // CHUNK):
        vc = vn_b[c * CHUNK:(c + 1) * CHUNK]
        z = bs_ref[...]
        for g in range(GM_GROUPS):
            zg = jnp.dot(w_tril[g], vc, preferred_element_type=F32)
            z = z + jnp.where(lane_group == g, zg, 0.0)
        gm_ref[c * CHUNK:(c + 1) * CHUNK, :] = (gate_u[c * CHUNK:(c + 1) * CHUNK] * z).astype(BF16)

    @pl.when(pl.program_id(0) % tiles_per_seq == tiles_per_seq - 1)
    def _():
        vn_ref[0] = vn[tm - CHUNK:, :]


def _proj_prompt(x2d, mod_b, lw, cs, sn, batch, seq, tm):
    rows = x2d.shape[0]
    tiles_per_seq = seq // tm
    const = lambda i: (0, 0)
    row_blk = lambda i: (i, 0)
    outs = [
        (QHEAD_W * MLA_HEADS, BF16), (QHEAD_W * MLA_HEADS, BF16), (MLA_W, BF16), (MLA_W, BF16),
        (KV_LORA, F32), (MLA_ROPE, F32), (SHIFT_W, F32), (RWKV_W, F32), (GM_W, BF16),
    ]
    out_shape = [jax.ShapeDtypeStruct((rows, w), dt) for w, dt in outs]
    out_specs = [pl.BlockSpec((tm, w), row_blk) for w, _ in outs]
    out_shape.append(jax.ShapeDtypeStruct((batch, CHUNK, GM_W), F32))
    out_specs.append(pl.BlockSpec((1, CHUNK, GM_W), lambda i: (i // tiles_per_seq, 0, 0)))
    return pl.pallas_call(
        functools.partial(_proj_prompt_kernel, tiles_per_seq=tiles_per_seq),
        grid=(rows // tm,),
        in_specs=[
            pl.BlockSpec((tm, D_MODEL), row_blk),
            pl.BlockSpec((1, 1, 3 * D_MODEL), lambda i: (i // tiles_per_seq, 0, 0)),
            pl.BlockSpec((1, D_MODEL), const),
            pl.BlockSpec((D_MODEL, IN_W_P), const),
            pl.BlockSpec((1, Q_LORA), const),
            pl.BlockSpec((Q_LORA, QHEAD_W * MLA_HEADS), const),
            pl.BlockSpec((1, KV_LORA), const),
            pl.BlockSpec((KV_LORA, MLA_W), const),
            pl.BlockSpec((KV_LORA, MLA_W), const),
            pl.BlockSpec((tm, LANES), lambda i: (i % tiles_per_seq, 0)),
            pl.BlockSpec((tm, LANES), lambda i: (i % tiles_per_seq, 0)),
            pl.BlockSpec((1, GM_W), const),
            pl.BlockSpec((1, GM_W), const),
            pl.BlockSpec((GM_GROUPS, CHUNK, CHUNK), lambda i: (0, 0, 0)),
            pl.BlockSpec((CHUNK, GM_W), const),
        ],
        out_specs=out_specs,
        out_shape=out_shape,
        compiler_params=_cparams("arbitrary"),
        name="proj_prompt",
    )(x2d, mod_b, lw["g_pre"], lw["w_in"], lw["g_q"], lw["w_uq"], lw["g_kv"], lw["w_uk"], lw["w_uv"],
      cs, sn, lw["ln_g"], lw["ln_b"], lw["gm_ws"], lw["gm_bias"])


def _proj_sample_kernel(x_ref, mod_ref, gpre_ref, win_ref, gq_ref, wuq_ref, gkv_ref, wukt_ref,
                        cs_ref, sn_ref, lng_ref, lnb_ref, coef_ref, bias_ref,
                        qa_ref, qp_ref, mg_ref, ckv_ref, kpe_ref, rw_ref, rg_ref, gm_ref, vn_ref):
    seg = _proj_common(x_ref, mod_ref, gpre_ref, win_ref)
    cs = cs_ref[...]
    sn = sn_ref[...]
    for h, (nope, pe) in enumerate(_q_heads(seg, gq_ref, wuq_ref, cs, sn)):
        qa_ref[:, h * KV_LORA:(h + 1) * KV_LORA] = jnp.dot(nope.astype(BF16), wukt_ref[h],
                                                           preferred_element_type=F32)
        qp_ref[:, h * LANES:(h + 1) * LANES] = pe
    ckv_ref[...] = _rms(seg(OFF_KV, KV_LORA), gkv_ref[...])
    kpe_ref[...] = _rope_pair(seg(OFF_KR, LANES), cs, sn)
    mg_ref[...] = _silu(seg(OFF_MG, MLA_W)).astype(BF16)
    rw_ref[...] = seg(OFF_RW, SHIFT_W)
    rg_ref[...] = _silu(seg(OFF_RG, RWKV_W))
    vn = _gm_norm(seg, lng_ref, lnb_ref)
    vn_ref[...] = vn
    z = vn * coef_ref[...] + bias_ref[...]
    gm_ref[...] = (_silu(seg(OFF_GG, GM_W)) * seg(OFF_GU, GM_W) * z).astype(BF16)


def _proj_sample(x2d, mod_rows, lw, cs, sn):
    rows = x2d.shape[0]
    const = lambda i: (0, 0)
    outs = [
        (KV_LORA * MLA_HEADS, F32), (LANES * MLA_HEADS, F32), (MLA_W, BF16), (KV_LORA, F32), (LANES, F32),
        (SHIFT_W, F32), (RWKV_W, F32), (GM_W, BF16), (GM_W, F32),
    ]
    return pl.pallas_call(
        _proj_sample_kernel,
        grid=(1,),
        in_specs=[
            pl.BlockSpec((rows, D_MODEL), const),
            pl.BlockSpec((rows, 3 * D_MODEL), const),
            pl.BlockSpec((1, D_MODEL), const),
            pl.BlockSpec((D_MODEL, IN_W_P), const),
            pl.BlockSpec((1, Q_LORA), const),
            pl.BlockSpec((Q_LORA, QHEAD_W * MLA_HEADS), const),
            pl.BlockSpec((1, KV_LORA), const),
            pl.BlockSpec((MLA_HEADS, MLA_NOPE, KV_LORA), lambda i: (0, 0, 0)),
            pl.BlockSpec((rows, LANES), const),
            pl.BlockSpec((rows, LANES), const),
            pl.BlockSpec((1, GM_W), const),
            pl.BlockSpec((1, GM_W), const),
            pl.BlockSpec((1, GM_W), const),
            pl.BlockSpec((1, GM_W), const),
        ],
        out_specs=[pl.BlockSpec((rows, w), const) for w, _ in outs],
        out_shape=[jax.ShapeDtypeStruct((rows, w), dt) for w, dt in outs],
        compiler_params=_cparams("arbitrary"),
        name="proj_sample",
    )(x2d, mod_rows, lw["g_pre"], lw["w_in"], lw["g_q"], lw["w_uq"], lw["g_kv"], lw["w_ukt"],
      cs, sn, lw["ln_g"], lw["ln_b"], lw["gm_coef0"], lw["gm_bias0"])


def _flash_kernel(q_ref, k_ref, v_ref, g_ref, o_ref, *, tq):
    seq = q_ref.shape[1]
    row = lax.broadcasted_iota(jnp.int32, (tq, tq), 0)
    col = lax.broadcasted_iota(jnp.int32, (tq, tq), 1)
    for qi in range(seq // tq):
        rows = slice(qi * tq, (qi + 1) * tq)
        q = q_ref[0, rows, :]
        m = jnp.full((tq, 1), NEG_BIG, F32)
        l = jnp.zeros((tq, 1), F32)
        acc = jnp.zeros((tq, MLA_V), F32)
        for ki in range(qi + 1):
            cols = slice(ki * tq, (ki + 1) * tq)
            s = lax.dot_general(q, k_ref[0, cols, :], (((1,), (1,)), ((), ())),
                                preferred_element_type=F32)
            if ki == qi:
                s = jnp.where(row >= col, s, NEG_BIG)
            m_new = jnp.maximum(m, jnp.max(s, axis=-1, keepdims=True))
            alpha = jnp.exp2(m - m_new)
            p = jnp.exp2(s - m_new)
            l = alpha * l + jnp.sum(p, axis=-1, keepdims=True)
            acc = alpha * acc + jnp.dot(p.astype(BF16), v_ref[0, cols, :], preferred_element_type=F32)
            m = m_new
        o_ref[0, rows, :] = (acc / l * g_ref[0, rows, :].astype(F32)).astype(BF16)


def _flash(q, k, v, g, batch, seq, tq):
    q3 = q.reshape(batch, seq, QHEAD_W * MLA_HEADS)
    k3 = k.reshape(batch, seq, QHEAD_W * MLA_HEADS)
    v3 = v.reshape(batch, seq, MLA_W)
    g3 = g.reshape(batch, seq, MLA_W)
    head_blk = lambda b, h: (b, 0, h)
    out = pl.pallas_call(
        functools.partial(_flash_kernel, tq=tq),
        grid=(batch, MLA_HEADS),
        in_specs=[
            pl.BlockSpec((1, seq, QHEAD_W), head_blk),
            pl.BlockSpec((1, seq, QHEAD_W), head_blk),
            pl.BlockSpec((1, seq, MLA_V), head_blk),
            pl.BlockSpec((1, seq, MLA_V), head_blk),
        ],
        out_specs=pl.BlockSpec((1, seq, MLA_V), head_blk),
        out_shape=jax.ShapeDtypeStruct((batch, seq, MLA_W), BF16),
        compiler_params=_cparams("arbitrary", "arbitrary"),
        name="flash_prompt",
    )(q3, k3, v3, g3)
    return out.reshape(batch * seq, MLA_W)


N_SLOTS = 2


def _paged_kernel(pt_ref, qa_ref, qp_ref, cn_ref, kn_ref, ckv_hbm, kpe_hbm, o_ref, ckv_buf, kpe_buf, sem,
                  *, layer, n_pages, page, group_tokens):
    b = pl.program_id(0)
    n_seq = pl.num_programs(0)
    slot = lax.rem(b, N_SLOTS)

    def page_copies(seq_idx, sl):
        copies = []
        for i in range(n_pages):
            pg = pt_ref[seq_idx, i]
            copies.append(pltpu.make_async_copy(
                ckv_hbm.at[layer, pg], ckv_buf.at[sl, pl.ds(i * page, page), :], sem.at[0, sl]))
            copies.append(pltpu.make_async_copy(
                kpe_hbm.at[layer, pg], kpe_buf.at[sl, :, pl.ds(i * page, page)], sem.at[1, sl]))
        return copies

    def start_all(copies):
        for i, c in enumerate(copies):
            c.start(priority=(i // 2) % 2)

    @pl.when(b == 0)
    def _():
        start_all(page_copies(0, 0))

    for c in page_copies(b, slot):
        c.wait()

    nxt = lax.rem(b + 1, n_seq)
    nxt_slot = lax.rem(b + 1, N_SLOTS)
    start_all(page_copies(nxt, nxt_slot))

    qa = qa_ref[0]
    qp = qp_ref[0][:, :MLA_ROPE]
    qa_b = qa.astype(BF16)
    qp_b = qp.astype(BF16)
    groups = [slice(g * group_tokens, (g + 1) * group_tokens) for g in range(n_pages * page // group_tokens)]
    ckv = [ckv_buf[slot, gs, :].astype(BF16) for gs in groups]
    kpe_t = [kpe_buf[slot, :, gs].astype(BF16) for gs in groups]
    s = [lax.dot_general(qa_b, ck, (((1,), (1,)), ((), ())), preferred_element_type=F32)
         + jnp.dot(qp_b, kp, preferred_element_type=F32) for ck, kp in zip(ckv, kpe_t)]
    m_g = [jnp.max(sg, axis=-1, keepdims=True) for sg in s]
    p = [jnp.exp2(sg - mg) for sg, mg in zip(s, m_g)]
    l_g = [jnp.sum(pg, axis=-1, keepdims=True) for pg in p]
    acc_g = [jnp.dot(pg.astype(BF16), ck, preferred_element_type=F32) for pg, ck in zip(p, ckv)]
    cn = cn_ref[0]
    kn = kn_ref[0][:, :MLA_ROPE]
    s_new = jnp.sum(qa * cn, axis=-1, keepdims=True) + jnp.sum(qp * kn, axis=-1, keepdims=True)
    m = functools.reduce(jnp.maximum, m_g, s_new)
    p_new = jnp.exp2(s_new - m)
    w_g = [jnp.exp2(mg - m) for mg in m_g]
    l = sum(wg * lg for wg, lg in zip(w_g, l_g)) + p_new
    o_ref[0] = (sum(wg * ag for wg, ag in zip(w_g, acc_g)) + p_new * cn) / l

    @pl.when(b == n_seq - 1)
    def _():
        for c in page_copies(nxt, nxt_slot):
            c.wait()


def _paged_attention(qa, qp, ckv_new, kpe_new, cache_ckv, cache_kpe_t, page_table, layer):
    dec_batch, n_pages = page_table.shape
    page = cache_ckv.shape[2]
    head_pad = ((0, 0), (0, SUBLANES - MLA_HEADS), (0, 0))
    qa3 = jnp.pad(qa.reshape(dec_batch, MLA_HEADS, KV_LORA), head_pad)
    qp3 = jnp.pad(qp.reshape(dec_batch, MLA_HEADS, LANES), head_pad)
    cn3 = ckv_new.reshape(dec_batch, 1, KV_LORA)
    kn3 = kpe_new.reshape(dec_batch, 1, LANES)
    row_blk = lambda b, pt: (b, 0, 0)
    out = pl.pallas_call(
        functools.partial(_paged_kernel, layer=layer, n_pages=n_pages, page=page,
                          group_tokens=_pick(n_pages * page, 2048)),
        grid_spec=pltpu.PrefetchScalarGridSpec(
            num_scalar_prefetch=1,
            grid=(dec_batch,),
            in_specs=[
                pl.BlockSpec((1, SUBLANES, KV_LORA), row_blk),
                pl.BlockSpec((1, SUBLANES, LANES), row_blk),
                pl.BlockSpec((1, 1, KV_LORA), row_blk),
                pl.BlockSpec((1, 1, LANES), row_blk),
                pl.BlockSpec(memory_space=pl.ANY),
                pl.BlockSpec(memory_space=pl.ANY),
            ],
            out_specs=pl.BlockSpec((1, SUBLANES, KV_LORA), row_blk),
            scratch_shapes=[
                pltpu.VMEM((N_SLOTS, n_pages * page, KV_LORA), F32),
                pltpu.VMEM((N_SLOTS, MLA_ROPE, n_pages * page), F32),
                pltpu.SemaphoreType.DMA((2, N_SLOTS)),
            ],
        ),
        out_shape=jax.ShapeDtypeStruct((dec_batch, SUBLANES, KV_LORA), F32),
        compiler_params=_cparams("arbitrary"),
        name="paged_attention",
    )(page_table, qa3, qp3, cn3, kn3, cache_ckv, cache_kpe_t)
    return out[:, :MLA_HEADS].reshape(dec_batch, MLA_HEADS * KV_LORA)


def _rwkv_prep_values(p, prev, mu_ref, w0_ref, wd_ref, a0_ref, wa_ref, kk_ref, ka_ref, rk_ref):
    xm = p + (prev - p) * mu_ref[...]
    r = xm[:, 0:RWKV_W]
    k = xm[:, RWKV_W:2 * RWKV_W]
    v = xm[:, 2 * RWKV_W:3 * RWKV_W]
    tail = xm[:, 3 * RWKV_W:]
    lane = lax.broadcasted_iota(jnp.int32, tail.shape, 1)
    lora_in = jnp.where(lane < DECAY_LORA, jnp.tanh(tail), tail).astype(BF16)
    dw = jnp.dot(lora_in, wd_ref[...], preferred_element_type=F32)
    da = jnp.dot(lora_in, wa_ref[...], preferred_element_type=F32)
    z = -(w0_ref[...] + dw)
    softplus = jnp.maximum(z, 0.0) + jnp.log(1.0 + jnp.exp(-jnp.abs(z)))
    w = -softplus - 0.5
    log_decay = -jnp.exp(w)
    a = jax.nn.sigmoid(a0_ref[...] + da)
    ones_bd = _head_ones(RWKV_W, RWKV_HEAD)
    kk = k * kk_ref[...]
    kk = kk / jnp.maximum(jnp.sqrt(_head_sum(kk * kk, ones_bd)), 1e-12)
    k_mod = k * (1.0 + (a - 1.0) * ka_ref[...])
    bonus = _head_sum(r * k_mod * rk_ref[...], ones_bd) * v
    return r, k_mod, v, kk, kk * a, log_decay, bonus


def _rwkv_param_specs(index_map):
    widths = (SHIFT_W, RWKV_W, None, RWKV_W, None, RWKV_W, RWKV_W, RWKV_W)
    return [pl.BlockSpec((LANES, RWKV_W) if w is None else (1, w), index_map) for w in widths]


def _rwkv_params(lw):
    return (lw["rw_mu"], lw["rw_w0"], lw["rw_wd"], lw["rw_a0"], lw["rw_wa"], lw["rw_k_k"], lw["rw_k_a"],
            lw["rw_r_k"])


def _rwkv_prep_sample_kernel(p_ref, prev_ref, mu_ref, w0_ref, wd_ref, a0_ref, wa_ref, kk_ref, ka_ref, rk_ref,
                             r_o, k_o, v_o, kk_o, b_o, d_o, bonus_o):
    r, k_mod, v, kk, b, log_decay, bonus = _rwkv_prep_values(
        p_ref[...], prev_ref[...], mu_ref, w0_ref, wd_ref, a0_ref, wa_ref, kk_ref, ka_ref, rk_ref)
    r_o[...] = r
    k_o[...] = k_mod
    v_o[...] = v
    kk_o[...] = kk
    b_o[...] = b
    d_o[...] = jnp.exp(log_decay)
    bonus_o[...] = bonus


def _rwkv_prep_sample(rw_in, lw, shift_rows):
    rows = rw_in.shape[0]
    const = lambda i: (0, 0)
    return pl.pallas_call(
        _rwkv_prep_sample_kernel,
        grid=(1,),
        in_specs=[pl.BlockSpec((rows, SHIFT_W), const)] * 2 + _rwkv_param_specs(const),
        out_specs=[pl.BlockSpec((rows, RWKV_W), const)] * 7,
        out_shape=[jax.ShapeDtypeStruct((rows, RWKV_W), F32)] * 7,
        compiler_params=_cparams("arbitrary"),
        name="rwkv_prep_sample",
    )(rw_in, shift_rows, *_rwkv_params(lw))


RW_CHUNK = 64
RW_SUB = 16


def _split3(x):
    hi = x.astype(BF16)
    r1 = x - hi.astype(F32)
    mid = r1.astype(BF16)
    lo = (r1 - mid.astype(F32)).astype(BF16)
    return hi, mid, lo


def _block_diag(y, bd_mask):
    return jnp.where(bd_mask, jnp.concatenate([y] * RWKV_HEADS, axis=0), 0.0).astype(BF16)


def _mm(x, y_bd):
    return jnp.dot(x.astype(BF16), y_bd, preferred_element_type=F32)


def _rwkv_chunks(vals, n0, masks):
    bd_mask, tri_incl, strict, incl, same_sub, eye_tiled, ones_bd = masks
    C = RW_CHUNK
    seqs = range(len(vals))
    bd = lambda y: _block_diag(y, bd_mask)
    r, kx, v, kap, bb, lam = ([val[j] for val in vals] for j in range(6))
    cum = [sum(jnp.dot(tri_incl, part, preferred_element_type=F32) for part in _split3(lam[i])) for i in seqs]
    cum_last = [c[C - 1:C, :] for c in cum]
    e_neg = [jnp.exp(-c) for c in cum]
    ap = [jnp.concatenate([kap[i] * jnp.exp(cum[i] - lam[i]), r[i] * jnp.exp(cum[i])], axis=0).astype(BF16)
          for i in seqs]
    rhs = [jnp.concatenate([bd(bb[i] * e_neg[i]), bd(kx[i] * e_neg[i])], axis=0) for i in seqs]
    g = [lax.dot_general(ap[i], rhs[i], (((1,), (1,)), ((), ())), preferred_element_type=F32) for i in seqs]
    l_ab = [jnp.where(strict, gi[:C, :RWKV_W], 0.0) for gi in g]
    l_ak = [jnp.where(strict, gi[:C, RWKV_W:], 0.0) for gi in g]
    q_pb = [jnp.where(incl, gi[C:, :RWKV_W], 0.0) for gi in g]
    q_pk = [jnp.where(incl, gi[C:, RWKV_W:], 0.0) for gi in g]
    v_bd = [bd(vi) for vi in v]
    apn = [jnp.dot(ap[i], bd(n0[i]), preferred_element_type=F32) for i in seqs]
    z_w = [apn[i][:C] + _mm(l_ak[i], v_bd[i]) for i in seqs]

    l_d = [jnp.where(same_sub, li, 0.0) for li in l_ab]
    z_l = [l_ab[i] - l_d[i] for i in seqs]
    x = [(-li).astype(BF16) for li in l_d]
    power = l_d
    n_double = RW_SUB.bit_length() - 1
    for step in range(n_double):
        xz = [jnp.dot(x[i], jnp.concatenate([bd(z_l[i]), bd(z_w[i])], axis=1), preferred_element_type=F32)
              for i in seqs]
        z_l = [z_l[i] + xz[i][:, :RWKV_W] for i in seqs]
        z_w = [z_w[i] + xz[i][:, RWKV_W:] for i in seqs]
        if step + 1 < n_double:
            power = [_mm(pw, bd(pw)) for pw in power]
            x = [pw.astype(BF16) for pw in power]
    n = z_l
    n_sq = [_mm(ni, bd(ni)) for ni in n]
    y = [z_w[i] + _mm(n_sq[i], bd(z_w[i])) for i in seqs]
    u = [_mm(n[i], bd(y[i])) - y[i] for i in seqs]

    o = [apn[i][C:] + _mm(q_pb[i], bd(u[i])) + _mm(q_pk[i], v_bd[i]) for i in seqs]

    e_end = [jnp.exp(cum_last[i] - cum[i]) for i in seqs]
    lhs_t = [jnp.concatenate([bb[i] * e_end[i], kx[i] * e_end[i]], axis=0).T.astype(BF16) for i in seqs]
    full = [jnp.dot(lhs_t[i], jnp.concatenate([u[i], v[i]], axis=0).astype(BF16), preferred_element_type=F32)
            for i in seqs]
    heads = [slice(h * RWKV_HEAD, (h + 1) * RWKV_HEAD) for h in range(RWKV_HEADS)]
    delta = [sum(jnp.where(bd_mask[hs], f[hs], 0.0) for hs in heads) for f in full]
    g_wide = [sum(jnp.dot(part, ones_bd, preferred_element_type=F32)
                  for part in _split3(jnp.where(eye_tiled, jnp.exp(cl), 0.0))) for cl in cum_last]
    return o, [n0[i] * g_wide[i] + delta[i] for i in seqs]


def _rwkv_prompt_kernel(p_ref, rg_ref, mu_ref, w0_ref, wd_ref, a0_ref, wa_ref, kk_ref, ka_ref, rk_ref,
                        lnxg_ref, lnxb_ref, o_ref, sfin_ref, state_ref, last_ref):
    nb, C, _ = p_ref.shape
    c_idx = pl.program_id(1)

    @pl.when(c_idx == 0)
    def _():
        state_ref[...] = jnp.zeros(state_ref.shape, F32)
        last_ref[...] = jnp.zeros(last_ref.shape, F32)

    p = p_ref[...].reshape(nb * C, SHIFT_W)
    row = lax.broadcasted_iota(jnp.int32, (nb * C, 1), 0)
    prev = pltpu.roll(p, 1, 0)
    for i in range(nb):
        prev = jnp.where(row == i * C, last_ref[i, 0:1, :], prev)
    vals = _rwkv_prep_values(p, prev, mu_ref, w0_ref, wd_ref, a0_ref, wa_ref, kk_ref, ka_ref, rk_ref)
    bonus = vals[6]

    lane = lax.broadcasted_iota(jnp.int32, (C, RWKV_W), 1)
    t_idx = lax.broadcasted_iota(jnp.int32, (C, RWKV_W), 0)
    s_idx = lane & (RWKV_HEAD - 1)
    row_big = lax.broadcasted_iota(jnp.int32, (RWKV_W, RWKV_W), 0)
    lane_big = lax.broadcasted_iota(jnp.int32, (RWKV_W, RWKV_W), 1)
    bd_mask = _group_index(row_big, RWKV_HEAD) == _group_index(lane_big, RWKV_HEAD)
    ones_bd = jnp.where(bd_mask, 1.0, 0.0).astype(BF16)
    tri_r = lax.broadcasted_iota(jnp.int32, (C, C), 0)
    tri_c = lax.broadcasted_iota(jnp.int32, (C, C), 1)
    masks = (bd_mask, jnp.where(tri_r >= tri_c, 1.0, 0.0).astype(BF16), t_idx > s_idx, t_idx >= s_idx,
             _group_index(t_idx, RW_SUB) == _group_index(s_idx, RW_SUB), t_idx == s_idx, ones_bd)

    per_seq = [tuple(x[i * C:(i + 1) * C] for x in vals[:6]) for i in range(nb)]
    outs, n_new = _rwkv_chunks(per_seq, [state_ref[i] for i in range(nb)], masks)
    for i in range(nb):
        state_ref[i] = n_new[i]
        last_ref[i, 0:1, :] = p[(i + 1) * C - 1:(i + 1) * C, :]
    o = jnp.concatenate(outs, axis=0)

    c = o - _head_sum(o, ones_bd) * (1.0 / RWKV_HEAD)
    var = _head_sum(c * c, ones_bd) * (1.0 / RWKV_HEAD)
    on = c * lax.rsqrt(var + GN_EPS) * lnxg_ref[...] + lnxb_ref[...]
    o_ref[...] = ((on + bonus) * rg_ref[...].reshape(nb * C, RWKV_W)).astype(BF16).reshape(nb, C, RWKV_W)

    @pl.when(c_idx == pl.num_programs(1) - 1)
    def _():
        sfin_ref[...] = state_ref[...]


def _rwkv_prompt(rw_in, rg, lw, batch, seq, nb):
    const = lambda b, c: (0, 0)
    blk = lambda b, c: (b, c, 0)
    o, s_fin = pl.pallas_call(
        _rwkv_prompt_kernel,
        grid=(batch // nb, seq // RW_CHUNK),
        in_specs=[pl.BlockSpec((nb, RW_CHUNK, SHIFT_W), blk), pl.BlockSpec((nb, RW_CHUNK, RWKV_W), blk)]
        + _rwkv_param_specs(const) + [pl.BlockSpec((1, RWKV_W), const)] * 2,
        out_specs=[pl.BlockSpec((nb, RW_CHUNK, RWKV_W), blk),
                   pl.BlockSpec((nb, RWKV_HEAD, RWKV_W), lambda b, c: (b, 0, 0))],
        out_shape=[jax.ShapeDtypeStruct((batch, seq, RWKV_W), BF16),
                   jax.ShapeDtypeStruct((batch, RWKV_HEAD, RWKV_W), F32)],
        scratch_shapes=[pltpu.VMEM((nb, RWKV_HEAD, RWKV_W), F32), pltpu.VMEM((nb, SUBLANES, SHIFT_W), F32)],
        compiler_params=_cparams("arbitrary", "arbitrary"),
        name="rwkv_prompt",
    )(rw_in.reshape(batch, seq, SHIFT_W), rg.reshape(batch, seq, RWKV_W), *_rwkv_params(lw),
      lw["lnx_g"], lw["lnx_b"])
    wkv = jnp.transpose(s_fin.reshape(batch, RWKV_HEAD, RWKV_HEADS, RWKV_HEAD), (0, 2, 3, 1))
    return o.reshape(batch * seq, RWKV_W), wkv


V_HALF = RWKV_HEAD // 2


def _scan_kernel(x_ref, v_ref, s0_ref, o_ref, sfin_ref, s_ref):
    tb = pl.program_id(1)

    @pl.when(tb == 0)
    def _():
        s_ref[...] = s0_ref[...]

    def step(t, carry):
        kk = x_ref[t, 0]
        b = x_ref[t, 1]
        d = x_ref[t, 2]
        kx = x_ref[t, 3]
        r = x_ref[t, 4]
        for vp in range(V_HALF):
            sv = s_ref[vp]
            sa = jnp.sum(sv * kk, axis=0, keepdims=True)
            sn = sv * d - sa * b + v_ref[t, pl.ds(vp, 1), :] * kx
            s_ref[vp] = sn
            o_ref[t, pl.ds(vp, 1), :] = jnp.sum(sn * r, axis=0, keepdims=True)
        return carry

    lax.fori_loop(0, x_ref.shape[0], step, 0)

    @pl.when(tb == pl.num_programs(1) - 1)
    def _():
        sfin_ref[...] = s_ref[...]


def _scan(x5, v_t, s0, t_blk):
    steps, _, _, lanes = x5.shape
    lane_tiles = lanes // LANES
    return pl.pallas_call(
        _scan_kernel,
        grid=(lane_tiles, steps // t_blk),
        in_specs=[
            pl.BlockSpec((t_blk, 5, RWKV_HEAD, LANES), lambda j, t: (t, 0, 0, j)),
            pl.BlockSpec((t_blk, V_HALF, LANES), lambda j, t: (t, 0, j)),
            pl.BlockSpec((V_HALF, RWKV_HEAD, LANES), lambda j, t: (0, 0, j)),
        ],
        out_specs=[
            pl.BlockSpec((t_blk, V_HALF, LANES), lambda j, t: (t, 0, j)),
            pl.BlockSpec((V_HALF, RWKV_HEAD, LANES), lambda j, t: (0, 0, j)),
        ],
        out_shape=[
            jax.ShapeDtypeStruct((steps, V_HALF, lanes), F32),
            jax.ShapeDtypeStruct((V_HALF, RWKV_HEAD, lanes), F32),
        ],
        scratch_shapes=[pltpu.VMEM((V_HALF, RWKV_HEAD, LANES), F32)],
        compiler_params=_cparams("arbitrary", "arbitrary"),
        name="rwkv_scan",
    )(x5, v_t, s0)


def _rwkv_mix(prep, state, batch, seq, t_blk):
    r, k_mod, v, kk, b, d, _ = prep
    H, N = RWKV_HEADS, RWKV_HEAD
    x5 = jnp.stack([kk, b, d, k_mod, r]).reshape(5, batch, seq, H, N)
    x5 = jnp.transpose(x5, (2, 0, 4, 1, 3)).reshape(seq, 5, N, 1, batch * H)
    x5 = jnp.broadcast_to(x5, (seq, 5, N, 2, batch * H)).reshape(seq, 5, N, 2 * batch * H)
    v_t = jnp.transpose(v.reshape(batch, seq, H, 2, V_HALF), (1, 4, 3, 0, 2)).reshape(seq, V_HALF, 2 * batch * H)
    s0 = jnp.transpose(state.reshape(batch, H, 2, V_HALF, N), (3, 4, 2, 0, 1)).reshape(V_HALF, N, 2 * batch * H)
    o_t, s_fin = _scan(x5, v_t, s0, t_blk)
    o = jnp.transpose(o_t.reshape(seq, V_HALF, 2, batch, H), (3, 0, 4, 2, 1)).reshape(batch * seq, H * N)
    s_new = jnp.transpose(s_fin.reshape(V_HALF, N, 2, batch, H), (3, 4, 2, 0, 1)).reshape(batch, H, N, N)
    return o, s_new


def _mix_out(mixed, x_ref, mod_ref, wout_ref, gpost_ref, y_ref):
    out = jnp.dot(mixed, wout_ref[...], preferred_element_type=F32)
    gate = _rows2d(mod_ref)[:, 2 * D_MODEL:]
    y_ref[...] = x_ref[...] + gate * _rms(out, gpost_ref[...])


def _out_prompt_kernel(om_ref, orw_ref, gm_ref, x_ref, mod_ref, wout_ref, gpost_ref, y_ref):
    mixed = jnp.concatenate([om_ref[...], orw_ref[...], gm_ref[...]], axis=-1)
    _mix_out(mixed, x_ref, mod_ref, wout_ref, gpost_ref, y_ref)


def _out_sample_kernel(lat_ref, mg_ref, orw_ref, bonus_ref, rg_ref, gm_ref, x_ref, mod_ref, wout_ref, gpost_ref,
                       lnxg_ref, lnxb_ref, wuv_ref, y_ref):
    lat = lat_ref[...]
    o_mla = jnp.concatenate(
        [jnp.dot(lat[:, h * KV_LORA:(h + 1) * KV_LORA].astype(BF16), wuv_ref[h],
                 preferred_element_type=F32) for h in range(MLA_HEADS)], axis=-1)
    o_mla = (o_mla * mg_ref[...].astype(F32)).astype(BF16)
    ones_bd = _head_ones(RWKV_W, RWKV_HEAD)
    o = orw_ref[...]
    c = o - _head_sum(o, ones_bd) * (1.0 / RWKV_HEAD)
    var = _head_sum(c * c, ones_bd) * (1.0 / RWKV_HEAD)
    on = c * lax.rsqrt(var + GN_EPS) * lnxg_ref[...] + lnxb_ref[...]
    o_rw = ((on + bonus_ref[...]) * rg_ref[...]).astype(BF16)
    mixed = jnp.concatenate([o_mla, o_rw, gm_ref[...]], axis=-1)
    _mix_out(mixed, x_ref, mod_ref, wout_ref, gpost_ref, y_ref)


def _out_prompt(o_mla, o_rw, gm, x2d, mod_b, lw, seq, tm):
    rows = x2d.shape[0]
    tiles_per_seq = seq // tm
    const = lambda i: (0, 0)
    row_blk = lambda i: (i, 0)
    return pl.pallas_call(
        _out_prompt_kernel,
        grid=(rows // tm,),
        in_specs=[
            pl.BlockSpec((tm, MLA_W), row_blk),
            pl.BlockSpec((tm, RWKV_W), row_blk),
            pl.BlockSpec((tm, GM_W), row_blk),
            pl.BlockSpec((tm, D_MODEL), row_blk),
            pl.BlockSpec((1, 1, 3 * D_MODEL), lambda i: (i // tiles_per_seq, 0, 0)),
            pl.BlockSpec((D_MODEL, D_MODEL), const),
            pl.BlockSpec((1, D_MODEL), const),
        ],
        out_specs=pl.BlockSpec((tm, D_MODEL), row_blk),
        out_shape=jax.ShapeDtypeStruct((rows, D_MODEL), F32),
        compiler_params=_cparams("arbitrary"),
        name="out_prompt",
    )(o_mla, o_rw, gm, x2d, mod_b, lw["w_out"], lw["g_post"])


def _out_sample(o_lat, mg, o_rw, bonus, rg, gm, x2d, mod_rows, lw):
    rows = x2d.shape[0]
    const = lambda i: (0, 0)
    full = lambda w: pl.BlockSpec((rows, w), const)
    return pl.pallas_call(
        _out_sample_kernel,
        grid=(1,),
        in_specs=[
            full(MLA_HEADS * KV_LORA), full(MLA_W), full(RWKV_W), full(RWKV_W), full(RWKV_W), full(GM_W),
            full(D_MODEL), full(3 * D_MODEL),
            pl.BlockSpec((D_MODEL, D_MODEL), const),
            pl.BlockSpec((1, D_MODEL), const),
            pl.BlockSpec((1, RWKV_W), const),
            pl.BlockSpec((1, RWKV_W), const),
            pl.BlockSpec((MLA_HEADS, KV_LORA, MLA_V), lambda i: (0, 0, 0)),
        ],
        out_specs=full(D_MODEL),
        out_shape=jax.ShapeDtypeStruct((rows, D_MODEL), F32),
        compiler_params=_cparams("arbitrary"),
        name="out_sample",
    )(o_lat, mg, o_rw, bonus, rg, gm, x2d, mod_rows, lw["w_out"], lw["g_post"], lw["lnx_g"], lw["lnx_b"],
      lw["w_uv_h"])


def _swap_halves(w):
    half = MLA_ROPE // 2
    return jnp.concatenate([w[..., half:], w[..., :half]], axis=-1)


def _rope_tables(pos):
    half = MLA_ROPE // 2
    inv = ROPE_THETA ** (-jnp.arange(half, dtype=F32) / half)
    ang = pos.astype(F32)[:, None] * inv[None, :]
    cos, sin = jnp.cos(ang), jnp.sin(ang)
    zeros = jnp.zeros((pos.shape[0], LANES - MLA_ROPE), F32)
    return (jnp.concatenate([cos, cos, zeros], axis=-1), jnp.concatenate([-sin, sin, zeros], axis=-1))


def _layer_weights(l, w_in, norm_pre_g, norm_post_g, q_norm_g, kv_norm_g, w_uq, w_uk, w_uv, rw_mu, rw_w0,
                   rw_w_decay_up, rw_a0, rw_w_a_up, rw_k_k, rw_k_a, rw_r_k, rw_lnx_g, rw_lnx_b, gm_ln_g,
                   gm_ln_b, gm_w_s, gm_b_s, w_out):
    w = w_in[l]
    split_lo, split_hi = Q_LORA + KV_LORA, Q_LORA + KV_LORA + MLA_ROPE
    kr = w[:, split_lo:split_hi]
    w_in_p = jnp.concatenate([w[:, :split_lo], w[:, split_hi:], kr, _swap_halves(kr)], axis=1).astype(BF16)
    uq = w_uq[l]
    pe = uq[..., MLA_NOPE:]
    w_uq_p = jnp.concatenate([uq, _swap_halves(pe)], axis=-1).reshape(Q_LORA, MLA_HEADS * QHEAD_W).astype(BF16)
    zeros_lora = jnp.zeros((LANES - DECAY_LORA, RWKV_W), F32)
    row = lambda a: a.reshape(1, -1)
    return {
        "w_in": w_in_p, "g_pre": row(norm_pre_g[l]), "g_post": row(norm_post_g[l]),
        "g_q": row(q_norm_g[l]), "g_kv": row(kv_norm_g[l]), "w_uq": w_uq_p,
        "w_uk": w_uk[l].reshape(KV_LORA, MLA_W).astype(BF16),
        "w_uv": w_uv[l].reshape(KV_LORA, MLA_W).astype(BF16),
        "w_ukt": jnp.transpose(w_uk[l], (1, 2, 0)).astype(BF16),
        "w_uv_h": jnp.transpose(w_uv[l], (1, 0, 2)).astype(BF16),
        "rw_mu": row(rw_mu[l]), "rw_w0": row(rw_w0[l]), "rw_a0": row(rw_a0[l]),
        "rw_wd": jnp.concatenate([rw_w_decay_up[l], zeros_lora], axis=0).astype(BF16),
        "rw_wa": jnp.concatenate([zeros_lora, rw_w_a_up[l]], axis=0).astype(BF16),
        "rw_k_k": row(rw_k_k[l]), "rw_k_a": row(rw_k_a[l]), "rw_r_k": row(rw_r_k[l]),
        "lnx_g": row(rw_lnx_g[l]), "lnx_b": row(rw_lnx_b[l]),
        "ln_g": row(gm_ln_g[l]), "ln_b": row(gm_ln_b[l]),
        "gm_ws": gm_w_s[l],
        "gm_bias": jnp.repeat(gm_b_s[l].T, GM_GROUP_W, axis=1),
        "gm_coef0": row(jnp.repeat(gm_w_s[l][:, 0, 0], GM_GROUP_W)),
        "gm_bias0": row(jnp.repeat(gm_b_s[l][:, 0], GM_GROUP_W)),
        "w_out": w_out[l].astype(BF16),
    }


def _pick(full, want):
    return want if full % want == 0 else full


def kernel(x_prompt, x_sample, c_prompt, c_sample, cache_ckv, cache_kpe, page_table, state_wkv, state_shift, w_ada, b_ada, norm_pre_g, norm_post_g, w_in, q_norm_g, kv_norm_g, w_uq, w_uk, w_uv, rw_mu, rw_w0, rw_w_decay_up, rw_a0, rw_w_a_up, rw_k_k, rw_k_a, rw_r_k, rw_lnx_g, rw_lnx_b, gm_ln_g, gm_ln_b, gm_w_s, gm_b_s, w_out):
    batch, seq, _ = x_prompt.shape
    dec_batch, dec_seq, _ = x_sample.shape
    depth = w_in.shape[0]
    n_pages = page_table.shape[1]
    past_len = n_pages * cache_ckv.shape[2]
    assert dec_seq == 1 and seq % CHUNK == 0

    assert seq % RW_CHUNK == 0
    tm_proj = _pick(seq, 512)
    tm_out = _pick(seq, 1024)
    tq = _pick(seq, 512)
    rw_nb = _pick(batch, 8)

    mod = _ada_mod(jnp.concatenate([c_prompt, c_sample], axis=0), w_ada, b_ada)
    cs_p, sn_p = _rope_tables(jnp.arange(seq))
    cs_s, sn_s = _rope_tables(jnp.full((dec_batch,), past_len))
    cache_kpe_t = jnp.swapaxes(cache_kpe, 2, 3)

    y_p = x_prompt.reshape(batch * seq, D_MODEL)
    y_s = x_sample.reshape(dec_batch, D_MODEL)
    outs = {k: [] for k in ("ckv_p", "kpe_p", "ckv_s", "kpe_s", "wkv_p", "wkv_s", "sh_p", "sh_s", "vc_p", "vc_s")}
    for l in range(depth):
        lw = _layer_weights(l, w_in, norm_pre_g, norm_post_g, q_norm_g, kv_norm_g, w_uq, w_uk, w_uv, rw_mu,
                            rw_w0, rw_w_decay_up, rw_a0, rw_w_a_up, rw_k_k, rw_k_a, rw_r_k, rw_lnx_g,
                            rw_lnx_b, gm_ln_g, gm_ln_b, gm_w_s, gm_b_s, w_out)
        mod_p = mod[l, :batch].reshape(batch, 1, 3 * D_MODEL)
        q, k, v, mg, ckv, kpe, rw_in, rg, gm, vn_last = _proj_prompt(y_p, mod_p, lw, cs_p, sn_p, batch, seq,
                                                                     tm_proj)
        o_mla = _flash(q, k, v, mg, batch, seq, tq)
        o_rw, wkv_new = _rwkv_prompt(rw_in, rg, lw, batch, seq, rw_nb)
        y_p = _out_prompt(o_mla, o_rw, gm, y_p, mod_p, lw, seq, tm_out)
        outs["ckv_p"].append(ckv.reshape(batch, seq, KV_LORA))
        outs["kpe_p"].append(kpe.reshape(batch, seq, MLA_ROPE))
        outs["wkv_p"].append(wkv_new)
        outs["sh_p"].append(rw_in.reshape(batch, seq, SHIFT_W)[:, -1])
        outs["vc_p"].append(vn_last)

        mod_s = mod[l, batch:]
        qa, qp, mg_s, ckv_s, kpe_s, rw_s, rg_s, gm_s, vn_s = _proj_sample(y_s, mod_s, lw, cs_s, sn_s)
        o_lat = _paged_attention(qa, qp, ckv_s, kpe_s, cache_ckv, cache_kpe_t, page_table, l)
        prep_s = _rwkv_prep_sample(rw_s, lw, state_shift[l])
        o_rw_s, wkv_s = _rwkv_mix(prep_s, state_wkv[l], dec_batch, 1, 1)
        y_s = _out_sample(o_lat, mg_s, o_rw_s, prep_s[6], rg_s, gm_s, y_s, mod_s, lw)
        outs["ckv_s"].append(ckv_s.reshape(dec_batch, 1, KV_LORA))
        outs["kpe_s"].append(kpe_s[:, :MLA_ROPE].reshape(dec_batch, 1, MLA_ROPE))
        outs["wkv_s"].append(wkv_s)
        outs["sh_s"].append(rw_s)
        outs["vc_s"].append(vn_s.reshape(dec_batch, 1, GM_W))

    st = lambda name: jnp.stack(outs[name])
    return (y_p.reshape(batch, seq, D_MODEL), y_s.reshape(dec_batch, 1, D_MODEL),
            st("ckv_p"), st("kpe_p"), st("ckv_s"), st("kpe_s"), st("wkv_p"), st("wkv_s"),
            st("sh_p"), st("sh_s"), st("vc_p"), st("vc_s"))
```

```python
import functools

import numpy as np
import jax
import jax.numpy as jnp
from jax import lax
from jax.experimental import pallas as pl
from jax.experimental.pallas import tpu as pltpu

F32 = jnp.float32
BF16 = jnp.bfloat16

D_MODEL = 1024
MLA_V = 128
MLA_W = D_MODEL // 2
MLA_HEADS = MLA_W // MLA_V
MLA_NOPE = 128
MLA_ROPE = 64
MLA_QK = MLA_NOPE + MLA_ROPE
Q_LORA = (3 * D_MODEL) // 8
KV_LORA = D_MODEL // 4
ROPE_THETA = 10000.0
ATTN_SCALE = MLA_QK ** -0.5
Q_SCALE = ATTN_SCALE * float(np.log2(np.e))
RWKV_W = D_MODEL // 4
RWKV_HEAD = 64
RWKV_HEADS = RWKV_W // RWKV_HEAD
DECAY_LORA = 64
ICLR_LORA = 64
SHIFT_W = 3 * RWKV_W + DECAY_LORA + ICLR_LORA
GN_EPS = 64e-5
GM_W = D_MODEL // 4
GM_GROUPS = 4
GM_GROUP_W = GM_W // GM_GROUPS
CHUNK = 128
RMS_EPS = 1e-6
LN_EPS = 1e-5

LANES = 128
SUBLANES = 8
VMEM_LIMIT_BYTES = 56 * 1024 * 1024

QHEAD_W = 2 * LANES
OFF_Q = 0
OFF_KR = OFF_Q + Q_LORA
OFF_KV = OFF_KR + LANES
OFF_MG = OFF_KV + KV_LORA
OFF_RW = OFF_MG + MLA_W
OFF_RG = OFF_RW + SHIFT_W
OFF_GU = OFF_RG + RWKV_W
OFF_GV = OFF_GU + GM_W
OFF_GG = OFF_GV + GM_W
IN_W_P = OFF_GG + GM_W

NEG_BIG = -1e30


def _cparams(*sem):
    return pltpu.CompilerParams(dimension_semantics=sem, vmem_limit_bytes=VMEM_LIMIT_BYTES)


def _silu(x):
    return x * jax.nn.sigmoid(x)


def _rms(x, g, eps=RMS_EPS):
    return x * lax.rsqrt(jnp.mean(x * x, axis=-1, keepdims=True) + eps) * g


def _rows2d(ref):
    m = ref[...]
    return m.reshape(m.shape[-2], m.shape[-1])


def _group_index(idx, group):
    shift = group.bit_length() - 1
    assert 1 << shift == group
    return lax.shift_right_logical(idx, shift)


def _head_ones(width, head):
    r = _group_index(lax.broadcasted_iota(jnp.int32, (width, width), 0), head)
    c = _group_index(lax.broadcasted_iota(jnp.int32, (width, width), 1), head)
    return jnp.where(r == c, 1.0, 0.0).astype(BF16)


def _head_sum(x, ones_bd):
    hi = x.astype(BF16)
    lo = (x - hi.astype(F32)).astype(BF16)
    return (jnp.dot(hi, ones_bd, preferred_element_type=F32)
            + jnp.dot(lo, ones_bd, preferred_element_type=F32))


def _rope_pair(blk, cs, sn):
    return blk * cs + pltpu.roll(blk, MLA_ROPE, 1) * sn


def _ada_kernel(c_ref, w_ref, b_ref, o_ref):
    c = _silu(c_ref[...]).astype(BF16)
    o_ref[0] = jnp.dot(c, w_ref[0], preferred_element_type=F32) + b_ref[0]


def _ada_mod(c_all, w_ada, b_ada):
    depth = w_ada.shape[0]
    n_rows = c_all.shape[0]
    n_tiles = (3 * D_MODEL) // D_MODEL
    return pl.pallas_call(
        _ada_kernel,
        grid=(depth, n_tiles),
        in_specs=[
            pl.BlockSpec((n_rows, D_MODEL), lambda l, j: (0, 0)),
            pl.BlockSpec((1, D_MODEL, D_MODEL), lambda l, j: (l, 0, j)),
            pl.BlockSpec((1, 1, D_MODEL), lambda l, j: (l, 0, j)),
        ],
        out_specs=pl.BlockSpec((1, n_rows, D_MODEL), lambda l, j: (l, 0, j)),
        out_shape=jax.ShapeDtypeStruct((depth, n_rows, 3 * D_MODEL), F32),
        compiler_params=_cparams("arbitrary", "arbitrary"),
        name="ada_mod",
    )(c_all, w_ada.astype(BF16), b_ada.reshape(depth, 1, 3 * D_MODEL))


def _proj_common(x_ref, mod_ref, gpre_ref, win_ref):
    mod = _rows2d(mod_ref)
    shift = mod[:, 0:D_MODEL]
    scale = mod[:, D_MODEL:2 * D_MODEL]
    h = _rms(x_ref[...], gpre_ref[...]) * (1.0 + scale) + shift
    hb = h.astype(BF16)

    def seg(off, width):
        return jnp.dot(hb, win_ref[:, off:off + width], preferred_element_type=F32)

    return seg


def _q_heads(seg, gq_ref, wuq_ref, cs, sn):
    q_kr = seg(OFF_Q, Q_LORA + LANES)
    qn = _rms(q_kr[:, :Q_LORA], gq_ref[...]).astype(BF16)
    q_all = jnp.dot(qn, wuq_ref[...], preferred_element_type=F32)
    heads = []
    for h in range(MLA_HEADS):
        nope = q_all[:, h * QHEAD_W:h * QHEAD_W + LANES] * Q_SCALE
        pe = _rope_pair(q_all[:, h * QHEAD_W + LANES:(h + 1) * QHEAD_W], cs, sn) * Q_SCALE
        heads.append((nope, pe))
    return heads, _rope_pair(q_kr[:, Q_LORA:], cs, sn)


def _gm_norm(seg, lng_ref, lnb_ref):
    v = seg(OFF_GV, GM_W)
    mu = jnp.mean(v, axis=-1, keepdims=True)
    c = v - mu
    var = jnp.mean(c * c, axis=-1, keepdims=True)
    return c * lax.rsqrt(var + LN_EPS) * lng_ref[...] + lnb_ref[...]


def _proj_prompt_kernel(x_ref, mod_ref, gpre_ref, win_ref, gq_ref, wuq_ref, gkv_ref, wuk_ref, wuv_ref,
                        cs_ref, sn_ref, lng_ref, lnb_ref, ws_ref, bs_ref,
                        q_ref, k_ref, v_ref, mg_ref, ckv_ref, kpe_ref, rw_ref, rg_ref, gm_ref, vn_ref,
                        *, tiles_per_seq):
    seg = _proj_common(x_ref, mod_ref, gpre_ref, win_ref)
    cs = cs_ref[...]
    sn = sn_ref[...]
    tm = x_ref.shape[0]

    heads, kpe = _q_heads(seg, gq_ref, wuq_ref, cs, sn)
    for h, (nope, pe) in enumerate(heads):
        q_ref[:, h * QHEAD_W:h * QHEAD_W + LANES] = nope.astype(BF16)
        q_ref[:, h * QHEAD_W + LANES:(h + 1) * QHEAD_W] = pe.astype(BF16)
    ckv = _rms(seg(OFF_KV, KV_LORA), gkv_ref[...])
    ckv_ref[...] = ckv
    kpe_ref[...] = kpe[:, :MLA_ROPE]
    ckv_b = ckv.astype(BF16)
    k_nope = jnp.dot(ckv_b, wuk_ref[...], preferred_element_type=F32)
    v_ref[...] = jnp.dot(ckv_b, wuv_ref[...], preferred_element_type=F32).astype(BF16)
    kpe_b = kpe.astype(BF16)
    for h in range(MLA_HEADS):
        k_ref[:, h * QHEAD_W:h * QHEAD_W + LANES] = k_nope[:, h * MLA_NOPE:(h + 1) * MLA_NOPE].astype(BF16)
        k_ref[:, h * QHEAD_W + LANES:(h + 1) * QHEAD_W] = kpe_b
    mg_ref[...] = _silu(seg(OFF_MG, MLA_W)).astype(BF16)

    rw_ref[...] = seg(OFF_RW, SHIFT_W)
    rg_ref[...] = _silu(seg(OFF_RG, RWKV_W))

    vn = _gm_norm(seg, lng_ref, lnb_ref)
    vn_b = vn.astype(BF16)
    gate_u = _silu(seg(OFF_GG, GM_W)) * seg(OFF_GU, GM_W)
    row = lax.broadcasted_iota(jnp.int32, (CHUNK, CHUNK), 0)
    col = lax.broadcasted_iota(jnp.int32, (CHUNK, CHUNK), 1)
    lane_group = _group_index(lax.broadcasted_iota(jnp.int32, (CHUNK, GM_W), 1), GM_GROUP_W)
    w_tril = [jnp.where(row >= col, ws_ref[g], 0.0).astype(BF16) for g in range(GM_GROUPS)]
    for c in range(tm // CHUNK):
        vc = vn_b[c * CHUNK:(c + 1) * CHUNK]
        z = bs_ref[...]
        for g in range(GM_GROUPS):
            zg = jnp.dot(w_tril[g], vc, preferred_element_type=F32)
            z = z + jnp.where(lane_group == g, zg, 0.0)
        gm_ref[c * CHUNK:(c + 1) * CHUNK, :] = (gate_u[c * CHUNK:(c + 1) * CHUNK] * z).astype(BF16)

    @pl.when(pl.program_id(0) % tiles_per_seq == tiles_per_seq - 1)
    def _():
        vn_ref[0] = vn[tm - CHUNK:, :]


def _proj_prompt(x2d, mod_b, lw, cs, sn, batch, seq, tm):
    rows = x2d.shape[0]
    tiles_per_seq = seq // tm
    const = lambda i: (0, 0)
    row_blk = lambda i: (i, 0)
    outs = [
        (QHEAD_W * MLA_HEADS, BF16), (QHEAD_W * MLA_HEADS, BF16), (MLA_W, BF16), (MLA_W, BF16),
        (KV_LORA, F32), (MLA_ROPE, F32), (SHIFT_W, F32), (RWKV_W, F32), (GM_W, BF16),
    ]
    out_shape = [jax.ShapeDtypeStruct((rows, w), dt) for w, dt in outs]
    out_specs = [pl.BlockSpec((tm, w), row_blk) for w, _ in outs]
    out_shape.append(jax.ShapeDtypeStruct((batch, CHUNK, GM_W), F32))
    out_specs.append(pl.BlockSpec((1, CHUNK, GM_W), lambda i: (i // tiles_per_seq, 0, 0)))
    return pl.pallas_call(
        functools.partial(_proj_prompt_kernel, tiles_per_seq=tiles_per_seq),
        grid=(rows // tm,),
        in_specs=[
            pl.BlockSpec((tm, D_MODEL), row_blk),
            pl.BlockSpec((1, 1, 3 * D_MODEL), lambda i: (i // tiles_per_seq, 0, 0)),
            pl.BlockSpec((1, D_MODEL), const),
            pl.BlockSpec((D_MODEL, IN_W_P), const),
            pl.BlockSpec((1, Q_LORA), const),
            pl.BlockSpec((Q_LORA, QHEAD_W * MLA_HEADS), const),
            pl.BlockSpec((1, KV_LORA), const),
            pl.BlockSpec((KV_LORA, MLA_W), const),
            pl.BlockSpec((KV_LORA, MLA_W), const),
            pl.BlockSpec((tm, LANES), lambda i: (i % tiles_per_seq, 0)),
            pl.BlockSpec((tm, LANES), lambda i: (i % tiles_per_seq, 0)),
            pl.BlockSpec((1, GM_W), const),
            pl.BlockSpec((1, GM_W), const),
            pl.BlockSpec((GM_GROUPS, CHUNK, CHUNK), lambda i: (0, 0, 0)),
            pl.BlockSpec((CHUNK, GM_W), const),
        ],
        out_specs=out_specs,
        out_shape=out_shape,
        compiler_params=_cparams("arbitrary"),
        name="proj_prompt",
    )(x2d, mod_b, lw["g_pre"], lw["w_in"], lw["g_q"], lw["w_uq"], lw["g_kv"], lw["w_uk"], lw["w_uv"],
      cs, sn, lw["ln_g"], lw["ln_b"], lw["gm_ws"], lw["gm_bias"])


def _proj_sample_kernel(x_ref, mod_ref, gpre_ref, win_ref, gq_ref, wuq_ref, gkv_ref, wukt_ref,
                        cs_ref, sn_ref, lng_ref, lnb_ref, coef_ref, bias_ref,
                        qa_ref, qp_ref, mg_ref, ckv_ref, kpe_ref, rw_ref, rg_ref, gm_ref, vn_ref):
    seg = _proj_common(x_ref, mod_ref, gpre_ref, win_ref)
    cs = cs_ref[...]
    sn = sn_ref[...]
    heads, kpe = _q_heads(seg, gq_ref, wuq_ref, cs, sn)
    for h, (nope, pe) in enumerate(heads):
        qa_ref[:, h * KV_LORA:(h + 1) * KV_LORA] = jnp.dot(nope.astype(BF16), wukt_ref[h],
                                                           preferred_element_type=F32)
        qp_ref[:, h * LANES:(h + 1) * LANES] = pe
    ckv_ref[...] = _rms(seg(OFF_KV, KV_LORA), gkv_ref[...])
    kpe_ref[...] = kpe
    mg_ref[...] = _silu(seg(OFF_MG, MLA_W)).astype(BF16)
    rw_ref[...] = seg(OFF_RW, SHIFT_W)
    rg_ref[...] = _silu(seg(OFF_RG, RWKV_W))
    vn = _gm_norm(seg, lng_ref, lnb_ref)
    vn_ref[...] = vn
    z = vn * coef_ref[...] + bias_ref[...]
    gm_ref[...] = (_silu(seg(OFF_GG, GM_W)) * seg(OFF_GU, GM_W) * z).astype(BF16)


def _proj_sample(x2d, mod_rows, lw, cs, sn):
    rows = x2d.shape[0]
    const = lambda i: (0, 0)
    outs = [
        (KV_LORA * MLA_HEADS, F32), (LANES * MLA_HEADS, F32), (MLA_W, BF16), (KV_LORA, F32), (LANES, F32),
        (SHIFT_W, F32), (RWKV_W, F32), (GM_W, BF16), (GM_W, F32),
    ]
    return pl.pallas_call(
        _proj_sample_kernel,
        grid=(1,),
        in_specs=[
            pl.BlockSpec((rows, D_MODEL), const),
            pl.BlockSpec((rows, 3 * D_MODEL), const),
            pl.BlockSpec((1, D_MODEL), const),
            pl.BlockSpec((D_MODEL, IN_W_P), const),
            pl.BlockSpec((1, Q_LORA), const),
            pl.BlockSpec((Q_LORA, QHEAD_W * MLA_HEADS), const),
            pl.BlockSpec((1, KV_LORA), const),
            pl.BlockSpec((MLA_HEADS, MLA_NOPE, KV_LORA), lambda i: (0, 0, 0)),
            pl.BlockSpec((rows, LANES), const),
            pl.BlockSpec((rows, LANES), const),
            pl.BlockSpec((1, GM_W), const),
            pl.BlockSpec((1, GM_W), const),
            pl.BlockSpec((1, GM_W), const),
            pl.BlockSpec((1, GM_W), const),
        ],
        out_specs=[pl.BlockSpec((rows, w), const) for w, _ in outs],
        out_shape=[jax.ShapeDtypeStruct((rows, w), dt) for w, dt in outs],
        compiler_params=_cparams("arbitrary"),
        name="proj_sample",
    )(x2d, mod_rows, lw["g_pre"], lw["w_in"], lw["g_q"], lw["w_uq"], lw["g_kv"], lw["w_ukt"],
      cs, sn, lw["ln_g"], lw["ln_b"], lw["gm_coef0"], lw["gm_bias0"])


def _flash_kernel(q_ref, k_ref, v_ref, g_ref, o_ref, *, tq):
    seq = q_ref.shape[1]
    row = lax.broadcasted_iota(jnp.int32, (tq, tq), 0)
    col = lax.broadcasted_iota(jnp.int32, (tq, tq), 1)
    for qi in range(seq // tq):
        rows = slice(qi * tq, (qi + 1) * tq)
        q = q_ref[0, rows, :]
        m = jnp.full((tq, 1), NEG_BIG, F32)
        l = jnp.zeros((tq, 1), F32)
        acc = jnp.zeros((tq, MLA_V), F32)
        for ki in range(qi + 1):
            cols = slice(ki * tq, (ki + 1) * tq)
            s = lax.dot_general(q, k_ref[0, cols, :], (((1,), (1,)), ((), ())),
                                preferred_element_type=F32)
            if ki == qi:
                s = jnp.where(row >= col, s, NEG_BIG)
            m_new = jnp.maximum(m, jnp.max(s, axis=-1, keepdims=True))
            alpha = jnp.exp2(m - m_new)
            p = jnp.exp2(s - m_new)
            l = alpha * l + jnp.sum(p, axis=-1, keepdims=True)
            acc = alpha * acc + jnp.dot(p.astype(BF16), v_ref[0, cols, :], preferred_element_type=F32)
            m = m_new
        o_ref[0, rows, :] = (acc / l * g_ref[0, rows, :].astype(F32)).astype(BF16)


def _flash(q, k, v, g, batch, seq, tq):
    q3 = q.reshape(batch, seq, QHEAD_W * MLA_HEADS)
    k3 = k.reshape(batch, seq, QHEAD_W * MLA_HEADS)
    v3 = v.reshape(batch, seq, MLA_W)
    g3 = g.reshape(batch, seq, MLA_W)
    head_blk = lambda b, h: (b, 0, h)
    out = pl.pallas_call(
        functools.partial(_flash_kernel, tq=tq),
        grid=(batch, MLA_HEADS),
        in_specs=[
            pl.BlockSpec((1, seq, QHEAD_W), head_blk),
            pl.BlockSpec((1, seq, QHEAD_W), head_blk),
            pl.BlockSpec((1, seq, MLA_V), head_blk),
            pl.BlockSpec((1, seq, MLA_V), head_blk),
        ],
        out_specs=pl.BlockSpec((1, seq, MLA_V), head_blk),
        out_shape=jax.ShapeDtypeStruct((batch, seq, MLA_W), BF16),
        compiler_params=_cparams("arbitrary", "arbitrary"),
        name="flash_prompt",
    )(q3, k3, v3, g3)
    return out.reshape(batch * seq, MLA_W)


N_SLOTS = 2


def _paged_kernel(pt_ref, qa_ref, qp_ref, cn_ref, kn_ref, ckv_hbm, kpe_hbm, o_ref, ckv_buf, kpe_buf, sem,
                  *, layer, n_pages, page, group_tokens):
    b = pl.program_id(0)
    n_seq = pl.num_programs(0)
    slot = lax.rem(b, N_SLOTS)

    def page_copies(seq_idx, sl):
        copies = []
        for i in range(n_pages):
            pg = pt_ref[seq_idx, i]
            copies.append(pltpu.make_async_copy(
                ckv_hbm.at[layer, pg], ckv_buf.at[sl, pl.ds(i * page, page), :], sem.at[0, sl]))
            copies.append(pltpu.make_async_copy(kpe_hbm.at[layer, pg], kpe_buf.at[sl, i], sem.at[1, sl]))
        return copies

    def start_all(copies):
        for i, c in enumerate(copies):
            c.start(priority=(i // 2) % 2)

    @pl.when(b == 0)
    def _():
        start_all(page_copies(0, 0))

    for c in page_copies(b, slot):
        c.wait()

    nxt = lax.rem(b + 1, n_seq)
    nxt_slot = lax.rem(b + 1, N_SLOTS)
    start_all(page_copies(nxt, nxt_slot))

    qa = qa_ref[0]
    qp = qp_ref[0][:, :MLA_ROPE]
    qa_b = qa.astype(BF16)
    qp_b = qp.astype(BF16)
    group_pages = group_tokens // page
    groups = [slice(g * group_tokens, (g + 1) * group_tokens) for g in range(n_pages // group_pages)]
    ckv = [ckv_buf[slot, gs, :].astype(BF16) for gs in groups]
    kpe_t = [jnp.concatenate([kpe_buf[slot, g * group_pages + i] for i in range(group_pages)],
                             axis=1).astype(BF16) for g in range(len(groups))]
    s = [lax.dot_general(qa_b, ck, (((1,), (1,)), ((), ())), preferred_element_type=F32)
         + jnp.dot(qp_b, kp, preferred_element_type=F32) for ck, kp in zip(ckv, kpe_t)]
    m_g = [jnp.max(sg, axis=-1, keepdims=True) for sg in s]
    p = [jnp.exp2(sg - mg) for sg, mg in zip(s, m_g)]
    l_g = [jnp.sum(pg, axis=-1, keepdims=True) for pg in p]
    acc_g = [jnp.dot(pg.astype(BF16), ck, preferred_element_type=F32) for pg, ck in zip(p, ckv)]
    cn = cn_ref[0]
    kn = kn_ref[0][:, :MLA_ROPE]
    s_new = jnp.sum(qa * cn, axis=-1, keepdims=True) + jnp.sum(qp * kn, axis=-1, keepdims=True)
    m = functools.reduce(jnp.maximum, m_g, s_new)
    p_new = jnp.exp2(s_new - m)
    w_g = [jnp.exp2(mg - m) for mg in m_g]
    l = sum(wg * lg for wg, lg in zip(w_g, l_g)) + p_new
    o_ref[0] = (sum(wg * ag for wg, ag in zip(w_g, acc_g)) + p_new * cn) / l

    @pl.when(b == n_seq - 1)
    def _():
        for c in page_copies(nxt, nxt_slot):
            c.wait()


def _paged_attention(qa, qp, ckv_new, kpe_new, cache_ckv, cache_kpe_t, page_table, layer):
    dec_batch, n_pages = page_table.shape
    page = cache_ckv.shape[2]
    head_pad = ((0, 0), (0, SUBLANES - MLA_HEADS), (0, 0))
    qa3 = jnp.pad(qa.reshape(dec_batch, MLA_HEADS, KV_LORA), head_pad)
    qp3 = jnp.pad(qp.reshape(dec_batch, MLA_HEADS, LANES), head_pad)
    cn3 = ckv_new.reshape(dec_batch, 1, KV_LORA)
    kn3 = kpe_new.reshape(dec_batch, 1, LANES)
    row_blk = lambda b, pt: (b, 0, 0)
    out = pl.pallas_call(
        functools.partial(_paged_kernel, layer=layer, n_pages=n_pages, page=page,
                          group_tokens=_pick(n_pages * page, 2048)),
        grid_spec=pltpu.PrefetchScalarGridSpec(
            num_scalar_prefetch=1,
            grid=(dec_batch,),
            in_specs=[
                pl.BlockSpec((1, SUBLANES, KV_LORA), row_blk),
                pl.BlockSpec((1, SUBLANES, LANES), row_blk),
                pl.BlockSpec((1, 1, KV_LORA), row_blk),
                pl.BlockSpec((1, 1, LANES), row_blk),
                pl.BlockSpec(memory_space=pl.ANY),
                pl.BlockSpec(memory_space=pl.ANY),
            ],
            out_specs=pl.BlockSpec((1, SUBLANES, KV_LORA), row_blk),
            scratch_shapes=[
                pltpu.VMEM((N_SLOTS, n_pages * page, KV_LORA), F32),
                pltpu.VMEM((N_SLOTS, n_pages, MLA_ROPE, page), F32),
                pltpu.SemaphoreType.DMA((2, N_SLOTS)),
            ],
        ),
        out_shape=jax.ShapeDtypeStruct((dec_batch, SUBLANES, KV_LORA), F32),
        compiler_params=_cparams("arbitrary"),
        name="paged_attention",
    )(page_table, qa3, qp3, cn3, kn3, cache_ckv, cache_kpe_t)
    return out[:, :MLA_HEADS].reshape(dec_batch, MLA_HEADS * KV_LORA)


def _rwkv_prep_values(p, prev, mu_ref, w0_ref, wd_ref, a0_ref, wa_ref, kk_ref, ka_ref, rk_ref):
    xm = p + (prev - p) * mu_ref[...]
    r = xm[:, 0:RWKV_W]
    k = xm[:, RWKV_W:2 * RWKV_W]
    v = xm[:, 2 * RWKV_W:3 * RWKV_W]
    tail = xm[:, 3 * RWKV_W:]
    lane = lax.broadcasted_iota(jnp.int32, tail.shape, 1)
    lora_in = jnp.where(lane < DECAY_LORA, jnp.tanh(tail), tail).astype(BF16)
    dw = jnp.dot(lora_in, wd_ref[...], preferred_element_type=F32)
    da = jnp.dot(lora_in, wa_ref[...], preferred_element_type=F32)
    z = -(w0_ref[...] + dw)
    softplus = jnp.maximum(z, 0.0) + jnp.log(1.0 + jnp.exp(-jnp.abs(z)))
    w = -softplus - 0.5
    log_decay = -jnp.exp(w)
    a = jax.nn.sigmoid(a0_ref[...] + da)
    ones_bd = _head_ones(RWKV_W, RWKV_HEAD)
    kk = k * kk_ref[...]
    kk = kk / jnp.maximum(jnp.sqrt(_head_sum(kk * kk, ones_bd)), 1e-12)
    k_mod = k * (1.0 + (a - 1.0) * ka_ref[...])
    bonus = _head_sum(r * k_mod * rk_ref[...], ones_bd) * v
    return r, k_mod, v, kk, kk * a, log_decay, bonus


def _rwkv_param_specs(index_map):
    widths = (SHIFT_W, RWKV_W, None, RWKV_W, None, RWKV_W, RWKV_W, RWKV_W)
    return [pl.BlockSpec((LANES, RWKV_W) if w is None else (1, w), index_map) for w in widths]


def _rwkv_params(lw):
    return (lw["rw_mu"], lw["rw_w0"], lw["rw_wd"], lw["rw_a0"], lw["rw_wa"], lw["rw_k_k"], lw["rw_k_a"],
            lw["rw_r_k"])


def _rwkv_prep_sample_kernel(p_ref, prev_ref, mu_ref, w0_ref, wd_ref, a0_ref, wa_ref, kk_ref, ka_ref, rk_ref,
                             r_o, k_o, v_o, kk_o, b_o, d_o, bonus_o):
    r, k_mod, v, kk, b, log_decay, bonus = _rwkv_prep_values(
        p_ref[...], prev_ref[...], mu_ref, w0_ref, wd_ref, a0_ref, wa_ref, kk_ref, ka_ref, rk_ref)
    r_o[...] = r
    k_o[...] = k_mod
    v_o[...] = v
    kk_o[...] = kk
    b_o[...] = b
    d_o[...] = jnp.exp(log_decay)
    bonus_o[...] = bonus


def _rwkv_prep_sample(rw_in, lw, shift_rows):
    rows = rw_in.shape[0]
    const = lambda i: (0, 0)
    return pl.pallas_call(
        _rwkv_prep_sample_kernel,
        grid=(1,),
        in_specs=[pl.BlockSpec((rows, SHIFT_W), const)] * 2 + _rwkv_param_specs(const),
        out_specs=[pl.BlockSpec((rows, RWKV_W), const)] * 7,
        out_shape=[jax.ShapeDtypeStruct((rows, RWKV_W), F32)] * 7,
        compiler_params=_cparams("arbitrary"),
        name="rwkv_prep_sample",
    )(rw_in, shift_rows, *_rwkv_params(lw))


RW_CHUNK = 64
RW_SUB = 16


def _split3(x):
    hi = x.astype(BF16)
    r1 = x - hi.astype(F32)
    mid = r1.astype(BF16)
    lo = (r1 - mid.astype(F32)).astype(BF16)
    return hi, mid, lo


def _block_diag(y, bd_mask):
    return jnp.where(bd_mask, jnp.concatenate([y] * RWKV_HEADS, axis=0), 0.0).astype(BF16)


def _mm(x, y_bd):
    return jnp.dot(x.astype(BF16), y_bd, preferred_element_type=F32)


def _rwkv_chunks(vals, n0, masks):
    bd_mask, tri_incl, strict, incl, same_sub, eye_tiled, ones_bd = masks
    C = RW_CHUNK
    seqs = range(len(vals))
    bd = lambda y: _block_diag(y, bd_mask)
    r, kx, v, kap, bb, lam = ([val[j] for val in vals] for j in range(6))
    cum = [sum(jnp.dot(tri_incl, part, preferred_element_type=F32) for part in _split3(lam[i])) for i in seqs]
    cum_last = [c[C - 1:C, :] for c in cum]
    e_neg = [jnp.exp(-c) for c in cum]
    ap = [jnp.concatenate([kap[i] * jnp.exp(cum[i] - lam[i]), r[i] * jnp.exp(cum[i])], axis=0).astype(BF16)
          for i in seqs]
    rhs = [jnp.concatenate([bd(bb[i] * e_neg[i]), bd(kx[i] * e_neg[i])], axis=0) for i in seqs]
    g = [lax.dot_general(ap[i], rhs[i], (((1,), (1,)), ((), ())), preferred_element_type=F32) for i in seqs]
    l_ab = [jnp.where(strict, gi[:C, :RWKV_W], 0.0) for gi in g]
    l_ak = [jnp.where(strict, gi[:C, RWKV_W:], 0.0) for gi in g]
    q_pb = [jnp.where(incl, gi[C:, :RWKV_W], 0.0) for gi in g]
    q_pk = [jnp.where(incl, gi[C:, RWKV_W:], 0.0) for gi in g]
    v_bd = [bd(vi) for vi in v]
    apn = [jnp.dot(ap[i], bd(n0[i]), preferred_element_type=F32) for i in seqs]
    z_w = [apn[i][:C] + _mm(l_ak[i], v_bd[i]) for i in seqs]

    l_d = [jnp.where(same_sub, li, 0.0) for li in l_ab]
    z_l = [l_ab[i] - l_d[i] for i in seqs]
    x = [(-li).astype(BF16) for li in l_d]
    power = l_d
    n_double = RW_SUB.bit_length() - 1
    for step in range(n_double):
        xz = [jnp.dot(x[i], jnp.concatenate([bd(z_l[i]), bd(z_w[i])], axis=1), preferred_element_type=F32)
              for i in seqs]
        z_l = [z_l[i] + xz[i][:, :RWKV_W] for i in seqs]
        z_w = [z_w[i] + xz[i][:, RWKV_W:] for i in seqs]
        if step + 1 < n_double:
            power = [_mm(pw, bd(pw)) for pw in power]
            x = [pw.astype(BF16) for pw in power]
    n = z_l
    n_sq = [_mm(ni, bd(ni)) for ni in n]
    y = [z_w[i] + _mm(n_sq[i], bd(z_w[i])) for i in seqs]
    u = [_mm(n[i], bd(y[i])) - y[i] for i in seqs]

    o = [apn[i][C:] + _mm(q_pb[i], bd(u[i])) + _mm(q_pk[i], v_bd[i]) for i in seqs]

    e_end = [jnp.exp(cum_last[i] - cum[i]) for i in seqs]
    lhs_t = [jnp.concatenate([bb[i] * e_end[i], kx[i] * e_end[i]], axis=0).T.astype(BF16) for i in seqs]
    full = [jnp.dot(lhs_t[i], jnp.concatenate([u[i], v[i]], axis=0).astype(BF16), preferred_element_type=F32)
            for i in seqs]
    heads = [slice(h * RWKV_HEAD, (h + 1) * RWKV_HEAD) for h in range(RWKV_HEADS)]
    delta = [sum(jnp.where(bd_mask[hs], f[hs], 0.0) for hs in heads) for f in full]
    g_wide = [sum(jnp.dot(part, ones_bd, preferred_element_type=F32)
                  for part in _split3(jnp.where(eye_tiled, jnp.exp(cl), 0.0))) for cl in cum_last]
    return o, [n0[i] * g_wide[i] + delta[i] for i in seqs]


def _rwkv_prompt_kernel(p_ref, rg_ref, mu_ref, w0_ref, wd_ref, a0_ref, wa_ref, kk_ref, ka_ref, rk_ref,
                        lnxg_ref, lnxb_ref, o_ref, sfin_ref, state_ref, last_ref):
    nb, C, _ = p_ref.shape
    c_idx = pl.program_id(1)

    @pl.when(c_idx == 0)
    def _():
        state_ref[...] = jnp.zeros(state_ref.shape, F32)
        last_ref[...] = jnp.zeros(last_ref.shape, F32)

    p = p_ref[...].reshape(nb * C, SHIFT_W)
    row = lax.broadcasted_iota(jnp.int32, (nb * C, 1), 0)
    prev = pltpu.roll(p, 1, 0)
    for i in range(nb):
        prev = jnp.where(row == i * C, last_ref[i, 0:1, :], prev)
    vals = _rwkv_prep_values(p, prev, mu_ref, w0_ref, wd_ref, a0_ref, wa_ref, kk_ref, ka_ref, rk_ref)
    bonus = vals[6]

    lane = lax.broadcasted_iota(jnp.int32, (C, RWKV_W), 1)
    t_idx = lax.broadcasted_iota(jnp.int32, (C, RWKV_W), 0)
    s_idx = lane & (RWKV_HEAD - 1)
    row_big = lax.broadcasted_iota(jnp.int32, (RWKV_W, RWKV_W), 0)
    lane_big = lax.broadcasted_iota(jnp.int32, (RWKV_W, RWKV_W), 1)
    bd_mask = _group_index(row_big, RWKV_HEAD) == _group_index(lane_big, RWKV_HEAD)
    ones_bd = jnp.where(bd_mask, 1.0, 0.0).astype(BF16)
    tri_r = lax.broadcasted_iota(jnp.int32, (C, C), 0)
    tri_c = lax.broadcasted_iota(jnp.int32, (C, C), 1)
    masks = (bd_mask, jnp.where(tri_r >= tri_c, 1.0, 0.0).astype(BF16), t_idx > s_idx, t_idx >= s_idx,
             _group_index(t_idx, RW_SUB) == _group_index(s_idx, RW_SUB), t_idx == s_idx, ones_bd)

    per_seq = [tuple(x[i * C:(i + 1) * C] for x in vals[:6]) for i in range(nb)]
    outs, n_new = _rwkv_chunks(per_seq, [state_ref[i] for i in range(nb)], masks)
    for i in range(nb):
        state_ref[i] = n_new[i]
        last_ref[i, 0:1, :] = p[(i + 1) * C - 1:(i + 1) * C, :]
    o = jnp.concatenate(outs, axis=0)

    c = o - _head_sum(o, ones_bd) * (1.0 / RWKV_HEAD)
    var = _head_sum(c * c, ones_bd) * (1.0 / RWKV_HEAD)
    on = c * lax.rsqrt(var + GN_EPS) * lnxg_ref[...] + lnxb_ref[...]
    o_ref[...] = ((on + bonus) * rg_ref[...].reshape(nb * C, RWKV_W)).astype(BF16).reshape(nb, C, RWKV_W)

    @pl.when(c_idx == pl.num_programs(1) - 1)
    def _():
        sfin_ref[...] = state_ref[...]


def _rwkv_prompt(rw_in, rg, lw, batch, seq, nb):
    const = lambda b, c: (0, 0)
    blk = lambda b, c: (b, c, 0)
    o, s_fin = pl.pallas_call(
        _rwkv_prompt_kernel,
        grid=(batch // nb, seq // RW_CHUNK),
        in_specs=[pl.BlockSpec((nb, RW_CHUNK, SHIFT_W), blk), pl.BlockSpec((nb, RW_CHUNK, RWKV_W), blk)]
        + _rwkv_param_specs(const) + [pl.BlockSpec((1, RWKV_W), const)] * 2,
        out_specs=[pl.BlockSpec((nb, RW_CHUNK, RWKV_W), blk),
                   pl.BlockSpec((nb, RWKV_HEAD, RWKV_W), lambda b, c: (b, 0, 0))],
        out_shape=[jax.ShapeDtypeStruct((batch, seq, RWKV_W), BF16),
                   jax.ShapeDtypeStruct((batch, RWKV_HEAD, RWKV_W), F32)],
        scratch_shapes=[pltpu.VMEM((nb, RWKV_HEAD, RWKV_W), F32), pltpu.VMEM((nb, SUBLANES, SHIFT_W), F32)],
        compiler_params=_cparams("arbitrary", "arbitrary"),
        name="rwkv_prompt",
    )(rw_in.reshape(batch, seq, SHIFT_W), rg.reshape(batch, seq, RWKV_W), *_rwkv_params(lw),
      lw["lnx_g"], lw["lnx_b"])
    wkv = jnp.transpose(s_fin.reshape(batch, RWKV_HEAD, RWKV_HEADS, RWKV_HEAD), (0, 2, 3, 1))
    return o.reshape(batch * seq, RWKV_W), wkv


V_HALF = RWKV_HEAD // 2


def _scan_kernel(x_ref, v_ref, s0_ref, o_ref, sfin_ref, s_ref):
    tb = pl.program_id(1)

    @pl.when(tb == 0)
    def _():
        s_ref[...] = s0_ref[...]

    def step(t, carry):
        kk = x_ref[t, 0]
        b = x_ref[t, 1]
        d = x_ref[t, 2]
        kx = x_ref[t, 3]
        r = x_ref[t, 4]
        for vp in range(V_HALF):
            sv = s_ref[vp]
            sa = jnp.sum(sv * kk, axis=0, keepdims=True)
            sn = sv * d - sa * b + v_ref[t, pl.ds(vp, 1), :] * kx
            s_ref[vp] = sn
            o_ref[t, pl.ds(vp, 1), :] = jnp.sum(sn * r, axis=0, keepdims=True)
        return carry

    lax.fori_loop(0, x_ref.shape[0], step, 0)

    @pl.when(tb == pl.num_programs(1) - 1)
    def _():
        sfin_ref[...] = s_ref[...]


def _scan(x5, v_t, s0, t_blk):
    steps, _, _, lanes = x5.shape
    lane_tiles = lanes // LANES
    return pl.pallas_call(
        _scan_kernel,
        grid=(lane_tiles, steps // t_blk),
        in_specs=[
            pl.BlockSpec((t_blk, 5, RWKV_HEAD, LANES), lambda j, t: (t, 0, 0, j)),
            pl.BlockSpec((t_blk, V_HALF, LANES), lambda j, t: (t, 0, j)),
            pl.BlockSpec((V_HALF, RWKV_HEAD, LANES), lambda j, t: (0, 0, j)),
        ],
        out_specs=[
            pl.BlockSpec((t_blk, V_HALF, LANES), lambda j, t: (t, 0, j)),
            pl.BlockSpec((V_HALF, RWKV_HEAD, LANES), lambda j, t: (0, 0, j)),
        ],
        out_shape=[
            jax.ShapeDtypeStruct((steps, V_HALF, lanes), F32),
            jax.ShapeDtypeStruct((V_HALF, RWKV_HEAD, lanes), F32),
        ],
        scratch_shapes=[pltpu.VMEM((V_HALF, RWKV_HEAD, LANES), F32)],
        compiler_params=_cparams("arbitrary", "arbitrary"),
        name="rwkv_scan",
    )(x5, v_t, s0)


def _rwkv_mix(prep, state, batch, seq, t_blk):
    r, k_mod, v, kk, b, d, _ = prep
    H, N = RWKV_HEADS, RWKV_HEAD
    x5 = jnp.stack([kk, b, d, k_mod, r]).reshape(5, batch, seq, H, N)
    x5 = jnp.transpose(x5, (2, 0, 4, 1, 3)).reshape(seq, 5, N, 1, batch * H)
    x5 = jnp.broadcast_to(x5, (seq, 5, N, 2, batch * H)).reshape(seq, 5, N, 2 * batch * H)
    v_t = jnp.transpose(v.reshape(batch, seq, H, 2, V_HALF), (1, 4, 3, 0, 2)).reshape(seq, V_HALF, 2 * batch * H)
    s0 = jnp.transpose(state.reshape(batch, H, 2, V_HALF, N), (3, 4, 2, 0, 1)).reshape(V_HALF, N, 2 * batch * H)
    o_t, s_fin = _scan(x5, v_t, s0, t_blk)
    o = jnp.transpose(o_t.reshape(seq, V_HALF, 2, batch, H), (3, 0, 4, 2, 1)).reshape(batch * seq, H * N)
    s_new = jnp.transpose(s_fin.reshape(V_HALF, N, 2, batch, H), (3, 4, 2, 0, 1)).reshape(batch, H, N, N)
    return o, s_new


def _mix_out(mixed, x_ref, mod_ref, wout_ref, gpost_ref, y_ref):
    out = jnp.dot(mixed, wout_ref[...], preferred_element_type=F32)
    gate = _rows2d(mod_ref)[:, 2 * D_MODEL:]
    y_ref[...] = x_ref[...] + gate * _rms(out, gpost_ref[...])


def _out_prompt_kernel(om_ref, orw_ref, gm_ref, x_ref, mod_ref, wout_ref, gpost_ref, y_ref):
    mixed = jnp.concatenate([om_ref[...], orw_ref[...], gm_ref[...]], axis=-1)
    _mix_out(mixed, x_ref, mod_ref, wout_ref, gpost_ref, y_ref)


def _out_sample_kernel(lat_ref, mg_ref, orw_ref, bonus_ref, rg_ref, gm_ref, x_ref, mod_ref, wout_ref, gpost_ref,
                       lnxg_ref, lnxb_ref, wuv_ref, y_ref):
    lat = lat_ref[...]
    o_mla = jnp.concatenate(
        [jnp.dot(lat[:, h * KV_LORA:(h + 1) * KV_LORA].astype(BF16), wuv_ref[h],
                 preferred_element_type=F32) for h in range(MLA_HEADS)], axis=-1)
    o_mla = (o_mla * mg_ref[...].astype(F32)).astype(BF16)
    ones_bd = _head_ones(RWKV_W, RWKV_HEAD)
    o = orw_ref[...]
    c = o - _head_sum(o, ones_bd) * (1.0 / RWKV_HEAD)
    var = _head_sum(c * c, ones_bd) * (1.0 / RWKV_HEAD)
    on = c * lax.rsqrt(var + GN_EPS) * lnxg_ref[...] + lnxb_ref[...]
    o_rw = ((on + bonus_ref[...]) * rg_ref[...]).astype(BF16)
    mixed = jnp.concatenate([o_mla, o_rw, gm_ref[...]], axis=-1)
    _mix_out(mixed, x_ref, mod_ref, wout_ref, gpost_ref, y_ref)


def _out_prompt(o_mla, o_rw, gm, x2d, mod_b, lw, seq, tm):
    rows = x2d.shape[0]
    tiles_per_seq = seq // tm
    const = lambda i: (0, 0)
    row_blk = lambda i: (i, 0)
    return pl.pallas_call(
        _out_prompt_kernel,
        grid=(rows // tm,),
        in_specs=[
            pl.BlockSpec((tm, MLA_W), row_blk),
            pl.BlockSpec((tm, RWKV_W), row_blk),
            pl.BlockSpec((tm, GM_W), row_blk),
            pl.BlockSpec((tm, D_MODEL), row_blk),
            pl.BlockSpec((1, 1, 3 * D_MODEL), lambda i: (i // tiles_per_seq, 0, 0)),
            pl.BlockSpec((D_MODEL, D_MODEL), const),
            pl.BlockSpec((1, D_MODEL), const),
        ],
        out_specs=pl.BlockSpec((tm, D_MODEL), row_blk),
        out_shape=jax.ShapeDtypeStruct((rows, D_MODEL), F32),
        compiler_params=_cparams("arbitrary"),
        name="out_prompt",
    )(o_mla, o_rw, gm, x2d, mod_b, lw["w_out"], lw["g_post"])


def _out_sample(o_lat, mg, o_rw, bonus, rg, gm, x2d, mod_rows, lw):
    rows = x2d.shape[0]
    const = lambda i: (0, 0)
    full = lambda w: pl.BlockSpec((rows, w), const)
    return pl.pallas_call(
        _out_sample_kernel,
        grid=(1,),
        in_specs=[
            full(MLA_HEADS * KV_LORA), full(MLA_W), full(RWKV_W), full(RWKV_W), full(RWKV_W), full(GM_W),
            full(D_MODEL), full(3 * D_MODEL),
            pl.BlockSpec((D_MODEL, D_MODEL), const),
            pl.BlockSpec((1, D_MODEL), const),
            pl.BlockSpec((1, RWKV_W), const),
            pl.BlockSpec((1, RWKV_W), const),
            pl.BlockSpec((MLA_HEADS, KV_LORA, MLA_V), lambda i: (0, 0, 0)),
        ],
        out_specs=full(D_MODEL),
        out_shape=jax.ShapeDtypeStruct((rows, D_MODEL), F32),
        compiler_params=_cparams("arbitrary"),
        name="out_sample",
    )(o_lat, mg, o_rw, bonus, rg, gm, x2d, mod_rows, lw["w_out"], lw["g_post"], lw["lnx_g"], lw["lnx_b"],
      lw["w_uv_h"])


def _swap_halves(w):
    half = MLA_ROPE // 2
    return jnp.concatenate([w[..., half:], w[..., :half]], axis=-1)


def _rope_tables(pos):
    half = MLA_ROPE // 2
    inv = ROPE_THETA ** (-jnp.arange(half, dtype=F32) / half)
    ang = pos.astype(F32)[:, None] * inv[None, :]
    cos, sin = jnp.cos(ang), jnp.sin(ang)
    zeros = jnp.zeros((pos.shape[0], LANES - MLA_ROPE), F32)
    return (jnp.concatenate([cos, cos, zeros], axis=-1), jnp.concatenate([-sin, sin, zeros], axis=-1))


def _layer_weights(l, w_in, norm_pre_g, norm_post_g, q_norm_g, kv_norm_g, w_uq, w_uk, w_uv, rw_mu, rw_w0,
                   rw_w_decay_up, rw_a0, rw_w_a_up, rw_k_k, rw_k_a, rw_r_k, rw_lnx_g, rw_lnx_b, gm_ln_g,
                   gm_ln_b, gm_w_s, gm_b_s, w_out):
    w = w_in[l]
    split_lo, split_hi = Q_LORA + KV_LORA, Q_LORA + KV_LORA + MLA_ROPE
    kr = w[:, split_lo:split_hi]
    w_in_p = jnp.concatenate([w[:, :Q_LORA], kr, _swap_halves(kr), w[:, Q_LORA:split_lo], w[:, split_hi:]],
                             axis=1).astype(BF16)
    uq = w_uq[l]
    pe = uq[..., MLA_NOPE:]
    w_uq_p = jnp.concatenate([uq, _swap_halves(pe)], axis=-1).reshape(Q_LORA, MLA_HEADS * QHEAD_W).astype(BF16)
    zeros_lora = jnp.zeros((LANES - DECAY_LORA, RWKV_W), F32)
    row = lambda a: a.reshape(1, -1)
    return {
        "w_in": w_in_p, "g_pre": row(norm_pre_g[l]), "g_post": row(norm_post_g[l]),
        "g_q": row(q_norm_g[l]), "g_kv": row(kv_norm_g[l]), "w_uq": w_uq_p,
        "w_uk": w_uk[l].reshape(KV_LORA, MLA_W).astype(BF16),
        "w_uv": w_uv[l].reshape(KV_LORA, MLA_W).astype(BF16),
        "w_ukt": jnp.transpose(w_uk[l], (1, 2, 0)).astype(BF16),
        "w_uv_h": jnp.transpose(w_uv[l], (1, 0, 2)).astype(BF16),
        "rw_mu": row(rw_mu[l]), "rw_w0": row(rw_w0[l]), "rw_a0": row(rw_a0[l]),
        "rw_wd": jnp.concatenate([rw_w_decay_up[l], zeros_lora], axis=0).astype(BF16),
        "rw_wa": jnp.concatenate([zeros_lora, rw_w_a_up[l]], axis=0).astype(BF16),
        "rw_k_k": row(rw_k_k[l]), "rw_k_a": row(rw_k_a[l]), "rw_r_k": row(rw_r_k[l]),
        "lnx_g": row(rw_lnx_g[l]), "lnx_b": row(rw_lnx_b[l]),
        "ln_g": row(gm_ln_g[l]), "ln_b": row(gm_ln_b[l]),
        "gm_ws": gm_w_s[l],
        "gm_bias": jnp.repeat(gm_b_s[l].T, GM_GROUP_W, axis=1),
        "gm_coef0": row(jnp.repeat(gm_w_s[l][:, 0, 0], GM_GROUP_W)),
        "gm_bias0": row(jnp.repeat(gm_b_s[l][:, 0], GM_GROUP_W)),
        "w_out": w_out[l].astype(BF16),
    }


def _pick(full, want):
    return want if full % want == 0 else full


def kernel(x_prompt, x_sample, c_prompt, c_sample, cache_ckv, cache_kpe, page_table, state_wkv, state_shift, w_ada, b_ada, norm_pre_g, norm_post_g, w_in, q_norm_g, kv_norm_g, w_uq, w_uk, w_uv, rw_mu, rw_w0, rw_w_decay_up, rw_a0, rw_w_a_up, rw_k_k, rw_k_a, rw_r_k, rw_lnx_g, rw_lnx_b, gm_ln_g, gm_ln_b, gm_w_s, gm_b_s, w_out):
    batch, seq, _ = x_prompt.shape
    dec_batch, dec_seq, _ = x_sample.shape
    depth = w_in.shape[0]
    n_pages = page_table.shape[1]
    past_len = n_pages * cache_ckv.shape[2]
    assert dec_seq == 1 and seq % CHUNK == 0

    assert seq % RW_CHUNK == 0
    tm_proj = _pick(seq, 512)
    tm_out = _pick(seq, 1024)
    tq = _pick(seq, 512)
    rw_nb = _pick(batch, 8)

    mod = _ada_mod(jnp.concatenate([c_prompt, c_sample], axis=0), w_ada, b_ada)
    cs_p, sn_p = _rope_tables(jnp.arange(seq))
    cs_s, sn_s = _rope_tables(jnp.full((dec_batch,), past_len))
    cache_kpe_t = jnp.swapaxes(cache_kpe, 2, 3)

    y_p = x_prompt.reshape(batch * seq, D_MODEL)
    y_s = x_sample.reshape(dec_batch, D_MODEL)
    outs = {k: [] for k in ("ckv_p", "kpe_p", "ckv_s", "kpe_s", "wkv_p", "wkv_s", "sh_p", "sh_s", "vc_p", "vc_s")}
    for l in range(depth):
        lw = _layer_weights(l, w_in, norm_pre_g, norm_post_g, q_norm_g, kv_norm_g, w_uq, w_uk, w_uv, rw_mu,
                            rw_w0, rw_w_decay_up, rw_a0, rw_w_a_up, rw_k_k, rw_k_a, rw_r_k, rw_lnx_g,
                            rw_lnx_b, gm_ln_g, gm_ln_b, gm_w_s, gm_b_s, w_out)
        mod_p = mod[l, :batch].reshape(batch, 1, 3 * D_MODEL)
        q, k, v, mg, ckv, kpe, rw_in, rg, gm, vn_last = _proj_prompt(y_p, mod_p, lw, cs_p, sn_p, batch, seq,
                                                                     tm_proj)
        o_mla = _flash(q, k, v, mg, batch, seq, tq)
        o_rw, wkv_new = _rwkv_prompt(rw_in, rg, lw, batch, seq, rw_nb)
        y_p = _out_prompt(o_mla, o_rw, gm, y_p, mod_p, lw, seq, tm_out)
        outs["ckv_p"].append(ckv.reshape(batch, seq, KV_LORA))
        outs["kpe_p"].append(kpe.reshape(batch, seq, MLA_ROPE))
        outs["wkv_p"].append(wkv_new)
        outs["sh_p"].append(rw_in.reshape(batch, seq, SHIFT_W)[:, -1])
        outs["vc_p"].append(vn_last)

        mod_s = mod[l, batch:]
        qa, qp, mg_s, ckv_s, kpe_s, rw_s, rg_s, gm_s, vn_s = _proj_sample(y_s, mod_s, lw, cs_s, sn_s)
        o_lat = _paged_attention(qa, qp, ckv_s, kpe_s, cache_ckv, cache_kpe_t, page_table, l)
        prep_s = _rwkv_prep_sample(rw_s, lw, state_shift[l])
        o_rw_s, wkv_s = _rwkv_mix(prep_s, state_wkv[l], dec_batch, 1, 1)
        y_s = _out_sample(o_lat, mg_s, o_rw_s, prep_s[6], rg_s, gm_s, y_s, mod_s, lw)
        outs["ckv_s"].append(ckv_s.reshape(dec_batch, 1, KV_LORA))
        outs["kpe_s"].append(kpe_s[:, :MLA_ROPE].reshape(dec_batch, 1, MLA_ROPE))
        outs["wkv_s"].append(wkv_s)
        outs["sh_s"].append(rw_s)
        outs["vc_s"].append(vn_s.reshape(dec_batch, 1, GM_W))

    st = lambda name: jnp.stack(outs[name])
    return (y_p.reshape(batch, seq, D_MODEL), y_s.reshape(dec_batch, 1, D_MODEL),
            st("ckv_p"), st("kpe_p"), st("ckv_s"), st("kpe_s"), st("wkv_p"), st("wkv_s"),
            st("sh_p"), st("sh_s"), st("vc_p"), st("vc_s"))
```

```python
import functools

import numpy as np
import jax
import jax.numpy as jnp
from jax import lax
from jax.experimental import pallas as pl
from jax.experimental.pallas import tpu as pltpu

F32 = jnp.float32
BF16 = jnp.bfloat16

D_MODEL = 1024
MLA_V = 128
MLA_W = D_MODEL // 2
MLA_HEADS = MLA_W // MLA_V
MLA_NOPE = 128
MLA_ROPE = 64
MLA_QK = MLA_NOPE + MLA_ROPE
Q_LORA = (3 * D_MODEL) // 8
KV_LORA = D_MODEL // 4
ROPE_THETA = 10000.0
ATTN_SCALE = MLA_QK ** -0.5
Q_SCALE = ATTN_SCALE * float(np.log2(np.e))
RWKV_W = D_MODEL // 4
RWKV_HEAD = 64
RWKV_HEADS = RWKV_W // RWKV_HEAD
DECAY_LORA = 64
ICLR_LORA = 64
SHIFT_W = 3 * RWKV_W + DECAY_LORA + ICLR_LORA
GN_EPS = 64e-5
GM_W = D_MODEL // 4
GM_GROUPS = 4
GM_GROUP_W = GM_W // GM_GROUPS
CHUNK = 128
RMS_EPS = 1e-6
LN_EPS = 1e-5

LANES = 128
SUBLANES = 8
VMEM_LIMIT_BYTES = 56 * 1024 * 1024

QHEAD_W = 2 * LANES
OFF_Q = 0
OFF_KR = OFF_Q + Q_LORA
OFF_KV = OFF_KR + LANES
OFF_MG = OFF_KV + KV_LORA
OFF_RW = OFF_MG + MLA_W
OFF_RG = OFF_RW + SHIFT_W
OFF_GU = OFF_RG + RWKV_W
OFF_GV = OFF_GU + GM_W
OFF_GG = OFF_GV + GM_W
IN_W_P = OFF_GG + GM_W

NEG_BIG = -1e30


def _cparams(*sem):
    return pltpu.CompilerParams(dimension_semantics=sem, vmem_limit_bytes=VMEM_LIMIT_BYTES)


def _silu(x):
    return x * jax.nn.sigmoid(x)


def _rms(x, g, eps=RMS_EPS):
    return x * lax.rsqrt(jnp.mean(x * x, axis=-1, keepdims=True) + eps) * g


def _rows2d(ref):
    m = ref[...]
    return m.reshape(m.shape[-2], m.shape[-1])


def _group_index(idx, group):
    shift = group.bit_length() - 1
    assert 1 << shift == group
    return lax.shift_right_logical(idx, shift)


def _head_ones(width, head):
    r = _group_index(lax.broadcasted_iota(jnp.int32, (width, width), 0), head)
    c = _group_index(lax.broadcasted_iota(jnp.int32, (width, width), 1), head)
    return jnp.where(r == c, 1.0, 0.0).astype(BF16)


def _head_sum(x, ones_bd):
    hi = x.astype(BF16)
    lo = (x - hi.astype(F32)).astype(BF16)
    return (jnp.dot(hi, ones_bd, preferred_element_type=F32)
            + jnp.dot(lo, ones_bd, preferred_element_type=F32))


def _rope_pair(blk, cs, sn):
    return blk * cs + pltpu.roll(blk, MLA_ROPE, 1) * sn


def _ada_kernel(c_ref, w_ref, b_ref, o_ref):
    c = _silu(c_ref[...]).astype(BF16)
    o_ref[0] = jnp.dot(c, w_ref[0], preferred_element_type=F32) + b_ref[0]


def _ada_mod(c_all, w_ada, b_ada):
    depth = w_ada.shape[0]
    n_rows = c_all.shape[0]
    n_tiles = (3 * D_MODEL) // D_MODEL
    return pl.pallas_call(
        _ada_kernel,
        grid=(depth, n_tiles),
        in_specs=[
            pl.BlockSpec((n_rows, D_MODEL), lambda l, j: (0, 0)),
            pl.BlockSpec((1, D_MODEL, D_MODEL), lambda l, j: (l, 0, j)),
            pl.BlockSpec((1, 1, D_MODEL), lambda l, j: (l, 0, j)),
        ],
        out_specs=pl.BlockSpec((1, n_rows, D_MODEL), lambda l, j: (l, 0, j)),
        out_shape=jax.ShapeDtypeStruct((depth, n_rows, 3 * D_MODEL), F32),
        compiler_params=_cparams("arbitrary", "arbitrary"),
        name="ada_mod",
    )(c_all, w_ada.astype(BF16), b_ada.reshape(depth, 1, 3 * D_MODEL))


def _proj_common(x_ref, mod_ref, gpre_ref, win_ref):
    mod = _rows2d(mod_ref)
    shift = mod[:, 0:D_MODEL]
    scale = mod[:, D_MODEL:2 * D_MODEL]
    h = _rms(x_ref[...], gpre_ref[...]) * (1.0 + scale) + shift
    hb = h.astype(BF16)

    def seg(off, width):
        return jnp.dot(hb, win_ref[:, off:off + width], preferred_element_type=F32)

    return seg


def _q_heads(seg, gq_ref, wuq_ref, cs, sn):
    q_kr = seg(OFF_Q, Q_LORA + LANES)
    qn = _rms(q_kr[:, :Q_LORA], gq_ref[...]).astype(BF16)
    q_all = jnp.dot(qn, wuq_ref[...], preferred_element_type=F32)
    heads = []
    for h in range(MLA_HEADS):
        nope = q_all[:, h * QHEAD_W:h * QHEAD_W + LANES] * Q_SCALE
        pe = _rope_pair(q_all[:, h * QHEAD_W + LANES:(h + 1) * QHEAD_W], cs, sn) * Q_SCALE
        heads.append((nope, pe))
    return heads, _rope_pair(q_kr[:, Q_LORA:], cs, sn)


def _gm_norm(seg, lng_ref, lnb_ref):
    v = seg(OFF_GV, GM_W)
    mu = jnp.mean(v, axis=-1, keepdims=True)
    c = v - mu
    var = jnp.mean(c * c, axis=-1, keepdims=True)
    return c * lax.rsqrt(var + LN_EPS) * lng_ref[...] + lnb_ref[...]


def _proj_prompt_kernel(x_ref, mod_ref, gpre_ref, win_ref, gq_ref, wuq_ref, gkv_ref, wuk_ref, wuv_ref,
                        cs_ref, sn_ref, lng_ref, lnb_ref, ws_ref, bs_ref,
                        q_ref, k_ref, v_ref, mg_ref, ckv_ref, kpe_ref, rw_ref, rg_ref, gm_ref, vn_ref,
                        *, tiles_per_seq):
    seg = _proj_common(x_ref, mod_ref, gpre_ref, win_ref)
    cs = cs_ref[...]
    sn = sn_ref[...]
    tm = x_ref.shape[0]

    heads, kpe = _q_heads(seg, gq_ref, wuq_ref, cs, sn)
    for h, (nope, pe) in enumerate(heads):
        q_ref[:, h * QHEAD_W:h * QHEAD_W + LANES] = nope.astype(BF16)
        q_ref[:, h * QHEAD_W + LANES:(h + 1) * QHEAD_W] = pe.astype(BF16)
    ckv = _rms(seg(OFF_KV, KV_LORA), gkv_ref[...])
    ckv_ref[...] = ckv
    kpe_ref[...] = kpe[:, :MLA_ROPE]
    ckv_b = ckv.astype(BF16)
    k_nope = jnp.dot(ckv_b, wuk_ref[...], preferred_element_type=F32)
    v_ref[...] = jnp.dot(ckv_b, wuv_ref[...], preferred_element_type=F32).astype(BF16)
    kpe_b = kpe.astype(BF16)
    for h in range(MLA_HEADS):
        k_ref[:, h * QHEAD_W:h * QHEAD_W + LANES] = k_nope[:, h * MLA_NOPE:(h + 1) * MLA_NOPE].astype(BF16)
        k_ref[:, h * QHEAD_W + LANES:(h + 1) * QHEAD_W] = kpe_b
    mg_ref[...] = _silu(seg(OFF_MG, MLA_W)).astype(BF16)

    rw_ref[...] = seg(OFF_RW, SHIFT_W)
    rg_ref[...] = _silu(seg(OFF_RG, RWKV_W))

    vn = _gm_norm(seg, lng_ref, lnb_ref)
    vn_b = vn.astype(BF16)
    gate_u = _silu(seg(OFF_GG, GM_W)) * seg(OFF_GU, GM_W)
    row = lax.broadcasted_iota(jnp.int32, (CHUNK, CHUNK), 0)
    col = lax.broadcasted_iota(jnp.int32, (CHUNK, CHUNK), 1)
    lane_group = _group_index(lax.broadcasted_iota(jnp.int32, (CHUNK, GM_W), 1), GM_GROUP_W)
    w_tril = [jnp.where(row >= col, ws_ref[g], 0.0).astype(BF16) for g in range(GM_GROUPS)]
    for c in range(tm // CHUNK):
        vc = vn_b[c * CHUNK:(c + 1) * CHUNK]
        z = bs_ref[...]
        for g in range(GM_GROUPS):
            zg = jnp.dot(w_tril[g], vc, preferred_element_type=F32)
            z = z + jnp.where(lane_group == g, zg, 0.0)
        gm_ref[c * CHUNK:(c + 1) * CHUNK, :] = (gate_u[c * CHUNK:(c + 1) * CHUNK] * z).astype(BF16)

    @pl.when(pl.program_id(0) % tiles_per_seq == tiles_per_seq - 1)
    def _():
        vn_ref[0] = vn[tm - CHUNK:, :]


def _proj_prompt(x2d, mod_b, lw, cs, sn, batch, seq, tm):
    rows = x2d.shape[0]
    tiles_per_seq = seq // tm
    const = lambda i: (0, 0)
    row_blk = lambda i: (i, 0)
    outs = [
        (QHEAD_W * MLA_HEADS, BF16), (QHEAD_W * MLA_HEADS, BF16), (MLA_W, BF16), (MLA_W, BF16),
        (KV_LORA, F32), (MLA_ROPE, F32), (SHIFT_W, F32), (RWKV_W, F32), (GM_W, BF16),
    ]
    out_shape = [jax.ShapeDtypeStruct((rows, w), dt) for w, dt in outs]
    out_specs = [pl.BlockSpec((tm, w), row_blk) for w, _ in outs]
    out_shape.append(jax.ShapeDtypeStruct((batch, CHUNK, GM_W), F32))
    out_specs.append(pl.BlockSpec((1, CHUNK, GM_W), lambda i: (i // tiles_per_seq, 0, 0)))
    return pl.pallas_call(
        functools.partial(_proj_prompt_kernel, tiles_per_seq=tiles_per_seq),
        grid=(rows // tm,),
        in_specs=[
            pl.BlockSpec((tm, D_MODEL), row_blk),
            pl.BlockSpec((1, 1, 3 * D_MODEL), lambda i: (i // tiles_per_seq, 0, 0)),
            pl.BlockSpec((1, D_MODEL), const),
            pl.BlockSpec((D_MODEL, IN_W_P), const),
            pl.BlockSpec((1, Q_LORA), const),
            pl.BlockSpec((Q_LORA, QHEAD_W * MLA_HEADS), const),
            pl.BlockSpec((1, KV_LORA), const),
            pl.BlockSpec((KV_LORA, MLA_W), const),
            pl.BlockSpec((KV_LORA, MLA_W), const),
            pl.BlockSpec((tm, LANES), lambda i: (i % tiles_per_seq, 0)),
            pl.BlockSpec((tm, LANES), lambda i: (i % tiles_per_seq, 0)),
            pl.BlockSpec((1, GM_W), const),
            pl.BlockSpec((1, GM_W), const),
            pl.BlockSpec((GM_GROUPS, CHUNK, CHUNK), lambda i: (0, 0, 0)),
            pl.BlockSpec((CHUNK, GM_W), const),
        ],
        out_specs=out_specs,
        out_shape=out_shape,
        compiler_params=_cparams("arbitrary"),
        name="proj_prompt",
    )(x2d, mod_b, lw["g_pre"], lw["w_in"], lw["g_q"], lw["w_uq"], lw["g_kv"], lw["w_uk"], lw["w_uv"],
      cs, sn, lw["ln_g"], lw["ln_b"], lw["gm_ws"], lw["gm_bias"])


def _proj_sample_kernel(x_ref, mod_ref, gpre_ref, win_ref, gq_ref, wuq_ref, gkv_ref, wukt_ref,
                        cs_ref, sn_ref, lng_ref, lnb_ref, coef_ref, bias_ref,
                        qa_ref, qp_ref, mg_ref, ckv_ref, kpe_ref, rw_ref, rg_ref, gm_ref, vn_ref):
    seg = _proj_common(x_ref, mod_ref, gpre_ref, win_ref)
    cs = cs_ref[...]
    sn = sn_ref[...]
    heads, kpe = _q_heads(seg, gq_ref, wuq_ref, cs, sn)
    for h, (nope, pe) in enumerate(heads):
        qa_ref[:, h * KV_LORA:(h + 1) * KV_LORA] = jnp.dot(nope.astype(BF16), wukt_ref[h],
                                                           preferred_element_type=F32)
        qp_ref[:, h * LANES:(h + 1) * LANES] = pe
    ckv_ref[...] = _rms(seg(OFF_KV, KV_LORA), gkv_ref[...])
    kpe_ref[...] = kpe
    mg_ref[...] = _silu(seg(OFF_MG, MLA_W)).astype(BF16)
    rw_ref[...] = seg(OFF_RW, SHIFT_W)
    rg_ref[...] = _silu(seg(OFF_RG, RWKV_W))
    vn = _gm_norm(seg, lng_ref, lnb_ref)
    vn_ref[...] = vn
    z = vn * coef_ref[...] + bias_ref[...]
    gm_ref[...] = (_silu(seg(OFF_GG, GM_W)) * seg(OFF_GU, GM_W) * z).astype(BF16)


def _proj_sample(x2d, mod_rows, lw, cs, sn):
    rows = x2d.shape[0]
    const = lambda i: (0, 0)
    outs = [
        (KV_LORA * MLA_HEADS, F32), (LANES * MLA_HEADS, F32), (MLA_W, BF16), (KV_LORA, F32), (LANES, F32),
        (SHIFT_W, F32), (RWKV_W, F32), (GM_W, BF16), (GM_W, F32),
    ]
    return pl.pallas_call(
        _proj_sample_kernel,
        grid=(1,),
        in_specs=[
            pl.BlockSpec((rows, D_MODEL), const),
            pl.BlockSpec((rows, 3 * D_MODEL), const),
            pl.BlockSpec((1, D_MODEL), const),
            pl.BlockSpec((D_MODEL, IN_W_P), const),
            pl.BlockSpec((1, Q_LORA), const),
            pl.BlockSpec((Q_LORA, QHEAD_W * MLA_HEADS), const),
            pl.BlockSpec((1, KV_LORA), const),
            pl.BlockSpec((MLA_HEADS, MLA_NOPE, KV_LORA), lambda i: (0, 0, 0)),
            pl.BlockSpec((rows, LANES), const),
            pl.BlockSpec((rows, LANES), const),
            pl.BlockSpec((1, GM_W), const),
            pl.BlockSpec((1, GM_W), const),
            pl.BlockSpec((1, GM_W), const),
            pl.BlockSpec((1, GM_W), const),
        ],
        out_specs=[pl.BlockSpec((rows, w), const) for w, _ in outs],
        out_shape=[jax.ShapeDtypeStruct((rows, w), dt) for w, dt in outs],
        compiler_params=_cparams("arbitrary"),
        name="proj_sample",
    )(x2d, mod_rows, lw["g_pre"], lw["w_in"], lw["g_q"], lw["w_uq"], lw["g_kv"], lw["w_ukt"],
      cs, sn, lw["ln_g"], lw["ln_b"], lw["gm_coef0"], lw["gm_bias0"])


def _flash_kernel(q_ref, k_ref, v_ref, g_ref, o_ref, *, tq):
    seq = q_ref.shape[1]
    row = lax.broadcasted_iota(jnp.int32, (tq, tq), 0)
    col = lax.broadcasted_iota(jnp.int32, (tq, tq), 1)
    for qi in range(seq // tq):
        rows = slice(qi * tq, (qi + 1) * tq)
        q = q_ref[0, rows, :]
        m = jnp.full((tq, 1), NEG_BIG, F32)
        l = jnp.zeros((tq, 1), F32)
        acc = jnp.zeros((tq, MLA_V), F32)
        for ki in range(qi + 1):
            cols = slice(ki * tq, (ki + 1) * tq)
            s = lax.dot_general(q, k_ref[0, cols, :], (((1,), (1,)), ((), ())),
                                preferred_element_type=F32)
            if ki == qi:
                s = jnp.where(row >= col, s, NEG_BIG)
            m_new = jnp.maximum(m, jnp.max(s, axis=-1, keepdims=True))
            alpha = jnp.exp2(m - m_new)
            p = jnp.exp2(s - m_new)
            l = alpha * l + jnp.sum(p, axis=-1, keepdims=True)
            acc = alpha * acc + jnp.dot(p.astype(BF16), v_ref[0, cols, :], preferred_element_type=F32)
            m = m_new
        o_ref[0, rows, :] = (acc / l * g_ref[0, rows, :].astype(F32)).astype(BF16)


def _flash(q, k, v, g, batch, seq, tq):
    q3 = q.reshape(batch, seq, QHEAD_W * MLA_HEADS)
    k3 = k.reshape(batch, seq, QHEAD_W * MLA_HEADS)
    v3 = v.reshape(batch, seq, MLA_W)
    g3 = g.reshape(batch, seq, MLA_W)
    head_blk = lambda b, h: (b, 0, h)
    out = pl.pallas_call(
        functools.partial(_flash_kernel, tq=tq),
        grid=(batch, MLA_HEADS),
        in_specs=[
            pl.BlockSpec((1, seq, QHEAD_W), head_blk),
            pl.BlockSpec((1, seq, QHEAD_W), head_blk),
            pl.BlockSpec((1, seq, MLA_V), head_blk),
            pl.BlockSpec((1, seq, MLA_V), head_blk),
        ],
        out_specs=pl.BlockSpec((1, seq, MLA_V), head_blk),
        out_shape=jax.ShapeDtypeStruct((batch, seq, MLA_W), BF16),
        compiler_params=_cparams("arbitrary", "arbitrary"),
        name="flash_prompt",
    )(q3, k3, v3, g3)
    return out.reshape(batch * seq, MLA_W)


N_SLOTS = 3
LOOKAHEAD = N_SLOTS - 1


def _paged_kernel(pt_ref, qa_ref, qp_ref, cn_ref, kn_ref, ckv_hbm, kpe_hbm, o_ref, ckv_buf, kpe_buf, sem,
                  *, layer, n_pages, page, group_tokens):
    b = pl.program_id(0)
    n_seq = pl.num_programs(0)
    slot = lax.rem(b, N_SLOTS)

    def page_copies(seq_idx, sl):
        copies = []
        for i in range(n_pages):
            pg = pt_ref[seq_idx, i]
            copies.append(pltpu.make_async_copy(
                ckv_hbm.at[layer, pg], ckv_buf.at[sl, pl.ds(i * page, page), :], sem.at[0, sl]))
            copies.append(pltpu.make_async_copy(kpe_hbm.at[layer, pg], kpe_buf.at[sl, i], sem.at[1, sl]))
        return copies

    def start_all(copies):
        for i, c in enumerate(copies):
            c.start(priority=(i // 2) % 2)

    @pl.when(b == 0)
    def _():
        for ahead in range(LOOKAHEAD):
            start_all(page_copies(ahead, ahead))

    for c in page_copies(b, slot):
        c.wait()

    nxt = lax.rem(b + LOOKAHEAD, n_seq)
    nxt_slot = lax.rem(b + LOOKAHEAD, N_SLOTS)
    start_all(page_copies(nxt, nxt_slot))

    qa = qa_ref[0]
    qp = qp_ref[0][:, :MLA_ROPE]
    qa_b = qa.astype(BF16)
    qp_b = qp.astype(BF16)
    group_pages = group_tokens // page
    groups = [slice(g * group_tokens, (g + 1) * group_tokens) for g in range(n_pages // group_pages)]
    ckv = [ckv_buf[slot, gs, :].astype(BF16) for gs in groups]
    kpe_t = [jnp.concatenate([kpe_buf[slot, g * group_pages + i] for i in range(group_pages)],
                             axis=1).astype(BF16) for g in range(len(groups))]
    s = [lax.dot_general(qa_b, ck, (((1,), (1,)), ((), ())), preferred_element_type=F32)
         + jnp.dot(qp_b, kp, preferred_element_type=F32) for ck, kp in zip(ckv, kpe_t)]
    m_g = [jnp.max(sg, axis=-1, keepdims=True) for sg in s]
    p = [jnp.exp2(sg - mg) for sg, mg in zip(s, m_g)]
    l_g = [jnp.sum(pg, axis=-1, keepdims=True) for pg in p]
    acc_g = [jnp.dot(pg.astype(BF16), ck, preferred_element_type=F32) for pg, ck in zip(p, ckv)]
    cn = cn_ref[0]
    kn = kn_ref[0][:, :MLA_ROPE]
    s_new = jnp.sum(qa * cn, axis=-1, keepdims=True) + jnp.sum(qp * kn, axis=-1, keepdims=True)
    m = functools.reduce(jnp.maximum, m_g, s_new)
    p_new = jnp.exp2(s_new - m)
    w_g = [jnp.exp2(mg - m) for mg in m_g]
    l = sum(wg * lg for wg, lg in zip(w_g, l_g)) + p_new
    o_ref[0] = (sum(wg * ag for wg, ag in zip(w_g, acc_g)) + p_new * cn) / l

    @pl.when(b == n_seq - 1)
    def _():
        for ahead in range(1, LOOKAHEAD + 1):
            for c in page_copies(lax.rem(b + ahead, n_seq), lax.rem(b + ahead, N_SLOTS)):
                c.wait()


def _paged_attention(qa, qp, ckv_new, kpe_new, cache_ckv, cache_kpe_t, page_table, layer):
    dec_batch, n_pages = page_table.shape
    page = cache_ckv.shape[2]
    head_pad = ((0, 0), (0, SUBLANES - MLA_HEADS), (0, 0))
    qa3 = jnp.pad(qa.reshape(dec_batch, MLA_HEADS, KV_LORA), head_pad)
    qp3 = jnp.pad(qp.reshape(dec_batch, MLA_HEADS, LANES), head_pad)
    cn3 = ckv_new.reshape(dec_batch, 1, KV_LORA)
    kn3 = kpe_new.reshape(dec_batch, 1, LANES)
    row_blk = lambda b, pt: (b, 0, 0)
    out = pl.pallas_call(
        functools.partial(_paged_kernel, layer=layer, n_pages=n_pages, page=page,
                          group_tokens=_pick(n_pages * page, 2048)),
        grid_spec=pltpu.PrefetchScalarGridSpec(
            num_scalar_prefetch=1,
            grid=(dec_batch,),
            in_specs=[
                pl.BlockSpec((1, SUBLANES, KV_LORA), row_blk),
                pl.BlockSpec((1, SUBLANES, LANES), row_blk),
                pl.BlockSpec((1, 1, KV_LORA), row_blk),
                pl.BlockSpec((1, 1, LANES), row_blk),
                pl.BlockSpec(memory_space=pl.ANY),
                pl.BlockSpec(memory_space=pl.ANY),
            ],
            out_specs=pl.BlockSpec((1, SUBLANES, KV_LORA), row_blk),
            scratch_shapes=[
                pltpu.VMEM((N_SLOTS, n_pages * page, KV_LORA), F32),
                pltpu.VMEM((N_SLOTS, n_pages, MLA_ROPE, page), F32),
                pltpu.SemaphoreType.DMA((2, N_SLOTS)),
            ],
        ),
        out_shape=jax.ShapeDtypeStruct((dec_batch, SUBLANES, KV_LORA), F32),
        compiler_params=_cparams("arbitrary"),
        name="paged_attention",
    )(page_table, qa3, qp3, cn3, kn3, cache_ckv, cache_kpe_t)
    return out[:, :MLA_HEADS].reshape(dec_batch, MLA_HEADS * KV_LORA)


def _rwkv_prep_values(p, prev, mu_ref, w0_ref, wd_ref, a0_ref, wa_ref, kk_ref, ka_ref, rk_ref):
    xm = p + (prev - p) * mu_ref[...]
    r = xm[:, 0:RWKV_W]
    k = xm[:, RWKV_W:2 * RWKV_W]
    v = xm[:, 2 * RWKV_W:3 * RWKV_W]
    tail = xm[:, 3 * RWKV_W:]
    lane = lax.broadcasted_iota(jnp.int32, tail.shape, 1)
    lora_in = jnp.where(lane < DECAY_LORA, jnp.tanh(tail), tail).astype(BF16)
    dw = jnp.dot(lora_in, wd_ref[...], preferred_element_type=F32)
    da = jnp.dot(lora_in, wa_ref[...], preferred_element_type=F32)
    z = -(w0_ref[...] + dw)
    softplus = jnp.maximum(z, 0.0) + jnp.log(1.0 + jnp.exp(-jnp.abs(z)))
    w = -softplus - 0.5
    log_decay = -jnp.exp(w)
    a = jax.nn.sigmoid(a0_ref[...] + da)
    ones_bd = _head_ones(RWKV_W, RWKV_HEAD)
    kk = k * kk_ref[...]
    kk = kk / jnp.maximum(jnp.sqrt(_head_sum(kk * kk, ones_bd)), 1e-12)
    k_mod = k * (1.0 + (a - 1.0) * ka_ref[...])
    bonus = _head_sum(r * k_mod * rk_ref[...], ones_bd) * v
    return r, k_mod, v, kk, kk * a, log_decay, bonus


def _rwkv_param_specs(index_map):
    widths = (SHIFT_W, RWKV_W, None, RWKV_W, None, RWKV_W, RWKV_W, RWKV_W)
    return [pl.BlockSpec((LANES, RWKV_W) if w is None else (1, w), index_map) for w in widths]


def _rwkv_params(lw):
    return (lw["rw_mu"], lw["rw_w0"], lw["rw_wd"], lw["rw_a0"], lw["rw_wa"], lw["rw_k_k"], lw["rw_k_a"],
            lw["rw_r_k"])


def _rwkv_prep_sample_kernel(p_ref, prev_ref, mu_ref, w0_ref, wd_ref, a0_ref, wa_ref, kk_ref, ka_ref, rk_ref,
                             r_o, k_o, v_o, kk_o, b_o, d_o, bonus_o):
    r, k_mod, v, kk, b, log_decay, bonus = _rwkv_prep_values(
        p_ref[...], prev_ref[...], mu_ref, w0_ref, wd_ref, a0_ref, wa_ref, kk_ref, ka_ref, rk_ref)
    r_o[...] = r
    k_o[...] = k_mod
    v_o[...] = v
    kk_o[...] = kk
    b_o[...] = b
    d_o[...] = jnp.exp(log_decay)
    bonus_o[...] = bonus


def _rwkv_prep_sample(rw_in, lw, shift_rows):
    rows = rw_in.shape[0]
    const = lambda i: (0, 0)
    return pl.pallas_call(
        _rwkv_prep_sample_kernel,
        grid=(1,),
        in_specs=[pl.BlockSpec((rows, SHIFT_W), const)] * 2 + _rwkv_param_specs(const),
        out_specs=[pl.BlockSpec((rows, RWKV_W), const)] * 7,
        out_shape=[jax.ShapeDtypeStruct((rows, RWKV_W), F32)] * 7,
        compiler_params=_cparams("arbitrary"),
        name="rwkv_prep_sample",
    )(rw_in, shift_rows, *_rwkv_params(lw))


RW_CHUNK = 64
RW_SUB = 16


def _split3(x):
    hi = x.astype(BF16)
    r1 = x - hi.astype(F32)
    mid = r1.astype(BF16)
    lo = (r1 - mid.astype(F32)).astype(BF16)
    return hi, mid, lo


def _block_diag(y, bd_mask):
    return jnp.where(bd_mask, jnp.concatenate([y] * RWKV_HEADS, axis=0), 0.0).astype(BF16)


def _mm(x, y_bd):
    return jnp.dot(x.astype(BF16), y_bd, preferred_element_type=F32)


def _rwkv_chunks(vals, n0, masks):
    bd_mask, tri_incl, strict, incl, same_sub, eye_tiled, ones_bd = masks
    C = RW_CHUNK
    seqs = range(len(vals))
    bd = lambda y: _block_diag(y, bd_mask)
    r, kx, v, kap, bb, lam = ([val[j] for val in vals] for j in range(6))
    cum = [sum(jnp.dot(tri_incl, part, preferred_element_type=F32) for part in _split3(lam[i])) for i in seqs]
    cum_last = [c[C - 1:C, :] for c in cum]
    e_neg = [jnp.exp(-c) for c in cum]
    ap = [jnp.concatenate([kap[i] * jnp.exp(cum[i] - lam[i]), r[i] * jnp.exp(cum[i])], axis=0).astype(BF16)
          for i in seqs]
    rhs = [jnp.concatenate([bd(bb[i] * e_neg[i]), bd(kx[i] * e_neg[i])], axis=0) for i in seqs]
    g = [lax.dot_general(ap[i], rhs[i], (((1,), (1,)), ((), ())), preferred_element_type=F32) for i in seqs]
    l_ab = [jnp.where(strict, gi[:C, :RWKV_W], 0.0) for gi in g]
    l_ak = [jnp.where(strict, gi[:C, RWKV_W:], 0.0) for gi in g]
    q_pb = [jnp.where(incl, gi[C:, :RWKV_W], 0.0) for gi in g]
    q_pk = [jnp.where(incl, gi[C:, RWKV_W:], 0.0) for gi in g]
    v_bd = [bd(vi) for vi in v]
    apn = [jnp.dot(ap[i], bd(n0[i]), preferred_element_type=F32) for i in seqs]
    z_w = [apn[i][:C] + _mm(l_ak[i], v_bd[i]) for i in seqs]

    l_d = [jnp.where(same_sub, li, 0.0) for li in l_ab]
    z_l = [l_ab[i] - l_d[i] for i in seqs]
    x = [(-li).astype(BF16) for li in l_d]
    power = l_d
    n_double = RW_SUB.bit_length() - 1
    for step in range(n_double):
        xz = [jnp.dot(x[i], jnp.concatenate([bd(z_l[i]), bd(z_w[i])], axis=1), preferred_element_type=F32)
              for i in seqs]
        z_l = [z_l[i] + xz[i][:, :RWKV_W] for i in seqs]
        z_w = [z_w[i] + xz[i][:, RWKV_W:] for i in seqs]
        if step + 1 < n_double:
            power = [_mm(pw, bd(pw)) for pw in power]
            x = [pw.astype(BF16) for pw in power]
    n = z_l
    n_sq = [_mm(ni, bd(ni)) for ni in n]
    y = [z_w[i] + _mm(n_sq[i], bd(z_w[i])) for i in seqs]
    u = [_mm(n[i], bd(y[i])) - y[i] for i in seqs]

    o = [apn[i][C:] + _mm(q_pb[i], bd(u[i])) + _mm(q_pk[i], v_bd[i]) for i in seqs]

    e_end = [jnp.exp(cum_last[i] - cum[i]) for i in seqs]
    lhs_t = [jnp.concatenate([bb[i] * e_end[i], kx[i] * e_end[i]], axis=0).T.astype(BF16) for i in seqs]
    full = [jnp.dot(lhs_t[i], jnp.concatenate([u[i], v[i]], axis=0).astype(BF16), preferred_element_type=F32)
            for i in seqs]
    heads = [slice(h * RWKV_HEAD, (h + 1) * RWKV_HEAD) for h in range(RWKV_HEADS)]
    delta = [sum(jnp.where(bd_mask[hs], f[hs], 0.0) for hs in heads) for f in full]
    g_wide = [sum(jnp.dot(part, ones_bd, preferred_element_type=F32)
                  for part in _split3(jnp.where(eye_tiled, jnp.exp(cl), 0.0))) for cl in cum_last]
    return o, [n0[i] * g_wide[i] + delta[i] for i in seqs]


def _rwkv_prompt_kernel(p_ref, rg_ref, mu_ref, w0_ref, wd_ref, a0_ref, wa_ref, kk_ref, ka_ref, rk_ref,
                        lnxg_ref, lnxb_ref, o_ref, sfin_ref, state_ref, last_ref):
    nb, C, _ = p_ref.shape
    c_idx = pl.program_id(1)

    @pl.when(c_idx == 0)
    def _():
        state_ref[...] = jnp.zeros(state_ref.shape, F32)
        last_ref[...] = jnp.zeros(last_ref.shape, F32)

    p = p_ref[...].reshape(nb * C, SHIFT_W)
    row = lax.broadcasted_iota(jnp.int32, (nb * C, 1), 0)
    prev = pltpu.roll(p, 1, 0)
    for i in range(nb):
        prev = jnp.where(row == i * C, last_ref[i, 0:1, :], prev)
    vals = _rwkv_prep_values(p, prev, mu_ref, w0_ref, wd_ref, a0_ref, wa_ref, kk_ref, ka_ref, rk_ref)
    bonus = vals[6]

    lane = lax.broadcasted_iota(jnp.int32, (C, RWKV_W), 1)
    t_idx = lax.broadcasted_iota(jnp.int32, (C, RWKV_W), 0)
    s_idx = lane & (RWKV_HEAD - 1)
    row_big = lax.broadcasted_iota(jnp.int32, (RWKV_W, RWKV_W), 0)
    lane_big = lax.broadcasted_iota(jnp.int32, (RWKV_W, RWKV_W), 1)
    bd_mask = _group_index(row_big, RWKV_HEAD) == _group_index(lane_big, RWKV_HEAD)
    ones_bd = jnp.where(bd_mask, 1.0, 0.0).astype(BF16)
    tri_r = lax.broadcasted_iota(jnp.int32, (C, C), 0)
    tri_c = lax.broadcasted_iota(jnp.int32, (C, C), 1)
    masks = (bd_mask, jnp.where(tri_r >= tri_c, 1.0, 0.0).astype(BF16), t_idx > s_idx, t_idx >= s_idx,
             _group_index(t_idx, RW_SUB) == _group_index(s_idx, RW_SUB), t_idx == s_idx, ones_bd)

    per_seq = [tuple(x[i * C:(i + 1) * C] for x in vals[:6]) for i in range(nb)]
    outs, n_new = _rwkv_chunks(per_seq, [state_ref[i] for i in range(nb)], masks)
    for i in range(nb):
        state_ref[i] = n_new[i]
        last_ref[i, 0:1, :] = p[(i + 1) * C - 1:(i + 1) * C, :]
    o = jnp.concatenate(outs, axis=0)

    c = o - _head_sum(o, ones_bd) * (1.0 / RWKV_HEAD)
    var = _head_sum(c * c, ones_bd) * (1.0 / RWKV_HEAD)
    on = c * lax.rsqrt(var + GN_EPS) * lnxg_ref[...] + lnxb_ref[...]
    o_ref[...] = ((on + bonus) * rg_ref[...].reshape(nb * C, RWKV_W)).astype(BF16).reshape(nb, C, RWKV_W)

    @pl.when(c_idx == pl.num_programs(1) - 1)
    def _():
        sfin_ref[...] = state_ref[...]


def _rwkv_prompt(rw_in, rg, lw, batch, seq, nb):
    const = lambda b, c: (0, 0)
    blk = lambda b, c: (b, c, 0)
    o, s_fin = pl.pallas_call(
        _rwkv_prompt_kernel,
        grid=(batch // nb, seq // RW_CHUNK),
        in_specs=[pl.BlockSpec((nb, RW_CHUNK, SHIFT_W), blk), pl.BlockSpec((nb, RW_CHUNK, RWKV_W), blk)]
        + _rwkv_param_specs(const) + [pl.BlockSpec((1, RWKV_W), const)] * 2,
        out_specs=[pl.BlockSpec((nb, RW_CHUNK, RWKV_W), blk),
                   pl.BlockSpec((nb, RWKV_HEAD, RWKV_W), lambda b, c: (b, 0, 0))],
        out_shape=[jax.ShapeDtypeStruct((batch, seq, RWKV_W), BF16),
                   jax.ShapeDtypeStruct((batch, RWKV_HEAD, RWKV_W), F32)],
        scratch_shapes=[pltpu.VMEM((nb, RWKV_HEAD, RWKV_W), F32), pltpu.VMEM((nb, SUBLANES, SHIFT_W), F32)],
        compiler_params=_cparams("arbitrary", "arbitrary"),
        name="rwkv_prompt",
    )(rw_in.reshape(batch, seq, SHIFT_W), rg.reshape(batch, seq, RWKV_W), *_rwkv_params(lw),
      lw["lnx_g"], lw["lnx_b"])
    wkv = jnp.transpose(s_fin.reshape(batch, RWKV_HEAD, RWKV_HEADS, RWKV_HEAD), (0, 2, 3, 1))
    return o.reshape(batch * seq, RWKV_W), wkv


V_HALF = RWKV_HEAD // 2


def _scan_kernel(x_ref, v_ref, s0_ref, o_ref, sfin_ref, s_ref):
    tb = pl.program_id(1)

    @pl.when(tb == 0)
    def _():
        s_ref[...] = s0_ref[...]

    def step(t, carry):
        kk = x_ref[t, 0]
        b = x_ref[t, 1]
        d = x_ref[t, 2]
        kx = x_ref[t, 3]
        r = x_ref[t, 4]
        for vp in range(V_HALF):
            sv = s_ref[vp]
            sa = jnp.sum(sv * kk, axis=0, keepdims=True)
            sn = sv * d - sa * b + v_ref[t, pl.ds(vp, 1), :] * kx
            s_ref[vp] = sn
            o_ref[t, pl.ds(vp, 1), :] = jnp.sum(sn * r, axis=0, keepdims=True)
        return carry

    lax.fori_loop(0, x_ref.shape[0], step, 0)

    @pl.when(tb == pl.num_programs(1) - 1)
    def _():
        sfin_ref[...] = s_ref[...]


def _scan(x5, v_t, s0, t_blk):
    steps, _, _, lanes = x5.shape
    lane_tiles = lanes // LANES
    return pl.pallas_call(
        _scan_kernel,
        grid=(lane_tiles, steps // t_blk),
        in_specs=[
            pl.BlockSpec((t_blk, 5, RWKV_HEAD, LANES), lambda j, t: (t, 0, 0, j)),
            pl.BlockSpec((t_blk, V_HALF, LANES), lambda j, t: (t, 0, j)),
            pl.BlockSpec((V_HALF, RWKV_HEAD, LANES), lambda j, t: (0, 0, j)),
        ],
        out_specs=[
            pl.BlockSpec((t_blk, V_HALF, LANES), lambda j, t: (t, 0, j)),
            pl.BlockSpec((V_HALF, RWKV_HEAD, LANES), lambda j, t: (0, 0, j)),
        ],
        out_shape=[
            jax.ShapeDtypeStruct((steps, V_HALF, lanes), F32),
            jax.ShapeDtypeStruct((V_HALF, RWKV_HEAD, lanes), F32),
        ],
        scratch_shapes=[pltpu.VMEM((V_HALF, RWKV_HEAD, LANES), F32)],
        compiler_params=_cparams("arbitrary", "arbitrary"),
        name="rwkv_scan",
    )(x5, v_t, s0)


def _rwkv_mix(prep, state, batch, seq, t_blk):
    r, k_mod, v, kk, b, d, _ = prep
    H, N = RWKV_HEADS, RWKV_HEAD
    x5 = jnp.stack([kk, b, d, k_mod, r]).reshape(5, batch, seq, H, N)
    x5 = jnp.transpose(x5, (2, 0, 4, 1, 3)).reshape(seq, 5, N, 1, batch * H)
    x5 = jnp.broadcast_to(x5, (seq, 5, N, 2, batch * H)).reshape(seq, 5, N, 2 * batch * H)
    v_t = jnp.transpose(v.reshape(batch, seq, H, 2, V_HALF), (1, 4, 3, 0, 2)).reshape(seq, V_HALF, 2 * batch * H)
    s0 = jnp.transpose(state.reshape(batch, H, 2, V_HALF, N), (3, 4, 2, 0, 1)).reshape(V_HALF, N, 2 * batch * H)
    o_t, s_fin = _scan(x5, v_t, s0, t_blk)
    o = jnp.transpose(o_t.reshape(seq, V_HALF, 2, batch, H), (3, 0, 4, 2, 1)).reshape(batch * seq, H * N)
    s_new = jnp.transpose(s_fin.reshape(V_HALF, N, 2, batch, H), (3, 4, 2, 0, 1)).reshape(batch, H, N, N)
    return o, s_new


def _mix_out(mixed, x_ref, mod_ref, wout_ref, gpost_ref, y_ref):
    out = jnp.dot(mixed, wout_ref[...], preferred_element_type=F32)
    gate = _rows2d(mod_ref)[:, 2 * D_MODEL:]
    y_ref[...] = x_ref[...] + gate * _rms(out, gpost_ref[...])


def _out_prompt_kernel(om_ref, orw_ref, gm_ref, x_ref, mod_ref, wout_ref, gpost_ref, y_ref):
    mixed = jnp.concatenate([om_ref[...], orw_ref[...], gm_ref[...]], axis=-1)
    _mix_out(mixed, x_ref, mod_ref, wout_ref, gpost_ref, y_ref)


def _out_sample_kernel(lat_ref, mg_ref, orw_ref, bonus_ref, rg_ref, gm_ref, x_ref, mod_ref, wout_ref, gpost_ref,
                       lnxg_ref, lnxb_ref, wuv_ref, y_ref):
    lat = lat_ref[...]
    o_mla = jnp.concatenate(
        [jnp.dot(lat[:, h * KV_LORA:(h + 1) * KV_LORA].astype(BF16), wuv_ref[h],
                 preferred_element_type=F32) for h in range(MLA_HEADS)], axis=-1)
    o_mla = (o_mla * mg_ref[...].astype(F32)).astype(BF16)
    ones_bd = _head_ones(RWKV_W, RWKV_HEAD)
    o = orw_ref[...]
    c = o - _head_sum(o, ones_bd) * (1.0 / RWKV_HEAD)
    var = _head_sum(c * c, ones_bd) * (1.0 / RWKV_HEAD)
    on = c * lax.rsqrt(var + GN_EPS) * lnxg_ref[...] + lnxb_ref[...]
    o_rw = ((on + bonus_ref[...]) * rg_ref[...]).astype(BF16)
    mixed = jnp.concatenate([o_mla, o_rw, gm_ref[...]], axis=-1)
    _mix_out(mixed, x_ref, mod_ref, wout_ref, gpost_ref, y_ref)


def _out_prompt(o_mla, o_rw, gm, x2d, mod_b, lw, seq, tm):
    rows = x2d.shape[0]
    tiles_per_seq = seq // tm
    const = lambda i: (0, 0)
    row_blk = lambda i: (i, 0)
    return pl.pallas_call(
        _out_prompt_kernel,
        grid=(rows // tm,),
        in_specs=[
            pl.BlockSpec((tm, MLA_W), row_blk),
            pl.BlockSpec((tm, RWKV_W), row_blk),
            pl.BlockSpec((tm, GM_W), row_blk),
            pl.BlockSpec((tm, D_MODEL), row_blk),
            pl.BlockSpec((1, 1, 3 * D_MODEL), lambda i: (i // tiles_per_seq, 0, 0)),
            pl.BlockSpec((D_MODEL, D_MODEL), const),
            pl.BlockSpec((1, D_MODEL), const),
        ],
        out_specs=pl.BlockSpec((tm, D_MODEL), row_blk),
        out_shape=jax.ShapeDtypeStruct((rows, D_MODEL), F32),
        compiler_params=_cparams("arbitrary"),
        name="out_prompt",
    )(o_mla, o_rw, gm, x2d, mod_b, lw["w_out"], lw["g_post"])


def _out_sample(o_lat, mg, o_rw, bonus, rg, gm, x2d, mod_rows, lw):
    rows = x2d.shape[0]
    const = lambda i: (0, 0)
    full = lambda w: pl.BlockSpec((rows, w), const)
    return pl.pallas_call(
        _out_sample_kernel,
        grid=(1,),
        in_specs=[
            full(MLA_HEADS * KV_LORA), full(MLA_W), full(RWKV_W), full(RWKV_W), full(RWKV_W), full(GM_W),
            full(D_MODEL), full(3 * D_MODEL),
            pl.BlockSpec((D_MODEL, D_MODEL), const),
            pl.BlockSpec((1, D_MODEL), const),
            pl.BlockSpec((1, RWKV_W), const),
            pl.BlockSpec((1, RWKV_W), const),
            pl.BlockSpec((MLA_HEADS, KV_LORA, MLA_V), lambda i: (0, 0, 0)),
        ],
        out_specs=full(D_MODEL),
        out_shape=jax.ShapeDtypeStruct((rows, D_MODEL), F32),
        compiler_params=_cparams("arbitrary"),
        name="out_sample",
    )(o_lat, mg, o_rw, bonus, rg, gm, x2d, mod_rows, lw["w_out"], lw["g_post"], lw["lnx_g"], lw["lnx_b"],
      lw["w_uv_h"])


def _swap_halves(w):
    half = MLA_ROPE // 2
    return jnp.concatenate([w[..., half:], w[..., :half]], axis=-1)


def _rope_tables(pos):
    half = MLA_ROPE // 2
    inv = ROPE_THETA ** (-jnp.arange(half, dtype=F32) / half)
    ang = pos.astype(F32)[:, None] * inv[None, :]
    cos, sin = jnp.cos(ang), jnp.sin(ang)
    zeros = jnp.zeros((pos.shape[0], LANES - MLA_ROPE), F32)
    return (jnp.concatenate([cos, cos, zeros], axis=-1), jnp.concatenate([-sin, sin, zeros], axis=-1))


def _layer_weights(l, w_in, norm_pre_g, norm_post_g, q_norm_g, kv_norm_g, w_uq, w_uk, w_uv, rw_mu, rw_w0,
                   rw_w_decay_up, rw_a0, rw_w_a_up, rw_k_k, rw_k_a, rw_r_k, rw_lnx_g, rw_lnx_b, gm_ln_g,
                   gm_ln_b, gm_w_s, gm_b_s, w_out):
    w = w_in[l]
    split_lo, split_hi = Q_LORA + KV_LORA, Q_LORA + KV_LORA + MLA_ROPE
    kr = w[:, split_lo:split_hi]
    w_in_p = jnp.concatenate([w[:, :Q_LORA], kr, _swap_halves(kr), w[:, Q_LORA:split_lo], w[:, split_hi:]],
                             axis=1).astype(BF16)
    uq = w_uq[l]
    pe = uq[..., MLA_NOPE:]
    w_uq_p = jnp.concatenate([uq, _swap_halves(pe)], axis=-1).reshape(Q_LORA, MLA_HEADS * QHEAD_W).astype(BF16)
    zeros_lora = jnp.zeros((LANES - DECAY_LORA, RWKV_W), F32)
    row = lambda a: a.reshape(1, -1)
    return {
        "w_in": w_in_p, "g_pre": row(norm_pre_g[l]), "g_post": row(norm_post_g[l]),
        "g_q": row(q_norm_g[l]), "g_kv": row(kv_norm_g[l]), "w_uq": w_uq_p,
        "w_uk": w_uk[l].reshape(KV_LORA, MLA_W).astype(BF16),
        "w_uv": w_uv[l].reshape(KV_LORA, MLA_W).astype(BF16),
        "w_ukt": jnp.transpose(w_uk[l], (1, 2, 0)).astype(BF16),
        "w_uv_h": jnp.transpose(w_uv[l], (1, 0, 2)).astype(BF16),
        "rw_mu": row(rw_mu[l]), "rw_w0": row(rw_w0[l]), "rw_a0": row(rw_a0[l]),
        "rw_wd": jnp.concatenate([rw_w_decay_up[l], zeros_lora], axis=0).astype(BF16),
        "rw_wa": jnp.concatenate([zeros_lora, rw_w_a_up[l]], axis=0).astype(BF16),
        "rw_k_k": row(rw_k_k[l]), "rw_k_a": row(rw_k_a[l]), "rw_r_k": row(rw_r_k[l]),
        "lnx_g": row(rw_lnx_g[l]), "lnx_b": row(rw_lnx_b[l]),
        "ln_g": row(gm_ln_g[l]), "ln_b": row(gm_ln_b[l]),
        "gm_ws": gm_w_s[l],
        "gm_bias": jnp.repeat(gm_b_s[l].T, GM_GROUP_W, axis=1),
        "gm_coef0": row(jnp.repeat(gm_w_s[l][:, 0, 0], GM_GROUP_W)),
        "gm_bias0": row(jnp.repeat(gm_b_s[l][:, 0], GM_GROUP_W)),
        "w_out": w_out[l].astype(BF16),
    }


def _pick(full, want):
    return want if full % want == 0 else full


def kernel(x_prompt, x_sample, c_prompt, c_sample, cache_ckv, cache_kpe, page_table, state_wkv, state_shift, w_ada, b_ada, norm_pre_g, norm_post_g, w_in, q_norm_g, kv_norm_g, w_uq, w_uk, w_uv, rw_mu, rw_w0, rw_w_decay_up, rw_a0, rw_w_a_up, rw_k_k, rw_k_a, rw_r_k, rw_lnx_g, rw_lnx_b, gm_ln_g, gm_ln_b, gm_w_s, gm_b_s, w_out):
    batch, seq, _ = x_prompt.shape
    dec_batch, dec_seq, _ = x_sample.shape
    depth = w_in.shape[0]
    n_pages = page_table.shape[1]
    past_len = n_pages * cache_ckv.shape[2]
    assert dec_seq == 1 and seq % CHUNK == 0

    assert seq % RW_CHUNK == 0
    tm_proj = _pick(seq, 512)
    tm_out = _pick(seq, 1024)
    tq = _pick(seq, 512)
    rw_nb = _pick(batch, 8)

    mod = _ada_mod(jnp.concatenate([c_prompt, c_sample], axis=0), w_ada, b_ada)
    cs_p, sn_p = _rope_tables(jnp.arange(seq))
    cs_s, sn_s = _rope_tables(jnp.full((dec_batch,), past_len))
    cache_kpe_t = jnp.swapaxes(cache_kpe, 2, 3)

    y_p = x_prompt.reshape(batch * seq, D_MODEL)
    y_s = x_sample.reshape(dec_batch, D_MODEL)
    outs = {k: [] for k in ("ckv_p", "kpe_p", "ckv_s", "kpe_s", "wkv_p", "wkv_s", "sh_p", "sh_s", "vc_p", "vc_s")}
    for l in range(depth):
        lw = _layer_weights(l, w_in, norm_pre_g, norm_post_g, q_norm_g, kv_norm_g, w_uq, w_uk, w_uv, rw_mu,
                            rw_w0, rw_w_decay_up, rw_a0, rw_w_a_up, rw_k_k, rw_k_a, rw_r_k, rw_lnx_g,
                            rw_lnx_b, gm_ln_g, gm_ln_b, gm_w_s, gm_b_s, w_out)
        mod_p = mod[l, :batch].reshape(batch, 1, 3 * D_MODEL)
        q, k, v, mg, ckv, kpe, rw_in, rg, gm, vn_last = _proj_prompt(y_p, mod_p, lw, cs_p, sn_p, batch, seq,
                                                                     tm_proj)
        o_mla = _flash(q, k, v, mg, batch, seq, tq)
        o_rw, wkv_new = _rwkv_prompt(rw_in, rg, lw, batch, seq, rw_nb)
        y_p = _out_prompt(o_mla, o_rw, gm, y_p, mod_p, lw, seq, tm_out)
        outs["ckv_p"].append(ckv.reshape(batch, seq, KV_LORA))
        outs["kpe_p"].append(kpe.reshape(batch, seq, MLA_ROPE))
        outs["wkv_p"].append(wkv_new)
        outs["sh_p"].append(rw_in.reshape(batch, seq, SHIFT_W)[:, -1])
        outs["vc_p"].append(vn_last)

        mod_s = mod[l, batch:]
        qa, qp, mg_s, ckv_s, kpe_s, rw_s, rg_s, gm_s, vn_s = _proj_sample(y_s, mod_s, lw, cs_s, sn_s)
        o_lat = _paged_attention(qa, qp, ckv_s, kpe_s, cache_ckv, cache_kpe_t, page_table, l)
        prep_s = _rwkv_prep_sample(rw_s, lw, state_shift[l])
        o_rw_s, wkv_s = _rwkv_mix(prep_s, state_wkv[l], dec_batch, 1, 1)
        y_s = _out_sample(o_lat, mg_s, o_rw_s, prep_s[6], rg_s, gm_s, y_s, mod_s, lw)
        outs["ckv_s"].append(ckv_s.reshape(dec_batch, 1, KV_LORA))
        outs["kpe_s"].append(kpe_s[:, :MLA_ROPE].reshape(dec_batch, 1, MLA_ROPE))
        outs["wkv_s"].append(wkv_s)
        outs["sh_s"].append(rw_s)
        outs["vc_s"].append(vn_s.reshape(dec_batch, 1, GM_W))

    st = lambda name: jnp.stack(outs[name])
    return (y_p.reshape(batch, seq, D_MODEL), y_s.reshape(dec_batch, 1, D_MODEL),
            st("ckv_p"), st("kpe_p"), st("ckv_s"), st("kpe_s"), st("wkv_p"), st("wkv_s"),
            st("sh_p"), st("sh_s"), st("vc_p"), st("vc_s"))
```

```python
import functools

import numpy as np
import jax
import jax.numpy as jnp
from jax import lax
from jax.experimental import pallas as pl
from jax.experimental.pallas import tpu as pltpu

F32 = jnp.float32
BF16 = jnp.bfloat16

D_MODEL = 1024
MLA_V = 128
MLA_W = D_MODEL // 2
MLA_HEADS = MLA_W // MLA_V
MLA_NOPE = 128
MLA_ROPE = 64
MLA_QK = MLA_NOPE + MLA_ROPE
Q_LORA = (3 * D_MODEL) // 8
KV_LORA = D_MODEL // 4
ROPE_THETA = 10000.0
ATTN_SCALE = MLA_QK ** -0.5
Q_SCALE = ATTN_SCALE * float(np.log2(np.e))
RWKV_W = D_MODEL // 4
RWKV_HEAD = 64
RWKV_HEADS = RWKV_W // RWKV_HEAD
DECAY_LORA = 64
ICLR_LORA = 64
SHIFT_W = 3 * RWKV_W + DECAY_LORA + ICLR_LORA
GN_EPS = 64e-5
GM_W = D_MODEL // 4
GM_GROUPS = 4
GM_GROUP_W = GM_W // GM_GROUPS
CHUNK = 128
RMS_EPS = 1e-6
LN_EPS = 1e-5

LANES = 128
SUBLANES = 8
VMEM_LIMIT_BYTES = 56 * 1024 * 1024

QHEAD_W = 2 * LANES
OFF_Q = 0
OFF_KR = OFF_Q + Q_LORA
OFF_KV = OFF_KR + LANES
OFF_MG = OFF_KV + KV_LORA
OFF_RW = OFF_MG + MLA_W
OFF_RG = OFF_RW + SHIFT_W
OFF_GU = OFF_RG + RWKV_W
OFF_GV = OFF_GU + GM_W
OFF_GG = OFF_GV + GM_W
IN_W_P = OFF_GG + GM_W

NEG_BIG = -1e30


def _cparams(*sem):
    return pltpu.CompilerParams(dimension_semantics=sem, vmem_limit_bytes=VMEM_LIMIT_BYTES)


def _silu(x):
    return x * jax.nn.sigmoid(x)


def _rms(x, g, eps=RMS_EPS):
    return x * lax.rsqrt(jnp.mean(x * x, axis=-1, keepdims=True) + eps) * g


def _rows2d(ref):
    m = ref[...]
    return m.reshape(m.shape[-2], m.shape[-1])


def _group_index(idx, group):
    shift = group.bit_length() - 1
    assert 1 << shift == group
    return lax.shift_right_logical(idx, shift)


def _head_ones(width, head):
    r = _group_index(lax.broadcasted_iota(jnp.int32, (width, width), 0), head)
    c = _group_index(lax.broadcasted_iota(jnp.int32, (width, width), 1), head)
    return jnp.where(r == c, 1.0, 0.0).astype(BF16)


def _head_sum(x, ones_bd):
    hi = x.astype(BF16)
    lo = (x - hi.astype(F32)).astype(BF16)
    return (jnp.dot(hi, ones_bd, preferred_element_type=F32)
            + jnp.dot(lo, ones_bd, preferred_element_type=F32))


def _rope_pair(blk, cs, sn):
    return blk * cs + pltpu.roll(blk, MLA_ROPE, 1) * sn


def _ada_kernel(c_ref, w_ref, b_ref, o_ref):
    c = _silu(c_ref[...]).astype(BF16)
    o_ref[0] = jnp.dot(c, w_ref[0].astype(BF16), preferred_element_type=F32) + b_ref[0]


def _ada_mod(c_all, w_ada, b_ada):
    depth = w_ada.shape[0]
    n_rows = c_all.shape[0]
    n_tiles = (3 * D_MODEL) // D_MODEL
    return pl.pallas_call(
        _ada_kernel,
        grid=(depth, n_tiles),
        in_specs=[
            pl.BlockSpec((n_rows, D_MODEL), lambda l, j: (0, 0)),
            pl.BlockSpec((1, D_MODEL, D_MODEL), lambda l, j: (l, 0, j)),
            pl.BlockSpec((1, 1, D_MODEL), lambda l, j: (l, 0, j)),
        ],
        out_specs=pl.BlockSpec((1, n_rows, D_MODEL), lambda l, j: (l, 0, j)),
        out_shape=jax.ShapeDtypeStruct((depth, n_rows, 3 * D_MODEL), F32),
        compiler_params=_cparams("arbitrary", "arbitrary"),
        name="ada_mod",
    )(c_all, w_ada, b_ada.reshape(depth, 1, 3 * D_MODEL))


def _proj_common(x_ref, mod_ref, gpre_ref, win_ref):
    mod = _rows2d(mod_ref)
    shift = mod[:, 0:D_MODEL]
    scale = mod[:, D_MODEL:2 * D_MODEL]
    h = _rms(x_ref[...], gpre_ref[...]) * (1.0 + scale) + shift
    hb = h.astype(BF16)

    def seg(off, width):
        return jnp.dot(hb, win_ref[:, off:off + width], preferred_element_type=F32)

    return seg


def _q_heads(seg, gq_ref, wuq_ref, cs, sn):
    q_kr = seg(OFF_Q, Q_LORA + LANES)
    qn = _rms(q_kr[:, :Q_LORA], gq_ref[...]).astype(BF16)
    q_all = jnp.dot(qn, wuq_ref[...], preferred_element_type=F32)
    heads = []
    for h in range(MLA_HEADS):
        nope = q_all[:, h * QHEAD_W:h * QHEAD_W + LANES] * Q_SCALE
        pe = _rope_pair(q_all[:, h * QHEAD_W + LANES:(h + 1) * QHEAD_W], cs, sn) * Q_SCALE
        heads.append((nope, pe))
    return heads, _rope_pair(q_kr[:, Q_LORA:], cs, sn)


def _gm_norm(seg, lng_ref, lnb_ref):
    v = seg(OFF_GV, GM_W)
    mu = jnp.mean(v, axis=-1, keepdims=True)
    c = v - mu
    var = jnp.mean(c * c, axis=-1, keepdims=True)
    return c * lax.rsqrt(var + LN_EPS) * lng_ref[...] + lnb_ref[...]


def _proj_prompt_kernel(x_ref, mod_ref, gpre_ref, win_ref, gq_ref, wuq_ref, gkv_ref, wuk_ref, wuv_ref,
                        cs_ref, sn_ref, lng_ref, lnb_ref, ws_ref, bs_ref, *rest, tiles_per_seq, n_prev):
    if n_prev:
        ckv_prev_ref, kpe_prev_ref = rest[:2]
        rest = rest[2:]
    q_ref, k_ref, v_ref, mg_ref, ckv_ref, kpe_ref, rw_ref, rg_ref, gm_ref, vn_ref = rest
    for j in range(n_prev):
        ckv_ref[j] = ckv_prev_ref[j]
        kpe_ref[j] = kpe_prev_ref[j]
    seg = _proj_common(x_ref, mod_ref, gpre_ref, win_ref)
    cs = cs_ref[...]
    sn = sn_ref[...]
    tm = x_ref.shape[0]

    heads, kpe = _q_heads(seg, gq_ref, wuq_ref, cs, sn)
    for h, (nope, pe) in enumerate(heads):
        q_ref[:, h * QHEAD_W:h * QHEAD_W + LANES] = nope.astype(BF16)
        q_ref[:, h * QHEAD_W + LANES:(h + 1) * QHEAD_W] = pe.astype(BF16)
    ckv = _rms(seg(OFF_KV, KV_LORA), gkv_ref[...])
    ckv_ref[n_prev] = ckv
    kpe_ref[n_prev] = kpe[:, :MLA_ROPE]
    ckv_b = ckv.astype(BF16)
    k_nope = jnp.dot(ckv_b, wuk_ref[...], preferred_element_type=F32)
    v_ref[...] = jnp.dot(ckv_b, wuv_ref[...], preferred_element_type=F32).astype(BF16)
    kpe_b = kpe.astype(BF16)
    for h in range(MLA_HEADS):
        k_ref[:, h * QHEAD_W:h * QHEAD_W + LANES] = k_nope[:, h * MLA_NOPE:(h + 1) * MLA_NOPE].astype(BF16)
        k_ref[:, h * QHEAD_W + LANES:(h + 1) * QHEAD_W] = kpe_b
    mg_ref[...] = _silu(seg(OFF_MG, MLA_W)).astype(BF16)

    rw_ref[...] = seg(OFF_RW, SHIFT_W)
    rg_ref[...] = _silu(seg(OFF_RG, RWKV_W))

    vn = _gm_norm(seg, lng_ref, lnb_ref)
    vn_b = vn.astype(BF16)
    gate_u = _silu(seg(OFF_GG, GM_W)) * seg(OFF_GU, GM_W)
    row = lax.broadcasted_iota(jnp.int32, (CHUNK, CHUNK), 0)
    col = lax.broadcasted_iota(jnp.int32, (CHUNK, CHUNK), 1)
    lane_group = _group_index(lax.broadcasted_iota(jnp.int32, (CHUNK, GM_W), 1), GM_GROUP_W)
    w_tril = [jnp.where(row >= col, ws_ref[g], 0.0).astype(BF16) for g in range(GM_GROUPS)]
    for c in range(tm // CHUNK):
        vc = vn_b[c * CHUNK:(c + 1) * CHUNK]
        z = bs_ref[...]
        for g in range(GM_GROUPS):
            zg = jnp.dot(w_tril[g], vc, preferred_element_type=F32)
            z = z + jnp.where(lane_group == g, zg, 0.0)
        gm_ref[c * CHUNK:(c + 1) * CHUNK, :] = (gate_u[c * CHUNK:(c + 1) * CHUNK] * z).astype(BF16)

    @pl.when(pl.program_id(0) % tiles_per_seq == tiles_per_seq - 1)
    def _():
        vn_ref[0] = vn[tm - CHUNK:, :]


def _proj_prompt(x2d, mod_b, lw, cs, sn, batch, seq, tm, prev_stacks):
    rows = x2d.shape[0]
    tiles_per_seq = seq // tm
    n_prev = 0 if prev_stacks is None else prev_stacks[0].shape[0]
    const = lambda i: (0, 0)
    row_blk = lambda i: (i, 0)
    stack_blk = lambda i: (0, i, 0)
    outs = [
        (QHEAD_W * MLA_HEADS, BF16), (QHEAD_W * MLA_HEADS, BF16), (MLA_W, BF16), (MLA_W, BF16),
        (KV_LORA, F32), (MLA_ROPE, F32), (SHIFT_W, F32), (RWKV_W, F32), (GM_W, BF16),
    ]
    out_shape = [jax.ShapeDtypeStruct((rows, w), dt) for w, dt in outs]
    out_specs = [pl.BlockSpec((tm, w), row_blk) for w, _ in outs]
    for idx in (4, 5):
        w, dt = outs[idx]
        out_shape[idx] = jax.ShapeDtypeStruct((n_prev + 1, rows, w), dt)
        out_specs[idx] = pl.BlockSpec((n_prev + 1, tm, w), stack_blk)
    out_shape.append(jax.ShapeDtypeStruct((batch, CHUNK, GM_W), F32))
    out_specs.append(pl.BlockSpec((1, CHUNK, GM_W), lambda i: (i // tiles_per_seq, 0, 0)))
    prev_specs = [] if not n_prev else [pl.BlockSpec((n_prev, tm, KV_LORA), stack_blk),
                                        pl.BlockSpec((n_prev, tm, MLA_ROPE), stack_blk)]
    return pl.pallas_call(
        functools.partial(_proj_prompt_kernel, tiles_per_seq=tiles_per_seq, n_prev=n_prev),
        grid=(rows // tm,),
        in_specs=[
            pl.BlockSpec((tm, D_MODEL), row_blk),
            pl.BlockSpec((1, 1, 3 * D_MODEL), lambda i: (i // tiles_per_seq, 0, 0)),
            pl.BlockSpec((1, D_MODEL), const),
            pl.BlockSpec((D_MODEL, IN_W_P), const),
            pl.BlockSpec((1, Q_LORA), const),
            pl.BlockSpec((Q_LORA, QHEAD_W * MLA_HEADS), const),
            pl.BlockSpec((1, KV_LORA), const),
            pl.BlockSpec((KV_LORA, MLA_W), const),
            pl.BlockSpec((KV_LORA, MLA_W), const),
            pl.BlockSpec((tm, LANES), lambda i: (i % tiles_per_seq, 0)),
            pl.BlockSpec((tm, LANES), lambda i: (i % tiles_per_seq, 0)),
            pl.BlockSpec((1, GM_W), const),
            pl.BlockSpec((1, GM_W), const),
            pl.BlockSpec((GM_GROUPS, CHUNK, CHUNK), lambda i: (0, 0, 0)),
            pl.BlockSpec((CHUNK, GM_W), const),
        ] + prev_specs,
        out_specs=out_specs,
        out_shape=out_shape,
        compiler_params=_cparams("arbitrary"),
        name="proj_prompt",
    )(x2d, mod_b, lw["g_pre"], lw["w_in"], lw["g_q"], lw["w_uq"], lw["g_kv"], lw["w_uk"], lw["w_uv"],
      cs, sn, lw["ln_g"], lw["ln_b"], lw["gm_ws"], lw["gm_bias"], *(prev_stacks or ()))


def _proj_sample_kernel(x_ref, mod_ref, gpre_ref, win_ref, gq_ref, wuq_ref, gkv_ref, wukt_ref,
                        cs_ref, sn_ref, lng_ref, lnb_ref, coef_ref, bias_ref,
                        qa_ref, qp_ref, mg_ref, ckv_ref, kpe_ref, rw_ref, rg_ref, gm_ref, vn_ref):
    seg = _proj_common(x_ref, mod_ref, gpre_ref, win_ref)
    cs = cs_ref[...]
    sn = sn_ref[...]
    heads, kpe = _q_heads(seg, gq_ref, wuq_ref, cs, sn)
    for h, (nope, pe) in enumerate(heads):
        qa_ref[:, h * KV_LORA:(h + 1) * KV_LORA] = jnp.dot(nope.astype(BF16), wukt_ref[h],
                                                           preferred_element_type=F32)
        qp_ref[:, h * LANES:(h + 1) * LANES] = pe
    ckv_ref[...] = _rms(seg(OFF_KV, KV_LORA), gkv_ref[...])
    kpe_ref[...] = kpe
    mg_ref[...] = _silu(seg(OFF_MG, MLA_W)).astype(BF16)
    rw_ref[...] = seg(OFF_RW, SHIFT_W)
    rg_ref[...] = _silu(seg(OFF_RG, RWKV_W))
    vn = _gm_norm(seg, lng_ref, lnb_ref)
    vn_ref[...] = vn
    z = vn * coef_ref[...] + bias_ref[...]
    gm_ref[...] = (_silu(seg(OFF_GG, GM_W)) * seg(OFF_GU, GM_W) * z).astype(BF16)


def _proj_sample(x2d, mod_rows, lw, cs, sn):
    rows = x2d.shape[0]
    const = lambda i: (0, 0)
    outs = [
        (KV_LORA * MLA_HEADS, F32), (LANES * MLA_HEADS, F32), (MLA_W, BF16), (KV_LORA, F32), (LANES, F32),
        (SHIFT_W, F32), (RWKV_W, F32), (GM_W, BF16), (GM_W, F32),
    ]
    return pl.pallas_call(
        _proj_sample_kernel,
        grid=(1,),
        in_specs=[
            pl.BlockSpec((rows, D_MODEL), const),
            pl.BlockSpec((rows, 3 * D_MODEL), const),
            pl.BlockSpec((1, D_MODEL), const),
            pl.BlockSpec((D_MODEL, IN_W_P), const),
            pl.BlockSpec((1, Q_LORA), const),
            pl.BlockSpec((Q_LORA, QHEAD_W * MLA_HEADS), const),
            pl.BlockSpec((1, KV_LORA), const),
            pl.BlockSpec((MLA_HEADS, MLA_NOPE, KV_LORA), lambda i: (0, 0, 0)),
            pl.BlockSpec((rows, LANES), const),
            pl.BlockSpec((rows, LANES), const),
            pl.BlockSpec((1, GM_W), const),
            pl.BlockSpec((1, GM_W), const),
            pl.BlockSpec((1, GM_W), const),
            pl.BlockSpec((1, GM_W), const),
        ],
        out_specs=[pl.BlockSpec((rows, w), const) for w, _ in outs],
        out_shape=[jax.ShapeDtypeStruct((rows, w), dt) for w, dt in outs],
        compiler_params=_cparams("arbitrary"),
        name="proj_sample",
    )(x2d, mod_rows, lw["g_pre"], lw["w_in"], lw["g_q"], lw["w_uq"], lw["g_kv"], lw["w_ukt"],
      cs, sn, lw["ln_g"], lw["ln_b"], lw["gm_coef0"], lw["gm_bias0"])


def _flash_kernel(q_ref, k_ref, v_ref, g_ref, o_ref, *, tq):
    seq = q_ref.shape[1]
    row = lax.broadcasted_iota(jnp.int32, (tq, tq), 0)
    col = lax.broadcasted_iota(jnp.int32, (tq, tq), 1)
    for qi in range(seq // tq):
        rows = slice(qi * tq, (qi + 1) * tq)
        q = q_ref[0, rows, :]
        m = jnp.full((tq, 1), NEG_BIG, F32)
        l = jnp.zeros((tq, 1), F32)
        acc = jnp.zeros((tq, MLA_V), F32)
        for ki in range(qi + 1):
            cols = slice(ki * tq, (ki + 1) * tq)
            s = lax.dot_general(q, k_ref[0, cols, :], (((1,), (1,)), ((), ())),
                                preferred_element_type=F32)
            if ki == qi:
                s = jnp.where(row >= col, s, NEG_BIG)
            m_new = jnp.maximum(m, jnp.max(s, axis=-1, keepdims=True))
            alpha = jnp.exp2(m - m_new)
            p = jnp.exp2(s - m_new)
            l = alpha * l + jnp.sum(p, axis=-1, keepdims=True)
            acc = alpha * acc + jnp.dot(p.astype(BF16), v_ref[0, cols, :], preferred_element_type=F32)
            m = m_new
        o_ref[0, rows, :] = (acc / l * g_ref[0, rows, :].astype(F32)).astype(BF16)


def _flash(q, k, v, g, batch, seq, tq):
    q3 = q.reshape(batch, seq, QHEAD_W * MLA_HEADS)
    k3 = k.reshape(batch, seq, QHEAD_W * MLA_HEADS)
    v3 = v.reshape(batch, seq, MLA_W)
    g3 = g.reshape(batch, seq, MLA_W)
    head_blk = lambda b, h: (b, 0, h)
    out = pl.pallas_call(
        functools.partial(_flash_kernel, tq=tq),
        grid=(batch, MLA_HEADS),
        in_specs=[
            pl.BlockSpec((1, seq, QHEAD_W), head_blk),
            pl.BlockSpec((1, seq, QHEAD_W), head_blk),
            pl.BlockSpec((1, seq, MLA_V), head_blk),
            pl.BlockSpec((1, seq, MLA_V), head_blk),
        ],
        out_specs=pl.BlockSpec((1, seq, MLA_V), head_blk),
        out_shape=jax.ShapeDtypeStruct((batch, seq, MLA_W), BF16),
        compiler_params=_cparams("arbitrary", "arbitrary"),
        name="flash_prompt",
    )(q3, k3, v3, g3)
    return out.reshape(batch * seq, MLA_W)


N_SLOTS = 3
LOOKAHEAD = N_SLOTS - 1


def _paged_kernel(pt_ref, qa_ref, qp_ref, cn_ref, kn_ref, ckv_hbm, kpe_hbm, o_ref, ckv_buf, kpe_buf, sem,
                  *, layer, n_pages, page, group_tokens):
    b = pl.program_id(0)
    n_seq = pl.num_programs(0)
    slot = lax.rem(b, N_SLOTS)

    def page_copies(seq_idx, sl):
        copies = []
        for i in range(n_pages):
            pg = pt_ref[seq_idx, i]
            copies.append(pltpu.make_async_copy(
                ckv_hbm.at[layer, pg], ckv_buf.at[sl, pl.ds(i * page, page), :], sem.at[0, sl]))
            copies.append(pltpu.make_async_copy(kpe_hbm.at[layer, pg], kpe_buf.at[sl, i], sem.at[1, sl]))
        return copies

    def start_all(copies):
        for i, c in enumerate(copies):
            c.start(priority=(i // 2) % 2)

    @pl.when(b == 0)
    def _():
        for ahead in range(LOOKAHEAD):
            start_all(page_copies(ahead, ahead))

    for c in page_copies(b, slot):
        c.wait()

    nxt = lax.rem(b + LOOKAHEAD, n_seq)
    nxt_slot = lax.rem(b + LOOKAHEAD, N_SLOTS)
    start_all(page_copies(nxt, nxt_slot))

    qa = qa_ref[0]
    qp = qp_ref[0][:, :MLA_ROPE]
    qa_b = qa.astype(BF16)
    qp_b = qp.astype(BF16)
    group_pages = group_tokens // page
    groups = [slice(g * group_tokens, (g + 1) * group_tokens) for g in range(n_pages // group_pages)]
    ckv = [ckv_buf[slot, gs, :].astype(BF16) for gs in groups]
    kpe_t = [jnp.concatenate([kpe_buf[slot, g * group_pages + i] for i in range(group_pages)],
                             axis=1).astype(BF16) for g in range(len(groups))]
    s = [lax.dot_general(qa_b, ck, (((1,), (1,)), ((), ())), preferred_element_type=F32)
         + jnp.dot(qp_b, kp, preferred_element_type=F32) for ck, kp in zip(ckv, kpe_t)]
    m_g = [jnp.max(sg, axis=-1, keepdims=True) for sg in s]
    p = [jnp.exp2(sg - mg) for sg, mg in zip(s, m_g)]
    l_g = [jnp.sum(pg, axis=-1, keepdims=True) for pg in p]
    acc_g = [jnp.dot(pg.astype(BF16), ck, preferred_element_type=F32) for pg, ck in zip(p, ckv)]
    cn = cn_ref[0]
    kn = kn_ref[0][:, :MLA_ROPE]
    s_new = jnp.sum(qa * cn, axis=-1, keepdims=True) + jnp.sum(qp * kn, axis=-1, keepdims=True)
    m = functools.reduce(jnp.maximum, m_g, s_new)
    p_new = jnp.exp2(s_new - m)
    w_g = [jnp.exp2(mg - m) for mg in m_g]
    l = sum(wg * lg for wg, lg in zip(w_g, l_g)) + p_new
    o_ref[0] = (sum(wg * ag for wg, ag in zip(w_g, acc_g)) + p_new * cn) / l

    @pl.when(b == n_seq - 1)
    def _():
        for ahead in range(1, LOOKAHEAD + 1):
            for c in page_copies(lax.rem(b + ahead, n_seq), lax.rem(b + ahead, N_SLOTS)):
                c.wait()


def _paged_attention(qa, qp, ckv_new, kpe_new, cache_ckv, cache_kpe_t, page_table, layer):
    dec_batch, n_pages = page_table.shape
    page = cache_ckv.shape[2]
    head_pad = ((0, 0), (0, SUBLANES - MLA_HEADS), (0, 0))
    qa3 = jnp.pad(qa.reshape(dec_batch, MLA_HEADS, KV_LORA), head_pad)
    qp3 = jnp.pad(qp.reshape(dec_batch, MLA_HEADS, LANES), head_pad)
    cn3 = ckv_new.reshape(dec_batch, 1, KV_LORA)
    kn3 = kpe_new.reshape(dec_batch, 1, LANES)
    row_blk = lambda b, pt: (b, 0, 0)
    out = pl.pallas_call(
        functools.partial(_paged_kernel, layer=layer, n_pages=n_pages, page=page,
                          group_tokens=_pick(n_pages * page, 2048)),
        grid_spec=pltpu.PrefetchScalarGridSpec(
            num_scalar_prefetch=1,
            grid=(dec_batch,),
            in_specs=[
                pl.BlockSpec((1, SUBLANES, KV_LORA), row_blk),
                pl.BlockSpec((1, SUBLANES, LANES), row_blk),
                pl.BlockSpec((1, 1, KV_LORA), row_blk),
                pl.BlockSpec((1, 1, LANES), row_blk),
                pl.BlockSpec(memory_space=pl.ANY),
                pl.BlockSpec(memory_space=pl.ANY),
            ],
            out_specs=pl.BlockSpec((1, SUBLANES, KV_LORA), row_blk),
            scratch_shapes=[
                pltpu.VMEM((N_SLOTS, n_pages * page, KV_LORA), F32),
                pltpu.VMEM((N_SLOTS, n_pages, MLA_ROPE, page), F32),
                pltpu.SemaphoreType.DMA((2, N_SLOTS)),
            ],
        ),
        out_shape=jax.ShapeDtypeStruct((dec_batch, SUBLANES, KV_LORA), F32),
        compiler_params=_cparams("arbitrary"),
        name="paged_attention",
    )(page_table, qa3, qp3, cn3, kn3, cache_ckv, cache_kpe_t)
    return out[:, :MLA_HEADS].reshape(dec_batch, MLA_HEADS * KV_LORA)


def _rwkv_prep_values(p, prev, mu_ref, w0_ref, wd_ref, a0_ref, wa_ref, kk_ref, ka_ref, rk_ref):
    xm = p + (prev - p) * mu_ref[...]
    r = xm[:, 0:RWKV_W]
    k = xm[:, RWKV_W:2 * RWKV_W]
    v = xm[:, 2 * RWKV_W:3 * RWKV_W]
    tail = xm[:, 3 * RWKV_W:]
    lane = lax.broadcasted_iota(jnp.int32, tail.shape, 1)
    lora_in = jnp.where(lane < DECAY_LORA, jnp.tanh(tail), tail).astype(BF16)
    dw = jnp.dot(lora_in, wd_ref[...], preferred_element_type=F32)
    da = jnp.dot(lora_in, wa_ref[...], preferred_element_type=F32)
    z = -(w0_ref[...] + dw)
    softplus = jnp.maximum(z, 0.0) + jnp.log(1.0 + jnp.exp(-jnp.abs(z)))
    w = -softplus - 0.5
    log_decay = -jnp.exp(w)
    a = jax.nn.sigmoid(a0_ref[...] + da)
    ones_bd = _head_ones(RWKV_W, RWKV_HEAD)
    kk = k * kk_ref[...]
    kk = kk / jnp.maximum(jnp.sqrt(_head_sum(kk * kk, ones_bd)), 1e-12)
    k_mod = k * (1.0 + (a - 1.0) * ka_ref[...])
    bonus = _head_sum(r * k_mod * rk_ref[...], ones_bd) * v
    return r, k_mod, v, kk, kk * a, log_decay, bonus


def _rwkv_param_specs(index_map):
    widths = (SHIFT_W, RWKV_W, None, RWKV_W, None, RWKV_W, RWKV_W, RWKV_W)
    return [pl.BlockSpec((LANES, RWKV_W) if w is None else (1, w), index_map) for w in widths]


def _rwkv_params(lw):
    return (lw["rw_mu"], lw["rw_w0"], lw["rw_wd"], lw["rw_a0"], lw["rw_wa"], lw["rw_k_k"], lw["rw_k_a"],
            lw["rw_r_k"])


def _rwkv_prep_sample_kernel(p_ref, prev_ref, mu_ref, w0_ref, wd_ref, a0_ref, wa_ref, kk_ref, ka_ref, rk_ref,
                             r_o, k_o, v_o, kk_o, b_o, d_o, bonus_o):
    r, k_mod, v, kk, b, log_decay, bonus = _rwkv_prep_values(
        p_ref[...], prev_ref[...], mu_ref, w0_ref, wd_ref, a0_ref, wa_ref, kk_ref, ka_ref, rk_ref)
    r_o[...] = r
    k_o[...] = k_mod
    v_o[...] = v
    kk_o[...] = kk
    b_o[...] = b
    d_o[...] = jnp.exp(log_decay)
    bonus_o[...] = bonus


def _rwkv_prep_sample(rw_in, lw, shift_rows):
    rows = rw_in.shape[0]
    const = lambda i: (0, 0)
    return pl.pallas_call(
        _rwkv_prep_sample_kernel,
        grid=(1,),
        in_specs=[pl.BlockSpec((rows, SHIFT_W), const)] * 2 + _rwkv_param_specs(const),
        out_specs=[pl.BlockSpec((rows, RWKV_W), const)] * 7,
        out_shape=[jax.ShapeDtypeStruct((rows, RWKV_W), F32)] * 7,
        compiler_params=_cparams("arbitrary"),
        name="rwkv_prep_sample",
    )(rw_in, shift_rows, *_rwkv_params(lw))


RW_CHUNK = 64
RW_SUB = 16


def _split3(x):
    hi = x.astype(BF16)
    r1 = x - hi.astype(F32)
    mid = r1.astype(BF16)
    lo = (r1 - mid.astype(F32)).astype(BF16)
    return hi, mid, lo


def _block_diag(y, bd_mask):
    return jnp.where(bd_mask, jnp.concatenate([y] * RWKV_HEADS, axis=0), 0.0).astype(BF16)


def _mm(x, y_bd):
    return jnp.dot(x.astype(BF16), y_bd, preferred_element_type=F32)


def _rwkv_chunks(vals, n0, masks):
    bd_mask, tri_incl, strict, incl, same_sub, eye_tiled, ones_bd = masks
    C = RW_CHUNK
    seqs = range(len(vals))
    bd = lambda y: _block_diag(y, bd_mask)
    r, kx, v, kap, bb, lam = ([val[j] for val in vals] for j in range(6))
    cum = [sum(jnp.dot(tri_incl, part, preferred_element_type=F32) for part in _split3(lam[i])) for i in seqs]
    cum_last = [c[C - 1:C, :] for c in cum]
    e_neg = [jnp.exp(-c) for c in cum]
    ap = [jnp.concatenate([kap[i] * jnp.exp(cum[i] - lam[i]), r[i] * jnp.exp(cum[i])], axis=0).astype(BF16)
          for i in seqs]
    rhs = [jnp.concatenate([bd(bb[i] * e_neg[i]), bd(kx[i] * e_neg[i])], axis=0) for i in seqs]
    g = [lax.dot_general(ap[i], rhs[i], (((1,), (1,)), ((), ())), preferred_element_type=F32) for i in seqs]
    l_ab = [jnp.where(strict, gi[:C, :RWKV_W], 0.0) for gi in g]
    l_ak = [jnp.where(strict, gi[:C, RWKV_W:], 0.0) for gi in g]
    q_pb = [jnp.where(incl, gi[C:, :RWKV_W], 0.0) for gi in g]
    q_pk = [jnp.where(incl, gi[C:, RWKV_W:], 0.0) for gi in g]
    apn = [jnp.dot(ap[i], bd(n0[i]), preferred_element_type=F32) for i in seqs]
    lq_v = [_mm(jnp.concatenate([l_ak[i], q_pk[i]], axis=0), bd(v[i])) for i in seqs]
    w = [apn[i][:C] + lq_v[i][:C] for i in seqs]

    l_d = [jnp.where(same_sub, li, 0.0) for li in l_ab]
    l_off = [l_ab[i] - l_d[i] for i in seqs]
    q_inv = [-li for li in l_d]
    power = l_d
    for _ in range(RW_SUB.bit_length() - 2):
        power = [_mm(pw, bd(pw)) for pw in power]
        q_inv = [q_inv[i] + power[i] + _mm(power[i], bd(q_inv[i])) for i in seqs]
    qz = [jnp.dot(q_inv[i].astype(BF16), jnp.concatenate([bd(l_off[i]), bd(w[i])], axis=1),
                  preferred_element_type=F32) for i in seqs]
    n = [l_off[i] + qz[i][:, :RWKV_W] for i in seqs]
    w1 = [w[i] + qz[i][:, RWKV_W:] for i in seqs]
    n_sq = [_mm(ni, bd(ni)) for ni in n]
    y = [w1[i] + _mm(n_sq[i], bd(w1[i])) for i in seqs]
    u = [_mm(n[i], bd(y[i])) - y[i] for i in seqs]

    o = [apn[i][C:] + _mm(q_pb[i], bd(u[i])) + lq_v[i][C:] for i in seqs]

    e_end = [jnp.exp(cum_last[i] - cum[i]) for i in seqs]
    lhs_t = [jnp.concatenate([bb[i] * e_end[i], kx[i] * e_end[i]], axis=0).T.astype(BF16) for i in seqs]
    full = [jnp.dot(lhs_t[i], jnp.concatenate([u[i], v[i]], axis=0).astype(BF16), preferred_element_type=F32)
            for i in seqs]
    heads = [slice(h * RWKV_HEAD, (h + 1) * RWKV_HEAD) for h in range(RWKV_HEADS)]
    delta = [sum(jnp.where(bd_mask[hs], f[hs], 0.0) for hs in heads) for f in full]
    g_wide = [sum(jnp.dot(part, ones_bd, preferred_element_type=F32)
                  for part in _split3(jnp.where(eye_tiled, jnp.exp(cl), 0.0))) for cl in cum_last]
    return o, [n0[i] * g_wide[i] + delta[i] for i in seqs]


def _rwkv_prompt_kernel(p_ref, rg_ref, mu_ref, w0_ref, wd_ref, a0_ref, wa_ref, kk_ref, ka_ref, rk_ref,
                        lnxg_ref, lnxb_ref, o_ref, sfin_ref, state_ref, last_ref):
    nb, C, _ = p_ref.shape
    c_idx = pl.program_id(1)

    @pl.when(c_idx == 0)
    def _():
        state_ref[...] = jnp.zeros(state_ref.shape, F32)
        last_ref[...] = jnp.zeros(last_ref.shape, F32)

    p = p_ref[...].reshape(nb * C, SHIFT_W)
    row = lax.broadcasted_iota(jnp.int32, (nb * C, 1), 0)
    prev = pltpu.roll(p, 1, 0)
    for i in range(nb):
        prev = jnp.where(row == i * C, last_ref[i, 0:1, :], prev)
    vals = _rwkv_prep_values(p, prev, mu_ref, w0_ref, wd_ref, a0_ref, wa_ref, kk_ref, ka_ref, rk_ref)
    bonus = vals[6]

    lane = lax.broadcasted_iota(jnp.int32, (C, RWKV_W), 1)
    t_idx = lax.broadcasted_iota(jnp.int32, (C, RWKV_W), 0)
    s_idx = lane & (RWKV_HEAD - 1)
    row_big = lax.broadcasted_iota(jnp.int32, (RWKV_W, RWKV_W), 0)
    lane_big = lax.broadcasted_iota(jnp.int32, (RWKV_W, RWKV_W), 1)
    bd_mask = _group_index(row_big, RWKV_HEAD) == _group_index(lane_big, RWKV_HEAD)
    ones_bd = jnp.where(bd_mask, 1.0, 0.0).astype(BF16)
    tri_r = lax.broadcasted_iota(jnp.int32, (C, C), 0)
    tri_c = lax.broadcasted_iota(jnp.int32, (C, C), 1)
    masks = (bd_mask, jnp.where(tri_r >= tri_c, 1.0, 0.0).astype(BF16), t_idx > s_idx, t_idx >= s_idx,
             _group_index(t_idx, RW_SUB) == _group_index(s_idx, RW_SUB), t_idx == s_idx, ones_bd)

    per_seq = [tuple(x[i * C:(i + 1) * C] for x in vals[:6]) for i in range(nb)]
    outs, n_new = _rwkv_chunks(per_seq, [state_ref[i] for i in range(nb)], masks)
    for i in range(nb):
        state_ref[i] = n_new[i]
        last_ref[i, 0:1, :] = p[(i + 1) * C - 1:(i + 1) * C, :]
    o = jnp.concatenate(outs, axis=0)

    c = o - _head_sum(o, ones_bd) * (1.0 / RWKV_HEAD)
    var = _head_sum(c * c, ones_bd) * (1.0 / RWKV_HEAD)
    on = c * lax.rsqrt(var + GN_EPS) * lnxg_ref[...] + lnxb_ref[...]
    o_ref[...] = ((on + bonus) * rg_ref[...].reshape(nb * C, RWKV_W)).astype(BF16).reshape(nb, C, RWKV_W)

    @pl.when(c_idx == pl.num_programs(1) - 1)
    def _():
        sfin_ref[...] = state_ref[...]


def _rwkv_prompt(rw_in, rg, lw, batch, seq, nb):
    const = lambda b, c: (0, 0)
    blk = lambda b, c: (b, c, 0)
    o, s_fin = pl.pallas_call(
        _rwkv_prompt_kernel,
        grid=(batch // nb, seq // RW_CHUNK),
        in_specs=[pl.BlockSpec((nb, RW_CHUNK, SHIFT_W), blk), pl.BlockSpec((nb, RW_CHUNK, RWKV_W), blk)]
        + _rwkv_param_specs(const) + [pl.BlockSpec((1, RWKV_W), const)] * 2,
        out_specs=[pl.BlockSpec((nb, RW_CHUNK, RWKV_W), blk),
                   pl.BlockSpec((nb, RWKV_HEAD, RWKV_W), lambda b, c: (b, 0, 0))],
        out_shape=[jax.ShapeDtypeStruct((batch, seq, RWKV_W), BF16),
                   jax.ShapeDtypeStruct((batch, RWKV_HEAD, RWKV_W), F32)],
        scratch_shapes=[pltpu.VMEM((nb, RWKV_HEAD, RWKV_W), F32), pltpu.VMEM((nb, SUBLANES, SHIFT_W), F32)],
        compiler_params=_cparams("arbitrary", "arbitrary"),
        name="rwkv_prompt",
    )(rw_in.reshape(batch, seq, SHIFT_W), rg.reshape(batch, seq, RWKV_W), *_rwkv_params(lw),
      lw["lnx_g"], lw["lnx_b"])
    wkv = jnp.transpose(s_fin.reshape(batch, RWKV_HEAD, RWKV_HEADS, RWKV_HEAD), (0, 2, 3, 1))
    return o.reshape(batch * seq, RWKV_W), wkv


V_HALF = RWKV_HEAD // 2


def _scan_kernel(x_ref, v_ref, s0_ref, o_ref, sfin_ref, s_ref):
    tb = pl.program_id(1)

    @pl.when(tb == 0)
    def _():
        s_ref[...] = s0_ref[...]

    def step(t, carry):
        kk = x_ref[t, 0]
        b = x_ref[t, 1]
        d = x_ref[t, 2]
        kx = x_ref[t, 3]
        r = x_ref[t, 4]
        for vp in range(V_HALF):
            sv = s_ref[vp]
            sa = jnp.sum(sv * kk, axis=0, keepdims=True)
            sn = sv * d - sa * b + v_ref[t, pl.ds(vp, 1), :] * kx
            s_ref[vp] = sn
            o_ref[t, pl.ds(vp, 1), :] = jnp.sum(sn * r, axis=0, keepdims=True)
        return carry

    lax.fori_loop(0, x_ref.shape[0], step, 0)

    @pl.when(tb == pl.num_programs(1) - 1)
    def _():
        sfin_ref[...] = s_ref[...]


def _scan(x5, v_t, s0, t_blk):
    steps, _, _, lanes = x5.shape
    lane_tiles = lanes // LANES
    return pl.pallas_call(
        _scan_kernel,
        grid=(lane_tiles, steps // t_blk),
        in_specs=[
            pl.BlockSpec((t_blk, 5, RWKV_HEAD, LANES), lambda j, t: (t, 0, 0, j)),
            pl.BlockSpec((t_blk, V_HALF, LANES), lambda j, t: (t, 0, j)),
            pl.BlockSpec((V_HALF, RWKV_HEAD, LANES), lambda j, t: (0, 0, j)),
        ],
        out_specs=[
            pl.BlockSpec((t_blk, V_HALF, LANES), lambda j, t: (t, 0, j)),
            pl.BlockSpec((V_HALF, RWKV_HEAD, LANES), lambda j, t: (0, 0, j)),
        ],
        out_shape=[
            jax.ShapeDtypeStruct((steps, V_HALF, lanes), F32),
            jax.ShapeDtypeStruct((V_HALF, RWKV_HEAD, lanes), F32),
        ],
        scratch_shapes=[pltpu.VMEM((V_HALF, RWKV_HEAD, LANES), F32)],
        compiler_params=_cparams("arbitrary", "arbitrary"),
        name="rwkv_scan",
    )(x5, v_t, s0)


def _rwkv_mix(prep, state, batch, seq, t_blk):
    r, k_mod, v, kk, b, d, _ = prep
    H, N = RWKV_HEADS, RWKV_HEAD
    x5 = jnp.stack([kk, b, d, k_mod, r]).reshape(5, batch, seq, H, N)
    x5 = jnp.transpose(x5, (2, 0, 4, 1, 3)).reshape(seq, 5, N, 1, batch * H)
    x5 = jnp.broadcast_to(x5, (seq, 5, N, 2, batch * H)).reshape(seq, 5, N, 2 * batch * H)
    v_t = jnp.transpose(v.reshape(batch, seq, H, 2, V_HALF), (1, 4, 3, 0, 2)).reshape(seq, V_HALF, 2 * batch * H)
    s0 = jnp.transpose(state.reshape(batch, H, 2, V_HALF, N), (3, 4, 2, 0, 1)).reshape(V_HALF, N, 2 * batch * H)
    o_t, s_fin = _scan(x5, v_t, s0, t_blk)
    o = jnp.transpose(o_t.reshape(seq, V_HALF, 2, batch, H), (3, 0, 4, 2, 1)).reshape(batch * seq, H * N)
    s_new = jnp.transpose(s_fin.reshape(V_HALF, N, 2, batch, H), (3, 4, 2, 0, 1)).reshape(batch, H, N, N)
    return o, s_new


def _mix_out(mixed, x_ref, mod_ref, wout_ref, gpost_ref, y_ref):
    out = jnp.dot(mixed, wout_ref[...], preferred_element_type=F32)
    gate = _rows2d(mod_ref)[:, 2 * D_MODEL:]
    y_ref[...] = x_ref[...] + gate * _rms(out, gpost_ref[...])


def _out_prompt_kernel(om_ref, orw_ref, gm_ref, x_ref, mod_ref, wout_ref, gpost_ref, y_ref):
    mixed = jnp.concatenate([om_ref[...], orw_ref[...], gm_ref[...]], axis=-1)
    _mix_out(mixed, x_ref, mod_ref, wout_ref, gpost_ref, y_ref)


def _out_sample_kernel(lat_ref, mg_ref, orw_ref, bonus_ref, rg_ref, gm_ref, x_ref, mod_ref, wout_ref, gpost_ref,
                       lnxg_ref, lnxb_ref, wuv_ref, y_ref):
    lat = lat_ref[...]
    o_mla = jnp.concatenate(
        [jnp.dot(lat[:, h * KV_LORA:(h + 1) * KV_LORA].astype(BF16), wuv_ref[h],
                 preferred_element_type=F32) for h in range(MLA_HEADS)], axis=-1)
    o_mla = (o_mla * mg_ref[...].astype(F32)).astype(BF16)
    ones_bd = _head_ones(RWKV_W, RWKV_HEAD)
    o = orw_ref[...]
    c = o - _head_sum(o, ones_bd) * (1.0 / RWKV_HEAD)
    var = _head_sum(c * c, ones_bd) * (1.0 / RWKV_HEAD)
    on = c * lax.rsqrt(var + GN_EPS) * lnxg_ref[...] + lnxb_ref[...]
    o_rw = ((on + bonus_ref[...]) * rg_ref[...]).astype(BF16)
    mixed = jnp.concatenate([o_mla, o_rw, gm_ref[...]], axis=-1)
    _mix_out(mixed, x_ref, mod_ref, wout_ref, gpost_ref, y_ref)


def _out_prompt(o_mla, o_rw, gm, x2d, mod_b, lw, seq, tm):
    rows = x2d.shape[0]
    tiles_per_seq = seq // tm
    const = lambda i: (0, 0)
    row_blk = lambda i: (i, 0)
    return pl.pallas_call(
        _out_prompt_kernel,
        grid=(rows // tm,),
        in_specs=[
            pl.BlockSpec((tm, MLA_W), row_blk),
            pl.BlockSpec((tm, RWKV_W), row_blk),
            pl.BlockSpec((tm, GM_W), row_blk),
            pl.BlockSpec((tm, D_MODEL), row_blk),
            pl.BlockSpec((1, 1, 3 * D_MODEL), lambda i: (i // tiles_per_seq, 0, 0)),
            pl.BlockSpec((D_MODEL, D_MODEL), const),
            pl.BlockSpec((1, D_MODEL), const),
        ],
        out_specs=pl.BlockSpec((tm, D_MODEL), row_blk),
        out_shape=jax.ShapeDtypeStruct((rows, D_MODEL), F32),
        compiler_params=_cparams("arbitrary"),
        name="out_prompt",
    )(o_mla, o_rw, gm, x2d, mod_b, lw["w_out"], lw["g_post"])


def _out_sample(o_lat, mg, o_rw, bonus, rg, gm, x2d, mod_rows, lw):
    rows = x2d.shape[0]
    const = lambda i: (0, 0)
    full = lambda w: pl.BlockSpec((rows, w), const)
    return pl.pallas_call(
        _out_sample_kernel,
        grid=(1,),
        in_specs=[
            full(MLA_HEADS * KV_LORA), full(MLA_W), full(RWKV_W), full(RWKV_W), full(RWKV_W), full(GM_W),
            full(D_MODEL), full(3 * D_MODEL),
            pl.BlockSpec((D_MODEL, D_MODEL), const),
            pl.BlockSpec((1, D_MODEL), const),
            pl.BlockSpec((1, RWKV_W), const),
            pl.BlockSpec((1, RWKV_W), const),
            pl.BlockSpec((MLA_HEADS, KV_LORA, MLA_V), lambda i: (0, 0, 0)),
        ],
        out_specs=full(D_MODEL),
        out_shape=jax.ShapeDtypeStruct((rows, D_MODEL), F32),
        compiler_params=_cparams("arbitrary"),
        name="out_sample",
    )(o_lat, mg, o_rw, bonus, rg, gm, x2d, mod_rows, lw["w_out"], lw["g_post"], lw["lnx_g"], lw["lnx_b"],
      lw["w_uv_h"])


def _swap_halves(w):
    half = MLA_ROPE // 2
    return jnp.concatenate([w[..., half:], w[..., :half]], axis=-1)


def _rope_tables(pos):
    half = MLA_ROPE // 2
    inv = ROPE_THETA ** (-jnp.arange(half, dtype=F32) / half)
    ang = pos.astype(F32)[:, None] * inv[None, :]
    cos, sin = jnp.cos(ang), jnp.sin(ang)
    zeros = jnp.zeros((pos.shape[0], LANES - MLA_ROPE), F32)
    return (jnp.concatenate([cos, cos, zeros], axis=-1), jnp.concatenate([-sin, sin, zeros], axis=-1))


def _layer_weights(l, w_in, norm_pre_g, norm_post_g, q_norm_g, kv_norm_g, w_uq, w_uk, w_uv, rw_mu, rw_w0,
                   rw_w_decay_up, rw_a0, rw_w_a_up, rw_k_k, rw_k_a, rw_r_k, rw_lnx_g, rw_lnx_b, gm_ln_g,
                   gm_ln_b, gm_w_s, gm_b_s, w_out):
    w = w_in[l]
    split_lo, split_hi = Q_LORA + KV_LORA, Q_LORA + KV_LORA + MLA_ROPE
    kr = w[:, split_lo:split_hi]
    w_in_p = jnp.concatenate([w[:, :Q_LORA], kr, _swap_halves(kr), w[:, Q_LORA:split_lo], w[:, split_hi:]],
                             axis=1).astype(BF16)
    uq = w_uq[l]
    pe = uq[..., MLA_NOPE:]
    w_uq_p = jnp.concatenate([uq, _swap_halves(pe)], axis=-1).reshape(Q_LORA, MLA_HEADS * QHEAD_W).astype(BF16)
    zeros_lora = jnp.zeros((LANES - DECAY_LORA, RWKV_W), F32)
    row = lambda a: a.reshape(1, -1)
    return {
        "w_in": w_in_p, "g_pre": row(norm_pre_g[l]), "g_post": row(norm_post_g[l]),
        "g_q": row(q_norm_g[l]), "g_kv": row(kv_norm_g[l]), "w_uq": w_uq_p,
        "w_uk": w_uk[l].reshape(KV_LORA, MLA_W).astype(BF16),
        "w_uv": w_uv[l].reshape(KV_LORA, MLA_W).astype(BF16),
        "w_ukt": jnp.transpose(w_uk[l], (1, 2, 0)).astype(BF16),
        "w_uv_h": jnp.transpose(w_uv[l], (1, 0, 2)).astype(BF16),
        "rw_mu": row(rw_mu[l]), "rw_w0": row(rw_w0[l]), "rw_a0": row(rw_a0[l]),
        "rw_wd": jnp.concatenate([rw_w_decay_up[l], zeros_lora], axis=0).astype(BF16),
        "rw_wa": jnp.concatenate([zeros_lora, rw_w_a_up[l]], axis=0).astype(BF16),
        "rw_k_k": row(rw_k_k[l]), "rw_k_a": row(rw_k_a[l]), "rw_r_k": row(rw_r_k[l]),
        "lnx_g": row(rw_lnx_g[l]), "lnx_b": row(rw_lnx_b[l]),
        "ln_g": row(gm_ln_g[l]), "ln_b": row(gm_ln_b[l]),
        "gm_ws": gm_w_s[l],
        "gm_bias": jnp.repeat(gm_b_s[l].T, GM_GROUP_W, axis=1),
        "gm_coef0": row(jnp.repeat(gm_w_s[l][:, 0, 0], GM_GROUP_W)),
        "gm_bias0": row(jnp.repeat(gm_b_s[l][:, 0], GM_GROUP_W)),
        "w_out": w_out[l].astype(BF16),
    }


def _pick(full, want):
    return want if full % want == 0 else full


def kernel(x_prompt, x_sample, c_prompt, c_sample, cache_ckv, cache_kpe, page_table, state_wkv, state_shift, w_ada, b_ada, norm_pre_g, norm_post_g, w_in, q_norm_g, kv_norm_g, w_uq, w_uk, w_uv, rw_mu, rw_w0, rw_w_decay_up, rw_a0, rw_w_a_up, rw_k_k, rw_k_a, rw_r_k, rw_lnx_g, rw_lnx_b, gm_ln_g, gm_ln_b, gm_w_s, gm_b_s, w_out):
    batch, seq, _ = x_prompt.shape
    dec_batch, dec_seq, _ = x_sample.shape
    depth = w_in.shape[0]
    n_pages = page_table.shape[1]
    past_len = n_pages * cache_ckv.shape[2]
    assert dec_seq == 1 and seq % CHUNK == 0

    assert seq % RW_CHUNK == 0
    tm_proj = _pick(seq, 512)
    tm_out = _pick(seq, 1024)
    tq = _pick(seq, 512)
    rw_nb = _pick(batch, 8)

    mod = _ada_mod(jnp.concatenate([c_prompt, c_sample], axis=0), w_ada, b_ada)
    cs_p, sn_p = _rope_tables(jnp.arange(seq))
    cs_s, sn_s = _rope_tables(jnp.full((dec_batch,), past_len))
    cache_kpe_t = jnp.swapaxes(cache_kpe, 2, 3)

    y_p = x_prompt.reshape(batch * seq, D_MODEL)
    y_s = x_sample.reshape(dec_batch, D_MODEL)
    outs = {k: [] for k in ("ckv_s", "kpe_s", "wkv_p", "wkv_s", "sh_p", "sh_s", "vc_p", "vc_s")}
    prompt_stacks = None
    for l in range(depth):
        lw = _layer_weights(l, w_in, norm_pre_g, norm_post_g, q_norm_g, kv_norm_g, w_uq, w_uk, w_uv, rw_mu,
                            rw_w0, rw_w_decay_up, rw_a0, rw_w_a_up, rw_k_k, rw_k_a, rw_r_k, rw_lnx_g,
                            rw_lnx_b, gm_ln_g, gm_ln_b, gm_w_s, gm_b_s, w_out)
        mod_p = mod[l, :batch].reshape(batch, 1, 3 * D_MODEL)
        q, k, v, mg, ckv_stack, kpe_stack, rw_in, rg, gm, vn_last = _proj_prompt(
            y_p, mod_p, lw, cs_p, sn_p, batch, seq, tm_proj, prompt_stacks)
        prompt_stacks = (ckv_stack, kpe_stack)
        o_mla = _flash(q, k, v, mg, batch, seq, tq)
        o_rw, wkv_new = _rwkv_prompt(rw_in, rg, lw, batch, seq, rw_nb)
        y_p = _out_prompt(o_mla, o_rw, gm, y_p, mod_p, lw, seq, tm_out)
        outs["wkv_p"].append(wkv_new)
        outs["sh_p"].append(rw_in.reshape(batch, seq, SHIFT_W)[:, -1])
        outs["vc_p"].append(vn_last)

        mod_s = mod[l, batch:]
        qa, qp, mg_s, ckv_s, kpe_s, rw_s, rg_s, gm_s, vn_s = _proj_sample(y_s, mod_s, lw, cs_s, sn_s)
        o_lat = _paged_attention(qa, qp, ckv_s, kpe_s, cache_ckv, cache_kpe_t, page_table, l)
        prep_s = _rwkv_prep_sample(rw_s, lw, state_shift[l])
        o_rw_s, wkv_s = _rwkv_mix(prep_s, state_wkv[l], dec_batch, 1, 1)
        y_s = _out_sample(o_lat, mg_s, o_rw_s, prep_s[6], rg_s, gm_s, y_s, mod_s, lw)
        outs["ckv_s"].append(ckv_s.reshape(dec_batch, 1, KV_LORA))
        outs["kpe_s"].append(kpe_s[:, :MLA_ROPE].reshape(dec_batch, 1, MLA_ROPE))
        outs["wkv_s"].append(wkv_s)
        outs["sh_s"].append(rw_s)
        outs["vc_s"].append(vn_s.reshape(dec_batch, 1, GM_W))

    st = lambda name: jnp.stack(outs[name])
    ckv_p, kpe_p = prompt_stacks
    return (y_p.reshape(batch, seq, D_MODEL), y_s.reshape(dec_batch, 1, D_MODEL),
            ckv_p.reshape(depth, batch, seq, KV_LORA), kpe_p.reshape(depth, batch, seq, MLA_ROPE), st("ckv_s"), st("kpe_s"), st("wkv_p"), st("wkv_s"),
            st("sh_p"), st("sh_s"), st("vc_p"), st("vc_s"))
```

```python
import functools

import numpy as np
import jax
import jax.numpy as jnp
from jax import lax
from jax.experimental import pallas as pl
from jax.experimental.pallas import tpu as pltpu

F32 = jnp.float32
BF16 = jnp.bfloat16

D_MODEL = 1024
MLA_V = 128
MLA_W = D_MODEL // 2
MLA_HEADS = MLA_W // MLA_V
MLA_NOPE = 128
MLA_ROPE = 64
MLA_QK = MLA_NOPE + MLA_ROPE
Q_LORA = (3 * D_MODEL) // 8
KV_LORA = D_MODEL // 4
ROPE_THETA = 10000.0
ATTN_SCALE = MLA_QK ** -0.5
Q_SCALE = ATTN_SCALE * float(np.log2(np.e))
RWKV_W = D_MODEL // 4
RWKV_HEAD = 64
RWKV_HEADS = RWKV_W // RWKV_HEAD
DECAY_LORA = 64
ICLR_LORA = 64
SHIFT_W = 3 * RWKV_W + DECAY_LORA + ICLR_LORA
GN_EPS = 64e-5
GM_W = D_MODEL // 4
GM_GROUPS = 4
GM_GROUP_W = GM_W // GM_GROUPS
CHUNK = 128
RMS_EPS = 1e-6
LN_EPS = 1e-5

LANES = 128
SUBLANES = 8
VMEM_LIMIT_BYTES = 56 * 1024 * 1024

QHEAD_W = 2 * LANES
OFF_Q = 0
OFF_KR = OFF_Q + Q_LORA
OFF_KV = OFF_KR + LANES
OFF_MG = OFF_KV + KV_LORA
OFF_RW = OFF_MG + MLA_W
OFF_RG = OFF_RW + SHIFT_W
OFF_GU = OFF_RG + RWKV_W
OFF_GV = OFF_GU + GM_W
OFF_GG = OFF_GV + GM_W
IN_W_P = OFF_GG + GM_W

NEG_BIG = -1e30


def _cparams(*sem):
    return pltpu.CompilerParams(dimension_semantics=sem, vmem_limit_bytes=VMEM_LIMIT_BYTES)


def _silu(x):
    return x * jax.nn.sigmoid(x)


def _rms(x, g, eps=RMS_EPS):
    return x * lax.rsqrt(jnp.mean(x * x, axis=-1, keepdims=True) + eps) * g


def _rows2d(ref):
    m = ref[...]
    return m.reshape(m.shape[-2], m.shape[-1])


def _group_index(idx, group):
    shift = group.bit_length() - 1
    assert 1 << shift == group
    return lax.shift_right_logical(idx, shift)


def _head_ones(width, head):
    r = _group_index(lax.broadcasted_iota(jnp.int32, (width, width), 0), head)
    c = _group_index(lax.broadcasted_iota(jnp.int32, (width, width), 1), head)
    return jnp.where(r == c, 1.0, 0.0).astype(BF16)


def _head_sum(x, ones_bd):
    hi = x.astype(BF16)
    lo = (x - hi.astype(F32)).astype(BF16)
    return (jnp.dot(hi, ones_bd, preferred_element_type=F32)
            + jnp.dot(lo, ones_bd, preferred_element_type=F32))


def _rope_pair(blk, cs, sn):
    return blk * cs + pltpu.roll(blk, MLA_ROPE, 1) * sn


def _ada_kernel(c_ref, w_ref, b_ref, o_ref):
    c = _silu(c_ref[...]).astype(BF16)
    o_ref[0] = jnp.dot(c, w_ref[0].astype(BF16), preferred_element_type=F32) + b_ref[0]


def _ada_mod(c_all, w_ada, b_ada):
    depth = w_ada.shape[0]
    n_rows = c_all.shape[0]
    n_tiles = (3 * D_MODEL) // D_MODEL
    return pl.pallas_call(
        _ada_kernel,
        grid=(depth, n_tiles),
        in_specs=[
            pl.BlockSpec((n_rows, D_MODEL), lambda l, j: (0, 0)),
            pl.BlockSpec((1, D_MODEL, D_MODEL), lambda l, j: (l, 0, j)),
            pl.BlockSpec((1, 1, D_MODEL), lambda l, j: (l, 0, j)),
        ],
        out_specs=pl.BlockSpec((1, n_rows, D_MODEL), lambda l, j: (l, 0, j)),
        out_shape=jax.ShapeDtypeStruct((depth, n_rows, 3 * D_MODEL), F32),
        compiler_params=_cparams("arbitrary", "arbitrary"),
        name="ada_mod",
    )(c_all, w_ada, b_ada.reshape(depth, 1, 3 * D_MODEL))


def _proj_common(x_ref, mod_ref, gpre_ref, win_ref):
    mod = _rows2d(mod_ref)
    shift = mod[:, 0:D_MODEL]
    scale = mod[:, D_MODEL:2 * D_MODEL]
    h = _rms(x_ref[...], gpre_ref[...]) * (1.0 + scale) + shift
    hb = h.astype(BF16)

    def seg(off, width):
        return jnp.dot(hb, win_ref[:, off:off + width], preferred_element_type=F32)

    return seg


def _q_heads(seg, gq_ref, wuq_ref, cs, sn):
    q_kr = seg(OFF_Q, Q_LORA + LANES)
    qn = _rms(q_kr[:, :Q_LORA], gq_ref[...]).astype(BF16)
    q_all = jnp.dot(qn, wuq_ref[...], preferred_element_type=F32)
    heads = []
    for h in range(MLA_HEADS):
        nope = q_all[:, h * QHEAD_W:h * QHEAD_W + LANES] * Q_SCALE
        pe = _rope_pair(q_all[:, h * QHEAD_W + LANES:(h + 1) * QHEAD_W], cs, sn) * Q_SCALE
        heads.append((nope, pe))
    return heads, _rope_pair(q_kr[:, Q_LORA:], cs, sn)


def _gm_norm(seg, lng_ref, lnb_ref):
    v = seg(OFF_GV, GM_W)
    mu = jnp.mean(v, axis=-1, keepdims=True)
    c = v - mu
    var = jnp.mean(c * c, axis=-1, keepdims=True)
    return c * lax.rsqrt(var + LN_EPS) * lng_ref[...] + lnb_ref[...]


def _proj_prompt_kernel(x_ref, mod_ref, gpre_ref, win_ref, gq_ref, wuq_ref, gkv_ref, wuk_ref, wuv_ref,
                        cs_ref, sn_ref, lng_ref, lnb_ref, ws_ref, bs_ref, *rest, tiles_per_seq, n_prev):
    if n_prev:
        ckv_prev_ref, kpe_prev_ref = rest[:2]
        rest = rest[2:]
    q_ref, k_ref, v_ref, mg_ref, ckv_ref, kpe_ref, rw_ref, rg_ref, gm_ref, vn_ref = rest
    for j in range(n_prev):
        ckv_ref[j] = ckv_prev_ref[j]
        kpe_ref[j] = kpe_prev_ref[j]
    seg = _proj_common(x_ref, mod_ref, gpre_ref, win_ref)
    cs = cs_ref[...]
    sn = sn_ref[...]
    tm = x_ref.shape[0]

    heads, kpe = _q_heads(seg, gq_ref, wuq_ref, cs, sn)
    for h, (nope, pe) in enumerate(heads):
        q_ref[:, h * QHEAD_W:h * QHEAD_W + LANES] = nope.astype(BF16)
        q_ref[:, h * QHEAD_W + LANES:(h + 1) * QHEAD_W] = pe.astype(BF16)
    ckv = _rms(seg(OFF_KV, KV_LORA), gkv_ref[...])
    ckv_ref[n_prev] = ckv
    kpe_ref[n_prev] = kpe[:, :MLA_ROPE]
    ckv_b = ckv.astype(BF16)
    k_nope = jnp.dot(ckv_b, wuk_ref[...], preferred_element_type=F32)
    v_ref[...] = jnp.dot(ckv_b, wuv_ref[...], preferred_element_type=F32).astype(BF16)
    kpe_b = kpe.astype(BF16)
    for h in range(MLA_HEADS):
        k_ref[:, h * QHEAD_W:h * QHEAD_W + LANES] = k_nope[:, h * MLA_NOPE:(h + 1) * MLA_NOPE].astype(BF16)
        k_ref[:, h * QHEAD_W + LANES:(h + 1) * QHEAD_W] = kpe_b
    mg_ref[...] = _silu(seg(OFF_MG, MLA_W)).astype(BF16)

    rw_ref[...] = seg(OFF_RW, SHIFT_W)
    rg_ref[...] = _silu(seg(OFF_RG, RWKV_W))

    vn = _gm_norm(seg, lng_ref, lnb_ref)
    vn_b = vn.astype(BF16)
    gate_u = _silu(seg(OFF_GG, GM_W)) * seg(OFF_GU, GM_W)
    row = lax.broadcasted_iota(jnp.int32, (CHUNK, CHUNK), 0)
    col = lax.broadcasted_iota(jnp.int32, (CHUNK, CHUNK), 1)
    lane_group = _group_index(lax.broadcasted_iota(jnp.int32, (CHUNK, GM_W), 1), GM_GROUP_W)
    w_tril = [jnp.where(row >= col, ws_ref[g], 0.0).astype(BF16) for g in range(GM_GROUPS)]
    for c in range(tm // CHUNK):
        vc = vn_b[c * CHUNK:(c + 1) * CHUNK]
        z = bs_ref[...]
        for g in range(GM_GROUPS):
            zg = jnp.dot(w_tril[g], vc, preferred_element_type=F32)
            z = z + jnp.where(lane_group == g, zg, 0.0)
        gm_ref[c * CHUNK:(c + 1) * CHUNK, :] = (gate_u[c * CHUNK:(c + 1) * CHUNK] * z).astype(BF16)

    @pl.when(pl.program_id(0) % tiles_per_seq == tiles_per_seq - 1)
    def _():
        vn_ref[0] = vn[tm - CHUNK:, :]


def _proj_prompt(x2d, mod_b, lw, cs, sn, batch, seq, tm, prev_stacks):
    rows = x2d.shape[0]
    tiles_per_seq = seq // tm
    n_prev = 0 if prev_stacks is None else prev_stacks[0].shape[0]
    const = lambda i: (0, 0)
    row_blk = lambda i: (i, 0)
    stack_blk = lambda i: (0, i, 0)
    outs = [
        (QHEAD_W * MLA_HEADS, BF16), (QHEAD_W * MLA_HEADS, BF16), (MLA_W, BF16), (MLA_W, BF16),
        (KV_LORA, F32), (MLA_ROPE, F32), (SHIFT_W, F32), (RWKV_W, F32), (GM_W, BF16),
    ]
    out_shape = [jax.ShapeDtypeStruct((rows, w), dt) for w, dt in outs]
    out_specs = [pl.BlockSpec((tm, w), row_blk) for w, _ in outs]
    for idx in (4, 5):
        w, dt = outs[idx]
        out_shape[idx] = jax.ShapeDtypeStruct((n_prev + 1, rows, w), dt)
        out_specs[idx] = pl.BlockSpec((n_prev + 1, tm, w), stack_blk)
    out_shape.append(jax.ShapeDtypeStruct((batch, CHUNK, GM_W), F32))
    out_specs.append(pl.BlockSpec((1, CHUNK, GM_W), lambda i: (i // tiles_per_seq, 0, 0)))
    prev_specs = [] if not n_prev else [pl.BlockSpec((n_prev, tm, KV_LORA), stack_blk),
                                        pl.BlockSpec((n_prev, tm, MLA_ROPE), stack_blk)]
    return pl.pallas_call(
        functools.partial(_proj_prompt_kernel, tiles_per_seq=tiles_per_seq, n_prev=n_prev),
        grid=(rows // tm,),
        in_specs=[
            pl.BlockSpec((tm, D_MODEL), row_blk),
            pl.BlockSpec((1, 1, 3 * D_MODEL), lambda i: (i // tiles_per_seq, 0, 0)),
            pl.BlockSpec((1, D_MODEL), const),
            pl.BlockSpec((D_MODEL, IN_W_P), const),
            pl.BlockSpec((1, Q_LORA), const),
            pl.BlockSpec((Q_LORA, QHEAD_W * MLA_HEADS), const),
            pl.BlockSpec((1, KV_LORA), const),
            pl.BlockSpec((KV_LORA, MLA_W), const),
            pl.BlockSpec((KV_LORA, MLA_W), const),
            pl.BlockSpec((tm, LANES), lambda i: (i % tiles_per_seq, 0)),
            pl.BlockSpec((tm, LANES), lambda i: (i % tiles_per_seq, 0)),
            pl.BlockSpec((1, GM_W), const),
            pl.BlockSpec((1, GM_W), const),
            pl.BlockSpec((GM_GROUPS, CHUNK, CHUNK), lambda i: (0, 0, 0)),
            pl.BlockSpec((CHUNK, GM_W), const),
        ] + prev_specs,
        out_specs=out_specs,
        out_shape=out_shape,
        compiler_params=_cparams("arbitrary"),
        name="proj_prompt",
    )(x2d, mod_b, lw["g_pre"], lw["w_in"], lw["g_q"], lw["w_uq"], lw["g_kv"], lw["w_uk"], lw["w_uv"],
      cs, sn, lw["ln_g"], lw["ln_b"], lw["gm_ws"], lw["gm_bias"], *(prev_stacks or ()))


def _proj_sample_kernel(x_ref, mod_ref, gpre_ref, win_ref, gq_ref, wuq_ref, gkv_ref, wukt_ref,
                        cs_ref, sn_ref, lng_ref, lnb_ref, coef_ref, bias_ref,
                        qa_ref, qp_ref, mg_ref, ckv_ref, kpe_ref, rw_ref, rg_ref, gm_ref, vn_ref):
    seg = _proj_common(x_ref, mod_ref, gpre_ref, win_ref)
    cs = cs_ref[...]
    sn = sn_ref[...]
    heads, kpe = _q_heads(seg, gq_ref, wuq_ref, cs, sn)
    for h, (nope, pe) in enumerate(heads):
        qa_ref[:, h * KV_LORA:(h + 1) * KV_LORA] = jnp.dot(nope.astype(BF16), wukt_ref[h],
                                                           preferred_element_type=F32)
        qp_ref[:, h * LANES:(h + 1) * LANES] = pe
    ckv_ref[...] = _rms(seg(OFF_KV, KV_LORA), gkv_ref[...])
    kpe_ref[...] = kpe
    mg_ref[...] = _silu(seg(OFF_MG, MLA_W)).astype(BF16)
    rw_ref[...] = seg(OFF_RW, SHIFT_W)
    rg_ref[...] = _silu(seg(OFF_RG, RWKV_W))
    vn = _gm_norm(seg, lng_ref, lnb_ref)
    vn_ref[...] = vn
    z = vn * coef_ref[...] + bias_ref[...]
    gm_ref[...] = (_silu(seg(OFF_GG, GM_W)) * seg(OFF_GU, GM_W) * z).astype(BF16)


def _proj_sample(x2d, mod_rows, lw, cs, sn):
    rows = x2d.shape[0]
    const = lambda i: (0, 0)
    outs = [
        (KV_LORA * MLA_HEADS, F32), (LANES * MLA_HEADS, F32), (MLA_W, BF16), (KV_LORA, F32), (LANES, F32),
        (SHIFT_W, F32), (RWKV_W, F32), (GM_W, BF16), (GM_W, F32),
    ]
    return pl.pallas_call(
        _proj_sample_kernel,
        grid=(1,),
        in_specs=[
            pl.BlockSpec((rows, D_MODEL), const),
            pl.BlockSpec((rows, 3 * D_MODEL), const),
            pl.BlockSpec((1, D_MODEL), const),
            pl.BlockSpec((D_MODEL, IN_W_P), const),
            pl.BlockSpec((1, Q_LORA), const),
            pl.BlockSpec((Q_LORA, QHEAD_W * MLA_HEADS), const),
            pl.BlockSpec((1, KV_LORA), const),
            pl.BlockSpec((MLA_HEADS, MLA_NOPE, KV_LORA), lambda i: (0, 0, 0)),
            pl.BlockSpec((rows, LANES), const),
            pl.BlockSpec((rows, LANES), const),
            pl.BlockSpec((1, GM_W), const),
            pl.BlockSpec((1, GM_W), const),
            pl.BlockSpec((1, GM_W), const),
            pl.BlockSpec((1, GM_W), const),
        ],
        out_specs=[pl.BlockSpec((rows, w), const) for w, _ in outs],
        out_shape=[jax.ShapeDtypeStruct((rows, w), dt) for w, dt in outs],
        compiler_params=_cparams("arbitrary"),
        name="proj_sample",
    )(x2d, mod_rows, lw["g_pre"], lw["w_in"], lw["g_q"], lw["w_uq"], lw["g_kv"], lw["w_ukt"],
      cs, sn, lw["ln_g"], lw["ln_b"], lw["gm_coef0"], lw["gm_bias0"])


def _flash_kernel(q_ref, k_ref, v_ref, g_ref, o_ref, *, tq):
    seq = q_ref.shape[1]
    row = lax.broadcasted_iota(jnp.int32, (tq, tq), 0)
    col = lax.broadcasted_iota(jnp.int32, (tq, tq), 1)
    for qi in range(seq // tq):
        rows = slice(qi * tq, (qi + 1) * tq)
        q = q_ref[0, rows, :]
        m = jnp.full((tq, 1), NEG_BIG, F32)
        l = jnp.zeros((tq, 1), F32)
        acc = jnp.zeros((tq, MLA_V), F32)
        for ki in range(qi + 1):
            cols = slice(ki * tq, (ki + 1) * tq)
            s = lax.dot_general(q, k_ref[0, cols, :], (((1,), (1,)), ((), ())),
                                preferred_element_type=F32)
            if ki == qi:
                s = jnp.where(row >= col, s, NEG_BIG)
            m_new = jnp.maximum(m, jnp.max(s, axis=-1, keepdims=True))
            alpha = jnp.exp2(m - m_new)
            p = jnp.exp2(s - m_new)
            l = alpha * l + jnp.sum(p, axis=-1, keepdims=True)
            acc = alpha * acc + jnp.dot(p.astype(BF16), v_ref[0, cols, :], preferred_element_type=F32)
            m = m_new
        o_ref[0, rows, :] = (acc / l * g_ref[0, rows, :].astype(F32)).astype(BF16)


def _flash(q, k, v, g, batch, seq, tq):
    q3 = q.reshape(batch, seq, QHEAD_W * MLA_HEADS)
    k3 = k.reshape(batch, seq, QHEAD_W * MLA_HEADS)
    v3 = v.reshape(batch, seq, MLA_W)
    g3 = g.reshape(batch, seq, MLA_W)
    head_blk = lambda b, h: (b, 0, h)
    out = pl.pallas_call(
        functools.partial(_flash_kernel, tq=tq),
        grid=(batch, MLA_HEADS),
        in_specs=[
            pl.BlockSpec((1, seq, QHEAD_W), head_blk),
            pl.BlockSpec((1, seq, QHEAD_W), head_blk),
            pl.BlockSpec((1, seq, MLA_V), head_blk),
            pl.BlockSpec((1, seq, MLA_V), head_blk),
        ],
        out_specs=pl.BlockSpec((1, seq, MLA_V), head_blk),
        out_shape=jax.ShapeDtypeStruct((batch, seq, MLA_W), BF16),
        compiler_params=_cparams("arbitrary", "arbitrary"),
        name="flash_prompt",
    )(q3, k3, v3, g3)
    return out.reshape(batch * seq, MLA_W)


N_SLOTS = 3
LOOKAHEAD = N_SLOTS - 1


def _paged_kernel(pt_ref, qa_ref, qp_ref, cn_ref, kn_ref, ckv_hbm, kpe_hbm, o_ref, ckv_buf, kpe_buf, sem,
                  *, layer, n_pages, page, group_tokens):
    b = pl.program_id(0)
    n_seq = pl.num_programs(0)
    slot = lax.rem(b, N_SLOTS)

    def page_copies(seq_idx, sl):
        copies = []
        for i in range(n_pages):
            pg = pt_ref[seq_idx, i]
            copies.append(pltpu.make_async_copy(
                ckv_hbm.at[layer, pg], ckv_buf.at[sl, pl.ds(i * page, page), :], sem.at[0, sl]))
            copies.append(pltpu.make_async_copy(kpe_hbm.at[layer, pg], kpe_buf.at[sl, i], sem.at[1, sl]))
        return copies

    def start_all(copies):
        for i, c in enumerate(copies):
            c.start(priority=(i // 2) % 2)

    @pl.when(b == 0)
    def _():
        for ahead in range(LOOKAHEAD):
            start_all(page_copies(ahead, ahead))

    for c in page_copies(b, slot):
        c.wait()

    nxt = lax.rem(b + LOOKAHEAD, n_seq)
    nxt_slot = lax.rem(b + LOOKAHEAD, N_SLOTS)
    start_all(page_copies(nxt, nxt_slot))

    qa = qa_ref[0]
    qp = qp_ref[0][:, :MLA_ROPE]
    qa_b = qa.astype(BF16)
    qp_b = qp.astype(BF16)
    group_pages = group_tokens // page
    groups = [slice(g * group_tokens, (g + 1) * group_tokens) for g in range(n_pages // group_pages)]
    ckv = [ckv_buf[slot, gs, :].astype(BF16) for gs in groups]
    kpe_t = [jnp.concatenate([kpe_buf[slot, g * group_pages + i] for i in range(group_pages)],
                             axis=1).astype(BF16) for g in range(len(groups))]
    s = [lax.dot_general(qa_b, ck, (((1,), (1,)), ((), ())), preferred_element_type=F32)
         + jnp.dot(qp_b, kp, preferred_element_type=F32) for ck, kp in zip(ckv, kpe_t)]
    m_g = [jnp.max(sg, axis=-1, keepdims=True) for sg in s]
    p = [jnp.exp2(sg - mg) for sg, mg in zip(s, m_g)]
    l_g = [jnp.sum(pg, axis=-1, keepdims=True) for pg in p]
    acc_g = [jnp.dot(pg.astype(BF16), ck, preferred_element_type=F32) for pg, ck in zip(p, ckv)]
    cn = cn_ref[0]
    kn = kn_ref[0][:, :MLA_ROPE]
    s_new = jnp.sum(qa * cn, axis=-1, keepdims=True) + jnp.sum(qp * kn, axis=-1, keepdims=True)
    m = functools.reduce(jnp.maximum, m_g, s_new)
    p_new = jnp.exp2(s_new - m)
    w_g = [jnp.exp2(mg - m) for mg in m_g]
    l = sum(wg * lg for wg, lg in zip(w_g, l_g)) + p_new
    o_ref[0] = (sum(wg * ag for wg, ag in zip(w_g, acc_g)) + p_new * cn) / l

    @pl.when(b == n_seq - 1)
    def _():
        for ahead in range(1, LOOKAHEAD + 1):
            for c in page_copies(lax.rem(b + ahead, n_seq), lax.rem(b + ahead, N_SLOTS)):
                c.wait()


def _paged_attention(qa, qp, ckv_new, kpe_new, cache_ckv, cache_kpe_t, page_table, layer):
    dec_batch, n_pages = page_table.shape
    page = cache_ckv.shape[2]
    head_pad = ((0, 0), (0, SUBLANES - MLA_HEADS), (0, 0))
    qa3 = jnp.pad(qa.reshape(dec_batch, MLA_HEADS, KV_LORA), head_pad)
    qp3 = jnp.pad(qp.reshape(dec_batch, MLA_HEADS, LANES), head_pad)
    cn3 = ckv_new.reshape(dec_batch, 1, KV_LORA)
    kn3 = kpe_new.reshape(dec_batch, 1, LANES)
    row_blk = lambda b, pt: (b, 0, 0)
    out = pl.pallas_call(
        functools.partial(_paged_kernel, layer=layer, n_pages=n_pages, page=page,
                          group_tokens=_pick(n_pages * page, 2048)),
        grid_spec=pltpu.PrefetchScalarGridSpec(
            num_scalar_prefetch=1,
            grid=(dec_batch,),
            in_specs=[
                pl.BlockSpec((1, SUBLANES, KV_LORA), row_blk),
                pl.BlockSpec((1, SUBLANES, LANES), row_blk),
                pl.BlockSpec((1, 1, KV_LORA), row_blk),
                pl.BlockSpec((1, 1, LANES), row_blk),
                pl.BlockSpec(memory_space=pl.ANY),
                pl.BlockSpec(memory_space=pl.ANY),
            ],
            out_specs=pl.BlockSpec((1, SUBLANES, KV_LORA), row_blk),
            scratch_shapes=[
                pltpu.VMEM((N_SLOTS, n_pages * page, KV_LORA), F32),
                pltpu.VMEM((N_SLOTS, n_pages, MLA_ROPE, page), F32),
                pltpu.SemaphoreType.DMA((2, N_SLOTS)),
            ],
        ),
        out_shape=jax.ShapeDtypeStruct((dec_batch, SUBLANES, KV_LORA), F32),
        compiler_params=_cparams("arbitrary"),
        name="paged_attention",
    )(page_table, qa3, qp3, cn3, kn3, cache_ckv, cache_kpe_t)
    return out[:, :MLA_HEADS].reshape(dec_batch, MLA_HEADS * KV_LORA)


def _rwkv_prep_values(p, prev, mu_ref, w0_ref, wd_ref, a0_ref, wa_ref, kk_ref, ka_ref, rk_ref):
    xm = p + (prev - p) * mu_ref[...]
    r = xm[:, 0:RWKV_W]
    k = xm[:, RWKV_W:2 * RWKV_W]
    v = xm[:, 2 * RWKV_W:3 * RWKV_W]
    tail = xm[:, 3 * RWKV_W:]
    lane = lax.broadcasted_iota(jnp.int32, tail.shape, 1)
    lora_in = jnp.where(lane < DECAY_LORA, jnp.tanh(tail), tail).astype(BF16)
    dw = jnp.dot(lora_in, wd_ref[...], preferred_element_type=F32)
    da = jnp.dot(lora_in, wa_ref[...], preferred_element_type=F32)
    z = -(w0_ref[...] + dw)
    softplus = jnp.maximum(z, 0.0) + jnp.log(1.0 + jnp.exp(-jnp.abs(z)))
    w = -softplus - 0.5
    log_decay = -jnp.exp(w)
    a = jax.nn.sigmoid(a0_ref[...] + da)
    ones_bd = _head_ones(RWKV_W, RWKV_HEAD)
    kk = k * kk_ref[...]
    kk = kk / jnp.maximum(jnp.sqrt(_head_sum(kk * kk, ones_bd)), 1e-12)
    k_mod = k * (1.0 + (a - 1.0) * ka_ref[...])
    bonus = _head_sum(r * k_mod * rk_ref[...], ones_bd) * v
    return r, k_mod, v, kk, kk * a, log_decay, bonus


def _rwkv_param_specs(index_map):
    widths = (SHIFT_W, RWKV_W, None, RWKV_W, None, RWKV_W, RWKV_W, RWKV_W)
    return [pl.BlockSpec((LANES, RWKV_W) if w is None else (1, w), index_map) for w in widths]


def _rwkv_params(lw):
    return (lw["rw_mu"], lw["rw_w0"], lw["rw_wd"], lw["rw_a0"], lw["rw_wa"], lw["rw_k_k"], lw["rw_k_a"],
            lw["rw_r_k"])


def _rwkv_prep_sample_kernel(p_ref, prev_ref, mu_ref, w0_ref, wd_ref, a0_ref, wa_ref, kk_ref, ka_ref, rk_ref,
                             r_o, k_o, v_o, kk_o, b_o, d_o, bonus_o):
    r, k_mod, v, kk, b, log_decay, bonus = _rwkv_prep_values(
        p_ref[...], prev_ref[...], mu_ref, w0_ref, wd_ref, a0_ref, wa_ref, kk_ref, ka_ref, rk_ref)
    r_o[...] = r
    k_o[...] = k_mod
    v_o[...] = v
    kk_o[...] = kk
    b_o[...] = b
    d_o[...] = jnp.exp(log_decay)
    bonus_o[...] = bonus


def _rwkv_prep_sample(rw_in, lw, shift_rows):
    rows = rw_in.shape[0]
    const = lambda i: (0, 0)
    return pl.pallas_call(
        _rwkv_prep_sample_kernel,
        grid=(1,),
        in_specs=[pl.BlockSpec((rows, SHIFT_W), const)] * 2 + _rwkv_param_specs(const),
        out_specs=[pl.BlockSpec((rows, RWKV_W), const)] * 7,
        out_shape=[jax.ShapeDtypeStruct((rows, RWKV_W), F32)] * 7,
        compiler_params=_cparams("arbitrary"),
        name="rwkv_prep_sample",
    )(rw_in, shift_rows, *_rwkv_params(lw))


RW_CHUNK = 64
RW_SUB = 16


def _split3(x):
    hi = x.astype(BF16)
    r1 = x - hi.astype(F32)
    mid = r1.astype(BF16)
    lo = (r1 - mid.astype(F32)).astype(BF16)
    return hi, mid, lo


def _block_diag(y, bd_mask):
    return jnp.where(bd_mask, jnp.concatenate([y] * RWKV_HEADS, axis=0), 0.0).astype(BF16)


def _mm(x, y_bd):
    return jnp.dot(x.astype(BF16), y_bd, preferred_element_type=F32)


def _rwkv_chunks(vals, n0, masks):
    bd_mask, tri_incl, strict, incl, same_sub, eye_tiled, ones_bd = masks
    C = RW_CHUNK
    seqs = range(len(vals))
    bd = lambda y: _block_diag(y, bd_mask)
    r, kx, v, kap, bb, lam = ([val[j] for val in vals] for j in range(6))
    cum = [sum(jnp.dot(tri_incl, part, preferred_element_type=F32) for part in _split3(lam[i])) for i in seqs]
    cum_last = [c[C - 1:C, :] for c in cum]
    e_neg = [jnp.exp(-c) for c in cum]
    ap = [jnp.concatenate([kap[i] * jnp.exp(cum[i] - lam[i]), r[i] * jnp.exp(cum[i])], axis=0).astype(BF16)
          for i in seqs]
    rhs = [jnp.concatenate([bd(bb[i] * e_neg[i]), bd(kx[i] * e_neg[i])], axis=0) for i in seqs]
    g = [lax.dot_general(ap[i], rhs[i], (((1,), (1,)), ((), ())), preferred_element_type=F32) for i in seqs]
    l_ab = [jnp.where(strict, gi[:C, :RWKV_W], 0.0) for gi in g]
    l_ak = [jnp.where(strict, gi[:C, RWKV_W:], 0.0) for gi in g]
    q_pb = [jnp.where(incl, gi[C:, :RWKV_W], 0.0) for gi in g]
    q_pk = [jnp.where(incl, gi[C:, RWKV_W:], 0.0) for gi in g]
    apn = [jnp.dot(ap[i], bd(n0[i]), preferred_element_type=F32) for i in seqs]
    lq_v = [_mm(jnp.concatenate([l_ak[i], q_pk[i]], axis=0), bd(v[i])) for i in seqs]
    w = [apn[i][:C] + lq_v[i][:C] for i in seqs]

    l_d = [jnp.where(same_sub, li, 0.0) for li in l_ab]
    l_off = [l_ab[i] - l_d[i] for i in seqs]
    q_inv = [-li for li in l_d]
    power = l_d
    for _ in range(RW_SUB.bit_length() - 2):
        power = [_mm(pw, bd(pw)) for pw in power]
        q_inv = [q_inv[i] + power[i] + _mm(power[i], bd(q_inv[i])) for i in seqs]
    qz = [jnp.dot(q_inv[i].astype(BF16), jnp.concatenate([bd(l_off[i]), bd(w[i])], axis=1),
                  preferred_element_type=F32) for i in seqs]
    n = [l_off[i] + qz[i][:, :RWKV_W] for i in seqs]
    w1 = [w[i] + qz[i][:, RWKV_W:] for i in seqs]
    n_sq = [_mm(ni, bd(ni)) for ni in n]
    y = [w1[i] + _mm(n_sq[i], bd(w1[i])) for i in seqs]
    u = [_mm(n[i], bd(y[i])) - y[i] for i in seqs]

    o = [apn[i][C:] + _mm(q_pb[i], bd(u[i])) + lq_v[i][C:] for i in seqs]

    e_end = [jnp.exp(cum_last[i] - cum[i]) for i in seqs]
    lhs_t = [jnp.concatenate([bb[i] * e_end[i], kx[i] * e_end[i]], axis=0).T.astype(BF16) for i in seqs]
    full = [jnp.dot(lhs_t[i], jnp.concatenate([u[i], v[i]], axis=0).astype(BF16), preferred_element_type=F32)
            for i in seqs]
    heads = [slice(h * RWKV_HEAD, (h + 1) * RWKV_HEAD) for h in range(RWKV_HEADS)]
    delta = [sum(jnp.where(bd_mask[hs], f[hs], 0.0) for hs in heads) for f in full]
    g_wide = [sum(jnp.dot(part, ones_bd, preferred_element_type=F32)
                  for part in _split3(jnp.where(eye_tiled, jnp.exp(cl), 0.0))) for cl in cum_last]
    return o, [n0[i] * g_wide[i] + delta[i] for i in seqs]


def _rwkv_prompt_kernel(p_ref, rg_ref, mu_ref, w0_ref, wd_ref, a0_ref, wa_ref, kk_ref, ka_ref, rk_ref,
                        lnxg_ref, lnxb_ref, o_ref, sfin_ref, state_ref, last_ref):
    nb, C, _ = p_ref.shape
    c_idx = pl.program_id(1)

    @pl.when(c_idx == 0)
    def _():
        state_ref[...] = jnp.zeros(state_ref.shape, F32)
        last_ref[...] = jnp.zeros(last_ref.shape, F32)

    p = p_ref[...].reshape(nb * C, SHIFT_W)
    row = lax.broadcasted_iota(jnp.int32, (nb * C, 1), 0)
    prev = pltpu.roll(p, 1, 0)
    for i in range(nb):
        prev = jnp.where(row == i * C, last_ref[i, 0:1, :], prev)
    vals = _rwkv_prep_values(p, prev, mu_ref, w0_ref, wd_ref, a0_ref, wa_ref, kk_ref, ka_ref, rk_ref)
    bonus = vals[6]

    lane = lax.broadcasted_iota(jnp.int32, (C, RWKV_W), 1)
    t_idx = lax.broadcasted_iota(jnp.int32, (C, RWKV_W), 0)
    s_idx = lane & (RWKV_HEAD - 1)
    row_big = lax.broadcasted_iota(jnp.int32, (RWKV_W, RWKV_W), 0)
    lane_big = lax.broadcasted_iota(jnp.int32, (RWKV_W, RWKV_W), 1)
    bd_mask = _group_index(row_big, RWKV_HEAD) == _group_index(lane_big, RWKV_HEAD)
    ones_bd = jnp.where(bd_mask, 1.0, 0.0).astype(BF16)
    tri_r = lax.broadcasted_iota(jnp.int32, (C, C), 0)
    tri_c = lax.broadcasted_iota(jnp.int32, (C, C), 1)
    masks = (bd_mask, jnp.where(tri_r >= tri_c, 1.0, 0.0).astype(BF16), t_idx > s_idx, t_idx >= s_idx,
             _group_index(t_idx, RW_SUB) == _group_index(s_idx, RW_SUB), t_idx == s_idx, ones_bd)

    per_seq = [tuple(x[i * C:(i + 1) * C] for x in vals[:6]) for i in range(nb)]
    outs, n_new = _rwkv_chunks(per_seq, [state_ref[i] for i in range(nb)], masks)
    for i in range(nb):
        state_ref[i] = n_new[i]
        last_ref[i, 0:1, :] = p[(i + 1) * C - 1:(i + 1) * C, :]
    o = jnp.concatenate(outs, axis=0)

    c = o - _head_sum(o, ones_bd) * (1.0 / RWKV_HEAD)
    var = _head_sum(c * c, ones_bd) * (1.0 / RWKV_HEAD)
    on = c * lax.rsqrt(var + GN_EPS) * lnxg_ref[...] + lnxb_ref[...]
    o_ref[...] = ((on + bonus) * rg_ref[...].reshape(nb * C, RWKV_W)).astype(BF16).reshape(nb, C, RWKV_W)

    @pl.when(c_idx == pl.num_programs(1) - 1)
    def _():
        sfin_ref[...] = state_ref[...]


def _rwkv_prompt(rw_in, rg, lw, batch, seq, nb):
    const = lambda b, c: (0, 0)
    blk = lambda b, c: (b, c, 0)
    o, s_fin = pl.pallas_call(
        _rwkv_prompt_kernel,
        grid=(batch // nb, seq // RW_CHUNK),
        in_specs=[pl.BlockSpec((nb, RW_CHUNK, SHIFT_W), blk), pl.BlockSpec((nb, RW_CHUNK, RWKV_W), blk)]
        + _rwkv_param_specs(const) + [pl.BlockSpec((1, RWKV_W), const)] * 2,
        out_specs=[pl.BlockSpec((nb, RW_CHUNK, RWKV_W), blk),
                   pl.BlockSpec((nb, RWKV_HEAD, RWKV_W), lambda b, c: (b, 0, 0))],
        out_shape=[jax.ShapeDtypeStruct((batch, seq, RWKV_W), BF16),
                   jax.ShapeDtypeStruct((batch, RWKV_HEAD, RWKV_W), F32)],
        scratch_shapes=[pltpu.VMEM((nb, RWKV_HEAD, RWKV_W), F32), pltpu.VMEM((nb, SUBLANES, SHIFT_W), F32)],
        compiler_params=_cparams("arbitrary", "arbitrary"),
        name="rwkv_prompt",
    )(rw_in.reshape(batch, seq, SHIFT_W), rg.reshape(batch, seq, RWKV_W), *_rwkv_params(lw),
      lw["lnx_g"], lw["lnx_b"])
    wkv = jnp.transpose(s_fin.reshape(batch, RWKV_HEAD, RWKV_HEADS, RWKV_HEAD), (0, 2, 3, 1))
    return o.reshape(batch * seq, RWKV_W), wkv


def _rwkv_step_kernel(x_ref, s_ref, *rest, n_prev):
    if n_prev:
        prev_ref, o_ref, snew_ref = rest
    else:
        o_ref, snew_ref = rest
    for j in range(n_prev):
        snew_ref[j] = prev_ref[j]
    kk, b, d, kx, r = (x_ref[j, 0] for j in range(5))
    for vi in range(RWKV_HEAD):
        sv = s_ref[0, 0, vi]
        sa = jnp.sum(sv * kk, axis=0, keepdims=True)
        sn = sv * d - sa * b + x_ref[5, 0, pl.ds(vi, 1), :] * kx
        snew_ref[n_prev, 0, vi] = sn
        o_ref[0, pl.ds(vi, 1), :] = jnp.sum(sn * r, axis=0, keepdims=True)


def _rwkv_step(prep, state_t, layer, prev_stack):
    r, k_mod, v, kk, b, d, _ = prep
    H, N = RWKV_HEADS, RWKV_HEAD
    batch = r.shape[0]
    n_prev = 0 if prev_stack is None else prev_stack.shape[0]
    x = jnp.transpose(jnp.stack([kk, b, d, k_mod, r, v]).reshape(6, batch, H, N), (0, 2, 3, 1))
    head_blk = lambda h: (0, h, 0, 0, 0)
    in_specs = [pl.BlockSpec((6, 1, N, batch), lambda h: (0, h, 0, 0)),
                pl.BlockSpec((1, 1, N, N, batch), lambda h: (layer, h, 0, 0, 0))]
    args = [x, state_t]
    if n_prev:
        in_specs.append(pl.BlockSpec((n_prev, 1, N, N, batch), head_blk))
        args.append(prev_stack)
    o, stack = pl.pallas_call(
        functools.partial(_rwkv_step_kernel, n_prev=n_prev),
        grid=(H,),
        in_specs=in_specs,
        out_specs=[pl.BlockSpec((1, N, batch), lambda h: (h, 0, 0)),
                   pl.BlockSpec((n_prev + 1, 1, N, N, batch), head_blk)],
        out_shape=[jax.ShapeDtypeStruct((H, N, batch), F32),
                   jax.ShapeDtypeStruct((n_prev + 1, H, N, N, batch), F32)],
        compiler_params=_cparams("arbitrary"),
        name="rwkv_step",
    )(*args)
    return jnp.transpose(o, (2, 0, 1)).reshape(batch, H * N), stack


def _mix_out(mixed, x_ref, mod_ref, wout_ref, gpost_ref, y_ref):
    out = jnp.dot(mixed, wout_ref[...], preferred_element_type=F32)
    gate = _rows2d(mod_ref)[:, 2 * D_MODEL:]
    y_ref[...] = x_ref[...] + gate * _rms(out, gpost_ref[...])


def _out_prompt_kernel(om_ref, orw_ref, gm_ref, x_ref, mod_ref, wout_ref, gpost_ref, y_ref):
    mixed = jnp.concatenate([om_ref[...], orw_ref[...], gm_ref[...]], axis=-1)
    _mix_out(mixed, x_ref, mod_ref, wout_ref, gpost_ref, y_ref)


def _out_sample_kernel(lat_ref, mg_ref, orw_ref, bonus_ref, rg_ref, gm_ref, x_ref, mod_ref, wout_ref, gpost_ref,
                       lnxg_ref, lnxb_ref, wuv_ref, y_ref):
    lat = lat_ref[...]
    o_mla = jnp.concatenate(
        [jnp.dot(lat[:, h * KV_LORA:(h + 1) * KV_LORA].astype(BF16), wuv_ref[h],
                 preferred_element_type=F32) for h in range(MLA_HEADS)], axis=-1)
    o_mla = (o_mla * mg_ref[...].astype(F32)).astype(BF16)
    ones_bd = _head_ones(RWKV_W, RWKV_HEAD)
    o = orw_ref[...]
    c = o - _head_sum(o, ones_bd) * (1.0 / RWKV_HEAD)
    var = _head_sum(c * c, ones_bd) * (1.0 / RWKV_HEAD)
    on = c * lax.rsqrt(var + GN_EPS) * lnxg_ref[...] + lnxb_ref[...]
    o_rw = ((on + bonus_ref[...]) * rg_ref[...]).astype(BF16)
    mixed = jnp.concatenate([o_mla, o_rw, gm_ref[...]], axis=-1)
    _mix_out(mixed, x_ref, mod_ref, wout_ref, gpost_ref, y_ref)


def _out_prompt(o_mla, o_rw, gm, x2d, mod_b, lw, seq, tm):
    rows = x2d.shape[0]
    tiles_per_seq = seq // tm
    const = lambda i: (0, 0)
    row_blk = lambda i: (i, 0)
    return pl.pallas_call(
        _out_prompt_kernel,
        grid=(rows // tm,),
        in_specs=[
            pl.BlockSpec((tm, MLA_W), row_blk),
            pl.BlockSpec((tm, RWKV_W), row_blk),
            pl.BlockSpec((tm, GM_W), row_blk),
            pl.BlockSpec((tm, D_MODEL), row_blk),
            pl.BlockSpec((1, 1, 3 * D_MODEL), lambda i: (i // tiles_per_seq, 0, 0)),
            pl.BlockSpec((D_MODEL, D_MODEL), const),
            pl.BlockSpec((1, D_MODEL), const),
        ],
        out_specs=pl.BlockSpec((tm, D_MODEL), row_blk),
        out_shape=jax.ShapeDtypeStruct((rows, D_MODEL), F32),
        compiler_params=_cparams("arbitrary"),
        name="out_prompt",
    )(o_mla, o_rw, gm, x2d, mod_b, lw["w_out"], lw["g_post"])


def _out_sample(o_lat, mg, o_rw, bonus, rg, gm, x2d, mod_rows, lw):
    rows = x2d.shape[0]
    const = lambda i: (0, 0)
    full = lambda w: pl.BlockSpec((rows, w), const)
    return pl.pallas_call(
        _out_sample_kernel,
        grid=(1,),
        in_specs=[
            full(MLA_HEADS * KV_LORA), full(MLA_W), full(RWKV_W), full(RWKV_W), full(RWKV_W), full(GM_W),
            full(D_MODEL), full(3 * D_MODEL),
            pl.BlockSpec((D_MODEL, D_MODEL), const),
            pl.BlockSpec((1, D_MODEL), const),
            pl.BlockSpec((1, RWKV_W), const),
            pl.BlockSpec((1, RWKV_W), const),
            pl.BlockSpec((MLA_HEADS, KV_LORA, MLA_V), lambda i: (0, 0, 0)),
        ],
        out_specs=full(D_MODEL),
        out_shape=jax.ShapeDtypeStruct((rows, D_MODEL), F32),
        compiler_params=_cparams("arbitrary"),
        name="out_sample",
    )(o_lat, mg, o_rw, bonus, rg, gm, x2d, mod_rows, lw["w_out"], lw["g_post"], lw["lnx_g"], lw["lnx_b"],
      lw["w_uv_h"])


def _swap_halves(w):
    half = MLA_ROPE // 2
    return jnp.concatenate([w[..., half:], w[..., :half]], axis=-1)


def _rope_tables(pos):
    half = MLA_ROPE // 2
    inv = ROPE_THETA ** (-jnp.arange(half, dtype=F32) / half)
    ang = pos.astype(F32)[:, None] * inv[None, :]
    cos, sin = jnp.cos(ang), jnp.sin(ang)
    zeros = jnp.zeros((pos.shape[0], LANES - MLA_ROPE), F32)
    return (jnp.concatenate([cos, cos, zeros], axis=-1), jnp.concatenate([-sin, sin, zeros], axis=-1))


def _layer_weights(l, w_in, norm_pre_g, norm_post_g, q_norm_g, kv_norm_g, w_uq, w_uk, w_uv, rw_mu, rw_w0,
                   rw_w_decay_up, rw_a0, rw_w_a_up, rw_k_k, rw_k_a, rw_r_k, rw_lnx_g, rw_lnx_b, gm_ln_g,
                   gm_ln_b, gm_w_s, gm_b_s, w_out):
    w = w_in[l]
    split_lo, split_hi = Q_LORA + KV_LORA, Q_LORA + KV_LORA + MLA_ROPE
    kr = w[:, split_lo:split_hi]
    w_in_p = jnp.concatenate([w[:, :Q_LORA], kr, _swap_halves(kr), w[:, Q_LORA:split_lo], w[:, split_hi:]],
                             axis=1).astype(BF16)
    uq = w_uq[l]
    pe = uq[..., MLA_NOPE:]
    w_uq_p = jnp.concatenate([uq, _swap_halves(pe)], axis=-1).reshape(Q_LORA, MLA_HEADS * QHEAD_W).astype(BF16)
    zeros_lora = jnp.zeros((LANES - DECAY_LORA, RWKV_W), F32)
    row = lambda a: a.reshape(1, -1)
    return {
        "w_in": w_in_p, "g_pre": row(norm_pre_g[l]), "g_post": row(norm_post_g[l]),
        "g_q": row(q_norm_g[l]), "g_kv": row(kv_norm_g[l]), "w_uq": w_uq_p,
        "w_uk": w_uk[l].reshape(KV_LORA, MLA_W).astype(BF16),
        "w_uv": w_uv[l].reshape(KV_LORA, MLA_W).astype(BF16),
        "w_ukt": jnp.transpose(w_uk[l], (1, 2, 0)).astype(BF16),
        "w_uv_h": jnp.transpose(w_uv[l], (1, 0, 2)).astype(BF16),
        "rw_mu": row(rw_mu[l]), "rw_w0": row(rw_w0[l]), "rw_a0": row(rw_a0[l]),
        "rw_wd": jnp.concatenate([rw_w_decay_up[l], zeros_lora], axis=0).astype(BF16),
        "rw_wa": jnp.concatenate([zeros_lora, rw_w_a_up[l]], axis=0).astype(BF16),
        "rw_k_k": row(rw_k_k[l]), "rw_k_a": row(rw_k_a[l]), "rw_r_k": row(rw_r_k[l]),
        "lnx_g": row(rw_lnx_g[l]), "lnx_b": row(rw_lnx_b[l]),
        "ln_g": row(gm_ln_g[l]), "ln_b": row(gm_ln_b[l]),
        "gm_ws": gm_w_s[l],
        "gm_bias": jnp.repeat(gm_b_s[l].T, GM_GROUP_W, axis=1),
        "gm_coef0": row(jnp.repeat(gm_w_s[l][:, 0, 0], GM_GROUP_W)),
        "gm_bias0": row(jnp.repeat(gm_b_s[l][:, 0], GM_GROUP_W)),
        "w_out": w_out[l].astype(BF16),
    }


def _pick(full, want):
    return want if full % want == 0 else full


def kernel(x_prompt, x_sample, c_prompt, c_sample, cache_ckv, cache_kpe, page_table, state_wkv, state_shift, w_ada, b_ada, norm_pre_g, norm_post_g, w_in, q_norm_g, kv_norm_g, w_uq, w_uk, w_uv, rw_mu, rw_w0, rw_w_decay_up, rw_a0, rw_w_a_up, rw_k_k, rw_k_a, rw_r_k, rw_lnx_g, rw_lnx_b, gm_ln_g, gm_ln_b, gm_w_s, gm_b_s, w_out):
    batch, seq, _ = x_prompt.shape
    dec_batch, dec_seq, _ = x_sample.shape
    depth = w_in.shape[0]
    n_pages = page_table.shape[1]
    past_len = n_pages * cache_ckv.shape[2]
    assert dec_seq == 1 and seq % CHUNK == 0

    assert seq % RW_CHUNK == 0
    tm_proj = _pick(seq, 512)
    tm_out = _pick(seq, 1024)
    tq = _pick(seq, 512)
    rw_nb = _pick(batch, 8)

    mod = _ada_mod(jnp.concatenate([c_prompt, c_sample], axis=0), w_ada, b_ada)
    cs_p, sn_p = _rope_tables(jnp.arange(seq))
    cs_s, sn_s = _rope_tables(jnp.full((dec_batch,), past_len))
    cache_kpe_t = jnp.swapaxes(cache_kpe, 2, 3)

    y_p = x_prompt.reshape(batch * seq, D_MODEL)
    y_s = x_sample.reshape(dec_batch, D_MODEL)
    outs = {k: [] for k in ("ckv_s", "kpe_s", "wkv_p", "sh_p", "sh_s", "vc_p", "vc_s")}
    prompt_stacks = None
    wkv_s_stack = None
    state_wkv_t = jnp.transpose(state_wkv, (0, 2, 3, 4, 1))
    for l in range(depth):
        lw = _layer_weights(l, w_in, norm_pre_g, norm_post_g, q_norm_g, kv_norm_g, w_uq, w_uk, w_uv, rw_mu,
                            rw_w0, rw_w_decay_up, rw_a0, rw_w_a_up, rw_k_k, rw_k_a, rw_r_k, rw_lnx_g,
                            rw_lnx_b, gm_ln_g, gm_ln_b, gm_w_s, gm_b_s, w_out)
        mod_p = mod[l, :batch].reshape(batch, 1, 3 * D_MODEL)
        q, k, v, mg, ckv_stack, kpe_stack, rw_in, rg, gm, vn_last = _proj_prompt(
            y_p, mod_p, lw, cs_p, sn_p, batch, seq, tm_proj, prompt_stacks)
        prompt_stacks = (ckv_stack, kpe_stack)
        o_mla = _flash(q, k, v, mg, batch, seq, tq)
        o_rw, wkv_new = _rwkv_prompt(rw_in, rg, lw, batch, seq, rw_nb)
        y_p = _out_prompt(o_mla, o_rw, gm, y_p, mod_p, lw, seq, tm_out)
        outs["wkv_p"].append(wkv_new)
        outs["sh_p"].append(rw_in.reshape(batch, seq, SHIFT_W)[:, -1])
        outs["vc_p"].append(vn_last)

        mod_s = mod[l, batch:]
        qa, qp, mg_s, ckv_s, kpe_s, rw_s, rg_s, gm_s, vn_s = _proj_sample(y_s, mod_s, lw, cs_s, sn_s)
        o_lat = _paged_attention(qa, qp, ckv_s, kpe_s, cache_ckv, cache_kpe_t, page_table, l)
        prep_s = _rwkv_prep_sample(rw_s, lw, state_shift[l])
        o_rw_s, wkv_s_stack = _rwkv_step(prep_s, state_wkv_t, l, wkv_s_stack)
        y_s = _out_sample(o_lat, mg_s, o_rw_s, prep_s[6], rg_s, gm_s, y_s, mod_s, lw)
        outs["ckv_s"].append(ckv_s.reshape(dec_batch, 1, KV_LORA))
        outs["kpe_s"].append(kpe_s[:, :MLA_ROPE].reshape(dec_batch, 1, MLA_ROPE))
        outs["sh_s"].append(rw_s)
        outs["vc_s"].append(vn_s.reshape(dec_batch, 1, GM_W))

    st = lambda name: jnp.stack(outs[name])
    ckv_p, kpe_p = prompt_stacks
    return (y_p.reshape(batch, seq, D_MODEL), y_s.reshape(dec_batch, 1, D_MODEL),
            ckv_p.reshape(depth, batch, seq, KV_LORA), kpe_p.reshape(depth, batch, seq, MLA_ROPE),
            st("ckv_s"), st("kpe_s"), st("wkv_p"), jnp.transpose(wkv_s_stack, (0, 4, 1, 2, 3)),
            st("sh_p"), st("sh_s"), st("vc_p"), st("vc_s"))
```

```python
import functools

import numpy as np
import jax
import jax.numpy as jnp
from jax import lax
from jax.experimental import pallas as pl
from jax.experimental.pallas import tpu as pltpu

F32 = jnp.float32
BF16 = jnp.bfloat16

D_MODEL = 1024
MLA_V = 128
MLA_W = D_MODEL // 2
MLA_HEADS = MLA_W // MLA_V
MLA_NOPE = 128
MLA_ROPE = 64
MLA_QK = MLA_NOPE + MLA_ROPE
Q_LORA = (3 * D_MODEL) // 8
KV_LORA = D_MODEL // 4
ROPE_THETA = 10000.0
ATTN_SCALE = MLA_QK ** -0.5
Q_SCALE = ATTN_SCALE * float(np.log2(np.e))
RWKV_W = D_MODEL // 4
RWKV_HEAD = 64
RWKV_HEADS = RWKV_W // RWKV_HEAD
DECAY_LORA = 64
ICLR_LORA = 64
SHIFT_W = 3 * RWKV_W + DECAY_LORA + ICLR_LORA
GN_EPS = 64e-5
GM_W = D_MODEL // 4
GM_GROUPS = 4
GM_GROUP_W = GM_W // GM_GROUPS
CHUNK = 128
RMS_EPS = 1e-6
LN_EPS = 1e-5

LANES = 128
SUBLANES = 8
VMEM_LIMIT_BYTES = 56 * 1024 * 1024

QHEAD_W = 2 * LANES
OFF_Q = 0
OFF_KR = OFF_Q + Q_LORA
OFF_KV = OFF_KR + LANES
OFF_MG = OFF_KV + KV_LORA
OFF_RW = OFF_MG + MLA_W
OFF_RG = OFF_RW + SHIFT_W
OFF_GU = OFF_RG + RWKV_W
OFF_GV = OFF_GU + GM_W
OFF_GG = OFF_GV + GM_W
IN_W_P = OFF_GG + GM_W

NEG_BIG = -1e30


def _cparams(*sem):
    return pltpu.CompilerParams(dimension_semantics=sem, vmem_limit_bytes=VMEM_LIMIT_BYTES)


def _silu(x):
    return x * jax.nn.sigmoid(x)


def _rms(x, g, eps=RMS_EPS):
    return x * lax.rsqrt(jnp.mean(x * x, axis=-1, keepdims=True) + eps) * g


def _rows2d(ref):
    m = ref[...]
    return m.reshape(m.shape[-2], m.shape[-1])


def _group_index(idx, group):
    shift = group.bit_length() - 1
    assert 1 << shift == group
    return lax.shift_right_logical(idx, shift)


def _head_ones(width, head):
    r = _group_index(lax.broadcasted_iota(jnp.int32, (width, width), 0), head)
    c = _group_index(lax.broadcasted_iota(jnp.int32, (width, width), 1), head)
    return jnp.where(r == c, 1.0, 0.0).astype(BF16)


def _head_sum(x, ones_bd):
    hi = x.astype(BF16)
    lo = (x - hi.astype(F32)).astype(BF16)
    return (jnp.dot(hi, ones_bd, preferred_element_type=F32)
            + jnp.dot(lo, ones_bd, preferred_element_type=F32))


def _rope_pair(blk, cs, sn):
    return blk * cs + pltpu.roll(blk, MLA_ROPE, 1) * sn


def _ada_kernel(c_ref, w_ref, b_ref, o_ref):
    c = _silu(c_ref[...]).astype(BF16)
    o_ref[0] = jnp.dot(c, w_ref[0].astype(BF16), preferred_element_type=F32) + b_ref[0]


def _ada_mod(c_all, w_ada, b_ada):
    depth = w_ada.shape[0]
    n_rows = c_all.shape[0]
    n_tiles = (3 * D_MODEL) // D_MODEL
    return pl.pallas_call(
        _ada_kernel,
        grid=(depth, n_tiles),
        in_specs=[
            pl.BlockSpec((n_rows, D_MODEL), lambda l, j: (0, 0)),
            pl.BlockSpec((1, D_MODEL, D_MODEL), lambda l, j: (l, 0, j)),
            pl.BlockSpec((1, 1, D_MODEL), lambda l, j: (l, 0, j)),
        ],
        out_specs=pl.BlockSpec((1, n_rows, D_MODEL), lambda l, j: (l, 0, j)),
        out_shape=jax.ShapeDtypeStruct((depth, n_rows, 3 * D_MODEL), F32),
        compiler_params=_cparams("arbitrary", "arbitrary"),
        name="ada_mod",
    )(c_all, w_ada, b_ada.reshape(depth, 1, 3 * D_MODEL))


def _proj_common(x_ref, mod_ref, gpre_ref, win_ref):
    mod = _rows2d(mod_ref)
    shift = mod[:, 0:D_MODEL]
    scale = mod[:, D_MODEL:2 * D_MODEL]
    h = _rms(x_ref[...], gpre_ref[...]) * (1.0 + scale) + shift
    hb = h.astype(BF16)

    def seg(off, width):
        return jnp.dot(hb, win_ref[:, off:off + width], preferred_element_type=F32)

    return seg


def _q_heads(seg, gq_ref, wuq_ref, cs, sn):
    q_kr = seg(OFF_Q, Q_LORA + LANES)
    qn = _rms(q_kr[:, :Q_LORA], gq_ref[...]).astype(BF16)
    q_all = jnp.dot(qn, wuq_ref[...], preferred_element_type=F32)
    heads = []
    for h in range(MLA_HEADS):
        nope = q_all[:, h * QHEAD_W:h * QHEAD_W + LANES] * Q_SCALE
        pe = _rope_pair(q_all[:, h * QHEAD_W + LANES:(h + 1) * QHEAD_W], cs, sn) * Q_SCALE
        heads.append((nope, pe))
    return heads, _rope_pair(q_kr[:, Q_LORA:], cs, sn)


def _gm_norm(seg, lng_ref, lnb_ref):
    v = seg(OFF_GV, GM_W)
    mu = jnp.mean(v, axis=-1, keepdims=True)
    c = v - mu
    var = jnp.mean(c * c, axis=-1, keepdims=True)
    return c * lax.rsqrt(var + LN_EPS) * lng_ref[...] + lnb_ref[...]


def _proj_prompt_kernel(x_ref, mod_ref, gpre_ref, win_ref, gq_ref, wuq_ref, gkv_ref, wuk_ref, wuv_ref,
                        cs_ref, sn_ref, lng_ref, lnb_ref, ws_ref, bs_ref, *rest, tiles_per_seq, n_prev):
    if n_prev:
        ckv_prev_ref, kpe_prev_ref = rest[:2]
        rest = rest[2:]
    q_ref, k_ref, v_ref, mg_ref, ckv_ref, kpe_ref, rw_ref, rg_ref, gm_ref, vn_ref = rest
    for j in range(n_prev):
        ckv_ref[j] = ckv_prev_ref[j]
        kpe_ref[j] = kpe_prev_ref[j]
    seg = _proj_common(x_ref, mod_ref, gpre_ref, win_ref)
    cs = cs_ref[...]
    sn = sn_ref[...]
    tm = x_ref.shape[0]

    heads, kpe = _q_heads(seg, gq_ref, wuq_ref, cs, sn)
    for h, (nope, pe) in enumerate(heads):
        q_ref[:, h * QHEAD_W:h * QHEAD_W + LANES] = nope.astype(BF16)
        q_ref[:, h * QHEAD_W + LANES:(h + 1) * QHEAD_W] = pe.astype(BF16)
    ckv = _rms(seg(OFF_KV, KV_LORA), gkv_ref[...])
    ckv_ref[n_prev] = ckv
    kpe_ref[n_prev] = kpe[:, :MLA_ROPE]
    ckv_b = ckv.astype(BF16)
    k_nope = jnp.dot(ckv_b, wuk_ref[...], preferred_element_type=F32)
    v_ref[...] = jnp.dot(ckv_b, wuv_ref[...], preferred_element_type=F32).astype(BF16)
    kpe_b = kpe.astype(BF16)
    for h in range(MLA_HEADS):
        k_ref[:, h * QHEAD_W:h * QHEAD_W + LANES] = k_nope[:, h * MLA_NOPE:(h + 1) * MLA_NOPE].astype(BF16)
        k_ref[:, h * QHEAD_W + LANES:(h + 1) * QHEAD_W] = kpe_b
    mg_ref[...] = _silu(seg(OFF_MG, MLA_W)).astype(BF16)

    rw_ref[...] = seg(OFF_RW, SHIFT_W)
    rg_ref[...] = _silu(seg(OFF_RG, RWKV_W))

    vn = _gm_norm(seg, lng_ref, lnb_ref)
    gate_u = _silu(seg(OFF_GG, GM_W)) * seg(OFF_GU, GM_W)
    row = lax.broadcasted_iota(jnp.int32, (CHUNK, CHUNK), 0)
    col = lax.broadcasted_iota(jnp.int32, (CHUNK, CHUNK), 1)
    w_cat = jnp.concatenate([jnp.where(row >= col, ws_ref[g], 0.0) for g in range(GM_GROUPS)],
                            axis=1).astype(BF16)
    own_group = (_group_index(lax.broadcasted_iota(jnp.int32, (GM_GROUPS * CHUNK, GM_W), 0), CHUNK)
                 == _group_index(lax.broadcasted_iota(jnp.int32, (GM_GROUPS * CHUNK, GM_W), 1), GM_GROUP_W))
    for c in range(tm // CHUNK):
        vc = vn[c * CHUNK:(c + 1) * CHUNK]
        vc_groups = jnp.where(own_group, jnp.concatenate([vc] * GM_GROUPS, axis=0), 0.0).astype(BF16)
        z = bs_ref[...] + jnp.dot(w_cat, vc_groups, preferred_element_type=F32)
        gm_ref[c * CHUNK:(c + 1) * CHUNK, :] = (gate_u[c * CHUNK:(c + 1) * CHUNK] * z).astype(BF16)

    @pl.when(pl.program_id(0) % tiles_per_seq == tiles_per_seq - 1)
    def _():
        vn_ref[0] = vn[tm - CHUNK:, :]


def _proj_prompt(x2d, mod_b, lw, cs, sn, batch, seq, tm, prev_stacks):
    rows = x2d.shape[0]
    tiles_per_seq = seq // tm
    n_prev = 0 if prev_stacks is None else prev_stacks[0].shape[0]
    const = lambda i: (0, 0)
    row_blk = lambda i: (i, 0)
    stack_blk = lambda i: (0, i, 0)
    outs = [
        (QHEAD_W * MLA_HEADS, BF16), (QHEAD_W * MLA_HEADS, BF16), (MLA_W, BF16), (MLA_W, BF16),
        (KV_LORA, F32), (MLA_ROPE, F32), (SHIFT_W, F32), (RWKV_W, F32), (GM_W, BF16),
    ]
    out_shape = [jax.ShapeDtypeStruct((rows, w), dt) for w, dt in outs]
    out_specs = [pl.BlockSpec((tm, w), row_blk) for w, _ in outs]
    for idx in (4, 5):
        w, dt = outs[idx]
        out_shape[idx] = jax.ShapeDtypeStruct((n_prev + 1, rows, w), dt)
        out_specs[idx] = pl.BlockSpec((n_prev + 1, tm, w), stack_blk)
    out_shape.append(jax.ShapeDtypeStruct((batch, CHUNK, GM_W), F32))
    out_specs.append(pl.BlockSpec((1, CHUNK, GM_W), lambda i: (i // tiles_per_seq, 0, 0)))
    prev_specs = [] if not n_prev else [pl.BlockSpec((n_prev, tm, KV_LORA), stack_blk),
                                        pl.BlockSpec((n_prev, tm, MLA_ROPE), stack_blk)]
    return pl.pallas_call(
        functools.partial(_proj_prompt_kernel, tiles_per_seq=tiles_per_seq, n_prev=n_prev),
        grid=(rows // tm,),
        in_specs=[
            pl.BlockSpec((tm, D_MODEL), row_blk),
            pl.BlockSpec((1, 1, 3 * D_MODEL), lambda i: (i // tiles_per_seq, 0, 0)),
            pl.BlockSpec((1, D_MODEL), const),
            pl.BlockSpec((D_MODEL, IN_W_P), const),
            pl.BlockSpec((1, Q_LORA), const),
            pl.BlockSpec((Q_LORA, QHEAD_W * MLA_HEADS), const),
            pl.BlockSpec((1, KV_LORA), const),
            pl.BlockSpec((KV_LORA, MLA_W), const),
            pl.BlockSpec((KV_LORA, MLA_W), const),
            pl.BlockSpec((tm, LANES), lambda i: (i % tiles_per_seq, 0)),
            pl.BlockSpec((tm, LANES), lambda i: (i % tiles_per_seq, 0)),
            pl.BlockSpec((1, GM_W), const),
            pl.BlockSpec((1, GM_W), const),
            pl.BlockSpec((GM_GROUPS, CHUNK, CHUNK), lambda i: (0, 0, 0)),
            pl.BlockSpec((CHUNK, GM_W), const),
        ] + prev_specs,
        out_specs=out_specs,
        out_shape=out_shape,
        compiler_params=_cparams("arbitrary"),
        name="proj_prompt",
    )(x2d, mod_b, lw["g_pre"], lw["w_in"], lw["g_q"], lw["w_uq"], lw["g_kv"], lw["w_uk"], lw["w_uv"],
      cs, sn, lw["ln_g"], lw["ln_b"], lw["gm_ws"], lw["gm_bias"], *(prev_stacks or ()))


def _proj_sample_kernel(x_ref, mod_ref, gpre_ref, win_ref, gq_ref, wuq_ref, gkv_ref, wukt_ref,
                        cs_ref, sn_ref, lng_ref, lnb_ref, coef_ref, bias_ref,
                        qa_ref, qp_ref, mg_ref, ckv_ref, kpe_ref, rw_ref, rg_ref, gm_ref, vn_ref):
    seg = _proj_common(x_ref, mod_ref, gpre_ref, win_ref)
    cs = cs_ref[...]
    sn = sn_ref[...]
    heads, kpe = _q_heads(seg, gq_ref, wuq_ref, cs, sn)
    for h, (nope, pe) in enumerate(heads):
        qa_ref[:, h * KV_LORA:(h + 1) * KV_LORA] = jnp.dot(nope.astype(BF16), wukt_ref[h],
                                                           preferred_element_type=F32)
        qp_ref[:, h * LANES:(h + 1) * LANES] = pe
    ckv_ref[...] = _rms(seg(OFF_KV, KV_LORA), gkv_ref[...])
    kpe_ref[...] = kpe
    mg_ref[...] = _silu(seg(OFF_MG, MLA_W)).astype(BF16)
    rw_ref[...] = seg(OFF_RW, SHIFT_W)
    rg_ref[...] = _silu(seg(OFF_RG, RWKV_W))
    vn = _gm_norm(seg, lng_ref, lnb_ref)
    vn_ref[...] = vn
    z = vn * coef_ref[...] + bias_ref[...]
    gm_ref[...] = (_silu(seg(OFF_GG, GM_W)) * seg(OFF_GU, GM_W) * z).astype(BF16)


def _proj_sample(x2d, mod_rows, lw, cs, sn):
    rows = x2d.shape[0]
    const = lambda i: (0, 0)
    outs = [
        (KV_LORA * MLA_HEADS, F32), (LANES * MLA_HEADS, F32), (MLA_W, BF16), (KV_LORA, F32), (LANES, F32),
        (SHIFT_W, F32), (RWKV_W, F32), (GM_W, BF16), (GM_W, F32),
    ]
    return pl.pallas_call(
        _proj_sample_kernel,
        grid=(1,),
        in_specs=[
            pl.BlockSpec((rows, D_MODEL), const),
            pl.BlockSpec((rows, 3 * D_MODEL), const),
            pl.BlockSpec((1, D_MODEL), const),
            pl.BlockSpec((D_MODEL, IN_W_P), const),
            pl.BlockSpec((1, Q_LORA), const),
            pl.BlockSpec((Q_LORA, QHEAD_W * MLA_HEADS), const),
            pl.BlockSpec((1, KV_LORA), const),
            pl.BlockSpec((MLA_HEADS, MLA_NOPE, KV_LORA), lambda i: (0, 0, 0)),
            pl.BlockSpec((rows, LANES), const),
            pl.BlockSpec((rows, LANES), const),
            pl.BlockSpec((1, GM_W), const),
            pl.BlockSpec((1, GM_W), const),
            pl.BlockSpec((1, GM_W), const),
            pl.BlockSpec((1, GM_W), const),
        ],
        out_specs=[pl.BlockSpec((rows, w), const) for w, _ in outs],
        out_shape=[jax.ShapeDtypeStruct((rows, w), dt) for w, dt in outs],
        compiler_params=_cparams("arbitrary"),
        name="proj_sample",
    )(x2d, mod_rows, lw["g_pre"], lw["w_in"], lw["g_q"], lw["w_uq"], lw["g_kv"], lw["w_ukt"],
      cs, sn, lw["ln_g"], lw["ln_b"], lw["gm_coef0"], lw["gm_bias0"])


def _flash_kernel(q_ref, k_ref, v_ref, g_ref, o_ref, *, tq):
    seq = q_ref.shape[1]
    row = lax.broadcasted_iota(jnp.int32, (tq, tq), 0)
    col = lax.broadcasted_iota(jnp.int32, (tq, tq), 1)
    for qi in range(seq // tq):
        rows = slice(qi * tq, (qi + 1) * tq)
        q = q_ref[0, rows, :]
        m = jnp.full((tq, 1), NEG_BIG, F32)
        l = jnp.zeros((tq, 1), F32)
        acc = jnp.zeros((tq, MLA_V), F32)
        for ki in range(qi + 1):
            cols = slice(ki * tq, (ki + 1) * tq)
            s = lax.dot_general(q, k_ref[0, cols, :], (((1,), (1,)), ((), ())),
                                preferred_element_type=F32)
            if ki == qi:
                s = jnp.where(row >= col, s, NEG_BIG)
            m_new = jnp.maximum(m, jnp.max(s, axis=-1, keepdims=True))
            alpha = jnp.exp2(m - m_new)
            p = jnp.exp2(s - m_new)
            l = alpha * l + jnp.sum(p, axis=-1, keepdims=True)
            acc = alpha * acc + jnp.dot(p.astype(BF16), v_ref[0, cols, :], preferred_element_type=F32)
            m = m_new
        o_ref[0, rows, :] = (acc / l * g_ref[0, rows, :].astype(F32)).astype(BF16)


def _flash(q, k, v, g, batch, seq, tq):
    q3 = q.reshape(batch, seq, QHEAD_W * MLA_HEADS)
    k3 = k.reshape(batch, seq, QHEAD_W * MLA_HEADS)
    v3 = v.reshape(batch, seq, MLA_W)
    g3 = g.reshape(batch, seq, MLA_W)
    head_blk = lambda b, h: (b, 0, h)
    out = pl.pallas_call(
        functools.partial(_flash_kernel, tq=tq),
        grid=(batch, MLA_HEADS),
        in_specs=[
            pl.BlockSpec((1, seq, QHEAD_W), head_blk),
            pl.BlockSpec((1, seq, QHEAD_W), head_blk),
            pl.BlockSpec((1, seq, MLA_V), head_blk),
            pl.BlockSpec((1, seq, MLA_V), head_blk),
        ],
        out_specs=pl.BlockSpec((1, seq, MLA_V), head_blk),
        out_shape=jax.ShapeDtypeStruct((batch, seq, MLA_W), BF16),
        compiler_params=_cparams("arbitrary", "arbitrary"),
        name="flash_prompt",
    )(q3, k3, v3, g3)
    return out.reshape(batch * seq, MLA_W)


N_SLOTS = 3
LOOKAHEAD = N_SLOTS - 1


def _paged_kernel(pt_ref, qa_ref, qp_ref, cn_ref, kn_ref, ckv_hbm, kpe_hbm, o_ref, ckv_buf, kpe_buf, sem,
                  *, layer, n_pages, page, group_tokens):
    b = pl.program_id(0)
    n_seq = pl.num_programs(0)
    slot = lax.rem(b, N_SLOTS)

    def page_copies(seq_idx, sl):
        copies = []
        for i in range(n_pages):
            pg = pt_ref[seq_idx, i]
            copies.append(pltpu.make_async_copy(
                ckv_hbm.at[layer, pg], ckv_buf.at[sl, pl.ds(i * page, page), :], sem.at[0, sl]))
            copies.append(pltpu.make_async_copy(kpe_hbm.at[layer, pg], kpe_buf.at[sl, i], sem.at[1, sl]))
        return copies

    def start_all(copies):
        for i, c in enumerate(copies):
            c.start(priority=(i // 2) % 2)

    @pl.when(b == 0)
    def _():
        for ahead in range(LOOKAHEAD):
            start_all(page_copies(ahead, ahead))

    for c in page_copies(b, slot):
        c.wait()

    nxt = lax.rem(b + LOOKAHEAD, n_seq)
    nxt_slot = lax.rem(b + LOOKAHEAD, N_SLOTS)
    start_all(page_copies(nxt, nxt_slot))

    qa = qa_ref[0]
    qp = qp_ref[0][:, :MLA_ROPE]
    qa_b = qa.astype(BF16)
    qp_b = qp.astype(BF16)
    group_pages = group_tokens // page
    groups = [slice(g * group_tokens, (g + 1) * group_tokens) for g in range(n_pages // group_pages)]
    ckv = [ckv_buf[slot, gs, :].astype(BF16) for gs in groups]
    kpe_t = [jnp.concatenate([kpe_buf[slot, g * group_pages + i] for i in range(group_pages)],
                             axis=1).astype(BF16) for g in range(len(groups))]
    s = [lax.dot_general(qa_b, ck, (((1,), (1,)), ((), ())), preferred_element_type=F32)
         + jnp.dot(qp_b, kp, preferred_element_type=F32) for ck, kp in zip(ckv, kpe_t)]
    m_g = [jnp.max(sg, axis=-1, keepdims=True) for sg in s]
    p = [jnp.exp2(sg - mg) for sg, mg in zip(s, m_g)]
    l_g = [jnp.sum(pg, axis=-1, keepdims=True) for pg in p]
    acc_g = [jnp.dot(pg.astype(BF16), ck, preferred_element_type=F32) for pg, ck in zip(p, ckv)]
    cn = cn_ref[0]
    kn = kn_ref[0][:, :MLA_ROPE]
    s_new = jnp.sum(qa * cn, axis=-1, keepdims=True) + jnp.sum(qp * kn, axis=-1, keepdims=True)
    m = functools.reduce(jnp.maximum, m_g, s_new)
    p_new = jnp.exp2(s_new - m)
    w_g = [jnp.exp2(mg - m) for mg in m_g]
    l = sum(wg * lg for wg, lg in zip(w_g, l_g)) + p_new
    o_ref[0] = (sum(wg * ag for wg, ag in zip(w_g, acc_g)) + p_new * cn) / l

    @pl.when(b == n_seq - 1)
    def _():
        for ahead in range(1, LOOKAHEAD + 1):
            for c in page_copies(lax.rem(b + ahead, n_seq), lax.rem(b + ahead, N_SLOTS)):
                c.wait()


def _paged_attention(qa, qp, ckv_new, kpe_new, cache_ckv, cache_kpe_t, page_table, layer):
    dec_batch, n_pages = page_table.shape
    page = cache_ckv.shape[2]
    head_pad = ((0, 0), (0, SUBLANES - MLA_HEADS), (0, 0))
    qa3 = jnp.pad(qa.reshape(dec_batch, MLA_HEADS, KV_LORA), head_pad)
    qp3 = jnp.pad(qp.reshape(dec_batch, MLA_HEADS, LANES), head_pad)
    cn3 = ckv_new.reshape(dec_batch, 1, KV_LORA)
    kn3 = kpe_new.reshape(dec_batch, 1, LANES)
    row_blk = lambda b, pt: (b, 0, 0)
    out = pl.pallas_call(
        functools.partial(_paged_kernel, layer=layer, n_pages=n_pages, page=page,
                          group_tokens=_pick(n_pages * page, 2048)),
        grid_spec=pltpu.PrefetchScalarGridSpec(
            num_scalar_prefetch=1,
            grid=(dec_batch,),
            in_specs=[
                pl.BlockSpec((1, SUBLANES, KV_LORA), row_blk),
                pl.BlockSpec((1, SUBLANES, LANES), row_blk),
                pl.BlockSpec((1, 1, KV_LORA), row_blk),
                pl.BlockSpec((1, 1, LANES), row_blk),
                pl.BlockSpec(memory_space=pl.ANY),
                pl.BlockSpec(memory_space=pl.ANY),
            ],
            out_specs=pl.BlockSpec((1, SUBLANES, KV_LORA), row_blk),
            scratch_shapes=[
                pltpu.VMEM((N_SLOTS, n_pages * page, KV_LORA), F32),
                pltpu.VMEM((N_SLOTS, n_pages, MLA_ROPE, page), F32),
                pltpu.SemaphoreType.DMA((2, N_SLOTS)),
            ],
        ),
        out_shape=jax.ShapeDtypeStruct((dec_batch, SUBLANES, KV_LORA), F32),
        compiler_params=_cparams("arbitrary"),
        name="paged_attention",
    )(page_table, qa3, qp3, cn3, kn3, cache_ckv, cache_kpe_t)
    return out[:, :MLA_HEADS].reshape(dec_batch, MLA_HEADS * KV_LORA)


def _rwkv_prep_values(p, prev, mu_ref, w0_ref, wd_ref, a0_ref, wa_ref, kk_ref, ka_ref, rk_ref):
    xm = p + (prev - p) * mu_ref[...]
    r = xm[:, 0:RWKV_W]
    k = xm[:, RWKV_W:2 * RWKV_W]
    v = xm[:, 2 * RWKV_W:3 * RWKV_W]
    tail = xm[:, 3 * RWKV_W:]
    lane = lax.broadcasted_iota(jnp.int32, tail.shape, 1)
    lora_in = jnp.where(lane < DECAY_LORA, jnp.tanh(tail), tail).astype(BF16)
    dw = jnp.dot(lora_in, wd_ref[...], preferred_element_type=F32)
    da = jnp.dot(lora_in, wa_ref[...], preferred_element_type=F32)
    z = -(w0_ref[...] + dw)
    softplus = jnp.maximum(z, 0.0) + jnp.log(1.0 + jnp.exp(-jnp.abs(z)))
    w = -softplus - 0.5
    log_decay = -jnp.exp(w)
    a = jax.nn.sigmoid(a0_ref[...] + da)
    ones_bd = _head_ones(RWKV_W, RWKV_HEAD)
    kk = k * kk_ref[...]
    kk = kk / jnp.maximum(jnp.sqrt(_head_sum(kk * kk, ones_bd)), 1e-12)
    k_mod = k * (1.0 + (a - 1.0) * ka_ref[...])
    bonus = _head_sum(r * k_mod * rk_ref[...], ones_bd) * v
    return r, k_mod, v, kk, kk * a, log_decay, bonus


def _rwkv_param_specs(index_map):
    widths = (SHIFT_W, RWKV_W, None, RWKV_W, None, RWKV_W, RWKV_W, RWKV_W)
    return [pl.BlockSpec((LANES, RWKV_W) if w is None else (1, w), index_map) for w in widths]


def _rwkv_params(lw):
    return (lw["rw_mu"], lw["rw_w0"], lw["rw_wd"], lw["rw_a0"], lw["rw_wa"], lw["rw_k_k"], lw["rw_k_a"],
            lw["rw_r_k"])


def _rwkv_prep_sample_kernel(p_ref, prev_ref, mu_ref, w0_ref, wd_ref, a0_ref, wa_ref, kk_ref, ka_ref, rk_ref,
                             r_o, k_o, v_o, kk_o, b_o, d_o, bonus_o):
    r, k_mod, v, kk, b, log_decay, bonus = _rwkv_prep_values(
        p_ref[...], prev_ref[...], mu_ref, w0_ref, wd_ref, a0_ref, wa_ref, kk_ref, ka_ref, rk_ref)
    r_o[...] = r
    k_o[...] = k_mod
    v_o[...] = v
    kk_o[...] = kk
    b_o[...] = b
    d_o[...] = jnp.exp(log_decay)
    bonus_o[...] = bonus


def _rwkv_prep_sample(rw_in, lw, shift_rows):
    rows = rw_in.shape[0]
    const = lambda i: (0, 0)
    return pl.pallas_call(
        _rwkv_prep_sample_kernel,
        grid=(1,),
        in_specs=[pl.BlockSpec((rows, SHIFT_W), const)] * 2 + _rwkv_param_specs(const),
        out_specs=[pl.BlockSpec((rows, RWKV_W), const)] * 7,
        out_shape=[jax.ShapeDtypeStruct((rows, RWKV_W), F32)] * 7,
        compiler_params=_cparams("arbitrary"),
        name="rwkv_prep_sample",
    )(rw_in, shift_rows, *_rwkv_params(lw))


RW_CHUNK = 64
RW_SUB = 16


def _split3(x):
    hi = x.astype(BF16)
    r1 = x - hi.astype(F32)
    mid = r1.astype(BF16)
    lo = (r1 - mid.astype(F32)).astype(BF16)
    return hi, mid, lo


def _block_diag(y, bd_mask):
    return jnp.where(bd_mask, jnp.concatenate([y] * RWKV_HEADS, axis=0), 0.0).astype(BF16)


def _mm(x, y_bd):
    return jnp.dot(x.astype(BF16), y_bd, preferred_element_type=F32)


def _rwkv_chunks(vals, n0, masks):
    bd_mask, tri_incl, strict, incl, same_sub, eye_tiled, ones_bd = masks
    C = RW_CHUNK
    seqs = range(len(vals))
    bd = lambda y: _block_diag(y, bd_mask)
    r, kx, v, kap, bb, lam = ([val[j] for val in vals] for j in range(6))
    cum = [sum(jnp.dot(tri_incl, part, preferred_element_type=F32) for part in _split3(lam[i])) for i in seqs]
    cum_last = [c[C - 1:C, :] for c in cum]
    e_neg = [jnp.exp(-c) for c in cum]
    ap = [jnp.concatenate([kap[i] * jnp.exp(cum[i] - lam[i]), r[i] * jnp.exp(cum[i])], axis=0).astype(BF16)
          for i in seqs]
    rhs = [jnp.concatenate([bd(bb[i] * e_neg[i]), bd(kx[i] * e_neg[i])], axis=0) for i in seqs]
    g = [lax.dot_general(ap[i], rhs[i], (((1,), (1,)), ((), ())), preferred_element_type=F32) for i in seqs]
    l_ab = [jnp.where(strict, gi[:C, :RWKV_W], 0.0) for gi in g]
    l_ak = [jnp.where(strict, gi[:C, RWKV_W:], 0.0) for gi in g]
    q_pb = [jnp.where(incl, gi[C:, :RWKV_W], 0.0) for gi in g]
    q_pk = [jnp.where(incl, gi[C:, RWKV_W:], 0.0) for gi in g]
    apn = [jnp.dot(ap[i], bd(n0[i]), preferred_element_type=F32) for i in seqs]
    lq_v = [_mm(jnp.concatenate([l_ak[i], q_pk[i]], axis=0), bd(v[i])) for i in seqs]
    w = [apn[i][:C] + lq_v[i][:C] for i in seqs]

    l_d = [jnp.where(same_sub, li, 0.0) for li in l_ab]
    l_off = [l_ab[i] - l_d[i] for i in seqs]
    q_inv = [-li for li in l_d]
    power = l_d
    for _ in range(RW_SUB.bit_length() - 2):
        power = [_mm(pw, bd(pw)) for pw in power]
        q_inv = [q_inv[i] + power[i] + _mm(power[i], bd(q_inv[i])) for i in seqs]
    qz = [jnp.dot(q_inv[i].astype(BF16), jnp.concatenate([bd(l_off[i]), bd(w[i])], axis=1),
                  preferred_element_type=F32) for i in seqs]
    n = [l_off[i] + qz[i][:, :RWKV_W] for i in seqs]
    w1 = [w[i] + qz[i][:, RWKV_W:] for i in seqs]
    n_sq = [_mm(ni, bd(ni)) for ni in n]
    y = [w1[i] + _mm(n_sq[i], bd(w1[i])) for i in seqs]
    u = [_mm(n[i], bd(y[i])) - y[i] for i in seqs]

    o = [apn[i][C:] + _mm(q_pb[i], bd(u[i])) + lq_v[i][C:] for i in seqs]

    e_end = [jnp.exp(cum_last[i] - cum[i]) for i in seqs]
    lhs_t = [jnp.concatenate([bb[i] * e_end[i], kx[i] * e_end[i]], axis=0).T.astype(BF16) for i in seqs]
    full = [jnp.dot(lhs_t[i], jnp.concatenate([u[i], v[i]], axis=0).astype(BF16), preferred_element_type=F32)
            for i in seqs]
    heads = [slice(h * RWKV_HEAD, (h + 1) * RWKV_HEAD) for h in range(RWKV_HEADS)]
    delta = [sum(jnp.where(bd_mask[hs], f[hs], 0.0) for hs in heads) for f in full]
    g_wide = [sum(jnp.dot(part, ones_bd, preferred_element_type=F32)
                  for part in _split3(jnp.where(eye_tiled, jnp.exp(cl), 0.0))) for cl in cum_last]
    return o, [n0[i] * g_wide[i] + delta[i] for i in seqs]


def _rwkv_prompt_kernel(p_ref, rg_ref, mu_ref, w0_ref, wd_ref, a0_ref, wa_ref, kk_ref, ka_ref, rk_ref,
                        lnxg_ref, lnxb_ref, o_ref, sfin_ref, state_ref, last_ref):
    nb, C, _ = p_ref.shape
    c_idx = pl.program_id(1)

    @pl.when(c_idx == 0)
    def _():
        state_ref[...] = jnp.zeros(state_ref.shape, F32)
        last_ref[...] = jnp.zeros(last_ref.shape, F32)

    p = p_ref[...].reshape(nb * C, SHIFT_W)
    row = lax.broadcasted_iota(jnp.int32, (nb * C, 1), 0)
    prev = pltpu.roll(p, 1, 0)
    for i in range(nb):
        prev = jnp.where(row == i * C, last_ref[i, 0:1, :], prev)
    vals = _rwkv_prep_values(p, prev, mu_ref, w0_ref, wd_ref, a0_ref, wa_ref, kk_ref, ka_ref, rk_ref)
    bonus = vals[6]

    lane = lax.broadcasted_iota(jnp.int32, (C, RWKV_W), 1)
    t_idx = lax.broadcasted_iota(jnp.int32, (C, RWKV_W), 0)
    s_idx = lane & (RWKV_HEAD - 1)
    row_big = lax.broadcasted_iota(jnp.int32, (RWKV_W, RWKV_W), 0)
    lane_big = lax.broadcasted_iota(jnp.int32, (RWKV_W, RWKV_W), 1)
    bd_mask = _group_index(row_big, RWKV_HEAD) == _group_index(lane_big, RWKV_HEAD)
    ones_bd = jnp.where(bd_mask, 1.0, 0.0).astype(BF16)
    tri_r = lax.broadcasted_iota(jnp.int32, (C, C), 0)
    tri_c = lax.broadcasted_iota(jnp.int32, (C, C), 1)
    masks = (bd_mask, jnp.where(tri_r >= tri_c, 1.0, 0.0).astype(BF16), t_idx > s_idx, t_idx >= s_idx,
             _group_index(t_idx, RW_SUB) == _group_index(s_idx, RW_SUB), t_idx == s_idx, ones_bd)

    per_seq = [tuple(x[i * C:(i + 1) * C] for x in vals[:6]) for i in range(nb)]
    outs, n_new = _rwkv_chunks(per_seq, [state_ref[i] for i in range(nb)], masks)
    for i in range(nb):
        state_ref[i] = n_new[i]
        last_ref[i, 0:1, :] = p[(i + 1) * C - 1:(i + 1) * C, :]
    o = jnp.concatenate(outs, axis=0)

    c = o - _head_sum(o, ones_bd) * (1.0 / RWKV_HEAD)
    var = _head_sum(c * c, ones_bd) * (1.0 / RWKV_HEAD)
    on = c * lax.rsqrt(var + GN_EPS) * lnxg_ref[...] + lnxb_ref[...]
    o_ref[...] = ((on + bonus) * rg_ref[...].reshape(nb * C, RWKV_W)).astype(BF16).reshape(nb, C, RWKV_W)

    @pl.when(c_idx == pl.num_programs(1) - 1)
    def _():
        sfin_ref[...] = state_ref[...]


def _rwkv_prompt(rw_in, rg, lw, batch, seq, nb):
    const = lambda b, c: (0, 0)
    blk = lambda b, c: (b, c, 0)
    o, s_fin = pl.pallas_call(
        _rwkv_prompt_kernel,
        grid=(batch // nb, seq // RW_CHUNK),
        in_specs=[pl.BlockSpec((nb, RW_CHUNK, SHIFT_W), blk), pl.BlockSpec((nb, RW_CHUNK, RWKV_W), blk)]
        + _rwkv_param_specs(const) + [pl.BlockSpec((1, RWKV_W), const)] * 2,
        out_specs=[pl.BlockSpec((nb, RW_CHUNK, RWKV_W), blk),
                   pl.BlockSpec((nb, RWKV_HEAD, RWKV_W), lambda b, c: (b, 0, 0))],
        out_shape=[jax.ShapeDtypeStruct((batch, seq, RWKV_W), BF16),
                   jax.ShapeDtypeStruct((batch, RWKV_HEAD, RWKV_W), F32)],
        scratch_shapes=[pltpu.VMEM((nb, RWKV_HEAD, RWKV_W), F32), pltpu.VMEM((nb, SUBLANES, SHIFT_W), F32)],
        compiler_params=_cparams("arbitrary", "arbitrary"),
        name="rwkv_prompt",
    )(rw_in.reshape(batch, seq, SHIFT_W), rg.reshape(batch, seq, RWKV_W), *_rwkv_params(lw),
      lw["lnx_g"], lw["lnx_b"])
    wkv = jnp.transpose(s_fin.reshape(batch, RWKV_HEAD, RWKV_HEADS, RWKV_HEAD), (0, 2, 3, 1))
    return o.reshape(batch * seq, RWKV_W), wkv


def _rwkv_step_kernel(x_ref, s_ref, *rest, n_prev):
    if n_prev:
        prev_ref, o_ref, snew_ref = rest
    else:
        o_ref, snew_ref = rest
    for j in range(n_prev):
        snew_ref[j] = prev_ref[j]
    kk, b, d, kx, r = (x_ref[j, 0] for j in range(5))
    for vi in range(RWKV_HEAD):
        sv = s_ref[0, 0, vi]
        sa = jnp.sum(sv * kk, axis=0, keepdims=True)
        sn = sv * d - sa * b + x_ref[5, 0, pl.ds(vi, 1), :] * kx
        snew_ref[n_prev, 0, vi] = sn
        o_ref[0, pl.ds(vi, 1), :] = jnp.sum(sn * r, axis=0, keepdims=True)


def _rwkv_step(prep, state_t, layer, prev_stack):
    r, k_mod, v, kk, b, d, _ = prep
    H, N = RWKV_HEADS, RWKV_HEAD
    batch = r.shape[0]
    n_prev = 0 if prev_stack is None else prev_stack.shape[0]
    x = jnp.transpose(jnp.stack([kk, b, d, k_mod, r, v]).reshape(6, batch, H, N), (0, 2, 3, 1))
    head_blk = lambda h: (0, h, 0, 0, 0)
    in_specs = [pl.BlockSpec((6, 1, N, batch), lambda h: (0, h, 0, 0)),
                pl.BlockSpec((1, 1, N, N, batch), lambda h: (layer, h, 0, 0, 0))]
    args = [x, state_t]
    if n_prev:
        in_specs.append(pl.BlockSpec((n_prev, 1, N, N, batch), head_blk))
        args.append(prev_stack)
    o, stack = pl.pallas_call(
        functools.partial(_rwkv_step_kernel, n_prev=n_prev),
        grid=(H,),
        in_specs=in_specs,
        out_specs=[pl.BlockSpec((1, N, batch), lambda h: (h, 0, 0)),
                   pl.BlockSpec((n_prev + 1, 1, N, N, batch), head_blk)],
        out_shape=[jax.ShapeDtypeStruct((H, N, batch), F32),
                   jax.ShapeDtypeStruct((n_prev + 1, H, N, N, batch), F32)],
        compiler_params=_cparams("arbitrary"),
        name="rwkv_step",
    )(*args)
    return jnp.transpose(o, (2, 0, 1)).reshape(batch, H * N), stack


def _mix_out(mixed, x_ref, mod_ref, wout_ref, gpost_ref, y_ref):
    out = jnp.dot(mixed, wout_ref[...], preferred_element_type=F32)
    gate = _rows2d(mod_ref)[:, 2 * D_MODEL:]
    y_ref[...] = x_ref[...] + gate * _rms(out, gpost_ref[...])


def _out_prompt_kernel(om_ref, orw_ref, gm_ref, x_ref, mod_ref, wout_ref, gpost_ref, y_ref):
    mixed = jnp.concatenate([om_ref[...], orw_ref[...], gm_ref[...]], axis=-1)
    _mix_out(mixed, x_ref, mod_ref, wout_ref, gpost_ref, y_ref)


def _out_sample_kernel(lat_ref, mg_ref, orw_ref, bonus_ref, rg_ref, gm_ref, x_ref, mod_ref, wout_ref, gpost_ref,
                       lnxg_ref, lnxb_ref, wuv_ref, y_ref):
    lat = lat_ref[...]
    o_mla = jnp.concatenate(
        [jnp.dot(lat[:, h * KV_LORA:(h + 1) * KV_LORA].astype(BF16), wuv_ref[h],
                 preferred_element_type=F32) for h in range(MLA_HEADS)], axis=-1)
    o_mla = (o_mla * mg_ref[...].astype(F32)).astype(BF16)
    ones_bd = _head_ones(RWKV_W, RWKV_HEAD)
    o = orw_ref[...]
    c = o - _head_sum(o, ones_bd) * (1.0 / RWKV_HEAD)
    var = _head_sum(c * c, ones_bd) * (1.0 / RWKV_HEAD)
    on = c * lax.rsqrt(var + GN_EPS) * lnxg_ref[...] + lnxb_ref[...]
    o_rw = ((on + bonus_ref[...]) * rg_ref[...]).astype(BF16)
    mixed = jnp.concatenate([o_mla, o_rw, gm_ref[...]], axis=-1)
    _mix_out(mixed, x_ref, mod_ref, wout_ref, gpost_ref, y_ref)


def _out_prompt(o_mla, o_rw, gm, x2d, mod_b, lw, seq, tm):
    rows = x2d.shape[0]
    tiles_per_seq = seq // tm
    const = lambda i: (0, 0)
    row_blk = lambda i: (i, 0)
    return pl.pallas_call(
        _out_prompt_kernel,
        grid=(rows // tm,),
        in_specs=[
            pl.BlockSpec((tm, MLA_W), row_blk),
            pl.BlockSpec((tm, RWKV_W), row_blk),
            pl.BlockSpec((tm, GM_W), row_blk),
            pl.BlockSpec((tm, D_MODEL), row_blk),
            pl.BlockSpec((1, 1, 3 * D_MODEL), lambda i: (i // tiles_per_seq, 0, 0)),
            pl.BlockSpec((D_MODEL, D_MODEL), const),
            pl.BlockSpec((1, D_MODEL), const),
        ],
        out_specs=pl.BlockSpec((tm, D_MODEL), row_blk),
        out_shape=jax.ShapeDtypeStruct((rows, D_MODEL), F32),
        compiler_params=_cparams("arbitrary"),
        name="out_prompt",
    )(o_mla, o_rw, gm, x2d, mod_b, lw["w_out"], lw["g_post"])


def _out_sample(o_lat, mg, o_rw, bonus, rg, gm, x2d, mod_rows, lw):
    rows = x2d.shape[0]
    const = lambda i: (0, 0)
    full = lambda w: pl.BlockSpec((rows, w), const)
    return pl.pallas_call(
        _out_sample_kernel,
        grid=(1,),
        in_specs=[
            full(MLA_HEADS * KV_LORA), full(MLA_W), full(RWKV_W), full(RWKV_W), full(RWKV_W), full(GM_W),
            full(D_MODEL), full(3 * D_MODEL),
            pl.BlockSpec((D_MODEL, D_MODEL), const),
            pl.BlockSpec((1, D_MODEL), const),
            pl.BlockSpec((1, RWKV_W), const),
            pl.BlockSpec((1, RWKV_W), const),
            pl.BlockSpec((MLA_HEADS, KV_LORA, MLA_V), lambda i: (0, 0, 0)),
        ],
        out_specs=full(D_MODEL),
        out_shape=jax.ShapeDtypeStruct((rows, D_MODEL), F32),
        compiler_params=_cparams("arbitrary"),
        name="out_sample",
    )(o_lat, mg, o_rw, bonus, rg, gm, x2d, mod_rows, lw["w_out"], lw["g_post"], lw["lnx_g"], lw["lnx_b"],
      lw["w_uv_h"])


def _swap_halves(w):
    half = MLA_ROPE // 2
    return jnp.concatenate([w[..., half:], w[..., :half]], axis=-1)


def _rope_tables(pos):
    half = MLA_ROPE // 2
    inv = ROPE_THETA ** (-jnp.arange(half, dtype=F32) / half)
    ang = pos.astype(F32)[:, None] * inv[None, :]
    cos, sin = jnp.cos(ang), jnp.sin(ang)
    zeros = jnp.zeros((pos.shape[0], LANES - MLA_ROPE), F32)
    return (jnp.concatenate([cos, cos, zeros], axis=-1), jnp.concatenate([-sin, sin, zeros], axis=-1))


def _layer_weights(l, w_in, norm_pre_g, norm_post_g, q_norm_g, kv_norm_g, w_uq, w_uk, w_uv, rw_mu, rw_w0,
                   rw_w_decay_up, rw_a0, rw_w_a_up, rw_k_k, rw_k_a, rw_r_k, rw_lnx_g, rw_lnx_b, gm_ln_g,
                   gm_ln_b, gm_w_s, gm_b_s, w_out):
    w = w_in[l]
    split_lo, split_hi = Q_LORA + KV_LORA, Q_LORA + KV_LORA + MLA_ROPE
    kr = w[:, split_lo:split_hi]
    w_in_p = jnp.concatenate([w[:, :Q_LORA], kr, _swap_halves(kr), w[:, Q_LORA:split_lo], w[:, split_hi:]],
                             axis=1).astype(BF16)
    uq = w_uq[l]
    pe = uq[..., MLA_NOPE:]
    w_uq_p = jnp.concatenate([uq, _swap_halves(pe)], axis=-1).reshape(Q_LORA, MLA_HEADS * QHEAD_W).astype(BF16)
    zeros_lora = jnp.zeros((LANES - DECAY_LORA, RWKV_W), F32)
    row = lambda a: a.reshape(1, -1)
    return {
        "w_in": w_in_p, "g_pre": row(norm_pre_g[l]), "g_post": row(norm_post_g[l]),
        "g_q": row(q_norm_g[l]), "g_kv": row(kv_norm_g[l]), "w_uq": w_uq_p,
        "w_uk": w_uk[l].reshape(KV_LORA, MLA_W).astype(BF16),
        "w_uv": w_uv[l].reshape(KV_LORA, MLA_W).astype(BF16),
        "w_ukt": jnp.transpose(w_uk[l], (1, 2, 0)).astype(BF16),
        "w_uv_h": jnp.transpose(w_uv[l], (1, 0, 2)).astype(BF16),
        "rw_mu": row(rw_mu[l]), "rw_w0": row(rw_w0[l]), "rw_a0": row(rw_a0[l]),
        "rw_wd": jnp.concatenate([rw_w_decay_up[l], zeros_lora], axis=0).astype(BF16),
        "rw_wa": jnp.concatenate([zeros_lora, rw_w_a_up[l]], axis=0).astype(BF16),
        "rw_k_k": row(rw_k_k[l]), "rw_k_a": row(rw_k_a[l]), "rw_r_k": row(rw_r_k[l]),
        "lnx_g": row(rw_lnx_g[l]), "lnx_b": row(rw_lnx_b[l]),
        "ln_g": row(gm_ln_g[l]), "ln_b": row(gm_ln_b[l]),
        "gm_ws": gm_w_s[l],
        "gm_bias": jnp.repeat(gm_b_s[l].T, GM_GROUP_W, axis=1),
        "gm_coef0": row(jnp.repeat(gm_w_s[l][:, 0, 0], GM_GROUP_W)),
        "gm_bias0": row(jnp.repeat(gm_b_s[l][:, 0], GM_GROUP_W)),
        "w_out": w_out[l].astype(BF16),
    }


def _pick(full, want):
    return want if full % want == 0 else full


def kernel(x_prompt, x_sample, c_prompt, c_sample, cache_ckv, cache_kpe, page_table, state_wkv, state_shift, w_ada, b_ada, norm_pre_g, norm_post_g, w_in, q_norm_g, kv_norm_g, w_uq, w_uk, w_uv, rw_mu, rw_w0, rw_w_decay_up, rw_a0, rw_w_a_up, rw_k_k, rw_k_a, rw_r_k, rw_lnx_g, rw_lnx_b, gm_ln_g, gm_ln_b, gm_w_s, gm_b_s, w_out):
    batch, seq, _ = x_prompt.shape
    dec_batch, dec_seq, _ = x_sample.shape
    depth = w_in.shape[0]
    n_pages = page_table.shape[1]
    past_len = n_pages * cache_ckv.shape[2]
    assert dec_seq == 1 and seq % CHUNK == 0

    assert seq % RW_CHUNK == 0
    tm_proj = _pick(seq, 512)
    tm_out = _pick(seq, 1024)
    tq = _pick(seq, 512)
    rw_nb = _pick(batch, 16)

    mod = _ada_mod(jnp.concatenate([c_prompt, c_sample], axis=0), w_ada, b_ada)
    cs_p, sn_p = _rope_tables(jnp.arange(seq))
    cs_s, sn_s = _rope_tables(jnp.full((dec_batch,), past_len))
    cache_kpe_t = jnp.swapaxes(cache_kpe, 2, 3)

    y_p = x_prompt.reshape(batch * seq, D_MODEL)
    y_s = x_sample.reshape(dec_batch, D_MODEL)
    outs = {k: [] for k in ("ckv_s", "kpe_s", "wkv_p", "sh_p", "sh_s", "vc_p", "vc_s")}
    prompt_stacks = None
    wkv_s_stack = None
    state_wkv_t = jnp.transpose(state_wkv, (0, 2, 3, 4, 1))
    for l in range(depth):
        lw = _layer_weights(l, w_in, norm_pre_g, norm_post_g, q_norm_g, kv_norm_g, w_uq, w_uk, w_uv, rw_mu,
                            rw_w0, rw_w_decay_up, rw_a0, rw_w_a_up, rw_k_k, rw_k_a, rw_r_k, rw_lnx_g,
                            rw_lnx_b, gm_ln_g, gm_ln_b, gm_w_s, gm_b_s, w_out)
        mod_p = mod[l, :batch].reshape(batch, 1, 3 * D_MODEL)
        q, k, v, mg, ckv_stack, kpe_stack, rw_in, rg, gm, vn_last = _proj_prompt(
            y_p, mod_p, lw, cs_p, sn_p, batch, seq, tm_proj, prompt_stacks)
        prompt_stacks = (ckv_stack, kpe_stack)
        o_mla = _flash(q, k, v, mg, batch, seq, tq)
        o_rw, wkv_new = _rwkv_prompt(rw_in, rg, lw, batch, seq, rw_nb)
        y_p = _out_prompt(o_mla, o_rw, gm, y_p, mod_p, lw, seq, tm_out)
        outs["wkv_p"].append(wkv_new)
        outs["sh_p"].append(rw_in.reshape(batch, seq, SHIFT_W)[:, -1])
        outs["vc_p"].append(vn_last)

        mod_s = mod[l, batch:]
        qa, qp, mg_s, ckv_s, kpe_s, rw_s, rg_s, gm_s, vn_s = _proj_sample(y_s, mod_s, lw, cs_s, sn_s)
        o_lat = _paged_attention(qa, qp, ckv_s, kpe_s, cache_ckv, cache_kpe_t, page_table, l)
        prep_s = _rwkv_prep_sample(rw_s, lw, state_shift[l])
        o_rw_s, wkv_s_stack = _rwkv_step(prep_s, state_wkv_t, l, wkv_s_stack)
        y_s = _out_sample(o_lat, mg_s, o_rw_s, prep_s[6], rg_s, gm_s, y_s, mod_s, lw)
        outs["ckv_s"].append(ckv_s.reshape(dec_batch, 1, KV_LORA))
        outs["kpe_s"].append(kpe_s[:, :MLA_ROPE].reshape(dec_batch, 1, MLA_ROPE))
        outs["sh_s"].append(rw_s)
        outs["vc_s"].append(vn_s.reshape(dec_batch, 1, GM_W))

    st = lambda name: jnp.stack(outs[name])
    ckv_p, kpe_p = prompt_stacks
    return (y_p.reshape(batch, seq, D_MODEL), y_s.reshape(dec_batch, 1, D_MODEL),
            ckv_p.reshape(depth, batch, seq, KV_LORA), kpe_p.reshape(depth, batch, seq, MLA_ROPE),
            st("ckv_s"), st("kpe_s"), st("wkv_p"), jnp.transpose(wkv_s_stack, (0, 4, 1, 2, 3)),
            st("sh_p"), st("sh_s"), st("vc_p"), st("vc_s"))
```

```python
import functools

import numpy as np
import jax
import jax.numpy as jnp
from jax import lax
from jax.experimental import pallas as pl
from jax.experimental.pallas import tpu as pltpu

F32 = jnp.float32
BF16 = jnp.bfloat16

D_MODEL = 1024
MLA_V = 128
MLA_W = D_MODEL // 2
MLA_HEADS = MLA_W // MLA_V
MLA_NOPE = 128
MLA_ROPE = 64
MLA_QK = MLA_NOPE + MLA_ROPE
Q_LORA = (3 * D_MODEL) // 8
KV_LORA = D_MODEL // 4
ROPE_THETA = 10000.0
ATTN_SCALE = MLA_QK ** -0.5
Q_SCALE = ATTN_SCALE * float(np.log2(np.e))
RWKV_W = D_MODEL // 4
RWKV_HEAD = 64
RWKV_HEADS = RWKV_W // RWKV_HEAD
DECAY_LORA = 64
ICLR_LORA = 64
SHIFT_W = 3 * RWKV_W + DECAY_LORA + ICLR_LORA
GN_EPS = 64e-5
GM_W = D_MODEL // 4
GM_GROUPS = 4
GM_GROUP_W = GM_W // GM_GROUPS
CHUNK = 128
RMS_EPS = 1e-6
LN_EPS = 1e-5

LANES = 128
SUBLANES = 8
VMEM_LIMIT_BYTES = 56 * 1024 * 1024

QHEAD_W = 2 * LANES
OFF_Q = 0
OFF_KR = OFF_Q + Q_LORA
OFF_KV = OFF_KR + LANES
OFF_MG = OFF_KV + KV_LORA
OFF_RW = OFF_MG + MLA_W
OFF_RG = OFF_RW + SHIFT_W
OFF_GU = OFF_RG + RWKV_W
OFF_GV = OFF_GU + GM_W
OFF_GG = OFF_GV + GM_W
IN_W_P = OFF_GG + GM_W

NEG_BIG = -1e30


def _cparams(*sem):
    return pltpu.CompilerParams(dimension_semantics=sem, vmem_limit_bytes=VMEM_LIMIT_BYTES)


def _silu(x):
    return x * jax.nn.sigmoid(x)


def _rms(x, g, eps=RMS_EPS):
    return x * lax.rsqrt(jnp.mean(x * x, axis=-1, keepdims=True) + eps) * g


def _rows2d(ref):
    m = ref[...]
    return m.reshape(m.shape[-2], m.shape[-1])


def _group_index(idx, group):
    shift = group.bit_length() - 1
    assert 1 << shift == group
    return lax.shift_right_logical(idx, shift)


def _head_ones(width, head):
    r = _group_index(lax.broadcasted_iota(jnp.int32, (width, width), 0), head)
    c = _group_index(lax.broadcasted_iota(jnp.int32, (width, width), 1), head)
    return jnp.where(r == c, 1.0, 0.0).astype(BF16)


def _head_sum(x, ones_bd):
    hi = x.astype(BF16)
    lo = (x - hi.astype(F32)).astype(BF16)
    return (jnp.dot(hi, ones_bd, preferred_element_type=F32)
            + jnp.dot(lo, ones_bd, preferred_element_type=F32))


def _rope_pair(blk, cs, sn):
    return blk * cs + pltpu.roll(blk, MLA_ROPE, 1) * sn


def _ada_kernel(c_ref, w_ref, b_ref, o_ref):
    c = _silu(c_ref[...]).astype(BF16)
    o_ref[0] = jnp.dot(c, w_ref[0].astype(BF16), preferred_element_type=F32) + b_ref[0]


def _ada_mod(c_all, w_ada, b_ada):
    depth = w_ada.shape[0]
    n_rows = c_all.shape[0]
    n_tiles = (3 * D_MODEL) // D_MODEL
    return pl.pallas_call(
        _ada_kernel,
        grid=(depth, n_tiles),
        in_specs=[
            pl.BlockSpec((n_rows, D_MODEL), lambda l, j: (0, 0)),
            pl.BlockSpec((1, D_MODEL, D_MODEL), lambda l, j: (l, 0, j)),
            pl.BlockSpec((1, 1, D_MODEL), lambda l, j: (l, 0, j)),
        ],
        out_specs=pl.BlockSpec((1, n_rows, D_MODEL), lambda l, j: (l, 0, j)),
        out_shape=jax.ShapeDtypeStruct((depth, n_rows, 3 * D_MODEL), F32),
        compiler_params=_cparams("arbitrary", "arbitrary"),
        name="ada_mod",
    )(c_all, w_ada, b_ada.reshape(depth, 1, 3 * D_MODEL))


def _proj_common(x_ref, mod_ref, gpre_ref, win_ref):
    mod = _rows2d(mod_ref)
    shift = mod[:, 0:D_MODEL]
    scale = mod[:, D_MODEL:2 * D_MODEL]
    h = _rms(x_ref[...], gpre_ref[...]) * (1.0 + scale) + shift
    hb = h.astype(BF16)

    def seg(off, width):
        return jnp.dot(hb, win_ref[:, off:off + width], preferred_element_type=F32)

    return seg


def _q_heads(seg, gq_ref, wuq_ref, cs, sn):
    q_kr = seg(OFF_Q, Q_LORA + LANES)
    qn = _rms(q_kr[:, :Q_LORA], gq_ref[...]).astype(BF16)
    q_all = jnp.dot(qn, wuq_ref[...], preferred_element_type=F32)
    heads = []
    for h in range(MLA_HEADS):
        nope = q_all[:, h * QHEAD_W:h * QHEAD_W + LANES] * Q_SCALE
        pe = _rope_pair(q_all[:, h * QHEAD_W + LANES:(h + 1) * QHEAD_W], cs, sn) * Q_SCALE
        heads.append((nope, pe))
    return heads, _rope_pair(q_kr[:, Q_LORA:], cs, sn)


def _gm_norm(seg, lng_ref, lnb_ref):
    v = seg(OFF_GV, GM_W)
    mu = jnp.mean(v, axis=-1, keepdims=True)
    c = v - mu
    var = jnp.mean(c * c, axis=-1, keepdims=True)
    return c * lax.rsqrt(var + LN_EPS) * lng_ref[...] + lnb_ref[...]


def _proj_prompt_kernel(x_ref, mod_ref, gpre_ref, win_ref, gq_ref, wuq_ref, gkv_ref, wuk_ref, wuv_ref,
                        cs_ref, sn_ref, lng_ref, lnb_ref, ws_ref, bs_ref, *rest, tiles_per_seq, n_prev):
    if n_prev:
        ckv_prev_ref, kpe_prev_ref = rest[:2]
        rest = rest[2:]
    q_ref, k_ref, v_ref, mg_ref, ckv_ref, kpe_ref, rw_ref, rg_ref, gm_ref, vn_ref = rest
    for j in range(n_prev):
        ckv_ref[j] = ckv_prev_ref[j]
        kpe_ref[j] = kpe_prev_ref[j]
    seg = _proj_common(x_ref, mod_ref, gpre_ref, win_ref)
    cs = cs_ref[...]
    sn = sn_ref[...]
    tm = x_ref.shape[0]

    heads, kpe = _q_heads(seg, gq_ref, wuq_ref, cs, sn)
    for h, (nope, pe) in enumerate(heads):
        q_ref[:, h * QHEAD_W:h * QHEAD_W + LANES] = nope.astype(BF16)
        q_ref[:, h * QHEAD_W + LANES:(h + 1) * QHEAD_W] = pe.astype(BF16)
    ckv = _rms(seg(OFF_KV, KV_LORA), gkv_ref[...])
    ckv_ref[n_prev] = ckv
    kpe_ref[n_prev] = kpe[:, :MLA_ROPE]
    ckv_b = ckv.astype(BF16)
    k_nope = jnp.dot(ckv_b, wuk_ref[...], preferred_element_type=F32)
    v_ref[...] = jnp.dot(ckv_b, wuv_ref[...], preferred_element_type=F32).astype(BF16)
    kpe_b = kpe.astype(BF16)
    for h in range(MLA_HEADS):
        k_ref[:, h * QHEAD_W:h * QHEAD_W + LANES] = k_nope[:, h * MLA_NOPE:(h + 1) * MLA_NOPE].astype(BF16)
        k_ref[:, h * QHEAD_W + LANES:(h + 1) * QHEAD_W] = kpe_b
    mg_ref[...] = _silu(seg(OFF_MG, MLA_W)).astype(BF16)

    rw_ref[...] = seg(OFF_RW, SHIFT_W)
    rg_ref[...] = _silu(seg(OFF_RG, RWKV_W))

    vn = _gm_norm(seg, lng_ref, lnb_ref)
    gate_u = _silu(seg(OFF_GG, GM_W)) * seg(OFF_GU, GM_W)
    row = lax.broadcasted_iota(jnp.int32, (CHUNK, CHUNK), 0)
    col = lax.broadcasted_iota(jnp.int32, (CHUNK, CHUNK), 1)
    w_cat = jnp.concatenate([jnp.where(row >= col, ws_ref[g], 0.0) for g in range(GM_GROUPS)],
                            axis=1).astype(BF16)
    own_group = (_group_index(lax.broadcasted_iota(jnp.int32, (GM_GROUPS * CHUNK, GM_W), 0), CHUNK)
                 == _group_index(lax.broadcasted_iota(jnp.int32, (GM_GROUPS * CHUNK, GM_W), 1), GM_GROUP_W))
    for c in range(tm // CHUNK):
        vc = vn[c * CHUNK:(c + 1) * CHUNK]
        vc_groups = jnp.where(own_group, jnp.concatenate([vc] * GM_GROUPS, axis=0), 0.0).astype(BF16)
        z = bs_ref[...] + jnp.dot(w_cat, vc_groups, preferred_element_type=F32)
        gm_ref[c * CHUNK:(c + 1) * CHUNK, :] = (gate_u[c * CHUNK:(c + 1) * CHUNK] * z).astype(BF16)

    @pl.when(pl.program_id(0) % tiles_per_seq == tiles_per_seq - 1)
    def _():
        vn_ref[0] = vn[tm - CHUNK:, :]


def _proj_prompt(x2d, mod_b, lw, cs, sn, batch, seq, tm, prev_stacks):
    rows = x2d.shape[0]
    tiles_per_seq = seq // tm
    n_prev = 0 if prev_stacks is None else prev_stacks[0].shape[0]
    const = lambda i: (0, 0)
    row_blk = lambda i: (i, 0)
    stack_blk = lambda i: (0, i, 0)
    outs = [
        (QHEAD_W * MLA_HEADS, BF16), (QHEAD_W * MLA_HEADS, BF16), (MLA_W, BF16), (MLA_W, BF16),
        (KV_LORA, F32), (MLA_ROPE, F32), (SHIFT_W, F32), (RWKV_W, F32), (GM_W, BF16),
    ]
    out_shape = [jax.ShapeDtypeStruct((rows, w), dt) for w, dt in outs]
    out_specs = [pl.BlockSpec((tm, w), row_blk) for w, _ in outs]
    for idx in (4, 5):
        w, dt = outs[idx]
        out_shape[idx] = jax.ShapeDtypeStruct((n_prev + 1, rows, w), dt)
        out_specs[idx] = pl.BlockSpec((n_prev + 1, tm, w), stack_blk)
    out_shape.append(jax.ShapeDtypeStruct((batch, CHUNK, GM_W), F32))
    out_specs.append(pl.BlockSpec((1, CHUNK, GM_W), lambda i: (i // tiles_per_seq, 0, 0)))
    prev_specs = [] if not n_prev else [pl.BlockSpec((n_prev, tm, KV_LORA), stack_blk),
                                        pl.BlockSpec((n_prev, tm, MLA_ROPE), stack_blk)]
    return pl.pallas_call(
        functools.partial(_proj_prompt_kernel, tiles_per_seq=tiles_per_seq, n_prev=n_prev),
        grid=(rows // tm,),
        in_specs=[
            pl.BlockSpec((tm, D_MODEL), row_blk),
            pl.BlockSpec((1, 1, 3 * D_MODEL), lambda i: (i // tiles_per_seq, 0, 0)),
            pl.BlockSpec((1, D_MODEL), const),
            pl.BlockSpec((D_MODEL, IN_W_P), const),
            pl.BlockSpec((1, Q_LORA), const),
            pl.BlockSpec((Q_LORA, QHEAD_W * MLA_HEADS), const),
            pl.BlockSpec((1, KV_LORA), const),
            pl.BlockSpec((KV_LORA, MLA_W), const),
            pl.BlockSpec((KV_LORA, MLA_W), const),
            pl.BlockSpec((tm, LANES), lambda i: (i % tiles_per_seq, 0)),
            pl.BlockSpec((tm, LANES), lambda i: (i % tiles_per_seq, 0)),
            pl.BlockSpec((1, GM_W), const),
            pl.BlockSpec((1, GM_W), const),
            pl.BlockSpec((GM_GROUPS, CHUNK, CHUNK), lambda i: (0, 0, 0)),
            pl.BlockSpec((CHUNK, GM_W), const),
        ] + prev_specs,
        out_specs=out_specs,
        out_shape=out_shape,
        compiler_params=_cparams("arbitrary"),
        name="proj_prompt",
    )(x2d, mod_b, lw["g_pre"], lw["w_in"], lw["g_q"], lw["w_uq"], lw["g_kv"], lw["w_uk"], lw["w_uv"],
      cs, sn, lw["ln_g"], lw["ln_b"], lw["gm_ws"], lw["gm_bias"], *(prev_stacks or ()))


def _proj_sample_kernel(x_ref, mod_ref, gpre_ref, win_ref, gq_ref, wuq_ref, gkv_ref, wukt_ref,
                        cs_ref, sn_ref, lng_ref, lnb_ref, coef_ref, bias_ref,
                        qa_ref, qp_ref, mg_ref, ckv_ref, kpe_ref, rw_ref, rg_ref, gm_ref, vn_ref):
    seg = _proj_common(x_ref, mod_ref, gpre_ref, win_ref)
    cs = cs_ref[...]
    sn = sn_ref[...]
    heads, kpe = _q_heads(seg, gq_ref, wuq_ref, cs, sn)
    for h, (nope, pe) in enumerate(heads):
        qa_ref[:, h * KV_LORA:(h + 1) * KV_LORA] = jnp.dot(nope.astype(BF16), wukt_ref[h],
                                                           preferred_element_type=F32)
        qp_ref[:, h * LANES:(h + 1) * LANES] = pe
    ckv_ref[...] = _rms(seg(OFF_KV, KV_LORA), gkv_ref[...])
    kpe_ref[...] = kpe
    mg_ref[...] = _silu(seg(OFF_MG, MLA_W)).astype(BF16)
    rw_ref[...] = seg(OFF_RW, SHIFT_W)
    rg_ref[...] = _silu(seg(OFF_RG, RWKV_W))
    vn = _gm_norm(seg, lng_ref, lnb_ref)
    vn_ref[...] = vn
    z = vn * coef_ref[...] + bias_ref[...]
    gm_ref[...] = (_silu(seg(OFF_GG, GM_W)) * seg(OFF_GU, GM_W) * z).astype(BF16)


def _proj_sample(x2d, mod_rows, lw, cs, sn):
    rows = x2d.shape[0]
    const = lambda i: (0, 0)
    outs = [
        (KV_LORA * MLA_HEADS, F32), (LANES * MLA_HEADS, F32), (MLA_W, BF16), (KV_LORA, F32), (LANES, F32),
        (SHIFT_W, F32), (RWKV_W, F32), (GM_W, BF16), (GM_W, F32),
    ]
    return pl.pallas_call(
        _proj_sample_kernel,
        grid=(1,),
        in_specs=[
            pl.BlockSpec((rows, D_MODEL), const),
            pl.BlockSpec((rows, 3 * D_MODEL), const),
            pl.BlockSpec((1, D_MODEL), const),
            pl.BlockSpec((D_MODEL, IN_W_P), const),
            pl.BlockSpec((1, Q_LORA), const),
            pl.BlockSpec((Q_LORA, QHEAD_W * MLA_HEADS), const),
            pl.BlockSpec((1, KV_LORA), const),
            pl.BlockSpec((MLA_HEADS, MLA_NOPE, KV_LORA), lambda i: (0, 0, 0)),
            pl.BlockSpec((rows, LANES), const),
            pl.BlockSpec((rows, LANES), const),
            pl.BlockSpec((1, GM_W), const),
            pl.BlockSpec((1, GM_W), const),
            pl.BlockSpec((1, GM_W), const),
            pl.BlockSpec((1, GM_W), const),
        ],
        out_specs=[pl.BlockSpec((rows, w), const) for w, _ in outs],
        out_shape=[jax.ShapeDtypeStruct((rows, w), dt) for w, dt in outs],
        compiler_params=_cparams("arbitrary"),
        name="proj_sample",
    )(x2d, mod_rows, lw["g_pre"], lw["w_in"], lw["g_q"], lw["w_uq"], lw["g_kv"], lw["w_ukt"],
      cs, sn, lw["ln_g"], lw["ln_b"], lw["gm_coef0"], lw["gm_bias0"])


def _flash_kernel(q_ref, k_ref, v_ref, g_ref, o_ref, *, tq):
    seq = q_ref.shape[1]
    row = lax.broadcasted_iota(jnp.int32, (tq, tq), 0)
    col = lax.broadcasted_iota(jnp.int32, (tq, tq), 1)
    for qi in range(seq // tq):
        rows = slice(qi * tq, (qi + 1) * tq)
        q = q_ref[0, rows, :]
        m = jnp.full((tq, 1), NEG_BIG, F32)
        l = jnp.zeros((tq, 1), F32)
        acc = jnp.zeros((tq, MLA_V), F32)
        for ki in range(qi + 1):
            cols = slice(ki * tq, (ki + 1) * tq)
            s = lax.dot_general(q, k_ref[0, cols, :], (((1,), (1,)), ((), ())),
                                preferred_element_type=F32)
            if ki == qi:
                s = jnp.where(row >= col, s, NEG_BIG)
            m_new = jnp.maximum(m, jnp.max(s, axis=-1, keepdims=True))
            alpha = jnp.exp2(m - m_new)
            p = jnp.exp2(s - m_new)
            l = alpha * l + jnp.sum(p, axis=-1, keepdims=True)
            acc = alpha * acc + jnp.dot(p.astype(BF16), v_ref[0, cols, :], preferred_element_type=F32)
            m = m_new
        o_ref[0, rows, :] = (acc / l * g_ref[0, rows, :].astype(F32)).astype(BF16)


def _flash(q, k, v, g, batch, seq, tq):
    q3 = q.reshape(batch, seq, QHEAD_W * MLA_HEADS)
    k3 = k.reshape(batch, seq, QHEAD_W * MLA_HEADS)
    v3 = v.reshape(batch, seq, MLA_W)
    g3 = g.reshape(batch, seq, MLA_W)
    head_blk = lambda b, h: (b, 0, h)
    out = pl.pallas_call(
        functools.partial(_flash_kernel, tq=tq),
        grid=(batch, MLA_HEADS),
        in_specs=[
            pl.BlockSpec((1, seq, QHEAD_W), head_blk),
            pl.BlockSpec((1, seq, QHEAD_W), head_blk),
            pl.BlockSpec((1, seq, MLA_V), head_blk),
            pl.BlockSpec((1, seq, MLA_V), head_blk),
        ],
        out_specs=pl.BlockSpec((1, seq, MLA_V), head_blk),
        out_shape=jax.ShapeDtypeStruct((batch, seq, MLA_W), BF16),
        compiler_params=_cparams("arbitrary", "arbitrary"),
        name="flash_prompt",
    )(q3, k3, v3, g3)
    return out.reshape(batch * seq, MLA_W)


N_SLOTS = 4
LOOKAHEAD = N_SLOTS - 1


def _paged_kernel(pt_ref, qa_ref, qp_ref, cn_ref, kn_ref, ckv_hbm, kpe_hbm, o_ref, ckv_buf, kpe_buf, sem,
                  *, layer, n_pages, page, group_tokens):
    b = pl.program_id(0)
    n_seq = pl.num_programs(0)
    slot = lax.rem(b, N_SLOTS)

    def page_copies(seq_idx, sl):
        copies = []
        for i in range(n_pages):
            pg = pt_ref[seq_idx, i]
            copies.append(pltpu.make_async_copy(
                ckv_hbm.at[layer, pg], ckv_buf.at[sl, pl.ds(i * page, page), :], sem.at[0, sl]))
            copies.append(pltpu.make_async_copy(kpe_hbm.at[layer, pg], kpe_buf.at[sl, i], sem.at[1, sl]))
        return copies

    def start_all(copies):
        for i, c in enumerate(copies):
            c.start(priority=(i // 2) % 2)

    @pl.when(b == 0)
    def _():
        for ahead in range(LOOKAHEAD):
            start_all(page_copies(ahead, ahead))

    for c in page_copies(b, slot):
        c.wait()

    nxt = lax.rem(b + LOOKAHEAD, n_seq)
    nxt_slot = lax.rem(b + LOOKAHEAD, N_SLOTS)
    start_all(page_copies(nxt, nxt_slot))

    qa = qa_ref[0]
    qp = qp_ref[0][:, :MLA_ROPE]
    qa_b = qa.astype(BF16)
    qp_b = qp.astype(BF16)
    group_pages = group_tokens // page
    groups = [slice(g * group_tokens, (g + 1) * group_tokens) for g in range(n_pages // group_pages)]
    ckv = [ckv_buf[slot, gs, :].astype(BF16) for gs in groups]
    kpe_t = [jnp.concatenate([kpe_buf[slot, g * group_pages + i] for i in range(group_pages)],
                             axis=1).astype(BF16) for g in range(len(groups))]
    s = [lax.dot_general(qa_b, ck, (((1,), (1,)), ((), ())), preferred_element_type=F32)
         + jnp.dot(qp_b, kp, preferred_element_type=F32) for ck, kp in zip(ckv, kpe_t)]
    m_g = [jnp.max(sg, axis=-1, keepdims=True) for sg in s]
    p = [jnp.exp2(sg - mg) for sg, mg in zip(s, m_g)]
    l_g = [jnp.sum(pg, axis=-1, keepdims=True) for pg in p]
    acc_g = [jnp.dot(pg.astype(BF16), ck, preferred_element_type=F32) for pg, ck in zip(p, ckv)]
    cn = cn_ref[0]
    kn = kn_ref[0][:, :MLA_ROPE]
    s_new = jnp.sum(qa * cn, axis=-1, keepdims=True) + jnp.sum(qp * kn, axis=-1, keepdims=True)
    m = functools.reduce(jnp.maximum, m_g, s_new)
    p_new = jnp.exp2(s_new - m)
    w_g = [jnp.exp2(mg - m) for mg in m_g]
    l = sum(wg * lg for wg, lg in zip(w_g, l_g)) + p_new
    o_ref[0] = (sum(wg * ag for wg, ag in zip(w_g, acc_g)) + p_new * cn) / l

    @pl.when(b == n_seq - 1)
    def _():
        for ahead in range(1, LOOKAHEAD + 1):
            for c in page_copies(lax.rem(b + ahead, n_seq), lax.rem(b + ahead, N_SLOTS)):
                c.wait()


def _paged_attention(qa, qp, ckv_new, kpe_new, cache_ckv, cache_kpe_t, page_table, layer):
    dec_batch, n_pages = page_table.shape
    page = cache_ckv.shape[2]
    head_pad = ((0, 0), (0, SUBLANES - MLA_HEADS), (0, 0))
    qa3 = jnp.pad(qa.reshape(dec_batch, MLA_HEADS, KV_LORA), head_pad)
    qp3 = jnp.pad(qp.reshape(dec_batch, MLA_HEADS, LANES), head_pad)
    cn3 = ckv_new.reshape(dec_batch, 1, KV_LORA)
    kn3 = kpe_new.reshape(dec_batch, 1, LANES)
    row_blk = lambda b, pt: (b, 0, 0)
    out = pl.pallas_call(
        functools.partial(_paged_kernel, layer=layer, n_pages=n_pages, page=page,
                          group_tokens=_pick(n_pages * page, 2048)),
        grid_spec=pltpu.PrefetchScalarGridSpec(
            num_scalar_prefetch=1,
            grid=(dec_batch,),
            in_specs=[
                pl.BlockSpec((1, SUBLANES, KV_LORA), row_blk),
                pl.BlockSpec((1, SUBLANES, LANES), row_blk),
                pl.BlockSpec((1, 1, KV_LORA), row_blk),
                pl.BlockSpec((1, 1, LANES), row_blk),
                pl.BlockSpec(memory_space=pl.ANY),
                pl.BlockSpec(memory_space=pl.ANY),
            ],
            out_specs=pl.BlockSpec((1, SUBLANES, KV_LORA), row_blk),
            scratch_shapes=[
                pltpu.VMEM((N_SLOTS, n_pages * page, KV_LORA), F32),
                pltpu.VMEM((N_SLOTS, n_pages, MLA_ROPE, page), F32),
                pltpu.SemaphoreType.DMA((2, N_SLOTS)),
            ],
        ),
        out_shape=jax.ShapeDtypeStruct((dec_batch, SUBLANES, KV_LORA), F32),
        compiler_params=_cparams("arbitrary"),
        name="paged_attention",
    )(page_table, qa3, qp3, cn3, kn3, cache_ckv, cache_kpe_t)
    return out[:, :MLA_HEADS].reshape(dec_batch, MLA_HEADS * KV_LORA)


def _rwkv_prep_values(p, prev, mu_ref, w0_ref, wd_ref, a0_ref, wa_ref, kk_ref, ka_ref, rk_ref):
    xm = p + (prev - p) * mu_ref[...]
    r = xm[:, 0:RWKV_W]
    k = xm[:, RWKV_W:2 * RWKV_W]
    v = xm[:, 2 * RWKV_W:3 * RWKV_W]
    tail = xm[:, 3 * RWKV_W:]
    lane = lax.broadcasted_iota(jnp.int32, tail.shape, 1)
    lora_in = jnp.where(lane < DECAY_LORA, jnp.tanh(tail), tail).astype(BF16)
    dw = jnp.dot(lora_in, wd_ref[...], preferred_element_type=F32)
    da = jnp.dot(lora_in, wa_ref[...], preferred_element_type=F32)
    z = -(w0_ref[...] + dw)
    softplus = jnp.maximum(z, 0.0) + jnp.log(1.0 + jnp.exp(-jnp.abs(z)))
    w = -softplus - 0.5
    log_decay = -jnp.exp(w)
    a = jax.nn.sigmoid(a0_ref[...] + da)
    ones_bd = _head_ones(RWKV_W, RWKV_HEAD)
    kk = k * kk_ref[...]
    kk = kk / jnp.maximum(jnp.sqrt(_head_sum(kk * kk, ones_bd)), 1e-12)
    k_mod = k * (1.0 + (a - 1.0) * ka_ref[...])
    bonus = _head_sum(r * k_mod * rk_ref[...], ones_bd) * v
    return r, k_mod, v, kk, kk * a, log_decay, bonus


def _rwkv_param_specs(index_map):
    widths = (SHIFT_W, RWKV_W, None, RWKV_W, None, RWKV_W, RWKV_W, RWKV_W)
    return [pl.BlockSpec((LANES, RWKV_W) if w is None else (1, w), index_map) for w in widths]


def _rwkv_params(lw):
    return (lw["rw_mu"], lw["rw_w0"], lw["rw_wd"], lw["rw_a0"], lw["rw_wa"], lw["rw_k_k"], lw["rw_k_a"],
            lw["rw_r_k"])


def _rwkv_prep_sample_kernel(p_ref, prev_ref, mu_ref, w0_ref, wd_ref, a0_ref, wa_ref, kk_ref, ka_ref, rk_ref,
                             r_o, k_o, v_o, kk_o, b_o, d_o, bonus_o):
    r, k_mod, v, kk, b, log_decay, bonus = _rwkv_prep_values(
        p_ref[...], prev_ref[...], mu_ref, w0_ref, wd_ref, a0_ref, wa_ref, kk_ref, ka_ref, rk_ref)
    r_o[...] = r
    k_o[...] = k_mod
    v_o[...] = v
    kk_o[...] = kk
    b_o[...] = b
    d_o[...] = jnp.exp(log_decay)
    bonus_o[...] = bonus


def _rwkv_prep_sample(rw_in, lw, shift_rows):
    rows = rw_in.shape[0]
    const = lambda i: (0, 0)
    return pl.pallas_call(
        _rwkv_prep_sample_kernel,
        grid=(1,),
        in_specs=[pl.BlockSpec((rows, SHIFT_W), const)] * 2 + _rwkv_param_specs(const),
        out_specs=[pl.BlockSpec((rows, RWKV_W), const)] * 7,
        out_shape=[jax.ShapeDtypeStruct((rows, RWKV_W), F32)] * 7,
        compiler_params=_cparams("arbitrary"),
        name="rwkv_prep_sample",
    )(rw_in, shift_rows, *_rwkv_params(lw))


RW_CHUNK = 64
RW_SUB = 16


def _split3(x):
    hi = x.astype(BF16)
    r1 = x - hi.astype(F32)
    mid = r1.astype(BF16)
    lo = (r1 - mid.astype(F32)).astype(BF16)
    return hi, mid, lo


def _block_diag(y, bd_mask):
    return jnp.where(bd_mask, jnp.concatenate([y] * RWKV_HEADS, axis=0), 0.0).astype(BF16)


def _mm(x, y_bd):
    return jnp.dot(x.astype(BF16), y_bd, preferred_element_type=F32)


def _rwkv_chunks(vals, n0, masks):
    bd_mask, tri_incl, strict, incl, same_sub, eye_tiled, ones_bd = masks
    C = RW_CHUNK
    seqs = range(len(vals))
    bd = lambda y: _block_diag(y, bd_mask)
    r, kx, v, kap, bb, lam = ([val[j] for val in vals] for j in range(6))
    cum = [sum(jnp.dot(tri_incl, part, preferred_element_type=F32) for part in _split3(lam[i])) for i in seqs]
    cum_last = [c[C - 1:C, :] for c in cum]
    e_neg = [jnp.exp(-c) for c in cum]
    ap = [jnp.concatenate([kap[i] * jnp.exp(cum[i] - lam[i]), r[i] * jnp.exp(cum[i])], axis=0).astype(BF16)
          for i in seqs]
    rhs = [jnp.concatenate([bd(bb[i] * e_neg[i]), bd(kx[i] * e_neg[i])], axis=0) for i in seqs]
    g = [lax.dot_general(ap[i], rhs[i], (((1,), (1,)), ((), ())), preferred_element_type=F32) for i in seqs]
    l_ab = [jnp.where(strict, gi[:C, :RWKV_W], 0.0) for gi in g]
    l_ak = [jnp.where(strict, gi[:C, RWKV_W:], 0.0) for gi in g]
    q_pb = [jnp.where(incl, gi[C:, :RWKV_W], 0.0) for gi in g]
    q_pk = [jnp.where(incl, gi[C:, RWKV_W:], 0.0) for gi in g]
    apn = [jnp.dot(ap[i], bd(n0[i]), preferred_element_type=F32) for i in seqs]
    lq_v = [_mm(jnp.concatenate([l_ak[i], q_pk[i]], axis=0), bd(v[i])) for i in seqs]
    w = [apn[i][:C] + lq_v[i][:C] for i in seqs]

    l_d = [jnp.where(same_sub, li, 0.0) for li in l_ab]
    l_off = [l_ab[i] - l_d[i] for i in seqs]
    q_inv = [-li for li in l_d]
    power = l_d
    for _ in range(RW_SUB.bit_length() - 2):
        power = [_mm(pw, bd(pw)) for pw in power]
        q_inv = [q_inv[i] + power[i] + _mm(power[i], bd(q_inv[i])) for i in seqs]
    qz = [jnp.dot(q_inv[i].astype(BF16), jnp.concatenate([bd(l_off[i]), bd(w[i])], axis=1),
                  preferred_element_type=F32) for i in seqs]
    n = [l_off[i] + qz[i][:, :RWKV_W] for i in seqs]
    w1 = [w[i] + qz[i][:, RWKV_W:] for i in seqs]
    n_sq = [_mm(ni, bd(ni)) for ni in n]
    y = [w1[i] + _mm(n_sq[i], bd(w1[i])) for i in seqs]
    u = [_mm(n[i], bd(y[i])) - y[i] for i in seqs]

    o = [apn[i][C:] + _mm(q_pb[i], bd(u[i])) + lq_v[i][C:] for i in seqs]

    e_end = [jnp.exp(cum_last[i] - cum[i]) for i in seqs]
    lhs_t = [jnp.concatenate([bb[i] * e_end[i], kx[i] * e_end[i]], axis=0).T.astype(BF16) for i in seqs]
    full = [jnp.dot(lhs_t[i], jnp.concatenate([u[i], v[i]], axis=0).astype(BF16), preferred_element_type=F32)
            for i in seqs]
    heads = [slice(h * RWKV_HEAD, (h + 1) * RWKV_HEAD) for h in range(RWKV_HEADS)]
    delta = [sum(jnp.where(bd_mask[hs], f[hs], 0.0) for hs in heads) for f in full]
    g_wide = [sum(jnp.dot(part, ones_bd, preferred_element_type=F32)
                  for part in _split3(jnp.where(eye_tiled, jnp.exp(cl), 0.0))) for cl in cum_last]
    return o, [n0[i] * g_wide[i] + delta[i] for i in seqs]


def _rwkv_prompt_kernel(p_ref, rg_ref, mu_ref, w0_ref, wd_ref, a0_ref, wa_ref, kk_ref, ka_ref, rk_ref,
                        lnxg_ref, lnxb_ref, o_ref, sfin_ref, state_ref, last_ref):
    nb, C, _ = p_ref.shape
    c_idx = pl.program_id(1)

    @pl.when(c_idx == 0)
    def _():
        state_ref[...] = jnp.zeros(state_ref.shape, F32)
        last_ref[...] = jnp.zeros(last_ref.shape, F32)

    p = p_ref[...].reshape(nb * C, SHIFT_W)
    row = lax.broadcasted_iota(jnp.int32, (nb * C, 1), 0)
    prev = pltpu.roll(p, 1, 0)
    for i in range(nb):
        prev = jnp.where(row == i * C, last_ref[i, 0:1, :], prev)
    vals = _rwkv_prep_values(p, prev, mu_ref, w0_ref, wd_ref, a0_ref, wa_ref, kk_ref, ka_ref, rk_ref)
    bonus = vals[6]

    lane = lax.broadcasted_iota(jnp.int32, (C, RWKV_W), 1)
    t_idx = lax.broadcasted_iota(jnp.int32, (C, RWKV_W), 0)
    s_idx = lane & (RWKV_HEAD - 1)
    row_big = lax.broadcasted_iota(jnp.int32, (RWKV_W, RWKV_W), 0)
    lane_big = lax.broadcasted_iota(jnp.int32, (RWKV_W, RWKV_W), 1)
    bd_mask = _group_index(row_big, RWKV_HEAD) == _group_index(lane_big, RWKV_HEAD)
    ones_bd = jnp.where(bd_mask, 1.0, 0.0).astype(BF16)
    tri_r = lax.broadcasted_iota(jnp.int32, (C, C), 0)
    tri_c = lax.broadcasted_iota(jnp.int32, (C, C), 1)
    masks = (bd_mask, jnp.where(tri_r >= tri_c, 1.0, 0.0).astype(BF16), t_idx > s_idx, t_idx >= s_idx,
             _group_index(t_idx, RW_SUB) == _group_index(s_idx, RW_SUB), t_idx == s_idx, ones_bd)

    per_seq = [tuple(x[i * C:(i + 1) * C] for x in vals[:6]) for i in range(nb)]
    outs, n_new = _rwkv_chunks(per_seq, [state_ref[i] for i in range(nb)], masks)
    for i in range(nb):
        state_ref[i] = n_new[i]
        last_ref[i, 0:1, :] = p[(i + 1) * C - 1:(i + 1) * C, :]
    o = jnp.concatenate(outs, axis=0)

    c = o - _head_sum(o, ones_bd) * (1.0 / RWKV_HEAD)
    var = _head_sum(c * c, ones_bd) * (1.0 / RWKV_HEAD)
    on = c * lax.rsqrt(var + GN_EPS) * lnxg_ref[...] + lnxb_ref[...]
    o_ref[...] = ((on + bonus) * rg_ref[...].reshape(nb * C, RWKV_W)).astype(BF16).reshape(nb, C, RWKV_W)

    @pl.when(c_idx == pl.num_programs(1) - 1)
    def _():
        sfin_ref[...] = state_ref[...]


def _rwkv_prompt(rw_in, rg, lw, batch, seq, nb):
    const = lambda b, c: (0, 0)
    blk = lambda b, c: (b, c, 0)
    o, s_fin = pl.pallas_call(
        _rwkv_prompt_kernel,
        grid=(batch // nb, seq // RW_CHUNK),
        in_specs=[pl.BlockSpec((nb, RW_CHUNK, SHIFT_W), blk), pl.BlockSpec((nb, RW_CHUNK, RWKV_W), blk)]
        + _rwkv_param_specs(const) + [pl.BlockSpec((1, RWKV_W), const)] * 2,
        out_specs=[pl.BlockSpec((nb, RW_CHUNK, RWKV_W), blk),
                   pl.BlockSpec((nb, RWKV_HEAD, RWKV_W), lambda b, c: (b, 0, 0))],
        out_shape=[jax.ShapeDtypeStruct((batch, seq, RWKV_W), BF16),
                   jax.ShapeDtypeStruct((batch, RWKV_HEAD, RWKV_W), F32)],
        scratch_shapes=[pltpu.VMEM((nb, RWKV_HEAD, RWKV_W), F32), pltpu.VMEM((nb, SUBLANES, SHIFT_W), F32)],
        compiler_params=_cparams("arbitrary", "arbitrary"),
        name="rwkv_prompt",
    )(rw_in.reshape(batch, seq, SHIFT_W), rg.reshape(batch, seq, RWKV_W), *_rwkv_params(lw),
      lw["lnx_g"], lw["lnx_b"])
    wkv = jnp.transpose(s_fin.reshape(batch, RWKV_HEAD, RWKV_HEADS, RWKV_HEAD), (0, 2, 3, 1))
    return o.reshape(batch * seq, RWKV_W), wkv


def _rwkv_step_kernel(x_ref, s_ref, *rest, n_prev):
    if n_prev:
        prev_ref, o_ref, snew_ref = rest
    else:
        o_ref, snew_ref = rest
    for j in range(n_prev):
        snew_ref[j] = prev_ref[j]
    kk, b, d, kx, r = (x_ref[j, 0] for j in range(5))
    for vi in range(RWKV_HEAD):
        sv = s_ref[0, 0, vi]
        sa = jnp.sum(sv * kk, axis=0, keepdims=True)
        sn = sv * d - sa * b + x_ref[5, 0, pl.ds(vi, 1), :] * kx
        snew_ref[n_prev, 0, vi] = sn
        o_ref[0, pl.ds(vi, 1), :] = jnp.sum(sn * r, axis=0, keepdims=True)


def _rwkv_step(prep, state_t, layer, prev_stack):
    r, k_mod, v, kk, b, d, _ = prep
    H, N = RWKV_HEADS, RWKV_HEAD
    batch = r.shape[0]
    n_prev = 0 if prev_stack is None else prev_stack.shape[0]
    x = jnp.transpose(jnp.stack([kk, b, d, k_mod, r, v]).reshape(6, batch, H, N), (0, 2, 3, 1))
    head_blk = lambda h: (0, h, 0, 0, 0)
    in_specs = [pl.BlockSpec((6, 1, N, batch), lambda h: (0, h, 0, 0)),
                pl.BlockSpec((1, 1, N, N, batch), lambda h: (layer, h, 0, 0, 0))]
    args = [x, state_t]
    if n_prev:
        in_specs.append(pl.BlockSpec((n_prev, 1, N, N, batch), head_blk))
        args.append(prev_stack)
    o, stack = pl.pallas_call(
        functools.partial(_rwkv_step_kernel, n_prev=n_prev),
        grid=(H,),
        in_specs=in_specs,
        out_specs=[pl.BlockSpec((1, N, batch), lambda h: (h, 0, 0)),
                   pl.BlockSpec((n_prev + 1, 1, N, N, batch), head_blk)],
        out_shape=[jax.ShapeDtypeStruct((H, N, batch), F32),
                   jax.ShapeDtypeStruct((n_prev + 1, H, N, N, batch), F32)],
        compiler_params=_cparams("arbitrary"),
        name="rwkv_step",
    )(*args)
    return jnp.transpose(o, (2, 0, 1)).reshape(batch, H * N), stack


def _mix_out(mixed, x_ref, mod_ref, wout_ref, gpost_ref, y_ref):
    out = jnp.dot(mixed, wout_ref[...], preferred_element_type=F32)
    gate = _rows2d(mod_ref)[:, 2 * D_MODEL:]
    y_ref[...] = x_ref[...] + gate * _rms(out, gpost_ref[...])


def _out_prompt_kernel(om_ref, orw_ref, gm_ref, x_ref, mod_ref, wout_ref, gpost_ref, y_ref):
    mixed = jnp.concatenate([om_ref[...], orw_ref[...], gm_ref[...]], axis=-1)
    _mix_out(mixed, x_ref, mod_ref, wout_ref, gpost_ref, y_ref)


def _out_sample_kernel(lat_ref, mg_ref, orw_ref, bonus_ref, rg_ref, gm_ref, x_ref, mod_ref, wout_ref, gpost_ref,
                       lnxg_ref, lnxb_ref, wuv_ref, y_ref):
    lat = lat_ref[...]
    o_mla = jnp.concatenate(
        [jnp.dot(lat[:, h * KV_LORA:(h + 1) * KV_LORA].astype(BF16), wuv_ref[h],
                 preferred_element_type=F32) for h in range(MLA_HEADS)], axis=-1)
    o_mla = (o_mla * mg_ref[...].astype(F32)).astype(BF16)
    ones_bd = _head_ones(RWKV_W, RWKV_HEAD)
    o = orw_ref[...]
    c = o - _head_sum(o, ones_bd) * (1.0 / RWKV_HEAD)
    var = _head_sum(c * c, ones_bd) * (1.0 / RWKV_HEAD)
    on = c * lax.rsqrt(var + GN_EPS) * lnxg_ref[...] + lnxb_ref[...]
    o_rw = ((on + bonus_ref[...]) * rg_ref[...]).astype(BF16)
    mixed = jnp.concatenate([o_mla, o_rw, gm_ref[...]], axis=-1)
    _mix_out(mixed, x_ref, mod_ref, wout_ref, gpost_ref, y_ref)


def _out_prompt(o_mla, o_rw, gm, x2d, mod_b, lw, seq, tm):
    rows = x2d.shape[0]
    tiles_per_seq = seq // tm
    const = lambda i: (0, 0)
    row_blk = lambda i: (i, 0)
    return pl.pallas_call(
        _out_prompt_kernel,
        grid=(rows // tm,),
        in_specs=[
            pl.BlockSpec((tm, MLA_W), row_blk),
            pl.BlockSpec((tm, RWKV_W), row_blk),
            pl.BlockSpec((tm, GM_W), row_blk),
            pl.BlockSpec((tm, D_MODEL), row_blk),
            pl.BlockSpec((1, 1, 3 * D_MODEL), lambda i: (i // tiles_per_seq, 0, 0)),
            pl.BlockSpec((D_MODEL, D_MODEL), const),
            pl.BlockSpec((1, D_MODEL), const),
        ],
        out_specs=pl.BlockSpec((tm, D_MODEL), row_blk),
        out_shape=jax.ShapeDtypeStruct((rows, D_MODEL), F32),
        compiler_params=_cparams("arbitrary"),
        name="out_prompt",
    )(o_mla, o_rw, gm, x2d, mod_b, lw["w_out"], lw["g_post"])


def _out_sample(o_lat, mg, o_rw, bonus, rg, gm, x2d, mod_rows, lw):
    rows = x2d.shape[0]
    const = lambda i: (0, 0)
    full = lambda w: pl.BlockSpec((rows, w), const)
    return pl.pallas_call(
        _out_sample_kernel,
        grid=(1,),
        in_specs=[
            full(MLA_HEADS * KV_LORA), full(MLA_W), full(RWKV_W), full(RWKV_W), full(RWKV_W), full(GM_W),
            full(D_MODEL), full(3 * D_MODEL),
            pl.BlockSpec((D_MODEL, D_MODEL), const),
            pl.BlockSpec((1, D_MODEL), const),
            pl.BlockSpec((1, RWKV_W), const),
            pl.BlockSpec((1, RWKV_W), const),
            pl.BlockSpec((MLA_HEADS, KV_LORA, MLA_V), lambda i: (0, 0, 0)),
        ],
        out_specs=full(D_MODEL),
        out_shape=jax.ShapeDtypeStruct((rows, D_MODEL), F32),
        compiler_params=_cparams("arbitrary"),
        name="out_sample",
    )(o_lat, mg, o_rw, bonus, rg, gm, x2d, mod_rows, lw["w_out"], lw["g_post"], lw["lnx_g"], lw["lnx_b"],
      lw["w_uv_h"])


def _swap_halves(w):
    half = MLA_ROPE // 2
    return jnp.concatenate([w[..., half:], w[..., :half]], axis=-1)


def _rope_tables(pos):
    half = MLA_ROPE // 2
    inv = ROPE_THETA ** (-jnp.arange(half, dtype=F32) / half)
    ang = pos.astype(F32)[:, None] * inv[None, :]
    cos, sin = jnp.cos(ang), jnp.sin(ang)
    zeros = jnp.zeros((pos.shape[0], LANES - MLA_ROPE), F32)
    return (jnp.concatenate([cos, cos, zeros], axis=-1), jnp.concatenate([-sin, sin, zeros], axis=-1))


def _layer_weights(l, w_in, norm_pre_g, norm_post_g, q_norm_g, kv_norm_g, w_uq, w_uk, w_uv, rw_mu, rw_w0,
                   rw_w_decay_up, rw_a0, rw_w_a_up, rw_k_k, rw_k_a, rw_r_k, rw_lnx_g, rw_lnx_b, gm_ln_g,
                   gm_ln_b, gm_w_s, gm_b_s, w_out):
    w = w_in[l]
    split_lo, split_hi = Q_LORA + KV_LORA, Q_LORA + KV_LORA + MLA_ROPE
    kr = w[:, split_lo:split_hi]
    w_in_p = jnp.concatenate([w[:, :Q_LORA], kr, _swap_halves(kr), w[:, Q_LORA:split_lo], w[:, split_hi:]],
                             axis=1).astype(BF16)
    uq = w_uq[l]
    pe = uq[..., MLA_NOPE:]
    w_uq_p = jnp.concatenate([uq, _swap_halves(pe)], axis=-1).reshape(Q_LORA, MLA_HEADS * QHEAD_W).astype(BF16)
    zeros_lora = jnp.zeros((LANES - DECAY_LORA, RWKV_W), F32)
    row = lambda a: a.reshape(1, -1)
    return {
        "w_in": w_in_p, "g_pre": row(norm_pre_g[l]), "g_post": row(norm_post_g[l]),
        "g_q": row(q_norm_g[l]), "g_kv": row(kv_norm_g[l]), "w_uq": w_uq_p,
        "w_uk": w_uk[l].reshape(KV_LORA, MLA_W).astype(BF16),
        "w_uv": w_uv[l].reshape(KV_LORA, MLA_W).astype(BF16),
        "w_ukt": jnp.transpose(w_uk[l], (1, 2, 0)).astype(BF16),
        "w_uv_h": jnp.transpose(w_uv[l], (1, 0, 2)).astype(BF16),
        "rw_mu": row(rw_mu[l]), "rw_w0": row(rw_w0[l]), "rw_a0": row(rw_a0[l]),
        "rw_wd": jnp.concatenate([rw_w_decay_up[l], zeros_lora], axis=0).astype(BF16),
        "rw_wa": jnp.concatenate([zeros_lora, rw_w_a_up[l]], axis=0).astype(BF16),
        "rw_k_k": row(rw_k_k[l]), "rw_k_a": row(rw_k_a[l]), "rw_r_k": row(rw_r_k[l]),
        "lnx_g": row(rw_lnx_g[l]), "lnx_b": row(rw_lnx_b[l]),
        "ln_g": row(gm_ln_g[l]), "ln_b": row(gm_ln_b[l]),
        "gm_ws": gm_w_s[l],
        "gm_bias": jnp.repeat(gm_b_s[l].T, GM_GROUP_W, axis=1),
        "gm_coef0": row(jnp.repeat(gm_w_s[l][:, 0, 0], GM_GROUP_W)),
        "gm_bias0": row(jnp.repeat(gm_b_s[l][:, 0], GM_GROUP_W)),
        "w_out": w_out[l].astype(BF16),
    }


def _pick(full, want):
    return want if full % want == 0 else full


def kernel(x_prompt, x_sample, c_prompt, c_sample, cache_ckv, cache_kpe, page_table, state_wkv, state_shift, w_ada, b_ada, norm_pre_g, norm_post_g, w_in, q_norm_g, kv_norm_g, w_uq, w_uk, w_uv, rw_mu, rw_w0, rw_w_decay_up, rw_a0, rw_w_a_up, rw_k_k, rw_k_a, rw_r_k, rw_lnx_g, rw_lnx_b, gm_ln_g, gm_ln_b, gm_w_s, gm_b_s, w_out):
    batch, seq, _ = x_prompt.shape
    dec_batch, dec_seq, _ = x_sample.shape
    depth = w_in.shape[0]
    n_pages = page_table.shape[1]
    past_len = n_pages * cache_ckv.shape[2]
    assert dec_seq == 1 and seq % CHUNK == 0

    assert seq % RW_CHUNK == 0
    tm_proj = _pick(seq, 512)
    tm_out = _pick(seq, 1024)
    tq = _pick(seq, 512)
    rw_nb = _pick(batch, 16)

    mod = _ada_mod(jnp.concatenate([c_prompt, c_sample], axis=0), w_ada, b_ada)
    cs_p, sn_p = _rope_tables(jnp.arange(seq))
    cs_s, sn_s = _rope_tables(jnp.full((dec_batch,), past_len))
    cache_kpe_t = jnp.swapaxes(cache_kpe, 2, 3)

    y_p = x_prompt.reshape(batch * seq, D_MODEL)
    y_s = x_sample.reshape(dec_batch, D_MODEL)
    outs = {k: [] for k in ("ckv_s", "kpe_s", "wkv_p", "sh_p", "sh_s", "vc_p", "vc_s")}
    prompt_stacks = None
    wkv_s_stack = None
    state_wkv_t = jnp.transpose(state_wkv, (0, 2, 3, 4, 1))
    for l in range(depth):
        lw = _layer_weights(l, w_in, norm_pre_g, norm_post_g, q_norm_g, kv_norm_g, w_uq, w_uk, w_uv, rw_mu,
                            rw_w0, rw_w_decay_up, rw_a0, rw_w_a_up, rw_k_k, rw_k_a, rw_r_k, rw_lnx_g,
                            rw_lnx_b, gm_ln_g, gm_ln_b, gm_w_s, gm_b_s, w_out)
        mod_p = mod[l, :batch].reshape(batch, 1, 3 * D_MODEL)
        q, k, v, mg, ckv_stack, kpe_stack, rw_in, rg, gm, vn_last = _proj_prompt(
            y_p, mod_p, lw, cs_p, sn_p, batch, seq, tm_proj, prompt_stacks)
        prompt_stacks = (ckv_stack, kpe_stack)
        o_mla = _flash(q, k, v, mg, batch, seq, tq)
        o_rw, wkv_new = _rwkv_prompt(rw_in, rg, lw, batch, seq, rw_nb)
        y_p = _out_prompt(o_mla, o_rw, gm, y_p, mod_p, lw, seq, tm_out)
        outs["wkv_p"].append(wkv_new)
        outs["sh_p"].append(rw_in.reshape(batch, seq, SHIFT_W)[:, -1])
        outs["vc_p"].append(vn_last)

        mod_s = mod[l, batch:]
        qa, qp, mg_s, ckv_s, kpe_s, rw_s, rg_s, gm_s, vn_s = _proj_sample(y_s, mod_s, lw, cs_s, sn_s)
        o_lat = _paged_attention(qa, qp, ckv_s, kpe_s, cache_ckv, cache_kpe_t, page_table, l)
        prep_s = _rwkv_prep_sample(rw_s, lw, state_shift[l])
        o_rw_s, wkv_s_stack = _rwkv_step(prep_s, state_wkv_t, l, wkv_s_stack)
        y_s = _out_sample(o_lat, mg_s, o_rw_s, prep_s[6], rg_s, gm_s, y_s, mod_s, lw)
        outs["ckv_s"].append(ckv_s.reshape(dec_batch, 1, KV_LORA))
        outs["kpe_s"].append(kpe_s[:, :MLA_ROPE].reshape(dec_batch, 1, MLA_ROPE))
        outs["sh_s"].append(rw_s)
        outs["vc_s"].append(vn_s.reshape(dec_batch, 1, GM_W))

    st = lambda name: jnp.stack(outs[name])
    ckv_p, kpe_p = prompt_stacks
    return (y_p.reshape(batch, seq, D_MODEL), y_s.reshape(dec_batch, 1, D_MODEL),
            ckv_p.reshape(depth, batch, seq, KV_LORA), kpe_p.reshape(depth, batch, seq, MLA_ROPE),
            st("ckv_s"), st("kpe_s"), st("wkv_p"), jnp.transpose(wkv_s_stack, (0, 4, 1, 2, 3)),
            st("sh_p"), st("sh_s"), st("vc_p"), st("vc_s"))
```

```python
import functools

import numpy as np
import jax
import jax.numpy as jnp
from jax import lax
from jax.experimental import pallas as pl
from jax.experimental.pallas import tpu as pltpu

F32 = jnp.float32
BF16 = jnp.bfloat16

D_MODEL = 1024
MLA_V = 128
MLA_W = D_MODEL // 2
MLA_HEADS = MLA_W // MLA_V
MLA_NOPE = 128
MLA_ROPE = 64
MLA_QK = MLA_NOPE + MLA_ROPE
Q_LORA = (3 * D_MODEL) // 8
KV_LORA = D_MODEL // 4
ROPE_THETA = 10000.0
ATTN_SCALE = MLA_QK ** -0.5
Q_SCALE = ATTN_SCALE * float(np.log2(np.e))
RWKV_W = D_MODEL // 4
RWKV_HEAD = 64
RWKV_HEADS = RWKV_W // RWKV_HEAD
DECAY_LORA = 64
ICLR_LORA = 64
SHIFT_W = 3 * RWKV_W + DECAY_LORA + ICLR_LORA
GN_EPS = 64e-5
GM_W = D_MODEL // 4
GM_GROUPS = 4
GM_GROUP_W = GM_W // GM_GROUPS
CHUNK = 128
RMS_EPS = 1e-6
LN_EPS = 1e-5

LANES = 128
SUBLANES = 8
VMEM_LIMIT_BYTES = 56 * 1024 * 1024

QHEAD_W = 2 * LANES
OFF_Q = 0
OFF_KR = OFF_Q + Q_LORA
OFF_KV = OFF_KR + LANES
OFF_MG = OFF_KV + KV_LORA
OFF_RW = OFF_MG + MLA_W
OFF_RG = OFF_RW + SHIFT_W
OFF_GU = OFF_RG + RWKV_W
OFF_GV = OFF_GU + GM_W
OFF_GG = OFF_GV + GM_W
IN_W_P = OFF_GG + GM_W

NEG_BIG = -1e30


def _cparams(*sem):
    return pltpu.CompilerParams(dimension_semantics=sem, vmem_limit_bytes=VMEM_LIMIT_BYTES)


def _silu(x):
    return x * jax.nn.sigmoid(x)


def _rms(x, g, eps=RMS_EPS):
    return x * lax.rsqrt(jnp.mean(x * x, axis=-1, keepdims=True) + eps) * g


def _rows2d(ref):
    m = ref[...]
    return m.reshape(m.shape[-2], m.shape[-1])


def _group_index(idx, group):
    shift = group.bit_length() - 1
    assert 1 << shift == group
    return lax.shift_right_logical(idx, shift)


def _head_ones(width, head):
    r = _group_index(lax.broadcasted_iota(jnp.int32, (width, width), 0), head)
    c = _group_index(lax.broadcasted_iota(jnp.int32, (width, width), 1), head)
    return jnp.where(r == c, 1.0, 0.0).astype(BF16)


def _head_sum(x, ones_bd):
    hi = x.astype(BF16)
    lo = (x - hi.astype(F32)).astype(BF16)
    return (jnp.dot(hi, ones_bd, preferred_element_type=F32)
            + jnp.dot(lo, ones_bd, preferred_element_type=F32))


def _rope_pair(blk, cs, sn):
    return blk * cs + pltpu.roll(blk, MLA_ROPE, 1) * sn


def _ada_kernel(c_ref, w_ref, b_ref, o_ref):
    c = _silu(c_ref[...]).astype(BF16)
    o_ref[0] = jnp.dot(c, w_ref[0].astype(BF16), preferred_element_type=F32) + b_ref[0]


def _ada_mod(c_all, w_ada, b_ada):
    depth = w_ada.shape[0]
    n_rows = c_all.shape[0]
    n_tiles = (3 * D_MODEL) // D_MODEL
    return pl.pallas_call(
        _ada_kernel,
        grid=(depth, n_tiles),
        in_specs=[
            pl.BlockSpec((n_rows, D_MODEL), lambda l, j: (0, 0)),
            pl.BlockSpec((1, D_MODEL, D_MODEL), lambda l, j: (l, 0, j)),
            pl.BlockSpec((1, 1, D_MODEL), lambda l, j: (l, 0, j)),
        ],
        out_specs=pl.BlockSpec((1, n_rows, D_MODEL), lambda l, j: (l, 0, j)),
        out_shape=jax.ShapeDtypeStruct((depth, n_rows, 3 * D_MODEL), F32),
        compiler_params=_cparams("arbitrary", "arbitrary"),
        name="ada_mod",
    )(c_all, w_ada, b_ada.reshape(depth, 1, 3 * D_MODEL))


def _proj_common(x_ref, mod_ref, gpre_ref, win_ref):
    mod = _rows2d(mod_ref)
    shift = mod[:, 0:D_MODEL]
    scale = mod[:, D_MODEL:2 * D_MODEL]
    h = _rms(x_ref[...], gpre_ref[...]) * (1.0 + scale) + shift
    hb = h.astype(BF16)

    def seg(off, width):
        return jnp.dot(hb, win_ref[:, off:off + width], preferred_element_type=F32)

    return seg


def _q_heads(seg, gq_ref, wuq_ref, cs, sn):
    q_kr = seg(OFF_Q, Q_LORA + LANES)
    qn = _rms(q_kr[:, :Q_LORA], gq_ref[...]).astype(BF16)
    q_all = jnp.dot(qn, wuq_ref[...], preferred_element_type=F32)
    heads = []
    for h in range(MLA_HEADS):
        nope = q_all[:, h * QHEAD_W:h * QHEAD_W + LANES] * Q_SCALE
        pe = _rope_pair(q_all[:, h * QHEAD_W + LANES:(h + 1) * QHEAD_W], cs, sn) * Q_SCALE
        heads.append((nope, pe))
    return heads, _rope_pair(q_kr[:, Q_LORA:], cs, sn)


def _gm_norm(seg, lng_ref, lnb_ref):
    v = seg(OFF_GV, GM_W)
    mu = jnp.mean(v, axis=-1, keepdims=True)
    c = v - mu
    var = jnp.mean(c * c, axis=-1, keepdims=True)
    return c * lax.rsqrt(var + LN_EPS) * lng_ref[...] + lnb_ref[...]


def _proj_prompt_kernel(x_ref, mod_ref, gpre_ref, win_ref, gq_ref, wuq_ref, gkv_ref, wuk_ref, wuv_ref,
                        cs_ref, sn_ref, lng_ref, lnb_ref, ws_ref, bs_ref, *rest, tiles_per_seq, n_prev):
    if n_prev:
        ckv_prev_ref, kpe_prev_ref = rest[:2]
        rest = rest[2:]
    q_ref, k_ref, v_ref, mg_ref, ckv_ref, kpe_ref, rw_ref, rg_ref, gm_ref, vn_ref = rest
    for j in range(n_prev):
        ckv_ref[j] = ckv_prev_ref[j]
        kpe_ref[j] = kpe_prev_ref[j]
    seg = _proj_common(x_ref, mod_ref, gpre_ref, win_ref)
    cs = cs_ref[...]
    sn = sn_ref[...]
    tm = x_ref.shape[0]

    heads, kpe = _q_heads(seg, gq_ref, wuq_ref, cs, sn)
    for h, (nope, pe) in enumerate(heads):
        q_ref[:, h * QHEAD_W:h * QHEAD_W + LANES] = nope.astype(BF16)
        q_ref[:, h * QHEAD_W + LANES:(h + 1) * QHEAD_W] = pe.astype(BF16)
    ckv = _rms(seg(OFF_KV, KV_LORA), gkv_ref[...])
    ckv_ref[n_prev] = ckv
    kpe_ref[n_prev] = kpe[:, :MLA_ROPE]
    ckv_b = ckv.astype(BF16)
    k_nope = jnp.dot(ckv_b, wuk_ref[...], preferred_element_type=F32)
    v_ref[...] = jnp.dot(ckv_b, wuv_ref[...], preferred_element_type=F32).astype(BF16)
    kpe_b = kpe.astype(BF16)
    for h in range(MLA_HEADS):
        k_ref[:, h * QHEAD_W:h * QHEAD_W + LANES] = k_nope[:, h * MLA_NOPE:(h + 1) * MLA_NOPE].astype(BF16)
        k_ref[:, h * QHEAD_W + LANES:(h + 1) * QHEAD_W] = kpe_b
    mg_ref[...] = _silu(seg(OFF_MG, MLA_W)).astype(BF16)

    rw_ref[...] = seg(OFF_RW, SHIFT_W)
    rg_ref[...] = _silu(seg(OFF_RG, RWKV_W))

    vn = _gm_norm(seg, lng_ref, lnb_ref)
    gate_u = _silu(seg(OFF_GG, GM_W)) * seg(OFF_GU, GM_W)
    row = lax.broadcasted_iota(jnp.int32, (CHUNK, CHUNK), 0)
    col = lax.broadcasted_iota(jnp.int32, (CHUNK, CHUNK), 1)
    w_cat = jnp.concatenate([jnp.where(row >= col, ws_ref[g], 0.0) for g in range(GM_GROUPS)],
                            axis=1).astype(BF16)
    own_group = (_group_index(lax.broadcasted_iota(jnp.int32, (GM_GROUPS * CHUNK, GM_W), 0), CHUNK)
                 == _group_index(lax.broadcasted_iota(jnp.int32, (GM_GROUPS * CHUNK, GM_W), 1), GM_GROUP_W))
    for c in range(tm // CHUNK):
        vc = vn[c * CHUNK:(c + 1) * CHUNK]
        vc_groups = jnp.where(own_group, jnp.concatenate([vc] * GM_GROUPS, axis=0), 0.0).astype(BF16)
        z = bs_ref[...] + jnp.dot(w_cat, vc_groups, preferred_element_type=F32)
        gm_ref[c * CHUNK:(c + 1) * CHUNK, :] = (gate_u[c * CHUNK:(c + 1) * CHUNK] * z).astype(BF16)

    @pl.when(pl.program_id(0) % tiles_per_seq == tiles_per_seq - 1)
    def _():
        vn_ref[0] = vn[tm - CHUNK:, :]


def _proj_prompt(x2d, mod_b, lw, cs, sn, batch, seq, tm, prev_stacks):
    rows = x2d.shape[0]
    tiles_per_seq = seq // tm
    n_prev = 0 if prev_stacks is None else prev_stacks[0].shape[0]
    const = lambda i: (0, 0)
    row_blk = lambda i: (i, 0)
    stack_blk = lambda i: (0, i, 0)
    outs = [
        (QHEAD_W * MLA_HEADS, BF16), (QHEAD_W * MLA_HEADS, BF16), (MLA_W, BF16), (MLA_W, BF16),
        (KV_LORA, F32), (MLA_ROPE, F32), (SHIFT_W, F32), (RWKV_W, F32), (GM_W, BF16),
    ]
    out_shape = [jax.ShapeDtypeStruct((rows, w), dt) for w, dt in outs]
    out_specs = [pl.BlockSpec((tm, w), row_blk) for w, _ in outs]
    for idx in (4, 5):
        w, dt = outs[idx]
        out_shape[idx] = jax.ShapeDtypeStruct((n_prev + 1, rows, w), dt)
        out_specs[idx] = pl.BlockSpec((n_prev + 1, tm, w), stack_blk)
    out_shape.append(jax.ShapeDtypeStruct((batch, CHUNK, GM_W), F32))
    out_specs.append(pl.BlockSpec((1, CHUNK, GM_W), lambda i: (i // tiles_per_seq, 0, 0)))
    prev_specs = [] if not n_prev else [pl.BlockSpec((n_prev, tm, KV_LORA), stack_blk),
                                        pl.BlockSpec((n_prev, tm, MLA_ROPE), stack_blk)]
    return pl.pallas_call(
        functools.partial(_proj_prompt_kernel, tiles_per_seq=tiles_per_seq, n_prev=n_prev),
        grid=(rows // tm,),
        in_specs=[
            pl.BlockSpec((tm, D_MODEL), row_blk),
            pl.BlockSpec((1, 1, 3 * D_MODEL), lambda i: (i // tiles_per_seq, 0, 0)),
            pl.BlockSpec((1, D_MODEL), const),
            pl.BlockSpec((D_MODEL, IN_W_P), const),
            pl.BlockSpec((1, Q_LORA), const),
            pl.BlockSpec((Q_LORA, QHEAD_W * MLA_HEADS), const),
            pl.BlockSpec((1, KV_LORA), const),
            pl.BlockSpec((KV_LORA, MLA_W), const),
            pl.BlockSpec((KV_LORA, MLA_W), const),
            pl.BlockSpec((tm, LANES), lambda i: (i % tiles_per_seq, 0)),
            pl.BlockSpec((tm, LANES), lambda i: (i % tiles_per_seq, 0)),
            pl.BlockSpec((1, GM_W), const),
            pl.BlockSpec((1, GM_W), const),
            pl.BlockSpec((GM_GROUPS, CHUNK, CHUNK), lambda i: (0, 0, 0)),
            pl.BlockSpec((CHUNK, GM_W), const),
        ] + prev_specs,
        out_specs=out_specs,
        out_shape=out_shape,
        compiler_params=_cparams("arbitrary"),
        name="proj_prompt",
    )(x2d, mod_b, lw["g_pre"], lw["w_in"], lw["g_q"], lw["w_uq"], lw["g_kv"], lw["w_uk"], lw["w_uv"],
      cs, sn, lw["ln_g"], lw["ln_b"], lw["gm_ws"], lw["gm_bias"], *(prev_stacks or ()))


def _proj_sample_kernel(x_ref, mod_ref, gpre_ref, win_ref, gq_ref, wuq_ref, gkv_ref, wukt_ref,
                        cs_ref, sn_ref, lng_ref, lnb_ref, coef_ref, bias_ref,
                        qa_ref, qp_ref, mg_ref, ckv_ref, kpe_ref, rw_ref, rg_ref, gm_ref, vn_ref):
    seg = _proj_common(x_ref, mod_ref, gpre_ref, win_ref)
    cs = cs_ref[...]
    sn = sn_ref[...]
    heads, kpe = _q_heads(seg, gq_ref, wuq_ref, cs, sn)
    for h, (nope, pe) in enumerate(heads):
        qa_ref[:, h * KV_LORA:(h + 1) * KV_LORA] = jnp.dot(nope.astype(BF16), wukt_ref[h],
                                                           preferred_element_type=F32)
        qp_ref[:, h * LANES:(h + 1) * LANES] = pe
    ckv_ref[...] = _rms(seg(OFF_KV, KV_LORA), gkv_ref[...])
    kpe_ref[...] = kpe
    mg_ref[...] = _silu(seg(OFF_MG, MLA_W)).astype(BF16)
    rw_ref[...] = seg(OFF_RW, SHIFT_W)
    rg_ref[...] = _silu(seg(OFF_RG, RWKV_W))
    vn = _gm_norm(seg, lng_ref, lnb_ref)
    vn_ref[...] = vn
    z = vn * coef_ref[...] + bias_ref[...]
    gm_ref[...] = (_silu(seg(OFF_GG, GM_W)) * seg(OFF_GU, GM_W) * z).astype(BF16)


def _proj_sample(x2d, mod_rows, lw, cs, sn):
    rows = x2d.shape[0]
    const = lambda i: (0, 0)
    outs = [
        (KV_LORA * MLA_HEADS, F32), (LANES * MLA_HEADS, F32), (MLA_W, BF16), (KV_LORA, F32), (LANES, F32),
        (SHIFT_W, F32), (RWKV_W, F32), (GM_W, BF16), (GM_W, F32),
    ]
    return pl.pallas_call(
        _proj_sample_kernel,
        grid=(1,),
        in_specs=[
            pl.BlockSpec((rows, D_MODEL), const),
            pl.BlockSpec((rows, 3 * D_MODEL), const),
            pl.BlockSpec((1, D_MODEL), const),
            pl.BlockSpec((D_MODEL, IN_W_P), const),
            pl.BlockSpec((1, Q_LORA), const),
            pl.BlockSpec((Q_LORA, QHEAD_W * MLA_HEADS), const),
            pl.BlockSpec((1, KV_LORA), const),
            pl.BlockSpec((MLA_HEADS, MLA_NOPE, KV_LORA), lambda i: (0, 0, 0)),
            pl.BlockSpec((rows, LANES), const),
            pl.BlockSpec((rows, LANES), const),
            pl.BlockSpec((1, GM_W), const),
            pl.BlockSpec((1, GM_W), const),
            pl.BlockSpec((1, GM_W), const),
            pl.BlockSpec((1, GM_W), const),
        ],
        out_specs=[pl.BlockSpec((rows, w), const) for w, _ in outs],
        out_shape=[jax.ShapeDtypeStruct((rows, w), dt) for w, dt in outs],
        compiler_params=_cparams("arbitrary"),
        name="proj_sample",
    )(x2d, mod_rows, lw["g_pre"], lw["w_in"], lw["g_q"], lw["w_uq"], lw["g_kv"], lw["w_ukt"],
      cs, sn, lw["ln_g"], lw["ln_b"], lw["gm_coef0"], lw["gm_bias0"])


def _flash_kernel(q_ref, k_ref, v_ref, g_ref, o_ref, *, tq):
    seq = q_ref.shape[1]
    row = lax.broadcasted_iota(jnp.int32, (tq, tq), 0)
    col = lax.broadcasted_iota(jnp.int32, (tq, tq), 1)
    for qi in range(seq // tq):
        rows = slice(qi * tq, (qi + 1) * tq)
        q = q_ref[0, rows, :]
        m = jnp.full((tq, 1), NEG_BIG, F32)
        l = jnp.zeros((tq, 1), F32)
        acc = jnp.zeros((tq, MLA_V), F32)
        for ki in range(qi + 1):
            cols = slice(ki * tq, (ki + 1) * tq)
            s = lax.dot_general(q, k_ref[0, cols, :], (((1,), (1,)), ((), ())),
                                preferred_element_type=F32)
            if ki == qi:
                s = jnp.where(row >= col, s, NEG_BIG)
            m_new = jnp.maximum(m, jnp.max(s, axis=-1, keepdims=True))
            alpha = jnp.exp2(m - m_new)
            p = jnp.exp2(s - m_new)
            l = alpha * l + jnp.sum(p, axis=-1, keepdims=True)
            acc = alpha * acc + jnp.dot(p.astype(BF16), v_ref[0, cols, :], preferred_element_type=F32)
            m = m_new
        o_ref[0, rows, :] = (acc / l * g_ref[0, rows, :].astype(F32)).astype(BF16)


def _flash(q, k, v, g, batch, seq, tq):
    q3 = q.reshape(batch, seq, QHEAD_W * MLA_HEADS)
    k3 = k.reshape(batch, seq, QHEAD_W * MLA_HEADS)
    v3 = v.reshape(batch, seq, MLA_W)
    g3 = g.reshape(batch, seq, MLA_W)
    head_blk = lambda b, h: (b, 0, h)
    out = pl.pallas_call(
        functools.partial(_flash_kernel, tq=tq),
        grid=(batch, MLA_HEADS),
        in_specs=[
            pl.BlockSpec((1, seq, QHEAD_W), head_blk),
            pl.BlockSpec((1, seq, QHEAD_W), head_blk),
            pl.BlockSpec((1, seq, MLA_V), head_blk),
            pl.BlockSpec((1, seq, MLA_V), head_blk),
        ],
        out_specs=pl.BlockSpec((1, seq, MLA_V), head_blk),
        out_shape=jax.ShapeDtypeStruct((batch, seq, MLA_W), BF16),
        compiler_params=_cparams("arbitrary", "arbitrary"),
        name="flash_prompt",
    )(q3, k3, v3, g3)
    return out.reshape(batch * seq, MLA_W)


N_SLOTS = 3
LOOKAHEAD = N_SLOTS - 1


def _paged_kernel(pt_ref, qa_ref, qp_ref, cn_ref, kn_ref, ckv_hbm, kpe_hbm, o_ref, ckv_buf, kpe_buf, sem,
                  *, layer, n_pages, page, group_tokens):
    b = pl.program_id(0)
    n_seq = pl.num_programs(0)
    slot = lax.rem(b, N_SLOTS)

    def page_copies(seq_idx, sl):
        copies = []
        for i in range(n_pages):
            pg = pt_ref[seq_idx, i]
            copies.append(pltpu.make_async_copy(
                ckv_hbm.at[layer, pg], ckv_buf.at[sl, pl.ds(i * page, page), :], sem.at[0, sl]))
            copies.append(pltpu.make_async_copy(kpe_hbm.at[layer, pg], kpe_buf.at[sl, i], sem.at[1, sl]))
        return copies

    def start_all(copies):
        for i, c in enumerate(copies):
            c.start(priority=(i // 2) % 2)

    @pl.when(b == 0)
    def _():
        for ahead in range(LOOKAHEAD):
            start_all(page_copies(ahead, ahead))

    for c in page_copies(b, slot):
        c.wait()

    nxt = lax.rem(b + LOOKAHEAD, n_seq)
    nxt_slot = lax.rem(b + LOOKAHEAD, N_SLOTS)
    start_all(page_copies(nxt, nxt_slot))

    qa = qa_ref[0]
    qp = qp_ref[0][:, :MLA_ROPE]
    qa_b = qa.astype(BF16)
    qp_b = qp.astype(BF16)
    group_pages = group_tokens // page
    groups = [slice(g * group_tokens, (g + 1) * group_tokens) for g in range(n_pages // group_pages)]
    ckv = [ckv_buf[slot, gs, :].astype(BF16) for gs in groups]
    kpe_t = [jnp.concatenate([kpe_buf[slot, g * group_pages + i] for i in range(group_pages)],
                             axis=1).astype(BF16) for g in range(len(groups))]
    s = [lax.dot_general(qa_b, ck, (((1,), (1,)), ((), ())), preferred_element_type=F32)
         + jnp.dot(qp_b, kp, preferred_element_type=F32) for ck, kp in zip(ckv, kpe_t)]
    m_g = [jnp.max(sg, axis=-1, keepdims=True) for sg in s]
    p = [jnp.exp2(sg - mg) for sg, mg in zip(s, m_g)]
    l_g = [jnp.sum(pg, axis=-1, keepdims=True) for pg in p]
    acc_g = [jnp.dot(pg.astype(BF16), ck, preferred_element_type=F32) for pg, ck in zip(p, ckv)]
    cn = cn_ref[0]
    kn = kn_ref[0][:, :MLA_ROPE]
    s_new = jnp.sum(qa * cn, axis=-1, keepdims=True) + jnp.sum(qp * kn, axis=-1, keepdims=True)
    m = functools.reduce(jnp.maximum, m_g, s_new)
    p_new = jnp.exp2(s_new - m)
    w_g = [jnp.exp2(mg - m) for mg in m_g]
    l = sum(wg * lg for wg, lg in zip(w_g, l_g)) + p_new
    o_ref[0] = (sum(wg * ag for wg, ag in zip(w_g, acc_g)) + p_new * cn) / l

    @pl.when(b == n_seq - 1)
    def _():
        for ahead in range(1, LOOKAHEAD + 1):
            for c in page_copies(lax.rem(b + ahead, n_seq), lax.rem(b + ahead, N_SLOTS)):
                c.wait()


def _paged_attention(qa, qp, ckv_new, kpe_new, cache_ckv, cache_kpe_t, page_table, layer):
    dec_batch, n_pages = page_table.shape
    page = cache_ckv.shape[2]
    head_pad = ((0, 0), (0, SUBLANES - MLA_HEADS), (0, 0))
    qa3 = jnp.pad(qa.reshape(dec_batch, MLA_HEADS, KV_LORA), head_pad)
    qp3 = jnp.pad(qp.reshape(dec_batch, MLA_HEADS, LANES), head_pad)
    cn3 = ckv_new.reshape(dec_batch, 1, KV_LORA)
    kn3 = kpe_new.reshape(dec_batch, 1, LANES)
    row_blk = lambda b, pt: (b, 0, 0)
    out = pl.pallas_call(
        functools.partial(_paged_kernel, layer=layer, n_pages=n_pages, page=page,
                          group_tokens=_pick(n_pages * page, 2048)),
        grid_spec=pltpu.PrefetchScalarGridSpec(
            num_scalar_prefetch=1,
            grid=(dec_batch,),
            in_specs=[
                pl.BlockSpec((1, SUBLANES, KV_LORA), row_blk),
                pl.BlockSpec((1, SUBLANES, LANES), row_blk),
                pl.BlockSpec((1, 1, KV_LORA), row_blk),
                pl.BlockSpec((1, 1, LANES), row_blk),
                pl.BlockSpec(memory_space=pl.ANY),
                pl.BlockSpec(memory_space=pl.ANY),
            ],
            out_specs=pl.BlockSpec((1, SUBLANES, KV_LORA), row_blk),
            scratch_shapes=[
                pltpu.VMEM((N_SLOTS, n_pages * page, KV_LORA), F32),
                pltpu.VMEM((N_SLOTS, n_pages, MLA_ROPE, page), F32),
                pltpu.SemaphoreType.DMA((2, N_SLOTS)),
            ],
        ),
        out_shape=jax.ShapeDtypeStruct((dec_batch, SUBLANES, KV_LORA), F32),
        compiler_params=_cparams("arbitrary"),
        name="paged_attention",
    )(page_table, qa3, qp3, cn3, kn3, cache_ckv, cache_kpe_t)
    return out[:, :MLA_HEADS].reshape(dec_batch, MLA_HEADS * KV_LORA)


def _rwkv_prep_values(p, prev, mu_ref, w0_ref, wd_ref, a0_ref, wa_ref, kk_ref, ka_ref, rk_ref):
    xm = p + (prev - p) * mu_ref[...]
    r = xm[:, 0:RWKV_W]
    k = xm[:, RWKV_W:2 * RWKV_W]
    v = xm[:, 2 * RWKV_W:3 * RWKV_W]
    tail = xm[:, 3 * RWKV_W:]
    lane = lax.broadcasted_iota(jnp.int32, tail.shape, 1)
    lora_in = jnp.where(lane < DECAY_LORA, jnp.tanh(tail), tail).astype(BF16)
    dw = jnp.dot(lora_in, wd_ref[...], preferred_element_type=F32)
    da = jnp.dot(lora_in, wa_ref[...], preferred_element_type=F32)
    z = -(w0_ref[...] + dw)
    softplus = jnp.maximum(z, 0.0) + jnp.log(1.0 + jnp.exp(-jnp.abs(z)))
    w = -softplus - 0.5
    log_decay = -jnp.exp(w)
    a = jax.nn.sigmoid(a0_ref[...] + da)
    ones_bd = _head_ones(RWKV_W, RWKV_HEAD)
    kk = k * kk_ref[...]
    kk = kk / jnp.maximum(jnp.sqrt(_head_sum(kk * kk, ones_bd)), 1e-12)
    k_mod = k * (1.0 + (a - 1.0) * ka_ref[...])
    bonus = _head_sum(r * k_mod * rk_ref[...], ones_bd) * v
    return r, k_mod, v, kk, kk * a, log_decay, bonus


def _rwkv_param_specs(index_map):
    widths = (SHIFT_W, RWKV_W, None, RWKV_W, None, RWKV_W, RWKV_W, RWKV_W)
    return [pl.BlockSpec((LANES, RWKV_W) if w is None else (1, w), index_map) for w in widths]


def _rwkv_params(lw):
    return (lw["rw_mu"], lw["rw_w0"], lw["rw_wd"], lw["rw_a0"], lw["rw_wa"], lw["rw_k_k"], lw["rw_k_a"],
            lw["rw_r_k"])


def _rwkv_prep_sample_kernel(p_ref, prev_ref, mu_ref, w0_ref, wd_ref, a0_ref, wa_ref, kk_ref, ka_ref, rk_ref,
                             r_o, k_o, v_o, kk_o, b_o, d_o, bonus_o):
    r, k_mod, v, kk, b, log_decay, bonus = _rwkv_prep_values(
        p_ref[...], prev_ref[...], mu_ref, w0_ref, wd_ref, a0_ref, wa_ref, kk_ref, ka_ref, rk_ref)
    r_o[...] = r
    k_o[...] = k_mod
    v_o[...] = v
    kk_o[...] = kk
    b_o[...] = b
    d_o[...] = jnp.exp(log_decay)
    bonus_o[...] = bonus


def _rwkv_prep_sample(rw_in, lw, shift_rows):
    rows = rw_in.shape[0]
    const = lambda i: (0, 0)
    return pl.pallas_call(
        _rwkv_prep_sample_kernel,
        grid=(1,),
        in_specs=[pl.BlockSpec((rows, SHIFT_W), const)] * 2 + _rwkv_param_specs(const),
        out_specs=[pl.BlockSpec((rows, RWKV_W), const)] * 7,
        out_shape=[jax.ShapeDtypeStruct((rows, RWKV_W), F32)] * 7,
        compiler_params=_cparams("arbitrary"),
        name="rwkv_prep_sample",
    )(rw_in, shift_rows, *_rwkv_params(lw))


RW_CHUNK = 64
RW_SUB = 16


def _split3(x):
    hi = x.astype(BF16)
    r1 = x - hi.astype(F32)
    mid = r1.astype(BF16)
    lo = (r1 - mid.astype(F32)).astype(BF16)
    return hi, mid, lo


def _block_diag(y, bd_mask):
    return jnp.where(bd_mask, jnp.concatenate([y] * RWKV_HEADS, axis=0), 0.0).astype(BF16)


def _mm(x, y_bd):
    return jnp.dot(x.astype(BF16), y_bd, preferred_element_type=F32)


def _rwkv_chunks(vals, n0, masks):
    bd_mask, tri_incl, strict, incl, same_sub, eye_tiled, ones_bd = masks
    C = RW_CHUNK
    seqs = range(len(vals))
    bd = lambda y: _block_diag(y, bd_mask)
    r, kx, v, kap, bb, lam = ([val[j] for val in vals] for j in range(6))
    cum = [sum(jnp.dot(tri_incl, part, preferred_element_type=F32) for part in _split3(lam[i])) for i in seqs]
    cum_last = [c[C - 1:C, :] for c in cum]
    e_neg = [jnp.exp(-c) for c in cum]
    ap = [jnp.concatenate([kap[i] * jnp.exp(cum[i] - lam[i]), r[i] * jnp.exp(cum[i])], axis=0).astype(BF16)
          for i in seqs]
    rhs = [jnp.concatenate([bd(bb[i] * e_neg[i]), bd(kx[i] * e_neg[i])], axis=0) for i in seqs]
    g = [lax.dot_general(ap[i], rhs[i], (((1,), (1,)), ((), ())), preferred_element_type=F32) for i in seqs]
    l_ab = [jnp.where(strict, gi[:C, :RWKV_W], 0.0) for gi in g]
    l_ak = [jnp.where(strict, gi[:C, RWKV_W:], 0.0) for gi in g]
    q_pb = [jnp.where(incl, gi[C:, :RWKV_W], 0.0) for gi in g]
    q_pk = [jnp.where(incl, gi[C:, RWKV_W:], 0.0) for gi in g]
    apn = [jnp.dot(ap[i], bd(n0[i]), preferred_element_type=F32) for i in seqs]
    lq_v = [_mm(jnp.concatenate([l_ak[i], q_pk[i]], axis=0), bd(v[i])) for i in seqs]
    w = [apn[i][:C] + lq_v[i][:C] for i in seqs]

    l_d = [jnp.where(same_sub, li, 0.0) for li in l_ab]
    l_off = [l_ab[i] - l_d[i] for i in seqs]
    q_inv = [-li for li in l_d]
    power = l_d
    for _ in range(RW_SUB.bit_length() - 2):
        power = [_mm(pw, bd(pw)) for pw in power]
        q_inv = [q_inv[i] + power[i] + _mm(power[i], bd(q_inv[i])) for i in seqs]
    qz = [jnp.dot(q_inv[i].astype(BF16), jnp.concatenate([bd(l_off[i]), bd(w[i])], axis=1),
                  preferred_element_type=F32) for i in seqs]
    n = [l_off[i] + qz[i][:, :RWKV_W] for i in seqs]
    w1 = [w[i] + qz[i][:, RWKV_W:] for i in seqs]
    n_sq = [_mm(ni, bd(ni)) for ni in n]
    y = [w1[i] + _mm(n_sq[i], bd(w1[i])) for i in seqs]
    u = [_mm(n[i], bd(y[i])) - y[i] for i in seqs]

    o = [apn[i][C:] + _mm(q_pb[i], bd(u[i])) + lq_v[i][C:] for i in seqs]

    e_end = [jnp.exp(cum_last[i] - cum[i]) for i in seqs]
    lhs_t = [jnp.concatenate([bb[i] * e_end[i], kx[i] * e_end[i]], axis=0).T.astype(BF16) for i in seqs]
    full = [jnp.dot(lhs_t[i], jnp.concatenate([u[i], v[i]], axis=0).astype(BF16), preferred_element_type=F32)
            for i in seqs]
    heads = [slice(h * RWKV_HEAD, (h + 1) * RWKV_HEAD) for h in range(RWKV_HEADS)]
    delta = [sum(jnp.where(bd_mask[hs], f[hs], 0.0) for hs in heads) for f in full]
    g_wide = [sum(jnp.dot(part, ones_bd, preferred_element_type=F32)
                  for part in _split3(jnp.where(eye_tiled, jnp.exp(cl), 0.0))) for cl in cum_last]
    return o, [n0[i] * g_wide[i] + delta[i] for i in seqs]


def _rwkv_prompt_kernel(p_ref, rg_ref, mu_ref, w0_ref, wd_ref, a0_ref, wa_ref, kk_ref, ka_ref, rk_ref,
                        lnxg_ref, lnxb_ref, o_ref, sfin_ref, state_ref, last_ref):
    nb, C, _ = p_ref.shape
    c_idx = pl.program_id(1)

    @pl.when(c_idx == 0)
    def _():
        state_ref[...] = jnp.zeros(state_ref.shape, F32)
        last_ref[...] = jnp.zeros(last_ref.shape, F32)

    p = p_ref[...].reshape(nb * C, SHIFT_W)
    row = lax.broadcasted_iota(jnp.int32, (nb * C, 1), 0)
    prev = pltpu.roll(p, 1, 0)
    for i in range(nb):
        prev = jnp.where(row == i * C, last_ref[i, 0:1, :], prev)
    vals = _rwkv_prep_values(p, prev, mu_ref, w0_ref, wd_ref, a0_ref, wa_ref, kk_ref, ka_ref, rk_ref)
    bonus = vals[6]

    lane = lax.broadcasted_iota(jnp.int32, (C, RWKV_W), 1)
    t_idx = lax.broadcasted_iota(jnp.int32, (C, RWKV_W), 0)
    s_idx = lane & (RWKV_HEAD - 1)
    row_big = lax.broadcasted_iota(jnp.int32, (RWKV_W, RWKV_W), 0)
    lane_big = lax.broadcasted_iota(jnp.int32, (RWKV_W, RWKV_W), 1)
    bd_mask = _group_index(row_big, RWKV_HEAD) == _group_index(lane_big, RWKV_HEAD)
    ones_bd = jnp.where(bd_mask, 1.0, 0.0).astype(BF16)
    tri_r = lax.broadcasted_iota(jnp.int32, (C, C), 0)
    tri_c = lax.broadcasted_iota(jnp.int32, (C, C), 1)
    masks = (bd_mask, jnp.where(tri_r >= tri_c, 1.0, 0.0).astype(BF16), t_idx > s_idx, t_idx >= s_idx,
             _group_index(t_idx, RW_SUB) == _group_index(s_idx, RW_SUB), t_idx == s_idx, ones_bd)

    per_seq = [tuple(x[i * C:(i + 1) * C] for x in vals[:6]) for i in range(nb)]
    outs, n_new = _rwkv_chunks(per_seq, [state_ref[i] for i in range(nb)], masks)
    for i in range(nb):
        state_ref[i] = n_new[i]
        last_ref[i, 0:1, :] = p[(i + 1) * C - 1:(i + 1) * C, :]
    o = jnp.concatenate(outs, axis=0)

    c = o - _head_sum(o, ones_bd) * (1.0 / RWKV_HEAD)
    var = _head_sum(c * c, ones_bd) * (1.0 / RWKV_HEAD)
    on = c * lax.rsqrt(var + GN_EPS) * lnxg_ref[...] + lnxb_ref[...]
    o_ref[...] = ((on + bonus) * rg_ref[...].reshape(nb * C, RWKV_W)).astype(BF16).reshape(nb, C, RWKV_W)

    @pl.when(c_idx == pl.num_programs(1) - 1)
    def _():
        sfin_ref[...] = state_ref[...]


def _rwkv_prompt(rw_in, rg, lw, batch, seq, nb):
    const = lambda b, c: (0, 0)
    blk = lambda b, c: (b, c, 0)
    o, s_fin = pl.pallas_call(
        _rwkv_prompt_kernel,
        grid=(batch // nb, seq // RW_CHUNK),
        in_specs=[pl.BlockSpec((nb, RW_CHUNK, SHIFT_W), blk), pl.BlockSpec((nb, RW_CHUNK, RWKV_W), blk)]
        + _rwkv_param_specs(const) + [pl.BlockSpec((1, RWKV_W), const)] * 2,
        out_specs=[pl.BlockSpec((nb, RW_CHUNK, RWKV_W), blk),
                   pl.BlockSpec((nb, RWKV_HEAD, RWKV_W), lambda b, c: (b, 0, 0))],
        out_shape=[jax.ShapeDtypeStruct((batch, seq, RWKV_W), BF16),
                   jax.ShapeDtypeStruct((batch, RWKV_HEAD, RWKV_W), F32)],
        scratch_shapes=[pltpu.VMEM((nb, RWKV_HEAD, RWKV_W), F32), pltpu.VMEM((nb, SUBLANES, SHIFT_W), F32)],
        compiler_params=_cparams("arbitrary", "arbitrary"),
        name="rwkv_prompt",
    )(rw_in.reshape(batch, seq, SHIFT_W), rg.reshape(batch, seq, RWKV_W), *_rwkv_params(lw),
      lw["lnx_g"], lw["lnx_b"])
    wkv = jnp.transpose(s_fin.reshape(batch, RWKV_HEAD, RWKV_HEADS, RWKV_HEAD), (0, 2, 3, 1))
    return o.reshape(batch * seq, RWKV_W), wkv


def _rwkv_step_kernel(x_ref, s_ref, *rest, n_prev):
    if n_prev:
        prev_ref, o_ref, snew_ref = rest
    else:
        o_ref, snew_ref = rest
    for j in range(n_prev):
        snew_ref[j] = prev_ref[j]
    kk, b, d, kx, r = (x_ref[j, 0] for j in range(5))
    for vi in range(RWKV_HEAD):
        sv = s_ref[0, 0, vi]
        sa = jnp.sum(sv * kk, axis=0, keepdims=True)
        sn = sv * d - sa * b + x_ref[5, 0, pl.ds(vi, 1), :] * kx
        snew_ref[n_prev, 0, vi] = sn
        o_ref[0, pl.ds(vi, 1), :] = jnp.sum(sn * r, axis=0, keepdims=True)


def _rwkv_step(prep, state_t, layer, prev_stack):
    r, k_mod, v, kk, b, d, _ = prep
    H, N = RWKV_HEADS, RWKV_HEAD
    batch = r.shape[0]
    n_prev = 0 if prev_stack is None else prev_stack.shape[0]
    x = jnp.transpose(jnp.stack([kk, b, d, k_mod, r, v]).reshape(6, batch, H, N), (0, 2, 3, 1))
    head_blk = lambda h: (0, h, 0, 0, 0)
    in_specs = [pl.BlockSpec((6, 1, N, batch), lambda h: (0, h, 0, 0)),
                pl.BlockSpec((1, 1, N, N, batch), lambda h: (layer, h, 0, 0, 0))]
    args = [x, state_t]
    if n_prev:
        in_specs.append(pl.BlockSpec((n_prev, 1, N, N, batch), head_blk))
        args.append(prev_stack)
    o, stack = pl.pallas_call(
        functools.partial(_rwkv_step_kernel, n_prev=n_prev),
        grid=(H,),
        in_specs=in_specs,
        out_specs=[pl.BlockSpec((1, N, batch), lambda h: (h, 0, 0)),
                   pl.BlockSpec((n_prev + 1, 1, N, N, batch), head_blk)],
        out_shape=[jax.ShapeDtypeStruct((H, N, batch), F32),
                   jax.ShapeDtypeStruct((n_prev + 1, H, N, N, batch), F32)],
        compiler_params=_cparams("arbitrary"),
        name="rwkv_step",
    )(*args)
    return jnp.transpose(o, (2, 0, 1)).reshape(batch, H * N), stack


def _mix_out(mixed, x_ref, mod_ref, wout_ref, gpost_ref, y_ref):
    out = jnp.dot(mixed, wout_ref[...], preferred_element_type=F32)
    gate = _rows2d(mod_ref)[:, 2 * D_MODEL:]
    y_ref[...] = x_ref[...] + gate * _rms(out, gpost_ref[...])


def _out_prompt_kernel(om_ref, orw_ref, gm_ref, x_ref, mod_ref, wout_ref, gpost_ref, y_ref):
    mixed = jnp.concatenate([om_ref[...], orw_ref[...], gm_ref[...]], axis=-1)
    _mix_out(mixed, x_ref, mod_ref, wout_ref, gpost_ref, y_ref)


def _out_sample_kernel(lat_ref, mg_ref, orw_ref, bonus_ref, rg_ref, gm_ref, x_ref, mod_ref, wout_ref, gpost_ref,
                       lnxg_ref, lnxb_ref, wuv_ref, y_ref):
    lat = lat_ref[...]
    o_mla = jnp.concatenate(
        [jnp.dot(lat[:, h * KV_LORA:(h + 1) * KV_LORA].astype(BF16), wuv_ref[h],
                 preferred_element_type=F32) for h in range(MLA_HEADS)], axis=-1)
    o_mla = (o_mla * mg_ref[...].astype(F32)).astype(BF16)
    ones_bd = _head_ones(RWKV_W, RWKV_HEAD)
    o = orw_ref[...]
    c = o - _head_sum(o, ones_bd) * (1.0 / RWKV_HEAD)
    var = _head_sum(c * c, ones_bd) * (1.0 / RWKV_HEAD)
    on = c * lax.rsqrt(var + GN_EPS) * lnxg_ref[...] + lnxb_ref[...]
    o_rw = ((on + bonus_ref[...]) * rg_ref[...]).astype(BF16)
    mixed = jnp.concatenate([o_mla, o_rw, gm_ref[...]], axis=-1)
    _mix_out(mixed, x_ref, mod_ref, wout_ref, gpost_ref, y_ref)


def _out_prompt(o_mla, o_rw, gm, x2d, mod_b, lw, seq, tm):
    rows = x2d.shape[0]
    tiles_per_seq = seq // tm
    const = lambda i: (0, 0)
    row_blk = lambda i: (i, 0)
    return pl.pallas_call(
        _out_prompt_kernel,
        grid=(rows // tm,),
        in_specs=[
            pl.BlockSpec((tm, MLA_W), row_blk),
            pl.BlockSpec((tm, RWKV_W), row_blk),
            pl.BlockSpec((tm, GM_W), row_blk),
            pl.BlockSpec((tm, D_MODEL), row_blk),
            pl.BlockSpec((1, 1, 3 * D_MODEL), lambda i: (i // tiles_per_seq, 0, 0)),
            pl.BlockSpec((D_MODEL, D_MODEL), const),
            pl.BlockSpec((1, D_MODEL), const),
        ],
        out_specs=pl.BlockSpec((tm, D_MODEL), row_blk),
        out_shape=jax.ShapeDtypeStruct((rows, D_MODEL), F32),
        compiler_params=_cparams("arbitrary"),
        name="out_prompt",
    )(o_mla, o_rw, gm, x2d, mod_b, lw["w_out"], lw["g_post"])


def _out_sample(o_lat, mg, o_rw, bonus, rg, gm, x2d, mod_rows, lw):
    rows = x2d.shape[0]
    const = lambda i: (0, 0)
    full = lambda w: pl.BlockSpec((rows, w), const)
    return pl.pallas_call(
        _out_sample_kernel,
        grid=(1,),
        in_specs=[
            full(MLA_HEADS * KV_LORA), full(MLA_W), full(RWKV_W), full(RWKV_W), full(RWKV_W), full(GM_W),
            full(D_MODEL), full(3 * D_MODEL),
            pl.BlockSpec((D_MODEL, D_MODEL), const),
            pl.BlockSpec((1, D_MODEL), const),
            pl.BlockSpec((1, RWKV_W), const),
            pl.BlockSpec((1, RWKV_W), const),
            pl.BlockSpec((MLA_HEADS, KV_LORA, MLA_V), lambda i: (0, 0, 0)),
        ],
        out_specs=full(D_MODEL),
        out_shape=jax.ShapeDtypeStruct((rows, D_MODEL), F32),
        compiler_params=_cparams("arbitrary"),
        name="out_sample",
    )(o_lat, mg, o_rw, bonus, rg, gm, x2d, mod_rows, lw["w_out"], lw["g_post"], lw["lnx_g"], lw["lnx_b"],
      lw["w_uv_h"])


def _swap_halves(w):
    half = MLA_ROPE // 2
    return jnp.concatenate([w[..., half:], w[..., :half]], axis=-1)


def _rope_tables(pos):
    half = MLA_ROPE // 2
    inv = ROPE_THETA ** (-jnp.arange(half, dtype=F32) / half)
    ang = pos.astype(F32)[:, None] * inv[None, :]
    cos, sin = jnp.cos(ang), jnp.sin(ang)
    zeros = jnp.zeros((pos.shape[0], LANES - MLA_ROPE), F32)
    return (jnp.concatenate([cos, cos, zeros], axis=-1), jnp.concatenate([-sin, sin, zeros], axis=-1))


def _layer_weights(l, w_in, norm_pre_g, norm_post_g, q_norm_g, kv_norm_g, w_uq, w_uk, w_uv, rw_mu, rw_w0,
                   rw_w_decay_up, rw_a0, rw_w_a_up, rw_k_k, rw_k_a, rw_r_k, rw_lnx_g, rw_lnx_b, gm_ln_g,
                   gm_ln_b, gm_w_s, gm_b_s, w_out):
    w = w_in[l]
    split_lo, split_hi = Q_LORA + KV_LORA, Q_LORA + KV_LORA + MLA_ROPE
    kr = w[:, split_lo:split_hi]
    w_in_p = jnp.concatenate([w[:, :Q_LORA], kr, _swap_halves(kr), w[:, Q_LORA:split_lo], w[:, split_hi:]],
                             axis=1).astype(BF16)
    uq = w_uq[l]
    pe = uq[..., MLA_NOPE:]
    w_uq_p = jnp.concatenate([uq, _swap_halves(pe)], axis=-1).reshape(Q_LORA, MLA_HEADS * QHEAD_W).astype(BF16)
    zeros_lora = jnp.zeros((LANES - DECAY_LORA, RWKV_W), F32)
    row = lambda a: a.reshape(1, -1)
    return {
        "w_in": w_in_p, "g_pre": row(norm_pre_g[l]), "g_post": row(norm_post_g[l]),
        "g_q": row(q_norm_g[l]), "g_kv": row(kv_norm_g[l]), "w_uq": w_uq_p,
        "w_uk": w_uk[l].reshape(KV_LORA, MLA_W).astype(BF16),
        "w_uv": w_uv[l].reshape(KV_LORA, MLA_W).astype(BF16),
        "w_ukt": jnp.transpose(w_uk[l], (1, 2, 0)).astype(BF16),
        "w_uv_h": jnp.transpose(w_uv[l], (1, 0, 2)).astype(BF16),
        "rw_mu": row(rw_mu[l]), "rw_w0": row(rw_w0[l]), "rw_a0": row(rw_a0[l]),
        "rw_wd": jnp.concatenate([rw_w_decay_up[l], zeros_lora], axis=0).astype(BF16),
        "rw_wa": jnp.concatenate([zeros_lora, rw_w_a_up[l]], axis=0).astype(BF16),
        "rw_k_k": row(rw_k_k[l]), "rw_k_a": row(rw_k_a[l]), "rw_r_k": row(rw_r_k[l]),
        "lnx_g": row(rw_lnx_g[l]), "lnx_b": row(rw_lnx_b[l]),
        "ln_g": row(gm_ln_g[l]), "ln_b": row(gm_ln_b[l]),
        "gm_ws": gm_w_s[l],
        "gm_bias": jnp.repeat(gm_b_s[l].T, GM_GROUP_W, axis=1),
        "gm_coef0": row(jnp.repeat(gm_w_s[l][:, 0, 0], GM_GROUP_W)),
        "gm_bias0": row(jnp.repeat(gm_b_s[l][:, 0], GM_GROUP_W)),
        "w_out": w_out[l].astype(BF16),
    }


def _pick(full, want):
    return want if full % want == 0 else full


def kernel(x_prompt, x_sample, c_prompt, c_sample, cache_ckv, cache_kpe, page_table, state_wkv, state_shift, w_ada, b_ada, norm_pre_g, norm_post_g, w_in, q_norm_g, kv_norm_g, w_uq, w_uk, w_uv, rw_mu, rw_w0, rw_w_decay_up, rw_a0, rw_w_a_up, rw_k_k, rw_k_a, rw_r_k, rw_lnx_g, rw_lnx_b, gm_ln_g, gm_ln_b, gm_w_s, gm_b_s, w_out):
    batch, seq, _ = x_prompt.shape
    dec_batch, dec_seq, _ = x_sample.shape
    depth = w_in.shape[0]
    n_pages = page_table.shape[1]
    past_len = n_pages * cache_ckv.shape[2]
    assert dec_seq == 1 and seq % CHUNK == 0

    assert seq % RW_CHUNK == 0
    tm_proj = _pick(seq, 512)
    tm_out = _pick(seq, 1024)
    tq = _pick(seq, 256)
    rw_nb = _pick(batch, 16)

    mod = _ada_mod(jnp.concatenate([c_prompt, c_sample], axis=0), w_ada, b_ada)
    cs_p, sn_p = _rope_tables(jnp.arange(seq))
    cs_s, sn_s = _rope_tables(jnp.full((dec_batch,), past_len))
    cache_kpe_t = jnp.swapaxes(cache_kpe, 2, 3)

    y_p = x_prompt.reshape(batch * seq, D_MODEL)
    y_s = x_sample.reshape(dec_batch, D_MODEL)
    outs = {k: [] for k in ("ckv_s", "kpe_s", "wkv_p", "sh_p", "sh_s", "vc_p", "vc_s")}
    prompt_stacks = None
    wkv_s_stack = None
    state_wkv_t = jnp.transpose(state_wkv, (0, 2, 3, 4, 1))
    for l in range(depth):
        lw = _layer_weights(l, w_in, norm_pre_g, norm_post_g, q_norm_g, kv_norm_g, w_uq, w_uk, w_uv, rw_mu,
                            rw_w0, rw_w_decay_up, rw_a0, rw_w_a_up, rw_k_k, rw_k_a, rw_r_k, rw_lnx_g,
                            rw_lnx_b, gm_ln_g, gm_ln_b, gm_w_s, gm_b_s, w_out)
        mod_p = mod[l, :batch].reshape(batch, 1, 3 * D_MODEL)
        q, k, v, mg, ckv_stack, kpe_stack, rw_in, rg, gm, vn_last = _proj_prompt(
            y_p, mod_p, lw, cs_p, sn_p, batch, seq, tm_proj, prompt_stacks)
        prompt_stacks = (ckv_stack, kpe_stack)
        o_mla = _flash(q, k, v, mg, batch, seq, tq)
        o_rw, wkv_new = _rwkv_prompt(rw_in, rg, lw, batch, seq, rw_nb)
        y_p = _out_prompt(o_mla, o_rw, gm, y_p, mod_p, lw, seq, tm_out)
        outs["wkv_p"].append(wkv_new)
        outs["sh_p"].append(rw_in.reshape(batch, seq, SHIFT_W)[:, -1])
        outs["vc_p"].append(vn_last)

        mod_s = mod[l, batch:]
        qa, qp, mg_s, ckv_s, kpe_s, rw_s, rg_s, gm_s, vn_s = _proj_sample(y_s, mod_s, lw, cs_s, sn_s)
        o_lat = _paged_attention(qa, qp, ckv_s, kpe_s, cache_ckv, cache_kpe_t, page_table, l)
        prep_s = _rwkv_prep_sample(rw_s, lw, state_shift[l])
        o_rw_s, wkv_s_stack = _rwkv_step(prep_s, state_wkv_t, l, wkv_s_stack)
        y_s = _out_sample(o_lat, mg_s, o_rw_s, prep_s[6], rg_s, gm_s, y_s, mod_s, lw)
        outs["ckv_s"].append(ckv_s.reshape(dec_batch, 1, KV_LORA))
        outs["kpe_s"].append(kpe_s[:, :MLA_ROPE].reshape(dec_batch, 1, MLA_ROPE))
        outs["sh_s"].append(rw_s)
        outs["vc_s"].append(vn_s.reshape(dec_batch, 1, GM_W))

    st = lambda name: jnp.stack(outs[name])
    ckv_p, kpe_p = prompt_stacks
    return (y_p.reshape(batch, seq, D_MODEL), y_s.reshape(dec_batch, 1, D_MODEL),
            ckv_p.reshape(depth, batch, seq, KV_LORA), kpe_p.reshape(depth, batch, seq, MLA_ROPE),
            st("ckv_s"), st("kpe_s"), st("wkv_p"), jnp.transpose(wkv_s_stack, (0, 4, 1, 2, 3)),
            st("sh_p"), st("sh_s"), st("vc_p"), st("vc_s"))
```
